```python
import math
import jax, jax.numpy as jnp
from jax import lax
import numpy as np

D_MODEL = 2048
BATCH = 4
SEQ = 2048
DEPTH = 1
DEC_BATCH = 128
DEC_SEQ = 1
PAST_LEN = 16384
PAGE_SIZE = 128

N_META = 16
NORM_EPS = 1e-5
SSD_WIDTH = D_MODEL
SSD_HEAD_DIM = 64
SSD_HEADS = SSD_WIDTH // SSD_HEAD_DIM
SSD_GROUPS = 4
SSD_HPG = SSD_HEADS // SSD_GROUPS
SSD_STATE = 128
SSD_CONV = 4
SSD_CHUNK = 128
SSD_XBC = SSD_WIDTH + 2 * SSD_GROUPS * SSD_STATE
S5_WIDTH = D_MODEL // 2
S5_GROUP = 16
S5_GROUPS = S5_WIDTH // S5_GROUP
S5_STATE = 64
IN_COLS = SSD_WIDTH + SSD_XBC + SSD_HEADS + 2 * S5_WIDTH + 2 * D_MODEL

kernel_name = "hybrid_ssd_s5_gated_decoder_step"


def _rmsnorm(x, w):
    xf = x.astype(jnp.float32)
    y = xf * lax.rsqrt(jnp.mean(xf * xf, axis=-1, keepdims=True) + NORM_EPS)
    return (y * w.astype(jnp.float32)).astype(x.dtype)


def _split_in(p):
    sizes = (SSD_WIDTH, SSD_XBC, SSD_HEADS, S5_WIDTH, S5_WIDTH, D_MODEL)
    idx, acc = [], 0
    for s in sizes:
        acc += s
        idx.append(acc)
    return jnp.split(p, idx, axis=-1)


def _causal_conv(u, prev, w, b):
    T = u.shape[1]
    up = jnp.concatenate([prev.astype(u.dtype), u], axis=1)
    y = b + up[:, 0:T] * w[0]
    for k in range(1, SSD_CONV):
        y = y + up[:, k:k + T] * w[k]
    return y, up[:, T:]


def _ssd_chunked(x, dt, A, bm, cm, h0):
    bsz, T = x.shape[:2]
    pad = (-T) % SSD_CHUNK
    padt = lambda a: jnp.pad(a, [(0, 0), (pad, 0)] + [(0, 0)] * (a.ndim - 2))
    x, dt, bm, cm = padt(x), padt(dt), padt(bm), padt(cm)
    nc = (T + pad) // SSD_CHUNK
    G, R, P, N, L = SSD_GROUPS, SSD_HPG, SSD_HEAD_DIM, SSD_STATE, SSD_CHUNK
    xd = (x * dt[..., None]).reshape(bsz, nc, L, G, R, P)
    a = (dt * A).reshape(bsz, nc, L, G, R)
    bc = bm.reshape(bsz, nc, L, G, N)
    cc = cm.reshape(bsz, nc, L, G, N)
    a_cum = jnp.cumsum(a, axis=2)
    seg = a_cum[:, :, :, None] - a_cum[:, :, None, :]
    causal = jnp.tril(jnp.ones((L, L), dtype=bool))[None, None, :, :, None, None]
    decay_ls = jnp.exp(jnp.where(causal, seg, -jnp.inf))
    cb = jnp.einsum('bclgn,bcsgn->bclsg', cc, bc)
    y_diag = jnp.einsum('bclsg,bclsgr,bcsgrp->bclgrp', cb, decay_ls, xd)
    decay_end = jnp.exp(a_cum[:, :, -1:] - a_cum)
    chunk_states = jnp.einsum('bclgn,bclgr,bclgrp->bcgrpn', bc, decay_end, xd)
    chunk_decay = jnp.exp(a_cum[:, :, -1])

    def step(h, inp):
        s_c, d_c = inp
        return d_c[..., None, None] * h + s_c, h

    h_last, h_in = lax.scan(step, h0.reshape(bsz, G, R, P, N),
                            (jnp.moveaxis(chunk_states, 1, 0), jnp.moveaxis(chunk_decay, 1, 0)))
    h_in = jnp.moveaxis(h_in, 0, 1)
    y_off = jnp.einsum('bclgn,bcgrpn,bclgr->bclgrp', cc, h_in, jnp.exp(a_cum))
    y = (y_diag + y_off).reshape(bsz, nc * L, G * R, P)[:, pad:]
    return y, h_last.reshape(bsz, G * R, P, N)


def _complex_affine_combine(e1, e2):
    a1r, a1i, b1r, b1i = e1
    a2r, a2i, b2r, b2i = e2
    return (a2r * a1r - a2i * a1i,
            a2r * a1i + a2i * a1r,
            a2r * b1r - a2i * b1i + b2r,
            a2r * b1i + a2i * b1r + b2i)


def _s5_scan(u, h0_re, h0_im, lam_re, lam_im, log_dt, b_re, b_im, c_re, c_im, d_skip):
    bsz, T, _ = u.shape
    f32 = jnp.float32
    uf = u.astype(f32)
    ug = uf.reshape(bsz, T, S5_GROUPS, S5_GROUP)
    lr, li = lam_re.astype(f32), lam_im.astype(f32)
    step = jnp.exp(log_dt.astype(f32))[:, None]
    mag = jnp.exp(lr * step)
    ab_re, ab_im = mag * jnp.cos(li * step), mag * jnp.sin(li * step)
    den = lr * lr + li * li
    num_re = ab_re - 1.0
    coef_re = (num_re * lr + ab_im * li) / den
    coef_im = (ab_im * lr - num_re * li) / den
    br, bi = b_re.astype(f32), b_im.astype(f32)
    bbar_re = coef_re[..., None] * br - coef_im[..., None] * bi
    bbar_im = coef_re[..., None] * bi + coef_im[..., None] * br
    bu_re = jnp.einsum('btgc,gnc->btgn', ug, bbar_re)
    bu_im = jnp.einsum('btgc,gnc->btgn', ug, bbar_im)
    h0r, h0i = h0_re.astype(f32), h0_im.astype(f32)
    bu_re = bu_re.at[:, 0].add(ab_re * h0r - ab_im * h0i)
    bu_im = bu_im.at[:, 0].add(ab_re * h0i + ab_im * h0r)
    a_re = jnp.broadcast_to(ab_re, bu_re.shape)
    a_im = jnp.broadcast_to(ab_im, bu_im.shape)
    _, _, h_re, h_im = lax.associative_scan(_complex_affine_combine, (a_re, a_im, bu_re, bu_im), axis=1)
    y = (jnp.einsum('gcn,btgn->btgc', c_re.astype(f32), h_re)
         - jnp.einsum('gcn,btgn->btgc', c_im.astype(f32), h_im))
    y = y.reshape(bsz, T, S5_WIDTH) + d_skip.astype(f32) * uf
    return y, h_re[:, -1], h_im[:, -1]


def _layer(h, ssd_h0, conv_prev, s5_re0, s5_im0, norm_w, w_in, conv_w, conv_b, dt_bias, a_log,
           d_ssd, ssd_norm_w, w_proj_a, lam_re, lam_im, log_dt_s5, b_re, b_im, c_re, c_im,
           d_s5, w_glu, b_glu, w_proj_b, w_out):
    f32 = jnp.float32
    bsz, T, _ = h.shape
    xn = _rmsnorm(h, norm_w)
    proj = xn @ w_in
    z_a, xbc, dt_raw, u_b, z_b, g_a, g_b = _split_in(proj)
    xbc, conv_new = _causal_conv(xbc, conv_prev, conv_w, conv_b)
    xbc = jax.nn.silu(xbc)
    xs, bm, cm = jnp.split(xbc, [SSD_WIDTH, SSD_WIDTH + SSD_GROUPS * SSD_STATE], axis=-1)
    xs_h = xs.astype(f32).reshape(bsz, T, SSD_HEADS, SSD_HEAD_DIM)
    dt = jax.nn.softplus(dt_raw.astype(f32) + dt_bias.astype(f32))
    A = -jnp.exp(a_log.astype(f32))
    y, ssd_new = _ssd_chunked(xs_h, dt, A,
                              bm.astype(f32).reshape(bsz, T, SSD_GROUPS, SSD_STATE),
                              cm.astype(f32).reshape(bsz, T, SSD_GROUPS, SSD_STATE),
                              ssd_h0.astype(f32))
    y = y + d_ssd.astype(f32)[:, None] * xs_h
    y = y.reshape(bsz, T, SSD_WIDTH) * jax.nn.silu(z_a.astype(f32))
    yg = y.reshape(bsz, T, SSD_GROUPS, SSD_WIDTH // SSD_GROUPS)
    yg = yg * lax.rsqrt(jnp.mean(yg * yg, axis=-1, keepdims=True) + NORM_EPS)
    y = yg.reshape(bsz, T, SSD_WIDTH) * ssd_norm_w.astype(f32)
    ya = y.astype(h.dtype) @ w_proj_a
    yb, s5r, s5i = _s5_scan(u_b, s5_re0, s5_im0, lam_re, lam_im, log_dt_s5, b_re, b_im,
                            c_re, c_im, d_s5)
    yb = jax.nn.gelu(yb).astype(h.dtype)
    yb = yb * jax.nn.sigmoid(yb @ w_glu + b_glu) * jax.nn.silu(z_b)
    yb = yb @ w_proj_b
    mixed = jax.nn.sigmoid(g_a) * ya + jax.nn.sigmoid(g_b) * yb
    out = h + mixed @ w_out
    return out, ssd_new.astype(h.dtype), conv_new, s5r.astype(h.dtype), s5i.astype(h.dtype)


def setup_inputs(seed: int = 0) -> dict:
    key = jax.random.key(seed)
    ks = jax.random.split(key, 32)
    f32 = jnp.float32
    nrm = lambda k, shape, s: jax.random.normal(k, shape, f32) * s
    dt0 = jnp.exp(jax.random.uniform(ks[10], (DEPTH, SSD_HEADS), f32, math.log(1e-3), math.log(1e-1)))
    n_idx = jnp.arange(S5_STATE, dtype=f32)
    return {
        "x_prompt": nrm(ks[0], (BATCH, SEQ, D_MODEL), 1.0),
        "x_sample": nrm(ks[1], (DEC_BATCH, DEC_SEQ, D_MODEL), 1.0),
        "state_ssd": nrm(ks[2], (DEPTH, DEC_BATCH, SSD_HEADS, SSD_HEAD_DIM, SSD_STATE), 0.3),
        "state_conv": nrm(ks[3], (DEPTH, DEC_BATCH, SSD_CONV - 1, SSD_XBC), 1.0),
        "state_s5_re": nrm(ks[4], (DEPTH, DEC_BATCH, S5_GROUPS, S5_STATE), 0.3),
        "state_s5_im": nrm(ks[5], (DEPTH, DEC_BATCH, S5_GROUPS, S5_STATE), 0.3),
        "meta_tokens": nrm(ks[6], (N_META, D_MODEL), 1.0),
        "norm_w": 1.0 + nrm(ks[7], (DEPTH, D_MODEL), 0.01),
        "w_in": nrm(ks[8], (DEPTH, D_MODEL, IN_COLS), D_MODEL ** -0.5),
        "conv_w": nrm(ks[9], (DEPTH, SSD_CONV, SSD_XBC), SSD_CONV ** -0.5),
        "conv_b": nrm(ks[11], (DEPTH, SSD_XBC), 0.01),
        "dt_bias": dt0 + jnp.log(-jnp.expm1(-dt0)),
        "a_log": jnp.log(jax.random.uniform(ks[12], (DEPTH, SSD_HEADS), f32, 1.0, 16.0)),
        "d_ssd": 1.0 + nrm(ks[13], (DEPTH, SSD_HEADS), 0.01),
        "ssd_norm_w": 1.0 + nrm(ks[14], (DEPTH, SSD_WIDTH), 0.01),
        "w_proj_a": nrm(ks[15], (DEPTH, SSD_WIDTH, D_MODEL), SSD_WIDTH ** -0.5),
        "lam_re": -0.5 + nrm(ks[16], (DEPTH, S5_GROUPS, S5_STATE), 0.01),
        "lam_im": math.pi * n_idx + nrm(ks[17], (DEPTH, S5_GROUPS, S5_STATE), 0.01),
        "log_dt_s5": jax.random.uniform(ks[18], (DEPTH, S5_GROUPS), f32, math.log(1e-3), math.log(1e-1)),
        "b_re": nrm(ks[19], (DEPTH, S5_GROUPS, S5_STATE, S5_GROUP), (2 * S5_GROUP) ** -0.5),
        "b_im": nrm(ks[20], (DEPTH, S5_GROUPS, S5_STATE, S5_GROUP), (2 * S5_GROUP) ** -0.5),
        "c_re": nrm(ks[21], (DEPTH, S5_GROUPS, S5_GROUP, S5_STATE), (2 * S5_STATE) ** -0.5),
        "c_im": nrm(ks[22], (DEPTH, S5_GROUPS, S5_GROUP, S5_STATE), (2 * S5_STATE) ** -0.5),
        "d_s5": nrm(ks[23], (DEPTH, S5_WIDTH), 1.0),
        "w_glu": nrm(ks[24], (DEPTH, S5_WIDTH, S5_WIDTH), S5_WIDTH ** -0.5),
        "b_glu": nrm(ks[25], (DEPTH, S5_WIDTH), 0.01),
        "w_proj_b": nrm(ks[26], (DEPTH, S5_WIDTH, D_MODEL), S5_WIDTH ** -0.5),
        "w_out": nrm(ks[27], (DEPTH, D_MODEL, D_MODEL), D_MODEL ** -0.5),
        "final_norm_w": 1.0 + nrm(ks[28], (D_MODEL,), 0.01),
    }


def reference(x_prompt, x_sample, state_ssd, state_conv, state_s5_re, state_s5_im, meta_tokens,
              norm_w, w_in, conv_w, conv_b, dt_bias, a_log, d_ssd, ssd_norm_w, w_proj_a,
              lam_re, lam_im, log_dt_s5, b_re, b_im, c_re, c_im, d_s5, w_glu, b_glu, w_proj_b,
              w_out, final_norm_w):
    bp = x_prompt.shape[0]
    meta = jnp.broadcast_to(meta_tokens.astype(x_prompt.dtype)[None], (bp, N_META, D_MODEL))
    hp = jnp.concatenate([meta, x_prompt], axis=1)
    hs = x_sample
    ssd_p, conv_p, re_p, im_p = [], [], [], []
    ssd_s, conv_s, re_s, im_s = [], [], [], []
    for i in range(DEPTH):
        lw = (norm_w[i], w_in[i], conv_w[i], conv_b[i], dt_bias[i], a_log[i], d_ssd[i],
              ssd_norm_w[i], w_proj_a[i], lam_re[i], lam_im[i], log_dt_s5[i], b_re[i], b_im[i],
              c_re[i], c_im[i], d_s5[i], w_glu[i], b_glu[i], w_proj_b[i], w_out[i])
        zs = jnp.zeros((bp, SSD_HEADS, SSD_HEAD_DIM, SSD_STATE), hp.dtype)
        zc = jnp.zeros((bp, SSD_CONV - 1, SSD_XBC), hp.dtype)
        z5 = jnp.zeros((bp, S5_GROUPS, S5_STATE), hp.dtype)
        hp, a1, a2, a3, a4 = _layer(hp, zs, zc, z5, z5, *lw)
        hs, s1, s2, s3, s4 = _layer(hs, state_ssd[i], state_conv[i], state_s5_re[i],
                                    state_s5_im[i], *lw)
        ssd_p.append(a1); conv_p.append(a2); re_p.append(a3); im_p.append(a4)
        ssd_s.append(s1); conv_s.append(s2); re_s.append(s3); im_s.append(s4)
    y_prompt = _rmsnorm(hp, final_norm_w)[:, N_META:]
    y_sample = _rmsnorm(hs, final_norm_w)
    return (y_prompt, y_sample, jnp.stack(ssd_p), jnp.stack(conv_p), jnp.stack(re_p), jnp.stack(im_p),
            jnp.stack(ssd_s), jnp.stack(conv_s), jnp.stack(re_s), jnp.stack(im_s))
```

```python
import functools

import jax
import jax.numpy as jnp
from jax import lax
from jax.experimental import pallas as pl
from jax.experimental.pallas import tpu as pltpu

F32 = jnp.float32
BF16 = jnp.bfloat16

NORM_EPS = 1e-5
N_META = 16
D_MODEL = 2048
SSD_HEAD_DIM = 64
SSD_HEADS = 32
SSD_GROUPS = 4
SSD_STATE = 128
SSD_WIDTH = SSD_HEADS * SSD_HEAD_DIM
SSD_XBC = SSD_WIDTH + 2 * SSD_GROUPS * SSD_STATE
SSD_CONV = 4
CHUNK = 128
S5_WIDTH = D_MODEL // 2
S5_GROUP = 16
S5_GROUPS = S5_WIDTH // S5_GROUP
S5_STATE = 64
S5_NSTATE = S5_GROUPS * S5_STATE
S5_BLOCKS = 4
S5_BLK_U = S5_WIDTH // S5_BLOCKS
S5_BLK_N = S5_NSTATE // S5_BLOCKS

LANES = 128
SUBLANES = 8
VMEM_LIMIT = 56 * 1024 * 1024

COL_ZA = 0
COL_GA = 2048
COL_GB = 4096
COL_XBC = 6144
COL_UB = 9216
COL_ZB = 10240
PROJ_COLS = 11264
PROJ_TN = 1024


def _sigmoid(x):
    return 1.0 / (1.0 + jnp.exp(-x))


def _silu(x):
    return x * _sigmoid(x)


def _softplus(x):
    return jnp.maximum(x, 0.0) + jnp.log1p(jnp.exp(-jnp.abs(x)))


def _gelu_tanh(x):
    c = 0.7978845608028654
    return 0.5 * x * (1.0 + jnp.tanh(c * (x + 0.044715 * (x * x * x))))


def _split3(x):
    x1 = x.astype(BF16)
    r1 = x - x1.astype(F32)
    x2 = r1.astype(BF16)
    x3 = (r1 - x2.astype(F32)).astype(BF16)
    return x1, x2, x3


def _params(*sem):
    return pltpu.CompilerParams(dimension_semantics=sem, vmem_limit_bytes=VMEM_LIMIT)


def _s5prep_kernel(lr_ref, li_ref, ldt_ref, btr_ref, bti_ref, abr_ref, abi_ref, bbr_ref, bbi_ref):
    lr = lr_ref[...]
    li = li_ref[...]
    step = jnp.exp(ldt_ref[...])
    mag = jnp.exp(lr * step)
    abr = mag * jnp.cos(li * step)
    abi = mag * jnp.sin(li * step)
    den = lr * lr + li * li
    numr = abr - 1.0
    cr = (numr * lr + abi * li) / den
    ci = (abi * lr - numr * li) / den
    abr_ref[...] = abr
    abi_ref[...] = abi
    btr = btr_ref[...]
    bti = bti_ref[...]
    crb = cr[:, None, :]
    cib = ci[:, None, :]
    bbr_ref[...] = crb * btr - cib * bti
    bbi_ref[...] = crb * bti + cib * btr


def _s5prep(lam_re, lam_im, log_dt, bt_re, bt_im):
    g, n = lam_re.shape
    full2 = pl.BlockSpec((g, n), lambda: (0, 0))
    full3 = pl.BlockSpec((g, S5_GROUP, n), lambda: (0, 0, 0))
    return pl.pallas_call(
        _s5prep_kernel,
        in_specs=[full2, full2, pl.BlockSpec((g, 1), lambda: (0, 0)), full3, full3],
        out_specs=[full2, full2, full3, full3],
        out_shape=[jax.ShapeDtypeStruct((g, n), F32)] * 2
        + [jax.ShapeDtypeStruct((g, S5_GROUP, n), F32)] * 2,
        name="s5prep",
    )(lam_re, lam_im, log_dt, bt_re, bt_im)


def _inproj_kernel(x_ref, nw_ref, w_ref, wdt_ref, proj_ref, dt_ref, xn_ref):
    @pl.when(pl.program_id(1) == 0)
    def _():
        x = x_ref[...]
        ms = jnp.mean(x * x, axis=-1, keepdims=True)
        xn = ((x * lax.rsqrt(ms + NORM_EPS)) * nw_ref[...]).astype(BF16)
        xn_ref[...] = xn
        dt_ref[...] = jnp.dot(xn, wdt_ref[...], preferred_element_type=F32)

    proj_ref[...] = jnp.dot(xn_ref[...], w_ref[...], preferred_element_type=F32)


def _inproj(x, norm_w, w_main, w_dt, tm):
    m, d = x.shape
    n = w_main.shape[1]
    return pl.pallas_call(
        _inproj_kernel,
        grid=(m // tm, n // PROJ_TN),
        in_specs=[
            pl.BlockSpec((tm, d), lambda i, j: (i, 0)),
            pl.BlockSpec((1, d), lambda i, j: (0, 0)),
            pl.BlockSpec((d, PROJ_TN), lambda i, j: (0, j)),
            pl.BlockSpec((d, LANES), lambda i, j: (0, 0)),
        ],
        out_specs=[
            pl.BlockSpec((tm, PROJ_TN), lambda i, j: (i, j)),
            pl.BlockSpec((tm, LANES), lambda i, j: (i, 0)),
        ],
        out_shape=[jax.ShapeDtypeStruct((m, n), F32), jax.ShapeDtypeStruct((m, LANES), F32)],
        scratch_shapes=[pltpu.VMEM((tm, d), BF16)],
        compiler_params=_params("arbitrary", "arbitrary"),
        name="inproj",
    )(x, norm_w, w_main, w_dt)


def _ssd_kernel(xbc_ref, za_ref, dtr_ref, ht0_ref, tail0_ref, convw_ref, convb_ref, dtb_ref, alog_ref,
                dexp_ref, nw_ref, y_ref, h_ref, ht_ref, tail_ref, ht_scr, ext_scr, *, mask_rows):
    c = pl.program_id(1)
    L = CHUNK
    P2 = 2 * SSD_HEAD_DIM
    GW = SSD_WIDTH // SSD_GROUPS

    @pl.when(c == 0)
    def _():
        ht_scr[...] = ht0_ref[...]
        ext_scr[0:SUBLANES, :] = tail0_ref[...]

    x = xbc_ref[...]
    ext_scr[SUBLANES:SUBLANES + L, :] = x
    w = convw_ref[...]
    conv = convb_ref[...] + ext_scr[SUBLANES - 3:SUBLANES - 3 + L, :] * w[0:1]
    conv = conv + ext_scr[SUBLANES - 2:SUBLANES - 2 + L, :] * w[1:2]
    conv = conv + ext_scr[SUBLANES - 1:SUBLANES - 1 + L, :] * w[2:3]
    conv = conv + x * w[3:4]
    ext_scr[0:SUBLANES, :] = x[L - SUBLANES:L, :]
    xbc = _silu(conv)
    xs = xbc[:, :SSD_WIDTH]
    bmat = xbc[:, SSD_WIDTH:SSD_WIDTH + SSD_GROUPS * SSD_STATE]
    cmat = xbc[:, SSD_WIDTH + SSD_GROUPS * SSD_STATE:]

    rows = lax.broadcasted_iota(jnp.int32, (L, L), 0)
    cols = lax.broadcasted_iota(jnp.int32, (L, L), 1)
    causal = rows >= cols
    lane_lo = cols < SSD_HEAD_DIM

    dt = _softplus(dtr_ref[...] + dtb_ref[...])
    if mask_rows:
        dt = jnp.where(rows < mask_rows, 0.0, dt)
    a = dt * (-jnp.exp(alog_ref[...]))
    tril = jnp.where(causal, 1.0, 0.0).astype(BF16)
    a1, a2, a3 = _split3(a)
    acum = (jnp.dot(tril, a1, preferred_element_type=F32)
            + jnp.dot(tril, a2, preferred_element_type=F32)
            + jnp.dot(tril, a3, preferred_element_type=F32))
    acum_t = acum.T

    dexp = dexp_ref[...]
    y_parts = []
    for g in range(SSD_GROUPS):
        bg = bmat[:, g * SSD_STATE:(g + 1) * SSD_STATE]
        cg = cmat[:, g * SSD_STATE:(g + 1) * SSD_STATE]
        cb = lax.dot_general(cg.astype(BF16), bg.astype(BF16), (((1,), (1,)), ((), ())),
                             preferred_element_type=F32)
        xw_parts = []
        elast_parts = []
        for jj in range(SSD_HEADS // SSD_GROUPS // 2):
            j = g * (SSD_HEADS // SSD_GROUPS // 2) + jj
            lo = j * P2
            ms_, es_, dends = [], [], []
            dtb = []
            for h in (2 * j, 2 * j + 1):
                colb = jnp.broadcast_to(acum[:, h:h + 1], (L, L))
                seg = colb - acum_t[h:h + 1, :]
                ms_.append(cb * jnp.exp(jnp.where(causal, seg, -jnp.inf)))
                e = jnp.exp(colb)
                es_.append(e)
                dends.append(jnp.exp(colb[L - 1:L, :] - colb))
                dtb.append(jnp.broadcast_to(dt[:, h:h + 1], (L, L)))
            xs_pair = xs[:, lo:lo + P2]
            xd_pair = xs_pair * jnp.where(lane_lo, dtb[0], dtb[1])
            ht_pair = ht_scr[:, lo:lo + P2]
            rhs = jnp.concatenate([
                jnp.where(lane_lo, xd_pair, 0.0), jnp.where(lane_lo, 0.0, xd_pair),
                jnp.where(lane_lo, ht_pair, 0.0), jnp.where(lane_lo, 0.0, ht_pair)], axis=0)
            lhs = jnp.concatenate([ms_[0], ms_[1], cg * es_[0], cg * es_[1]], axis=1)
            y_pair = jnp.dot(lhs.astype(BF16), rhs.astype(BF16), preferred_element_type=F32)
            y_parts.append(y_pair + dexp[:, lo:lo + P2] * xs_pair)
            xw_parts.append(xd_pair * jnp.where(lane_lo, dends[0], dends[1]))
            elast_parts.append(jnp.where(lane_lo[0:1, :], es_[0][L - 1:L, :], es_[1][L - 1:L, :]))
        xw = jnp.concatenate(xw_parts, axis=1)
        elast = jnp.concatenate(elast_parts, axis=1)
        st = lax.dot_general(bg.astype(BF16), xw.astype(BF16), (((0,), (0,)), ((), ())),
                             preferred_element_type=F32)
        ht_scr[:, g * GW:(g + 1) * GW] = ht_scr[:, g * GW:(g + 1) * GW] * elast + st

    y = jnp.concatenate(y_parts, axis=1)
    y = y * _silu(za_ref[...])
    nw = nw_ref[...]
    outs = []
    for g in range(SSD_GROUPS):
        yg = y[:, g * GW:(g + 1) * GW]
        ms = jnp.mean(yg * yg, axis=-1, keepdims=True)
        outs.append((yg * lax.rsqrt(ms + NORM_EPS)) * nw[:, g * GW:(g + 1) * GW])
    y_ref[...] = jnp.concatenate(outs, axis=1).astype(y_ref.dtype)

    @pl.when(c == pl.num_programs(1) - 1)
    def _():
        ht = ht_scr[...]
        ht_ref[...] = ht
        h_ref[...] = ht.T
        tail_ref[...] = x[L - SUBLANES:L, :]


def _ssd(proj, dt_raw, ht0, tail0, conv_w, conv_b, dt_bias, a_log, d_exp, norm_w, *, batch, nchunks,
         row_block0, mask_rows):
    L = CHUNK
    rows = batch * nchunks * L
    rb = lambda b, c: row_block0 + b * nchunks + c
    const2 = lambda shape: pl.BlockSpec(shape, lambda b, c: (0, 0))
    return pl.pallas_call(
        functools.partial(_ssd_kernel, mask_rows=mask_rows),
        grid=(batch, nchunks),
        in_specs=[
            pl.BlockSpec((L, SSD_XBC), lambda b, c: (rb(b, c), COL_XBC // SSD_XBC)),
            pl.BlockSpec((L, SSD_WIDTH), lambda b, c: (rb(b, c), COL_ZA // SSD_WIDTH)),
            pl.BlockSpec((L, LANES), lambda b, c: (rb(b, c), 0)),
            const2((SSD_STATE, SSD_WIDTH)),
            const2((SUBLANES, SSD_XBC)),
            const2((SSD_CONV, SSD_XBC)),
            const2((1, SSD_XBC)),
            const2((1, LANES)),
            const2((1, LANES)),
            const2((1, SSD_WIDTH)),
            const2((1, SSD_WIDTH)),
        ],
        out_specs=[
            pl.BlockSpec((L, SSD_WIDTH), lambda b, c: (b * nchunks + c, 0)),
            pl.BlockSpec((None, SSD_WIDTH, SSD_STATE), lambda b, c: (b, 0, 0)),
            pl.BlockSpec((None, SSD_STATE, SSD_WIDTH), lambda b, c: (b, 0, 0)),
            pl.BlockSpec((None, SUBLANES, SSD_XBC), lambda b, c: (b, 0, 0)),
        ],
        out_shape=[
            jax.ShapeDtypeStruct((rows, SSD_WIDTH), BF16),
            jax.ShapeDtypeStruct((batch, SSD_WIDTH, SSD_STATE), F32),
            jax.ShapeDtypeStruct((batch, SSD_STATE, SSD_WIDTH), F32),
            jax.ShapeDtypeStruct((batch, SUBLANES, SSD_XBC), F32),
        ],
        scratch_shapes=[pltpu.VMEM((SSD_STATE, SSD_WIDTH), F32),
                        pltpu.VMEM((SUBLANES + L, SSD_XBC), F32)],
        compiler_params=_params("arbitrary", "arbitrary"),
        name="ssd",
    )(proj, proj, dt_raw, ht0, tail0, conv_w, conv_b, dt_bias, a_log, d_exp, norm_w)


def _ssd_step_kernel(xbc_ref, za_ref, dtr_ref, cs_ref, h_ref, convw_ref, convb_ref, dtb_ref, alog_ref,
                     dexp_ref, nw_ref, y_ref, hout_ref, csout_ref):
    R = SUBLANES
    GW = SSD_WIDTH // SSD_GROUPS
    HPG = SSD_HEADS // SSD_GROUPS
    x = xbc_ref[...]
    w = convw_ref[...]
    s0 = cs_ref[:, 0:SSD_XBC]
    s1 = cs_ref[:, SSD_XBC:2 * SSD_XBC]
    s2 = cs_ref[:, 2 * SSD_XBC:3 * SSD_XBC]
    conv = convb_ref[...] + s0 * w[0:1]
    conv = conv + s1 * w[1:2]
    conv = conv + s2 * w[2:3]
    conv = conv + x * w[3:4]
    csout_ref[:, 0:SSD_XBC] = s1
    csout_ref[:, SSD_XBC:2 * SSD_XBC] = s2
    csout_ref[:, 2 * SSD_XBC:3 * SSD_XBC] = x
    xbc = _silu(conv)
    xs = xbc[:, :SSD_WIDTH]
    bmat = xbc[:, SSD_WIDTH:SSD_WIDTH + SSD_GROUPS * SSD_STATE]
    cmat = xbc[:, SSD_WIDTH + SSD_GROUPS * SSD_STATE:]

    dt = _softplus(dtr_ref[...] + dtb_ref[...])
    da = jnp.exp(dt * (-jnp.exp(alog_ref[...])))
    dt_t = dt.T
    da_t = da.T
    expand = lambda v: jnp.concatenate(
        [jnp.broadcast_to(v[h:h + 1, :], (SSD_HEAD_DIM, R)) for h in range(SSD_HEADS)], axis=0)
    xd_t = xs.T * expand(dt_t)
    da_te = expand(da_t)

    ycols = []
    for i in range(R):
        bexp = jnp.concatenate(
            [jnp.broadcast_to(bmat[i:i + 1, g * SSD_STATE:(g + 1) * SSD_STATE], (GW, SSD_STATE))
             for g in range(SSD_GROUPS)], axis=0)
        cexp = jnp.concatenate(
            [jnp.broadcast_to(cmat[i:i + 1, g * SSD_STATE:(g + 1) * SSD_STATE], (GW, SSD_STATE))
             for g in range(SSD_GROUPS)], axis=0)
        hn = h_ref[i] * da_te[:, i:i + 1] + xd_t[:, i:i + 1] * bexp
        hout_ref[i] = hn
        ycols.append(jnp.sum(hn * cexp, axis=1, keepdims=True))
    y = jnp.concatenate(ycols, axis=1).T
    y = y + dexp_ref[...] * xs
    y = y * _silu(za_ref[...])
    nw = nw_ref[...]
    outs = []
    for g in range(SSD_GROUPS):
        yg = y[:, g * GW:(g + 1) * GW]
        ms = jnp.mean(yg * yg, axis=-1, keepdims=True)
        outs.append((yg * lax.rsqrt(ms + NORM_EPS)) * nw[:, g * GW:(g + 1) * GW])
    y_ref[...] = jnp.concatenate(outs, axis=1).astype(y_ref.dtype)
    del HPG


def _ssd_step(proj, dt_raw, conv_state, ssd_state, conv_w, conv_b, dt_bias, a_log, d_exp, norm_w, *, nseq):
    R = SUBLANES
    const2 = lambda shape: pl.BlockSpec(shape, lambda i: (0, 0))
    return pl.pallas_call(
        _ssd_step_kernel,
        grid=(nseq // R,),
        in_specs=[
            pl.BlockSpec((R, SSD_XBC), lambda i: (i, COL_XBC // SSD_XBC)),
            pl.BlockSpec((R, SSD_WIDTH), lambda i: (i, COL_ZA // SSD_WIDTH)),
            pl.BlockSpec((R, LANES), lambda i: (i, 0)),
            pl.BlockSpec((R, (SSD_CONV - 1) * SSD_XBC), lambda i: (i, 0)),
            pl.BlockSpec((R, SSD_WIDTH, SSD_STATE), lambda i: (i, 0, 0)),
            const2((SSD_CONV, SSD_XBC)),
            const2((1, SSD_XBC)),
            const2((1, LANES)),
            const2((1, LANES)),
            const2((1, SSD_WIDTH)),
            const2((1, SSD_WIDTH)),
        ],
        out_specs=[
            pl.BlockSpec((R, SSD_WIDTH), lambda i: (i, 0)),
            pl.BlockSpec((R, SSD_WIDTH, SSD_STATE), lambda i: (i, 0, 0)),
            pl.BlockSpec((R, (SSD_CONV - 1) * SSD_XBC), lambda i: (i, 0)),
        ],
        out_shape=[
            jax.ShapeDtypeStruct((nseq, SSD_WIDTH), BF16),
            jax.ShapeDtypeStruct((nseq, SSD_WIDTH, SSD_STATE), F32),
            jax.ShapeDtypeStruct((nseq, (SSD_CONV - 1) * SSD_XBC), F32),
        ],
        compiler_params=_params("arbitrary"),
        name="ssd_step",
    )(proj, proj, dt_raw, conv_state, ssd_state, conv_w, conv_b, dt_bias, a_log, d_exp, norm_w)


def _s5_kernel(u_ref, h0r_ref, h0i_ref, btr_ref, bti_ref, ctr_ref, cti_ref, ar_ref, ai_ref, d_ref,
               y_ref, hr_ref, hi_ref, bur, bui, *, seq):
    u = u_ref[...]
    ub = u.astype(BF16)
    bur[...] = jnp.dot(ub, btr_ref[...], preferred_element_type=F32)
    bui[...] = jnp.dot(ub, bti_ref[...], preferred_element_type=F32)
    ar = ar_ref[...]
    ai = ai_ref[...]

    def body(t, carry):
        hr, hi = carry
        nr = ar * hr - ai * hi + bur[pl.ds(t, 1), :]
        ni = ar * hi + ai * hr + bui[pl.ds(t, 1), :]
        bur[pl.ds(t, 1), :] = nr
        bui[pl.ds(t, 1), :] = ni
        return nr, ni

    hr, hi = lax.fori_loop(0, seq, body, (h0r_ref[...], h0i_ref[...]), unroll=8)
    hr_ref[...] = hr
    hi_ref[...] = hi
    y = (jnp.dot(bur[...].astype(BF16), ctr_ref[...], preferred_element_type=F32)
         - jnp.dot(bui[...].astype(BF16), cti_ref[...], preferred_element_type=F32))
    y_ref[...] = _gelu_tanh(y + d_ref[...] * u)


def _s5(proj, h0r, h0i, bt_re, bt_im, ct_re, ct_im, ab_re, ab_im, d_s5, *, batch, seq, row_block0):
    ub0 = COL_UB // S5_BLK_U
    vec = lambda width: pl.BlockSpec((1, width), lambda b, j: (0, j))
    wspec = lambda r, c: pl.BlockSpec((None, r, c), lambda b, j: (j, 0, 0))
    st_out = pl.BlockSpec((None, 1, S5_BLK_N), lambda b, j: (b, 0, j))
    return pl.pallas_call(
        functools.partial(_s5_kernel, seq=seq),
        grid=(batch, S5_BLOCKS),
        in_specs=[
            pl.BlockSpec((seq, S5_BLK_U), lambda b, j: (row_block0 + b, ub0 + j)),
            vec(S5_BLK_N), vec(S5_BLK_N),
            wspec(S5_BLK_U, S5_BLK_N), wspec(S5_BLK_U, S5_BLK_N),
            wspec(S5_BLK_N, S5_BLK_U), wspec(S5_BLK_N, S5_BLK_U),
            vec(S5_BLK_N), vec(S5_BLK_N), vec(S5_BLK_U),
        ],
        out_specs=[pl.BlockSpec((seq, S5_BLK_U), lambda b, j: (b, j)), st_out, st_out],
        out_shape=[
            jax.ShapeDtypeStruct((batch * seq, S5_WIDTH), F32),
            jax.ShapeDtypeStruct((batch, 1, S5_NSTATE), F32),
            jax.ShapeDtypeStruct((batch, 1, S5_NSTATE), F32),
        ],
        scratch_shapes=[pltpu.VMEM((seq, S5_BLK_N), F32), pltpu.VMEM((seq, S5_BLK_N), F32)],
        compiler_params=_params("arbitrary", "arbitrary"),
        name="s5",
    )(proj, h0r, h0i, bt_re, bt_im, ct_re, ct_im, ab_re, ab_im, d_s5)


def _s5_step_kernel(u_ref, h0r_ref, h0i_ref, btr_ref, bti_ref, ctr_ref, cti_ref, ar_ref, ai_ref, d_ref,
                    y_ref, hr_ref, hi_ref):
    u = u_ref[...]
    ub = u.astype(BF16)
    ar = ar_ref[...]
    ai = ai_ref[...]
    h0r = h0r_ref[...]
    h0i = h0i_ref[...]
    hr = jnp.dot(ub, btr_ref[...], preferred_element_type=F32) + (ar * h0r - ai * h0i)
    hi = jnp.dot(ub, bti_ref[...], preferred_element_type=F32) + (ar * h0i + ai * h0r)
    hr_ref[...] = hr
    hi_ref[...] = hi
    y = (jnp.dot(hr.astype(BF16), ctr_ref[...], preferred_element_type=F32)
         - jnp.dot(hi.astype(BF16), cti_ref[...], preferred_element_type=F32))
    y_ref[...] = _gelu_tanh(y + d_ref[...] * u)


def _s5_step(proj, h0r, h0i, bt_re, bt_im, ct_re, ct_im, ab_re, ab_im, d_s5, *, nseq):
    ub0 = COL_UB // S5_BLK_U
    vec = lambda width: pl.BlockSpec((1, width), lambda j: (0, j))
    wspec = lambda r, c: pl.BlockSpec((None, r, c), lambda j: (j, 0, 0))
    st = pl.BlockSpec((nseq, S5_BLK_N), lambda j: (0, j))
    return pl.pallas_call(
        _s5_step_kernel,
        grid=(S5_BLOCKS,),
        in_specs=[
            pl.BlockSpec((nseq, S5_BLK_U), lambda j: (0, ub0 + j)),
            st, st,
            wspec(S5_BLK_U, S5_BLK_N), wspec(S5_BLK_U, S5_BLK_N),
            wspec(S5_BLK_N, S5_BLK_U), wspec(S5_BLK_N, S5_BLK_U),
            vec(S5_BLK_N), vec(S5_BLK_N), vec(S5_BLK_U),
        ],
        out_specs=[pl.BlockSpec((nseq, S5_BLK_U), lambda j: (0, j)), st, st],
        out_shape=[
            jax.ShapeDtypeStruct((nseq, S5_WIDTH), F32),
            jax.ShapeDtypeStruct((nseq, S5_NSTATE), F32),
            jax.ShapeDtypeStruct((nseq, S5_NSTATE), F32),
        ],
        compiler_params=_params("arbitrary"),
        name="s5_step",
    )(proj, h0r, h0i, bt_re, bt_im, ct_re, ct_im, ab_re, ab_im, d_s5)


def _tail_kernel(yn_ref, ybg_ref, zb_ref, ga_ref, gb_ref, x_ref, wpa_ref, wglu_ref, bglu_ref, wpb_ref,
                 wout_ref, fnw_ref, out_ref):
    ya = jnp.dot(yn_ref[...], wpa_ref[...], preferred_element_type=F32)
    yb = ybg_ref[...]
    glu = jnp.dot(yb.astype(BF16), wglu_ref[...], preferred_element_type=F32) + bglu_ref[...]
    yb = (yb * _sigmoid(glu)) * _silu(zb_ref[...])
    ybp = jnp.dot(yb.astype(BF16), wpb_ref[...], preferred_element_type=F32)
    mixed = _sigmoid(ga_ref[...]) * ya + _sigmoid(gb_ref[...]) * ybp
    o = x_ref[...] + jnp.dot(mixed.astype(BF16), wout_ref[...], preferred_element_type=F32)
    ms = jnp.mean(o * o, axis=-1, keepdims=True)
    out_ref[...] = (o * lax.rsqrt(ms + NORM_EPS)) * fnw_ref[...]


def _tail(yn, ybg, proj, x, w_proj_a, w_glu, b_glu, w_proj_b, w_out, final_norm_w, *, tm):
    m, d = x.shape
    resident = lambda shape: pl.BlockSpec(shape, lambda i: (0, 0), pipeline_mode=pl.Buffered(1))
    return pl.pallas_call(
        _tail_kernel,
        grid=(m // tm,),
        in_specs=[
            pl.BlockSpec((tm, SSD_WIDTH), lambda i: (i, 0)),
            pl.BlockSpec((tm, S5_WIDTH), lambda i: (i, 0)),
            pl.BlockSpec((tm, S5_WIDTH), lambda i: (i, COL_ZB // S5_WIDTH)),
            pl.BlockSpec((tm, d), lambda i: (i, COL_GA // D_MODEL)),
            pl.BlockSpec((tm, d), lambda i: (i, COL_GB // D_MODEL)),
            pl.BlockSpec((tm, d), lambda i: (i, 0)),
            resident((SSD_WIDTH, d)),
            resident((S5_WIDTH, S5_WIDTH)),
            resident((1, S5_WIDTH)),
            resident((S5_WIDTH, d)),
            resident((d, d)),
            resident((1, d)),
        ],
        out_specs=pl.BlockSpec((tm, d), lambda i: (i, 0)),
        out_shape=jax.ShapeDtypeStruct((m, d), F32),
        compiler_params=_params("arbitrary"),
        name="tail",
    )(yn, ybg, proj, proj, proj, x, w_proj_a, w_glu, b_glu, w_proj_b, w_out, final_norm_w)


def _block_diag16(w):
    g, r, c = w.shape
    nb = g // 16
    eye = jnp.eye(16, dtype=w.dtype)
    out = jnp.einsum("jgrc,gh->jgrhc", w.reshape(nb, 16, r, c), eye)
    return out.reshape(nb, 16 * r, 16 * c)


def kernel(x_prompt, x_sample, state_ssd, state_conv, state_s5_re, state_s5_im, meta_tokens, norm_w, w_in,
           conv_w, conv_b, dt_bias, a_log, d_ssd, ssd_norm_w, w_proj_a, lam_re, lam_im, log_dt_s5, b_re, b_im,
           c_re, c_im, d_s5, w_glu, b_glu, w_proj_b, w_out, final_norm_w):
    bsz, seq, d = x_prompt.shape
    nseq = x_sample.shape[0]
    assert d == D_MODEL and seq % CHUNK == 0 and nseq % SUBLANES == 0 and norm_w.shape[0] == 1
    assert meta_tokens.shape[0] == N_META and N_META <= CHUNK

    wi = w_in[0]
    o_xbc = SSD_WIDTH
    o_dt = o_xbc + SSD_XBC
    o_ub = o_dt + SSD_HEADS
    o_zb = o_ub + S5_WIDTH
    o_ga = o_zb + S5_WIDTH
    o_gb = o_ga + D_MODEL
    w_main = jnp.concatenate(
        [wi[:, :o_xbc], wi[:, o_ga:o_gb], wi[:, o_gb:], wi[:, o_xbc:o_dt], wi[:, o_ub:o_zb], wi[:, o_zb:o_ga]],
        axis=1).astype(BF16)
    w_dt = jnp.pad(wi[:, o_dt:o_ub], ((0, 0), (0, LANES - SSD_HEADS))).astype(BF16)
    pad_heads = lambda v: jnp.pad(v.reshape(1, SSD_HEADS), ((0, 0), (0, LANES - SSD_HEADS)))
    dtb = pad_heads(dt_bias[0])
    alog = pad_heads(a_log[0])
    d_exp = jnp.repeat(d_ssd[0], SSD_HEAD_DIM).reshape(1, SSD_WIDTH)
    nw1 = norm_w[0].reshape(1, d)
    ssd_nw = ssd_norm_w[0].reshape(1, SSD_WIDTH)
    convw = conv_w[0]
    convb = conv_b[0].reshape(1, SSD_XBC)
    wpa = w_proj_a[0].astype(BF16)
    wglu = w_glu[0].astype(BF16)
    bglu = b_glu[0].reshape(1, S5_WIDTH)
    wpb = w_proj_b[0].astype(BF16)
    wout = w_out[0].astype(BF16)
    fnw = final_norm_w.reshape(1, d)
    ds5 = d_s5[0].reshape(1, S5_WIDTH)

    ab_re, ab_im, bbt_re, bbt_im = _s5prep(
        lam_re[0], lam_im[0], log_dt_s5[0].reshape(S5_GROUPS, 1),
        jnp.transpose(b_re[0], (0, 2, 1)), jnp.transpose(b_im[0], (0, 2, 1)))
    ab_re = ab_re.reshape(1, S5_NSTATE)
    ab_im = ab_im.reshape(1, S5_NSTATE)
    bt_re = _block_diag16(bbt_re).astype(BF16)
    bt_im = _block_diag16(bbt_im).astype(BF16)
    ct_re = _block_diag16(jnp.transpose(c_re[0], (0, 2, 1))).astype(BF16)
    ct_im = _block_diag16(jnp.transpose(c_im[0], (0, 2, 1))).astype(BF16)

    x_main = x_prompt.reshape(bsz * seq, d)
    x_small = jnp.concatenate(
        [x_sample.reshape(nseq, d), jnp.zeros((CHUNK - N_META, d), x_prompt.dtype),
         meta_tokens.astype(x_prompt.dtype)], axis=0)
    assert nseq % CHUNK == 0
    meta_blk = nseq // CHUNK
    tm_main = 1024 if (bsz * seq) % 1024 == 0 else CHUNK
    proj_m, dt_m = _inproj(x_main, nw1, w_main, w_dt, tm=tm_main)
    proj_s, dt_s = _inproj(x_small, nw1, w_main, w_dt, tm=nseq + CHUNK)

    ssd_args = (convw, convb, dtb, alog, d_exp, ssd_nw)
    s5_args = (bt_re, bt_im, ct_re, ct_im, ab_re, ab_im, ds5)

    zeros_ht = jnp.zeros((SSD_STATE, SSD_WIDTH), F32)
    zeros_tail = jnp.zeros((SUBLANES, SSD_XBC), F32)
    zeros_s5 = jnp.zeros((1, S5_NSTATE), F32)
    _, _, ht_meta, tail_meta = _ssd(proj_s, dt_s, zeros_ht, zeros_tail, *ssd_args, batch=1, nchunks=1,
                                    row_block0=meta_blk, mask_rows=CHUNK - N_META)
    _, s5r_meta, s5i_meta = _s5(proj_s, zeros_s5, zeros_s5, *s5_args, batch=1, seq=CHUNK, row_block0=meta_blk)

    yn_m, h_m, _, tail_m = _ssd(proj_m, dt_m, ht_meta[0], tail_meta[0], *ssd_args, batch=bsz,
                                nchunks=seq // CHUNK, row_block0=0, mask_rows=0)
    ybg_m, s5r_m, s5i_m = _s5(proj_m, s5r_meta[0], s5i_meta[0], *s5_args, batch=bsz, seq=seq, row_block0=0)
    tail_w = (wpa, wglu, bglu, wpb, wout, fnw)
    y_prompt = _tail(yn_m, ybg_m, proj_m, x_main, *tail_w, tm=256)

    yn_s, h_s, cs_s = _ssd_step(proj_s, dt_s, state_conv[0].reshape(nseq, (SSD_CONV - 1) * SSD_XBC),
                                state_ssd[0].reshape(nseq, SSD_WIDTH, SSD_STATE), *ssd_args, nseq=nseq)
    ybg_s, s5r_s, s5i_s = _s5_step(proj_s, state_s5_re[0].reshape(nseq, S5_NSTATE),
                                   state_s5_im[0].reshape(nseq, S5_NSTATE), *s5_args, nseq=nseq)
    y_sample = _tail(yn_s, ybg_s, proj_s, x_sample.reshape(nseq, d), *tail_w, tm=nseq)

    dt_out = x_prompt.dtype
    return (
        y_prompt.reshape(bsz, seq, d),
        y_sample.reshape(nseq, 1, d),
        h_m.reshape(1, bsz, SSD_HEADS, SSD_HEAD_DIM, SSD_STATE).astype(dt_out),
        tail_m[:, SUBLANES - (SSD_CONV - 1):, :].reshape(1, bsz, SSD_CONV - 1, SSD_XBC),
        s5r_m.reshape(1, bsz, S5_GROUPS, S5_STATE).astype(dt_out),
        s5i_m.reshape(1, bsz, S5_GROUPS, S5_STATE).astype(dt_out),
        h_s.reshape(1, nseq, SSD_HEADS, SSD_HEAD_DIM, SSD_STATE).astype(dt_out),
        cs_s.reshape(1, nseq, SSD_CONV - 1, SSD_XBC),
        s5r_s.reshape(1, nseq, S5_GROUPS, S5_STATE).astype(dt_out),
        s5i_s.reshape(1, nseq, S5_GROUPS, S5_STATE).astype(dt_out),
    )
```

```python
import functools

import jax
import jax.numpy as jnp
from jax import lax
from jax.experimental import pallas as pl
from jax.experimental.pallas import tpu as pltpu

F32 = jnp.float32
BF16 = jnp.bfloat16

NORM_EPS = 1e-5
N_META = 16
D_MODEL = 2048
SSD_HEAD_DIM = 64
SSD_HEADS = 32
SSD_GROUPS = 4
SSD_STATE = 128
SSD_WIDTH = SSD_HEADS * SSD_HEAD_DIM
SSD_XBC = SSD_WIDTH + 2 * SSD_GROUPS * SSD_STATE
SSD_CONV = 4
CHUNK = 128
S5_WIDTH = D_MODEL // 2
S5_GROUP = 16
S5_GROUPS = S5_WIDTH // S5_GROUP
S5_STATE = 64
S5_NSTATE = S5_GROUPS * S5_STATE
S5_BLOCKS = 4
S5_BLK_U = S5_WIDTH // S5_BLOCKS
S5_BLK_N = S5_NSTATE // S5_BLOCKS

LANES = 128
SUBLANES = 8
VMEM_LIMIT = 56 * 1024 * 1024

COL_ZA = 0
COL_GA = 2048
COL_GB = 4096
COL_XBC = 6144
COL_UB = 9216
COL_ZB = 10240
PROJ_COLS = 11264
PROJ_TN = 1024


def _sigmoid(x):
    return 1.0 / (1.0 + jnp.exp(-x))


def _silu(x):
    return x * _sigmoid(x)


def _softplus(x):
    return jnp.maximum(x, 0.0) + jnp.log1p(jnp.exp(-jnp.abs(x)))


def _gelu_tanh(x):
    c = 0.7978845608028654
    return 0.5 * x * (1.0 + jnp.tanh(c * (x + 0.044715 * (x * x * x))))


def _split3(x):
    x1 = x.astype(BF16)
    r1 = x - x1.astype(F32)
    x2 = r1.astype(BF16)
    x3 = (r1 - x2.astype(F32)).astype(BF16)
    return x1, x2, x3


def _params(*sem):
    return pltpu.CompilerParams(dimension_semantics=sem, vmem_limit_bytes=VMEM_LIMIT)


def _s5prep_kernel(lr_ref, li_ref, ldt_ref, btr_ref, bti_ref, abr_ref, abi_ref, bbr_ref, bbi_ref):
    lr = lr_ref[...]
    li = li_ref[...]
    step = jnp.exp(ldt_ref[...])
    mag = jnp.exp(lr * step)
    abr = mag * jnp.cos(li * step)
    abi = mag * jnp.sin(li * step)
    den = lr * lr + li * li
    numr = abr - 1.0
    cr = (numr * lr + abi * li) / den
    ci = (abi * lr - numr * li) / den
    abr_ref[...] = abr
    abi_ref[...] = abi
    btr = btr_ref[...]
    bti = bti_ref[...]
    crb = cr[:, None, :]
    cib = ci[:, None, :]
    bbr_ref[...] = crb * btr - cib * bti
    bbi_ref[...] = crb * bti + cib * btr


def _s5prep(lam_re, lam_im, log_dt, bt_re, bt_im):
    g, n = lam_re.shape
    full2 = pl.BlockSpec((g, n), lambda: (0, 0))
    full3 = pl.BlockSpec((g, S5_GROUP, n), lambda: (0, 0, 0))
    return pl.pallas_call(
        _s5prep_kernel,
        in_specs=[full2, full2, pl.BlockSpec((g, 1), lambda: (0, 0)), full3, full3],
        out_specs=[full2, full2, full3, full3],
        out_shape=[jax.ShapeDtypeStruct((g, n), F32)] * 2
        + [jax.ShapeDtypeStruct((g, S5_GROUP, n), F32)] * 2,
        name="s5prep",
    )(lam_re, lam_im, log_dt, bt_re, bt_im)


def _s5pow_kernel(ar_ref, ai_ref, pr_ref, pi_ref):
    pr_ref[0:1, :] = ar_ref[...]
    pi_ref[0:1, :] = ai_ref[...]
    m = 1
    while m < pr_ref.shape[0]:
        br = pr_ref[m - 1:m, :]
        bi = pi_ref[m - 1:m, :]
        xr = pr_ref[0:m, :]
        xi = pi_ref[0:m, :]
        pr_ref[m:2 * m, :] = xr * br - xi * bi
        pi_ref[m:2 * m, :] = xr * bi + xi * br
        m *= 2


def _s5pow(ab_re, ab_im, npow):
    n = ab_re.shape[1]
    vec = pl.BlockSpec((1, n), lambda: (0, 0))
    tab = pl.BlockSpec((npow, n), lambda: (0, 0))
    return pl.pallas_call(
        _s5pow_kernel,
        in_specs=[vec, vec],
        out_specs=[tab, tab],
        out_shape=[jax.ShapeDtypeStruct((npow, n), F32)] * 2,
        name="s5pow",
    )(ab_re, ab_im)


def _inproj_kernel(x_ref, nw_ref, w_ref, wdt_ref, proj_ref, dt_ref, xn_ref):
    @pl.when(pl.program_id(1) == 0)
    def _():
        x = x_ref[...]
        ms = jnp.mean(x * x, axis=-1, keepdims=True)
        xn = ((x * lax.rsqrt(ms + NORM_EPS)) * nw_ref[...]).astype(BF16)
        xn_ref[...] = xn
        dt_ref[...] = jnp.dot(xn, wdt_ref[...], preferred_element_type=F32)

    proj_ref[...] = jnp.dot(xn_ref[...], w_ref[...], preferred_element_type=F32)


def _inproj(x, norm_w, w_main, w_dt, tm):
    m, d = x.shape
    n = w_main.shape[1]
    return pl.pallas_call(
        _inproj_kernel,
        grid=(m // tm, n // PROJ_TN),
        in_specs=[
            pl.BlockSpec((tm, d), lambda i, j: (i, 0)),
            pl.BlockSpec((1, d), lambda i, j: (0, 0)),
            pl.BlockSpec((d, PROJ_TN), lambda i, j: (0, j)),
            pl.BlockSpec((d, LANES), lambda i, j: (0, 0)),
        ],
        out_specs=[
            pl.BlockSpec((tm, PROJ_TN), lambda i, j: (i, j)),
            pl.BlockSpec((tm, LANES), lambda i, j: (i, 0)),
        ],
        out_shape=[jax.ShapeDtypeStruct((m, n), F32), jax.ShapeDtypeStruct((m, LANES), F32)],
        scratch_shapes=[pltpu.VMEM((tm, d), BF16)],
        compiler_params=_params("arbitrary", "arbitrary"),
        name="inproj",
    )(x, norm_w, w_main, w_dt)


def _ssd_kernel(xbc_ref, za_ref, dtr_ref, ht0_ref, tail0_ref, convw_ref, convb_ref, dtb_ref, alog_ref,
                dexp_ref, nw_ref, y_ref, h_ref, ht_ref, tail_ref, ht_scr, ext_scr, *, mask_rows):
    c = pl.program_id(1)
    L = CHUNK
    P2 = 2 * SSD_HEAD_DIM
    GW = SSD_WIDTH // SSD_GROUPS

    @pl.when(c == 0)
    def _():
        ht_scr[...] = ht0_ref[...]
        ext_scr[0:SUBLANES, :] = tail0_ref[...]

    x = xbc_ref[...]
    ext_scr[SUBLANES:SUBLANES + L, :] = x
    w = convw_ref[...]
    conv = convb_ref[...] + ext_scr[SUBLANES - 3:SUBLANES - 3 + L, :] * w[0:1]
    conv = conv + ext_scr[SUBLANES - 2:SUBLANES - 2 + L, :] * w[1:2]
    conv = conv + ext_scr[SUBLANES - 1:SUBLANES - 1 + L, :] * w[2:3]
    conv = conv + x * w[3:4]
    ext_scr[0:SUBLANES, :] = x[L - SUBLANES:L, :]
    xbc = _silu(conv)
    xs = xbc[:, :SSD_WIDTH]
    bmat = xbc[:, SSD_WIDTH:SSD_WIDTH + SSD_GROUPS * SSD_STATE]
    cmat = xbc[:, SSD_WIDTH + SSD_GROUPS * SSD_STATE:]

    rows = lax.broadcasted_iota(jnp.int32, (L, L), 0)
    cols = lax.broadcasted_iota(jnp.int32, (L, L), 1)
    causal = rows >= cols
    lane_lo = cols < SSD_HEAD_DIM

    dt = _softplus(dtr_ref[...] + dtb_ref[...])
    if mask_rows:
        dt = jnp.where(rows < mask_rows, 0.0, dt)
    a = dt * (-jnp.exp(alog_ref[...]))
    tril = jnp.where(causal, 1.0, 0.0).astype(BF16)
    a1, a2, a3 = _split3(a)
    acum = (jnp.dot(tril, a1, preferred_element_type=F32)
            + jnp.dot(tril, a2, preferred_element_type=F32)
            + jnp.dot(tril, a3, preferred_element_type=F32))
    acum_t = acum.T

    dexp = dexp_ref[...]
    y_parts = []
    for g in range(SSD_GROUPS):
        bg = bmat[:, g * SSD_STATE:(g + 1) * SSD_STATE]
        cg = cmat[:, g * SSD_STATE:(g + 1) * SSD_STATE]
        cb = lax.dot_general(cg.astype(BF16), bg.astype(BF16), (((1,), (1,)), ((), ())),
                             preferred_element_type=F32)
        xw_parts = []
        elast_parts = []
        for jj in range(SSD_HEADS // SSD_GROUPS // 2):
            j = g * (SSD_HEADS // SSD_GROUPS // 2) + jj
            lo = j * P2
            ms_, es_, dends = [], [], []
            dtb = []
            for h in (2 * j, 2 * j + 1):
                colb = jnp.broadcast_to(acum[:, h:h + 1], (L, L))
                seg = colb - acum_t[h:h + 1, :]
                ms_.append(cb * jnp.exp(jnp.where(causal, seg, -jnp.inf)))
                e = jnp.exp(colb)
                es_.append(e)
                dends.append(jnp.exp(colb[L - 1:L, :] - colb))
                dtb.append(jnp.broadcast_to(dt[:, h:h + 1], (L, L)))
            xs_pair = xs[:, lo:lo + P2]
            xd_pair = xs_pair * jnp.where(lane_lo, dtb[0], dtb[1])
            ht_pair = ht_scr[:, lo:lo + P2]
            rhs = jnp.concatenate([
                jnp.where(lane_lo, xd_pair, 0.0), jnp.where(lane_lo, 0.0, xd_pair),
                jnp.where(lane_lo, ht_pair, 0.0), jnp.where(lane_lo, 0.0, ht_pair)], axis=0)
            lhs = jnp.concatenate([ms_[0], ms_[1], cg * es_[0], cg * es_[1]], axis=1)
            y_pair = jnp.dot(lhs.astype(BF16), rhs.astype(BF16), preferred_element_type=F32)
            y_parts.append(y_pair + dexp[:, lo:lo + P2] * xs_pair)
            xw_parts.append(xd_pair * jnp.where(lane_lo, dends[0], dends[1]))
            elast_parts.append(jnp.where(lane_lo[0:1, :], es_[0][L - 1:L, :], es_[1][L - 1:L, :]))
        xw = jnp.concatenate(xw_parts, axis=1)
        elast = jnp.concatenate(elast_parts, axis=1)
        st = lax.dot_general(bg.astype(BF16), xw.astype(BF16), (((0,), (0,)), ((), ())),
                             preferred_element_type=F32)
        ht_scr[:, g * GW:(g + 1) * GW] = ht_scr[:, g * GW:(g + 1) * GW] * elast + st

    y = jnp.concatenate(y_parts, axis=1)
    y = y * _silu(za_ref[...])
    nw = nw_ref[...]
    outs = []
    for g in range(SSD_GROUPS):
        yg = y[:, g * GW:(g + 1) * GW]
        ms = jnp.mean(yg * yg, axis=-1, keepdims=True)
        outs.append((yg * lax.rsqrt(ms + NORM_EPS)) * nw[:, g * GW:(g + 1) * GW])
    y_ref[...] = jnp.concatenate(outs, axis=1).astype(y_ref.dtype)

    @pl.when(c == pl.num_programs(1) - 1)
    def _():
        ht = ht_scr[...]
        ht_ref[...] = ht
        h_ref[...] = ht.T
        tail_ref[...] = x[L - SUBLANES:L, :]


def _ssd(proj, dt_raw, ht0, tail0, conv_w, conv_b, dt_bias, a_log, d_exp, norm_w, *, batch, nchunks,
         row_block0, mask_rows):
    L = CHUNK
    rows = batch * nchunks * L
    rb = lambda b, c: row_block0 + b * nchunks + c
    const2 = lambda shape: pl.BlockSpec(shape, lambda b, c: (0, 0))
    return pl.pallas_call(
        functools.partial(_ssd_kernel, mask_rows=mask_rows),
        grid=(batch, nchunks),
        in_specs=[
            pl.BlockSpec((L, SSD_XBC), lambda b, c: (rb(b, c), COL_XBC // SSD_XBC)),
            pl.BlockSpec((L, SSD_WIDTH), lambda b, c: (rb(b, c), COL_ZA // SSD_WIDTH)),
            pl.BlockSpec((L, LANES), lambda b, c: (rb(b, c), 0)),
            const2((SSD_STATE, SSD_WIDTH)),
            const2((SUBLANES, SSD_XBC)),
            const2((SSD_CONV, SSD_XBC)),
            const2((1, SSD_XBC)),
            const2((1, LANES)),
            const2((1, LANES)),
            const2((1, SSD_WIDTH)),
            const2((1, SSD_WIDTH)),
        ],
        out_specs=[
            pl.BlockSpec((L, SSD_WIDTH), lambda b, c: (b * nchunks + c, 0)),
            pl.BlockSpec((None, SSD_WIDTH, SSD_STATE), lambda b, c: (b, 0, 0)),
            pl.BlockSpec((None, SSD_STATE, SSD_WIDTH), lambda b, c: (b, 0, 0)),
            pl.BlockSpec((None, SUBLANES, SSD_XBC), lambda b, c: (b, 0, 0)),
        ],
        out_shape=[
            jax.ShapeDtypeStruct((rows, SSD_WIDTH), BF16),
            jax.ShapeDtypeStruct((batch, SSD_WIDTH, SSD_STATE), F32),
            jax.ShapeDtypeStruct((batch, SSD_STATE, SSD_WIDTH), F32),
            jax.ShapeDtypeStruct((batch, SUBLANES, SSD_XBC), F32),
        ],
        scratch_shapes=[pltpu.VMEM((SSD_STATE, SSD_WIDTH), F32),
                        pltpu.VMEM((SUBLANES + L, SSD_XBC), F32)],
        compiler_params=_params("arbitrary", "arbitrary"),
        name="ssd",
    )(proj, proj, dt_raw, ht0, tail0, conv_w, conv_b, dt_bias, a_log, d_exp, norm_w)


def _ssd_step_kernel(xbc_ref, za_ref, dtr_ref, cs_ref, h_ref, convw_ref, convb_ref, dtb_ref, alog_ref,
                     dexp_ref, nw_ref, y_ref, hout_ref, csout_ref):
    R = SUBLANES
    GW = SSD_WIDTH // SSD_GROUPS
    HPG = SSD_HEADS // SSD_GROUPS
    x = xbc_ref[...]
    w = convw_ref[...]
    s0 = cs_ref[:, 0:SSD_XBC]
    s1 = cs_ref[:, SSD_XBC:2 * SSD_XBC]
    s2 = cs_ref[:, 2 * SSD_XBC:3 * SSD_XBC]
    conv = convb_ref[...] + s0 * w[0:1]
    conv = conv + s1 * w[1:2]
    conv = conv + s2 * w[2:3]
    conv = conv + x * w[3:4]
    csout_ref[:, 0:SSD_XBC] = s1
    csout_ref[:, SSD_XBC:2 * SSD_XBC] = s2
    csout_ref[:, 2 * SSD_XBC:3 * SSD_XBC] = x
    xbc = _silu(conv)
    xs = xbc[:, :SSD_WIDTH]
    bmat = xbc[:, SSD_WIDTH:SSD_WIDTH + SSD_GROUPS * SSD_STATE]
    cmat = xbc[:, SSD_WIDTH + SSD_GROUPS * SSD_STATE:]

    dt = _softplus(dtr_ref[...] + dtb_ref[...])
    da = jnp.exp(dt * (-jnp.exp(alog_ref[...])))
    dt_t = dt.T
    da_t = da.T
    expand = lambda v: jnp.concatenate(
        [jnp.broadcast_to(v[h:h + 1, :], (SSD_HEAD_DIM, R)) for h in range(SSD_HEADS)], axis=0)
    xd_t = xs.T * expand(dt_t)
    da_te = expand(da_t)

    ycols = []
    for i in range(R):
        bexp = jnp.concatenate(
            [jnp.broadcast_to(bmat[i:i + 1, g * SSD_STATE:(g + 1) * SSD_STATE], (GW, SSD_STATE))
             for g in range(SSD_GROUPS)], axis=0)
        cexp = jnp.concatenate(
            [jnp.broadcast_to(cmat[i:i + 1, g * SSD_STATE:(g + 1) * SSD_STATE], (GW, SSD_STATE))
             for g in range(SSD_GROUPS)], axis=0)
        hn = h_ref[i] * da_te[:, i:i + 1] + xd_t[:, i:i + 1] * bexp
        hout_ref[i] = hn
        ycols.append(jnp.sum(hn * cexp, axis=1, keepdims=True))
    y = jnp.concatenate(ycols, axis=1).T
    y = y + dexp_ref[...] * xs
    y = y * _silu(za_ref[...])
    nw = nw_ref[...]
    outs = []
    for g in range(SSD_GROUPS):
        yg = y[:, g * GW:(g + 1) * GW]
        ms = jnp.mean(yg * yg, axis=-1, keepdims=True)
        outs.append((yg * lax.rsqrt(ms + NORM_EPS)) * nw[:, g * GW:(g + 1) * GW])
    y_ref[...] = jnp.concatenate(outs, axis=1).astype(y_ref.dtype)
    del HPG


def _ssd_step(proj, dt_raw, conv_state, ssd_state, conv_w, conv_b, dt_bias, a_log, d_exp, norm_w, *, nseq):
    R = SUBLANES
    const2 = lambda shape: pl.BlockSpec(shape, lambda i: (0, 0))
    return pl.pallas_call(
        _ssd_step_kernel,
        grid=(nseq // R,),
        in_specs=[
            pl.BlockSpec((R, SSD_XBC), lambda i: (i, COL_XBC // SSD_XBC)),
            pl.BlockSpec((R, SSD_WIDTH), lambda i: (i, COL_ZA // SSD_WIDTH)),
            pl.BlockSpec((R, LANES), lambda i: (i, 0)),
            pl.BlockSpec((R, (SSD_CONV - 1) * SSD_XBC), lambda i: (i, 0)),
            pl.BlockSpec((R, SSD_WIDTH, SSD_STATE), lambda i: (i, 0, 0)),
            const2((SSD_CONV, SSD_XBC)),
            const2((1, SSD_XBC)),
            const2((1, LANES)),
            const2((1, LANES)),
            const2((1, SSD_WIDTH)),
            const2((1, SSD_WIDTH)),
        ],
        out_specs=[
            pl.BlockSpec((R, SSD_WIDTH), lambda i: (i, 0)),
            pl.BlockSpec((R, SSD_WIDTH, SSD_STATE), lambda i: (i, 0, 0)),
            pl.BlockSpec((R, (SSD_CONV - 1) * SSD_XBC), lambda i: (i, 0)),
        ],
        out_shape=[
            jax.ShapeDtypeStruct((nseq, SSD_WIDTH), BF16),
            jax.ShapeDtypeStruct((nseq, SSD_WIDTH, SSD_STATE), F32),
            jax.ShapeDtypeStruct((nseq, (SSD_CONV - 1) * SSD_XBC), F32),
        ],
        compiler_params=_params("arbitrary"),
        name="ssd_step",
    )(proj, proj, dt_raw, conv_state, ssd_state, conv_w, conv_b, dt_bias, a_log, d_exp, norm_w)


def _s5_pitch(seg_len):
    return seg_len if (seg_len // SUBLANES) % 2 else seg_len + SUBLANES


def _s5_kernel(u_ref, h0r_ref, h0i_ref, btr_ref, bti_ref, ctr_ref, cti_ref, pr_ref, pi_ref, ar_ref, ai_ref,
               d_ref, y_ref, hr_ref, hi_ref, pad, perm, bur, bui, hb, *, seq):
    NS = SUBLANES
    SL = seq // NS
    pitch = _s5_pitch(SL)
    nk = S5_BLK_U // LANES
    n = S5_BLK_N

    for s in range(NS):
        for k in range(nk):
            pad[k, s * pitch:s * pitch + SL, :] = u_ref[s * SL:(s + 1) * SL, k * LANES:(k + 1) * LANES]

    def gather(j, c):
        r0 = pl.multiple_of(j * NS, NS)
        for k in range(nk):
            perm[pl.ds(r0, NS), k * LANES:(k + 1) * LANES] = pad[k, pl.ds(j, NS, stride=pitch), :]
        return c

    lax.fori_loop(0, SL, gather, 0, unroll=8)
    up = perm[...]
    ub = up.astype(BF16)
    bur[...] = jnp.dot(ub, btr_ref[...], preferred_element_type=F32)
    bui[...] = jnp.dot(ub, bti_ref[...], preferred_element_type=F32)

    ar = jnp.broadcast_to(ar_ref[...], (NS, n))
    ai = jnp.broadcast_to(ai_ref[...], (NS, n))

    def step(carry, r0):
        hr, hi = carry
        nr = ar * hr - ai * hi + bur[pl.ds(r0, NS), :]
        ni = ar * hi + ai * hr + bui[pl.ds(r0, NS), :]
        return nr, ni

    def scan_ends(j, carry):
        return step(carry, pl.multiple_of(j * NS, NS))

    zero = jnp.zeros((NS, n), F32)
    er, ei = lax.fori_loop(0, SL, scan_ends, (zero, zero), unroll=4)

    asr = pr_ref[SUBLANES - 1:SUBLANES, :]
    asi = pi_ref[SUBLANES - 1:SUBLANES, :]
    gr = [h0r_ref[...]]
    gi = [h0i_ref[...]]
    for s in range(NS):
        gr.append(er[s:s + 1, :] + (asr * gr[s] - asi * gi[s]))
        gi.append(ei[s:s + 1, :] + (asr * gi[s] + asi * gr[s]))
    hr_ref[...] = gr[NS]
    hi_ref[...] = gi[NS]

    def scan_store(jp, carry):
        r0 = pl.multiple_of(jp * 2 * NS, 2 * NS)
        c1 = step(carry, r0)
        c2 = step(c1, r0 + NS)
        hb[pl.ds(r0, 2 * NS), 0:n] = jnp.concatenate([c1[0], c2[0]], axis=0).astype(BF16)
        hb[pl.ds(r0, 2 * NS), n:2 * n] = jnp.concatenate([c1[1], c2[1]], axis=0).astype(BF16)
        return c2

    lax.fori_loop(0, SL // 2, scan_store,
                  (jnp.concatenate(gr[:NS], axis=0), jnp.concatenate(gi[:NS], axis=0)), unroll=2)

    y = (jnp.dot(hb[:, 0:n], ctr_ref[...], preferred_element_type=F32)
         - jnp.dot(hb[:, n:2 * n], cti_ref[...], preferred_element_type=F32))
    perm[...] = _gelu_tanh(y + d_ref[...] * up)

    def scatter(j, c):
        r0 = pl.multiple_of(j * NS, NS)
        for k in range(nk):
            pad[k, pl.ds(j, NS, stride=pitch), :] = perm[pl.ds(r0, NS), k * LANES:(k + 1) * LANES]
        return c

    lax.fori_loop(0, SL, scatter, 0, unroll=8)
    for s in range(NS):
        for k in range(nk):
            y_ref[s * SL:(s + 1) * SL, k * LANES:(k + 1) * LANES] = pad[k, s * pitch:s * pitch + SL, :]


def _s5(proj, h0r, h0i, bt_re, bt_im, ct_re, ct_im, pw_re, pw_im, ab_re, ab_im, d_s5, *, batch, seq, row_block0):
    ub0 = COL_UB // S5_BLK_U
    seg = seq // SUBLANES
    vec = lambda width: pl.BlockSpec((1, width), lambda b, j: (0, j))
    wspec = lambda r, c: pl.BlockSpec((None, r, c), lambda b, j: (j, 0, 0))
    pspec = pl.BlockSpec((SUBLANES, S5_BLK_N), lambda b, j: (seg // SUBLANES - 1, j))
    st_out = pl.BlockSpec((None, 1, S5_BLK_N), lambda b, j: (b, 0, j))
    return pl.pallas_call(
        functools.partial(_s5_kernel, seq=seq),
        grid=(batch, S5_BLOCKS),
        in_specs=[
            pl.BlockSpec((seq, S5_BLK_U), lambda b, j: (row_block0 + b, ub0 + j)),
            vec(S5_BLK_N), vec(S5_BLK_N),
            wspec(S5_BLK_U, S5_BLK_N), wspec(S5_BLK_U, S5_BLK_N),
            wspec(S5_BLK_N, S5_BLK_U), wspec(S5_BLK_N, S5_BLK_U),
            pspec, pspec,
            vec(S5_BLK_N), vec(S5_BLK_N), vec(S5_BLK_U),
        ],
        out_specs=[pl.BlockSpec((seq, S5_BLK_U), lambda b, j: (b, j)), st_out, st_out],
        out_shape=[
            jax.ShapeDtypeStruct((batch * seq, S5_WIDTH), F32),
            jax.ShapeDtypeStruct((batch, 1, S5_NSTATE), F32),
            jax.ShapeDtypeStruct((batch, 1, S5_NSTATE), F32),
        ],
        scratch_shapes=[
            pltpu.VMEM((S5_BLK_U // LANES, SUBLANES * _s5_pitch(seg), LANES), F32),
            pltpu.VMEM((seq, S5_BLK_U), F32),
            pltpu.VMEM((seq, S5_BLK_N), F32),
            pltpu.VMEM((seq, S5_BLK_N), F32),
            pltpu.VMEM((seq, 2 * S5_BLK_N), BF16),
        ],
        compiler_params=_params("arbitrary", "arbitrary"),
        name="s5",
    )(proj, h0r, h0i, bt_re, bt_im, ct_re, ct_im, pw_re, pw_im, ab_re, ab_im, d_s5)


def _s5_step_kernel(u_ref, h0r_ref, h0i_ref, btr_ref, bti_ref, ctr_ref, cti_ref, ar_ref, ai_ref, d_ref,
                    y_ref, hr_ref, hi_ref):
    u = u_ref[...]
    ub = u.astype(BF16)
    ar = ar_ref[...]
    ai = ai_ref[...]
    h0r = h0r_ref[...]
    h0i = h0i_ref[...]
    hr = jnp.dot(ub, btr_ref[...], preferred_element_type=F32) + (ar * h0r - ai * h0i)
    hi = jnp.dot(ub, bti_ref[...], preferred_element_type=F32) + (ar * h0i + ai * h0r)
    hr_ref[...] = hr
    hi_ref[...] = hi
    y = (jnp.dot(hr.astype(BF16), ctr_ref[...], preferred_element_type=F32)
         - jnp.dot(hi.astype(BF16), cti_ref[...], preferred_element_type=F32))
    y_ref[...] = _gelu_tanh(y + d_ref[...] * u)


def _s5_step(proj, h0r, h0i, bt_re, bt_im, ct_re, ct_im, ab_re, ab_im, d_s5, *, nseq):
    ub0 = COL_UB // S5_BLK_U
    vec = lambda width: pl.BlockSpec((1, width), lambda j: (0, j))
    wspec = lambda r, c: pl.BlockSpec((None, r, c), lambda j: (j, 0, 0))
    st = pl.BlockSpec((nseq, S5_BLK_N), lambda j: (0, j))
    return pl.pallas_call(
        _s5_step_kernel,
        grid=(S5_BLOCKS,),
        in_specs=[
            pl.BlockSpec((nseq, S5_BLK_U), lambda j: (0, ub0 + j)),
            st, st,
            wspec(S5_BLK_U, S5_BLK_N), wspec(S5_BLK_U, S5_BLK_N),
            wspec(S5_BLK_N, S5_BLK_U), wspec(S5_BLK_N, S5_BLK_U),
            vec(S5_BLK_N), vec(S5_BLK_N), vec(S5_BLK_U),
        ],
        out_specs=[pl.BlockSpec((nseq, S5_BLK_U), lambda j: (0, j)), st, st],
        out_shape=[
            jax.ShapeDtypeStruct((nseq, S5_WIDTH), F32),
            jax.ShapeDtypeStruct((nseq, S5_NSTATE), F32),
            jax.ShapeDtypeStruct((nseq, S5_NSTATE), F32),
        ],
        compiler_params=_params("arbitrary"),
        name="s5_step",
    )(proj, h0r, h0i, bt_re, bt_im, ct_re, ct_im, ab_re, ab_im, d_s5)


def _tail_kernel(yn_ref, ybg_ref, zb_ref, ga_ref, gb_ref, x_ref, wpa_ref, wglu_ref, bglu_ref, wpb_ref,
                 wout_ref, fnw_ref, out_ref):
    ya = jnp.dot(yn_ref[...], wpa_ref[...], preferred_element_type=F32)
    yb = ybg_ref[...]
    glu = jnp.dot(yb.astype(BF16), wglu_ref[...], preferred_element_type=F32) + bglu_ref[...]
    yb = (yb * _sigmoid(glu)) * _silu(zb_ref[...])
    ybp = jnp.dot(yb.astype(BF16), wpb_ref[...], preferred_element_type=F32)
    mixed = _sigmoid(ga_ref[...]) * ya + _sigmoid(gb_ref[...]) * ybp
    o = x_ref[...] + jnp.dot(mixed.astype(BF16), wout_ref[...], preferred_element_type=F32)
    ms = jnp.mean(o * o, axis=-1, keepdims=True)
    out_ref[...] = (o * lax.rsqrt(ms + NORM_EPS)) * fnw_ref[...]


def _tail(yn, ybg, proj, x, w_proj_a, w_glu, b_glu, w_proj_b, w_out, final_norm_w, *, tm):
    m, d = x.shape
    resident = lambda shape: pl.BlockSpec(shape, lambda i: (0, 0), pipeline_mode=pl.Buffered(1))
    return pl.pallas_call(
        _tail_kernel,
        grid=(m // tm,),
        in_specs=[
            pl.BlockSpec((tm, SSD_WIDTH), lambda i: (i, 0)),
            pl.BlockSpec((tm, S5_WIDTH), lambda i: (i, 0)),
            pl.BlockSpec((tm, S5_WIDTH), lambda i: (i, COL_ZB // S5_WIDTH)),
            pl.BlockSpec((tm, d), lambda i: (i, COL_GA // D_MODEL)),
            pl.BlockSpec((tm, d), lambda i: (i, COL_GB // D_MODEL)),
            pl.BlockSpec((tm, d), lambda i: (i, 0)),
            resident((SSD_WIDTH, d)),
            resident((S5_WIDTH, S5_WIDTH)),
            resident((1, S5_WIDTH)),
            resident((S5_WIDTH, d)),
            resident((d, d)),
            resident((1, d)),
        ],
        out_specs=pl.BlockSpec((tm, d), lambda i: (i, 0)),
        out_shape=jax.ShapeDtypeStruct((m, d), F32),
        compiler_params=_params("arbitrary"),
        name="tail",
    )(yn, ybg, proj, proj, proj, x, w_proj_a, w_glu, b_glu, w_proj_b, w_out, final_norm_w)


def _block_diag16(w):
    g, r, c = w.shape
    nb = g // 16
    eye = jnp.eye(16, dtype=w.dtype)
    out = jnp.einsum("jgrc,gh->jgrhc", w.reshape(nb, 16, r, c), eye)
    return out.reshape(nb, 16 * r, 16 * c)


def kernel(x_prompt, x_sample, state_ssd, state_conv, state_s5_re, state_s5_im, meta_tokens, norm_w, w_in,
           conv_w, conv_b, dt_bias, a_log, d_ssd, ssd_norm_w, w_proj_a, lam_re, lam_im, log_dt_s5, b_re, b_im,
           c_re, c_im, d_s5, w_glu, b_glu, w_proj_b, w_out, final_norm_w):
    bsz, seq, d = x_prompt.shape
    nseq = x_sample.shape[0]
    assert d == D_MODEL and seq % CHUNK == 0 and nseq % SUBLANES == 0 and norm_w.shape[0] == 1
    assert meta_tokens.shape[0] == N_META and N_META <= CHUNK

    wi = w_in[0]
    o_xbc = SSD_WIDTH
    o_dt = o_xbc + SSD_XBC
    o_ub = o_dt + SSD_HEADS
    o_zb = o_ub + S5_WIDTH
    o_ga = o_zb + S5_WIDTH
    o_gb = o_ga + D_MODEL
    w_main = jnp.concatenate(
        [wi[:, :o_xbc], wi[:, o_ga:o_gb], wi[:, o_gb:], wi[:, o_xbc:o_dt], wi[:, o_ub:o_zb], wi[:, o_zb:o_ga]],
        axis=1).astype(BF16)
    w_dt = jnp.pad(wi[:, o_dt:o_ub], ((0, 0), (0, LANES - SSD_HEADS))).astype(BF16)
    pad_heads = lambda v: jnp.pad(v.reshape(1, SSD_HEADS), ((0, 0), (0, LANES - SSD_HEADS)))
    dtb = pad_heads(dt_bias[0])
    alog = pad_heads(a_log[0])
    d_exp = jnp.repeat(d_ssd[0], SSD_HEAD_DIM).reshape(1, SSD_WIDTH)
    nw1 = norm_w[0].reshape(1, d)
    ssd_nw = ssd_norm_w[0].reshape(1, SSD_WIDTH)
    convw = conv_w[0]
    convb = conv_b[0].reshape(1, SSD_XBC)
    wpa = w_proj_a[0].astype(BF16)
    wglu = w_glu[0].astype(BF16)
    bglu = b_glu[0].reshape(1, S5_WIDTH)
    wpb = w_proj_b[0].astype(BF16)
    wout = w_out[0].astype(BF16)
    fnw = final_norm_w.reshape(1, d)
    ds5 = d_s5[0].reshape(1, S5_WIDTH)

    ab_re, ab_im, bbt_re, bbt_im = _s5prep(
        lam_re[0], lam_im[0], log_dt_s5[0].reshape(S5_GROUPS, 1),
        jnp.transpose(b_re[0], (0, 2, 1)), jnp.transpose(b_im[0], (0, 2, 1)))
    ab_re = ab_re.reshape(1, S5_NSTATE)
    ab_im = ab_im.reshape(1, S5_NSTATE)
    bt_re = _block_diag16(bbt_re).astype(BF16)
    bt_im = _block_diag16(bbt_im).astype(BF16)
    ct_re = _block_diag16(jnp.transpose(c_re[0], (0, 2, 1))).astype(BF16)
    ct_im = _block_diag16(jnp.transpose(c_im[0], (0, 2, 1))).astype(BF16)

    x_main = x_prompt.reshape(bsz * seq, d)
    x_small = jnp.concatenate(
        [x_sample.reshape(nseq, d), jnp.zeros((CHUNK - N_META, d), x_prompt.dtype),
         meta_tokens.astype(x_prompt.dtype)], axis=0)
    assert nseq % CHUNK == 0
    meta_blk = nseq // CHUNK
    tm_main = 1024 if (bsz * seq) % 1024 == 0 else CHUNK
    proj_m, dt_m = _inproj(x_main, nw1, w_main, w_dt, tm=tm_main)
    proj_s, dt_s = _inproj(x_small, nw1, w_main, w_dt, tm=nseq + CHUNK)

    ssd_args = (convw, convb, dtb, alog, d_exp, ssd_nw)
    pw_re, pw_im = _s5pow(ab_re, ab_im, max(seq, CHUNK) // SUBLANES)
    s5_w = (bt_re, bt_im, ct_re, ct_im)
    s5_v = (ab_re, ab_im, ds5)

    zeros_ht = jnp.zeros((SSD_STATE, SSD_WIDTH), F32)
    zeros_tail = jnp.zeros((SUBLANES, SSD_XBC), F32)
    zeros_s5 = jnp.zeros((1, S5_NSTATE), F32)
    _, _, ht_meta, tail_meta = _ssd(proj_s, dt_s, zeros_ht, zeros_tail, *ssd_args, batch=1, nchunks=1,
                                    row_block0=meta_blk, mask_rows=CHUNK - N_META)
    _, s5r_meta, s5i_meta = _s5(proj_s, zeros_s5, zeros_s5, *s5_w, pw_re, pw_im, *s5_v, batch=1, seq=CHUNK,
                                row_block0=meta_blk)

    yn_m, h_m, _, tail_m = _ssd(proj_m, dt_m, ht_meta[0], tail_meta[0], *ssd_args, batch=bsz,
                                nchunks=seq // CHUNK, row_block0=0, mask_rows=0)
    ybg_m, s5r_m, s5i_m = _s5(proj_m, s5r_meta[0], s5i_meta[0], *s5_w, pw_re, pw_im, *s5_v, batch=bsz, seq=seq,
                              row_block0=0)
    tail_w = (wpa, wglu, bglu, wpb, wout, fnw)
    y_prompt = _tail(yn_m, ybg_m, proj_m, x_main, *tail_w, tm=256)

    yn_s, h_s, cs_s = _ssd_step(proj_s, dt_s, state_conv[0].reshape(nseq, (SSD_CONV - 1) * SSD_XBC),
                                state_ssd[0].reshape(nseq, SSD_WIDTH, SSD_STATE), *ssd_args, nseq=nseq)
    ybg_s, s5r_s, s5i_s = _s5_step(proj_s, state_s5_re[0].reshape(nseq, S5_NSTATE),
                                   state_s5_im[0].reshape(nseq, S5_NSTATE), *s5_w, *s5_v, nseq=nseq)
    y_sample = _tail(yn_s, ybg_s, proj_s, x_sample.reshape(nseq, d), *tail_w, tm=nseq)

    dt_out = x_prompt.dtype
    return (
        y_prompt.reshape(bsz, seq, d),
        y_sample.reshape(nseq, 1, d),
        h_m.reshape(1, bsz, SSD_HEADS, SSD_HEAD_DIM, SSD_STATE).astype(dt_out),
        tail_m[:, SUBLANES - (SSD_CONV - 1):, :].reshape(1, bsz, SSD_CONV - 1, SSD_XBC),
        s5r_m.reshape(1, bsz, S5_GROUPS, S5_STATE).astype(dt_out),
        s5i_m.reshape(1, bsz, S5_GROUPS, S5_STATE).astype(dt_out),
        h_s.reshape(1, nseq, SSD_HEADS, SSD_HEAD_DIM, SSD_STATE).astype(dt_out),
        cs_s.reshape(1, nseq, SSD_CONV - 1, SSD_XBC),
        s5r_s.reshape(1, nseq, S5_GROUPS, S5_STATE).astype(dt_out),
        s5i_s.reshape(1, nseq, S5_GROUPS, S5_STATE).astype(dt_out),
    )
```

```python
import functools

import jax
import jax.numpy as jnp
from jax import lax
from jax.experimental import pallas as pl
from jax.experimental.pallas import tpu as pltpu

F32 = jnp.float32
BF16 = jnp.bfloat16

NORM_EPS = 1e-5
LOG2E = 1.4426950408889634
N_META = 16
D_MODEL = 2048
SSD_HEAD_DIM = 64
SSD_HEADS = 32
SSD_GROUPS = 4
SSD_STATE = 128
SSD_WIDTH = SSD_HEADS * SSD_HEAD_DIM
SSD_XBC = SSD_WIDTH + 2 * SSD_GROUPS * SSD_STATE
SSD_CONV = 4
CHUNK = 128
S5_WIDTH = D_MODEL // 2
S5_GROUP = 16
S5_GROUPS = S5_WIDTH // S5_GROUP
S5_STATE = 64
S5_NSTATE = S5_GROUPS * S5_STATE
S5_BLOCKS = 4
S5_BLK_U = S5_WIDTH // S5_BLOCKS
S5_BLK_N = S5_NSTATE // S5_BLOCKS

LANES = 128
SUBLANES = 8
VMEM_LIMIT = 56 * 1024 * 1024

COL_ZA = 0
COL_GA = 2048
COL_GB = 4096
COL_XBC = 6144
COL_UB = 9216
COL_ZB = 10240
PROJ_COLS = 11264
PROJ_TN = 1024


def _sigmoid(x):
    return 0.5 * jnp.tanh(0.5 * x) + 0.5


def _silu(x):
    h = 0.5 * x
    return h + h * jnp.tanh(h)


def _softplus(x):
    return jnp.maximum(x, 0.0) + jnp.log1p(jnp.exp(-jnp.abs(x)))


def _gelu_tanh(x):
    c = 0.7978845608028654
    return 0.5 * x * (1.0 + jnp.tanh(c * (x + 0.044715 * (x * x * x))))


def _split3(x):
    x1 = x.astype(BF16)
    r1 = x - x1.astype(F32)
    x2 = r1.astype(BF16)
    x3 = (r1 - x2.astype(F32)).astype(BF16)
    return x1, x2, x3


def _params(*sem):
    return pltpu.CompilerParams(dimension_semantics=sem, vmem_limit_bytes=VMEM_LIMIT)


def _s5prep_kernel(lr_ref, li_ref, ldt_ref, btr_ref, bti_ref, abr_ref, abi_ref, bbr_ref, bbi_ref):
    lr = lr_ref[...]
    li = li_ref[...]
    step = jnp.exp(ldt_ref[...])
    mag = jnp.exp(lr * step)
    abr = mag * jnp.cos(li * step)
    abi = mag * jnp.sin(li * step)
    den = lr * lr + li * li
    numr = abr - 1.0
    cr = (numr * lr + abi * li) / den
    ci = (abi * lr - numr * li) / den
    abr_ref[...] = abr
    abi_ref[...] = abi
    btr = btr_ref[...]
    bti = bti_ref[...]
    crb = cr[:, None, :]
    cib = ci[:, None, :]
    bbr_ref[...] = crb * btr - cib * bti
    bbi_ref[...] = crb * bti + cib * btr


def _s5prep(lam_re, lam_im, log_dt, bt_re, bt_im):
    g, n = lam_re.shape
    full2 = pl.BlockSpec((g, n), lambda: (0, 0))
    full3 = pl.BlockSpec((g, S5_GROUP, n), lambda: (0, 0, 0))
    return pl.pallas_call(
        _s5prep_kernel,
        in_specs=[full2, full2, pl.BlockSpec((g, 1), lambda: (0, 0)), full3, full3],
        out_specs=[full2, full2, full3, full3],
        out_shape=[jax.ShapeDtypeStruct((g, n), F32)] * 2
        + [jax.ShapeDtypeStruct((g, S5_GROUP, n), F32)] * 2,
        name="s5prep",
    )(lam_re, lam_im, log_dt, bt_re, bt_im)


def _s5pow_kernel(ar_ref, ai_ref, pr_ref, pi_ref):
    pr_ref[0:1, :] = ar_ref[...]
    pi_ref[0:1, :] = ai_ref[...]
    m = 1
    while m < pr_ref.shape[0]:
        br = pr_ref[m - 1:m, :]
        bi = pi_ref[m - 1:m, :]
        xr = pr_ref[0:m, :]
        xi = pi_ref[0:m, :]
        pr_ref[m:2 * m, :] = xr * br - xi * bi
        pi_ref[m:2 * m, :] = xr * bi + xi * br
        m *= 2


def _s5pow(ab_re, ab_im, npow):
    n = ab_re.shape[1]
    vec = pl.BlockSpec((1, n), lambda: (0, 0))
    tab = pl.BlockSpec((npow, n), lambda: (0, 0))
    return pl.pallas_call(
        _s5pow_kernel,
        in_specs=[vec, vec],
        out_specs=[tab, tab],
        out_shape=[jax.ShapeDtypeStruct((npow, n), F32)] * 2,
        name="s5pow",
    )(ab_re, ab_im)


def _inproj_kernel(x_ref, nw_ref, w_ref, wdt_ref, proj_ref, dt_ref, xn_ref):
    @pl.when(pl.program_id(1) == 0)
    def _():
        x = x_ref[...]
        ms = jnp.mean(x * x, axis=-1, keepdims=True)
        xn = ((x * lax.rsqrt(ms + NORM_EPS)) * nw_ref[...]).astype(BF16)
        xn_ref[...] = xn
        dt_ref[...] = jnp.dot(xn, wdt_ref[...], preferred_element_type=F32)

    proj_ref[...] = jnp.dot(xn_ref[...], w_ref[...], preferred_element_type=F32)


def _inproj(x, norm_w, w_main, w_dt, tm):
    m, d = x.shape
    n = w_main.shape[1]
    return pl.pallas_call(
        _inproj_kernel,
        grid=(m // tm, n // PROJ_TN),
        in_specs=[
            pl.BlockSpec((tm, d), lambda i, j: (i, 0)),
            pl.BlockSpec((1, d), lambda i, j: (0, 0)),
            pl.BlockSpec((d, PROJ_TN), lambda i, j: (0, j)),
            pl.BlockSpec((d, LANES), lambda i, j: (0, 0)),
        ],
        out_specs=[
            pl.BlockSpec((tm, PROJ_TN), lambda i, j: (i, j)),
            pl.BlockSpec((tm, LANES), lambda i, j: (i, 0)),
        ],
        out_shape=[jax.ShapeDtypeStruct((m, n), F32), jax.ShapeDtypeStruct((m, LANES), F32)],
        scratch_shapes=[pltpu.VMEM((tm, d), BF16)],
        compiler_params=_params("arbitrary", "arbitrary"),
        name="inproj",
    )(x, norm_w, w_main, w_dt)


def _ssd_kernel(xbc_ref, za_ref, dtr_ref, ht0_ref, tail0_ref, convw_ref, convb_ref, dtb_ref, alog_ref,
                dexp_ref, nw_ref, y_ref, h_ref, ht_ref, tail_ref, ht_scr, ext_scr, *, mask_rows):
    c = pl.program_id(1)
    L = CHUNK
    P2 = 2 * SSD_HEAD_DIM
    GW = SSD_WIDTH // SSD_GROUPS

    @pl.when(c == 0)
    def _():
        ht_scr[...] = ht0_ref[...]
        ext_scr[0:SUBLANES, :] = tail0_ref[...]

    x = xbc_ref[...]
    ext_scr[SUBLANES:SUBLANES + L, :] = x
    w = convw_ref[...]
    conv = convb_ref[...] + ext_scr[SUBLANES - 3:SUBLANES - 3 + L, :] * w[0:1]
    conv = conv + ext_scr[SUBLANES - 2:SUBLANES - 2 + L, :] * w[1:2]
    conv = conv + ext_scr[SUBLANES - 1:SUBLANES - 1 + L, :] * w[2:3]
    conv = conv + x * w[3:4]
    ext_scr[0:SUBLANES, :] = x[L - SUBLANES:L, :]
    xbc = _silu(conv)
    xs = xbc[:, :SSD_WIDTH]
    bmat = xbc[:, SSD_WIDTH:SSD_WIDTH + SSD_GROUPS * SSD_STATE]
    cmat = xbc[:, SSD_WIDTH + SSD_GROUPS * SSD_STATE:]

    rows = lax.broadcasted_iota(jnp.int32, (L, L), 0)
    cols = lax.broadcasted_iota(jnp.int32, (L, L), 1)
    causal = rows >= cols
    lane_lo = cols < SSD_HEAD_DIM

    dt = _softplus(dtr_ref[...] + dtb_ref[...])
    if mask_rows:
        dt = jnp.where(rows < mask_rows, 0.0, dt)
    a = dt * (-jnp.exp(alog_ref[...]))
    tril = jnp.where(causal, 1.0, 0.0).astype(BF16)
    a1, a2, a3 = _split3(a)
    acum = (jnp.dot(tril, a1, preferred_element_type=F32)
            + jnp.dot(tril, a2, preferred_element_type=F32)
            + jnp.dot(tril, a3, preferred_element_type=F32))
    a2 = acum * LOG2E
    e_cum = jnp.exp2(a2)
    w_end = dt * jnp.exp2(a2[L - 1:L, :] - a2)
    a2dt_t = (a2 - jnp.log2(dt)).T

    dexp = dexp_ref[...]
    xs_b = xs.astype(BF16)
    y_parts = []
    for g in range(SSD_GROUPS):
        bg = bmat[:, g * SSD_STATE:(g + 1) * SSD_STATE].astype(BF16)
        cg = cmat[:, g * SSD_STATE:(g + 1) * SSD_STATE].astype(BF16)
        cb = lax.dot_general(cg, bg, (((1,), (1,)), ((), ())), preferred_element_type=F32)
        ht_g = ht_scr[:, g * GW:(g + 1) * GW]
        y_off = jnp.dot(cg, ht_g.astype(BF16), preferred_element_type=F32)
        xw_parts = []
        elast_parts = []
        for jj in range(GW // P2):
            lo = g * GW + jj * P2
            h0 = lo // SSD_HEAD_DIM
            yd, eb, wb = [], [], []
            for h in (h0, h0 + 1):
                colb = jnp.broadcast_to(a2[:, h:h + 1], (L, L))
                m = cb * jnp.exp2(jnp.where(causal, colb - a2dt_t[h:h + 1, :], -jnp.inf))
                yd.append(jnp.dot(m.astype(BF16), xs_b[:, lo:lo + P2], preferred_element_type=F32))
                eb.append(jnp.broadcast_to(e_cum[:, h:h + 1], (L, L)))
                wb.append(jnp.broadcast_to(w_end[:, h:h + 1], (L, L)))
            xs_pair = xs[:, lo:lo + P2]
            y_pair = (jnp.where(lane_lo, yd[0], yd[1])
                      + y_off[:, jj * P2:(jj + 1) * P2] * jnp.where(lane_lo, eb[0], eb[1]))
            y_parts.append(y_pair + dexp[:, lo:lo + P2] * xs_pair)
            xw_parts.append(xs_pair * jnp.where(lane_lo, wb[0], wb[1]))
            elast_parts.append(jnp.where(lane_lo[0:1, :], eb[0][L - 1:L, :], eb[1][L - 1:L, :]))
        xw = jnp.concatenate(xw_parts, axis=1)
        elast = jnp.concatenate(elast_parts, axis=1)
        st = lax.dot_general(bg, xw.astype(BF16), (((0,), (0,)), ((), ())),
                             preferred_element_type=F32)
        ht_scr[:, g * GW:(g + 1) * GW] = ht_g * elast + st

    y = jnp.concatenate(y_parts, axis=1)
    y = y * _silu(za_ref[...])
    nw = nw_ref[...]
    outs = []
    for g in range(SSD_GROUPS):
        yg = y[:, g * GW:(g + 1) * GW]
        ms = jnp.mean(yg * yg, axis=-1, keepdims=True)
        outs.append((yg * lax.rsqrt(ms + NORM_EPS)) * nw[:, g * GW:(g + 1) * GW])
    y_ref[...] = jnp.concatenate(outs, axis=1).astype(y_ref.dtype)

    @pl.when(c == pl.num_programs(1) - 1)
    def _():
        ht = ht_scr[...]
        ht_ref[...] = ht
        h_ref[...] = ht.T
        tail_ref[...] = x[L - SUBLANES:L, :]


def _ssd(proj, dt_raw, ht0, tail0, conv_w, conv_b, dt_bias, a_log, d_exp, norm_w, *, batch, nchunks,
         row_block0, mask_rows):
    L = CHUNK
    rows = batch * nchunks * L
    rb = lambda b, c: row_block0 + b * nchunks + c
    const2 = lambda shape: pl.BlockSpec(shape, lambda b, c: (0, 0))
    return pl.pallas_call(
        functools.partial(_ssd_kernel, mask_rows=mask_rows),
        grid=(batch, nchunks),
        in_specs=[
            pl.BlockSpec((L, SSD_XBC), lambda b, c: (rb(b, c), COL_XBC // SSD_XBC)),
            pl.BlockSpec((L, SSD_WIDTH), lambda b, c: (rb(b, c), COL_ZA // SSD_WIDTH)),
            pl.BlockSpec((L, LANES), lambda b, c: (rb(b, c), 0)),
            const2((SSD_STATE, SSD_WIDTH)),
            const2((SUBLANES, SSD_XBC)),
            const2((SSD_CONV, SSD_XBC)),
            const2((1, SSD_XBC)),
            const2((1, LANES)),
            const2((1, LANES)),
            const2((1, SSD_WIDTH)),
            const2((1, SSD_WIDTH)),
        ],
        out_specs=[
            pl.BlockSpec((L, SSD_WIDTH), lambda b, c: (b * nchunks + c, 0)),
            pl.BlockSpec((None, SSD_WIDTH, SSD_STATE), lambda b, c: (b, 0, 0)),
            pl.BlockSpec((None, SSD_STATE, SSD_WIDTH), lambda b, c: (b, 0, 0)),
            pl.BlockSpec((None, SUBLANES, SSD_XBC), lambda b, c: (b, 0, 0)),
        ],
        out_shape=[
            jax.ShapeDtypeStruct((rows, SSD_WIDTH), BF16),
            jax.ShapeDtypeStruct((batch, SSD_WIDTH, SSD_STATE), F32),
            jax.ShapeDtypeStruct((batch, SSD_STATE, SSD_WIDTH), F32),
            jax.ShapeDtypeStruct((batch, SUBLANES, SSD_XBC), F32),
        ],
        scratch_shapes=[pltpu.VMEM((SSD_STATE, SSD_WIDTH), F32),
                        pltpu.VMEM((SUBLANES + L, SSD_XBC), F32)],
        compiler_params=_params("arbitrary", "arbitrary"),
        name="ssd",
    )(proj, proj, dt_raw, ht0, tail0, conv_w, conv_b, dt_bias, a_log, d_exp, norm_w)


def _ssd_step_kernel(xbc_ref, za_ref, dtr_ref, cs_ref, h_ref, convw_ref, convb_ref, dtb_ref, alog_ref,
                     dexp_ref, nw_ref, y_ref, hout_ref, csout_ref):
    R = SUBLANES
    GW = SSD_WIDTH // SSD_GROUPS
    HPG = SSD_HEADS // SSD_GROUPS
    x = xbc_ref[...]
    w = convw_ref[...]
    s0 = cs_ref[:, 0:SSD_XBC]
    s1 = cs_ref[:, SSD_XBC:2 * SSD_XBC]
    s2 = cs_ref[:, 2 * SSD_XBC:3 * SSD_XBC]
    conv = convb_ref[...] + s0 * w[0:1]
    conv = conv + s1 * w[1:2]
    conv = conv + s2 * w[2:3]
    conv = conv + x * w[3:4]
    csout_ref[:, 0:SSD_XBC] = s1
    csout_ref[:, SSD_XBC:2 * SSD_XBC] = s2
    csout_ref[:, 2 * SSD_XBC:3 * SSD_XBC] = x
    xbc = _silu(conv)
    xs = xbc[:, :SSD_WIDTH]
    bmat = xbc[:, SSD_WIDTH:SSD_WIDTH + SSD_GROUPS * SSD_STATE]
    cmat = xbc[:, SSD_WIDTH + SSD_GROUPS * SSD_STATE:]

    dt = _softplus(dtr_ref[...] + dtb_ref[...])
    da = jnp.exp(dt * (-jnp.exp(alog_ref[...])))
    dt_t = dt.T
    da_t = da.T
    expand = lambda v: jnp.concatenate(
        [jnp.broadcast_to(v[h:h + 1, :], (SSD_HEAD_DIM, R)) for h in range(SSD_HEADS)], axis=0)
    xd_t = xs.T * expand(dt_t)
    da_te = expand(da_t)

    ycols = []
    for i in range(R):
        bexp = jnp.concatenate(
            [jnp.broadcast_to(bmat[i:i + 1, g * SSD_STATE:(g + 1) * SSD_STATE], (GW, SSD_STATE))
             for g in range(SSD_GROUPS)], axis=0)
        cexp = jnp.concatenate(
            [jnp.broadcast_to(cmat[i:i + 1, g * SSD_STATE:(g + 1) * SSD_STATE], (GW, SSD_STATE))
             for g in range(SSD_GROUPS)], axis=0)
        hn = h_ref[i] * da_te[:, i:i + 1] + xd_t[:, i:i + 1] * bexp
        hout_ref[i] = hn
        ycols.append(jnp.sum(hn * cexp, axis=1, keepdims=True))
    y = jnp.concatenate(ycols, axis=1).T
    y = y + dexp_ref[...] * xs
    y = y * _silu(za_ref[...])
    nw = nw_ref[...]
    outs = []
    for g in range(SSD_GROUPS):
        yg = y[:, g * GW:(g + 1) * GW]
        ms = jnp.mean(yg * yg, axis=-1, keepdims=True)
        outs.append((yg * lax.rsqrt(ms + NORM_EPS)) * nw[:, g * GW:(g + 1) * GW])
    y_ref[...] = jnp.concatenate(outs, axis=1).astype(y_ref.dtype)
    del HPG


def _ssd_step(proj, dt_raw, conv_state, ssd_state, conv_w, conv_b, dt_bias, a_log, d_exp, norm_w, *, nseq):
    R = SUBLANES
    const2 = lambda shape: pl.BlockSpec(shape, lambda i: (0, 0))
    return pl.pallas_call(
        _ssd_step_kernel,
        grid=(nseq // R,),
        in_specs=[
            pl.BlockSpec((R, SSD_XBC), lambda i: (i, COL_XBC // SSD_XBC)),
            pl.BlockSpec((R, SSD_WIDTH), lambda i: (i, COL_ZA // SSD_WIDTH)),
            pl.BlockSpec((R, LANES), lambda i: (i, 0)),
            pl.BlockSpec((R, (SSD_CONV - 1) * SSD_XBC), lambda i: (i, 0)),
            pl.BlockSpec((R, SSD_WIDTH, SSD_STATE), lambda i: (i, 0, 0)),
            const2((SSD_CONV, SSD_XBC)),
            const2((1, SSD_XBC)),
            const2((1, LANES)),
            const2((1, LANES)),
            const2((1, SSD_WIDTH)),
            const2((1, SSD_WIDTH)),
        ],
        out_specs=[
            pl.BlockSpec((R, SSD_WIDTH), lambda i: (i, 0)),
            pl.BlockSpec((R, SSD_WIDTH, SSD_STATE), lambda i: (i, 0, 0)),
            pl.BlockSpec((R, (SSD_CONV - 1) * SSD_XBC), lambda i: (i, 0)),
        ],
        out_shape=[
            jax.ShapeDtypeStruct((nseq, SSD_WIDTH), BF16),
            jax.ShapeDtypeStruct((nseq, SSD_WIDTH, SSD_STATE), F32),
            jax.ShapeDtypeStruct((nseq, (SSD_CONV - 1) * SSD_XBC), F32),
        ],
        compiler_params=_params("arbitrary"),
        name="ssd_step",
    )(proj, proj, dt_raw, conv_state, ssd_state, conv_w, conv_b, dt_bias, a_log, d_exp, norm_w)


def _s5_pitch(seg_len):
    return seg_len if (seg_len // SUBLANES) % 2 else seg_len + SUBLANES


def _s5_kernel(u_ref, h0r_ref, h0i_ref, btr_ref, bti_ref, ctr_ref, cti_ref, pr_ref, pi_ref, ar_ref, ai_ref,
               d_ref, y_ref, hr_ref, hi_ref, pad, perm, bur, bui, hb, *, seq):
    NS = SUBLANES
    SL = seq // NS
    pitch = _s5_pitch(SL)
    nk = S5_BLK_U // LANES
    n = S5_BLK_N

    for s in range(NS):
        for k in range(nk):
            pad[k, s * pitch:s * pitch + SL, :] = u_ref[s * SL:(s + 1) * SL, k * LANES:(k + 1) * LANES]

    def gather(j, c):
        r0 = pl.multiple_of(j * NS, NS)
        for k in range(nk):
            perm[pl.ds(r0, NS), k * LANES:(k + 1) * LANES] = pad[k, pl.ds(j, NS, stride=pitch), :]
        return c

    lax.fori_loop(0, SL, gather, 0, unroll=8)
    up = perm[...]
    ub = up.astype(BF16)
    bur[...] = jnp.dot(ub, btr_ref[...], preferred_element_type=F32)
    bui[...] = jnp.dot(ub, bti_ref[...], preferred_element_type=F32)

    ar = jnp.broadcast_to(ar_ref[...], (NS, n))
    ai = jnp.broadcast_to(ai_ref[...], (NS, n))

    def step(carry, r0):
        hr, hi = carry
        nr = ar * hr - ai * hi + bur[pl.ds(r0, NS), :]
        ni = ar * hi + ai * hr + bui[pl.ds(r0, NS), :]
        return nr, ni

    def scan_ends(j, carry):
        return step(carry, pl.multiple_of(j * NS, NS))

    zero = jnp.zeros((NS, n), F32)
    er, ei = lax.fori_loop(0, SL, scan_ends, (zero, zero), unroll=4)

    asr = pr_ref[SUBLANES - 1:SUBLANES, :]
    asi = pi_ref[SUBLANES - 1:SUBLANES, :]
    gr = [h0r_ref[...]]
    gi = [h0i_ref[...]]
    for s in range(NS):
        gr.append(er[s:s + 1, :] + (asr * gr[s] - asi * gi[s]))
        gi.append(ei[s:s + 1, :] + (asr * gi[s] + asi * gr[s]))
    hr_ref[...] = gr[NS]
    hi_ref[...] = gi[NS]

    def scan_store(jp, carry):
        r0 = pl.multiple_of(jp * 2 * NS, 2 * NS)
        c1 = step(carry, r0)
        c2 = step(c1, r0 + NS)
        hb[pl.ds(r0, 2 * NS), 0:n] = jnp.concatenate([c1[0], c2[0]], axis=0).astype(BF16)
        hb[pl.ds(r0, 2 * NS), n:2 * n] = jnp.concatenate([c1[1], c2[1]], axis=0).astype(BF16)
        return c2

    lax.fori_loop(0, SL // 2, scan_store,
                  (jnp.concatenate(gr[:NS], axis=0), jnp.concatenate(gi[:NS], axis=0)), unroll=2)

    y = (jnp.dot(hb[:, 0:n], ctr_ref[...], preferred_element_type=F32)
         - jnp.dot(hb[:, n:2 * n], cti_ref[...], preferred_element_type=F32))
    perm[...] = _gelu_tanh(y + d_ref[...] * up)

    def scatter(j, c):
        r0 = pl.multiple_of(j * NS, NS)
        for k in range(nk):
            pad[k, pl.ds(j, NS, stride=pitch), :] = perm[pl.ds(r0, NS), k * LANES:(k + 1) * LANES]
        return c

    lax.fori_loop(0, SL, scatter, 0, unroll=8)
    for s in range(NS):
        for k in range(nk):
            y_ref[s * SL:(s + 1) * SL, k * LANES:(k + 1) * LANES] = pad[k, s * pitch:s * pitch + SL, :]


def _s5(proj, h0r, h0i, bt_re, bt_im, ct_re, ct_im, pw_re, pw_im, ab_re, ab_im, d_s5, *, batch, seq, row_block0):
    ub0 = COL_UB // S5_BLK_U
    seg = seq // SUBLANES
    vec = lambda width: pl.BlockSpec((1, width), lambda b, j: (0, j))
    wspec = lambda r, c: pl.BlockSpec((None, r, c), lambda b, j: (j, 0, 0))
    pspec = pl.BlockSpec((SUBLANES, S5_BLK_N), lambda b, j: (seg // SUBLANES - 1, j))
    st_out = pl.BlockSpec((None, 1, S5_BLK_N), lambda b, j: (b, 0, j))
    return pl.pallas_call(
        functools.partial(_s5_kernel, seq=seq),
        grid=(batch, S5_BLOCKS),
        in_specs=[
            pl.BlockSpec((seq, S5_BLK_U), lambda b, j: (row_block0 + b, ub0 + j)),
            vec(S5_BLK_N), vec(S5_BLK_N),
            wspec(S5_BLK_U, S5_BLK_N), wspec(S5_BLK_U, S5_BLK_N),
            wspec(S5_BLK_N, S5_BLK_U), wspec(S5_BLK_N, S5_BLK_U),
            pspec, pspec,
            vec(S5_BLK_N), vec(S5_BLK_N), vec(S5_BLK_U),
        ],
        out_specs=[pl.BlockSpec((seq, S5_BLK_U), lambda b, j: (b, j)), st_out, st_out],
        out_shape=[
            jax.ShapeDtypeStruct((batch * seq, S5_WIDTH), F32),
            jax.ShapeDtypeStruct((batch, 1, S5_NSTATE), F32),
            jax.ShapeDtypeStruct((batch, 1, S5_NSTATE), F32),
        ],
        scratch_shapes=[
            pltpu.VMEM((S5_BLK_U // LANES, SUBLANES * _s5_pitch(seg), LANES), F32),
            pltpu.VMEM((seq, S5_BLK_U), F32),
            pltpu.VMEM((seq, S5_BLK_N), F32),
            pltpu.VMEM((seq, S5_BLK_N), F32),
            pltpu.VMEM((seq, 2 * S5_BLK_N), BF16),
        ],
        compiler_params=_params("arbitrary", "arbitrary"),
        name="s5",
    )(proj, h0r, h0i, bt_re, bt_im, ct_re, ct_im, pw_re, pw_im, ab_re, ab_im, d_s5)


def _s5_step_kernel(u_ref, h0r_ref, h0i_ref, btr_ref, bti_ref, ctr_ref, cti_ref, ar_ref, ai_ref, d_ref,
                    y_ref, hr_ref, hi_ref):
    u = u_ref[...]
    ub = u.astype(BF16)
    ar = ar_ref[...]
    ai = ai_ref[...]
    h0r = h0r_ref[...]
    h0i = h0i_ref[...]
    hr = jnp.dot(ub, btr_ref[...], preferred_element_type=F32) + (ar * h0r - ai * h0i)
    hi = jnp.dot(ub, bti_ref[...], preferred_element_type=F32) + (ar * h0i + ai * h0r)
    hr_ref[...] = hr
    hi_ref[...] = hi
    y = (jnp.dot(hr.astype(BF16), ctr_ref[...], preferred_element_type=F32)
         - jnp.dot(hi.astype(BF16), cti_ref[...], preferred_element_type=F32))
    y_ref[...] = _gelu_tanh(y + d_ref[...] * u)


def _s5_step(proj, h0r, h0i, bt_re, bt_im, ct_re, ct_im, ab_re, ab_im, d_s5, *, nseq):
    ub0 = COL_UB // S5_BLK_U
    vec = lambda width: pl.BlockSpec((1, width), lambda j: (0, j))
    wspec = lambda r, c: pl.BlockSpec((None, r, c), lambda j: (j, 0, 0))
    st = pl.BlockSpec((nseq, S5_BLK_N), lambda j: (0, j))
    return pl.pallas_call(
        _s5_step_kernel,
        grid=(S5_BLOCKS,),
        in_specs=[
            pl.BlockSpec((nseq, S5_BLK_U), lambda j: (0, ub0 + j)),
            st, st,
            wspec(S5_BLK_U, S5_BLK_N), wspec(S5_BLK_U, S5_BLK_N),
            wspec(S5_BLK_N, S5_BLK_U), wspec(S5_BLK_N, S5_BLK_U),
            vec(S5_BLK_N), vec(S5_BLK_N), vec(S5_BLK_U),
        ],
        out_specs=[pl.BlockSpec((nseq, S5_BLK_U), lambda j: (0, j)), st, st],
        out_shape=[
            jax.ShapeDtypeStruct((nseq, S5_WIDTH), F32),
            jax.ShapeDtypeStruct((nseq, S5_NSTATE), F32),
            jax.ShapeDtypeStruct((nseq, S5_NSTATE), F32),
        ],
        compiler_params=_params("arbitrary"),
        name="s5_step",
    )(proj, h0r, h0i, bt_re, bt_im, ct_re, ct_im, ab_re, ab_im, d_s5)


def _tail_kernel(yn_ref, ybg_ref, zb_ref, ga_ref, gb_ref, x_ref, wpa_ref, wglu_ref, bglu_ref, wpb_ref,
                 wout_ref, fnw_ref, out_ref):
    ya = jnp.dot(yn_ref[...], wpa_ref[...], preferred_element_type=F32)
    yb = ybg_ref[...]
    glu = jnp.dot(yb.astype(BF16), wglu_ref[...], preferred_element_type=F32) + bglu_ref[...]
    yb = (yb * _sigmoid(glu)) * _silu(zb_ref[...])
    ybp = jnp.dot(yb.astype(BF16), wpb_ref[...], preferred_element_type=F32)
    mixed = _sigmoid(ga_ref[...]) * ya + _sigmoid(gb_ref[...]) * ybp
    o = x_ref[...] + jnp.dot(mixed.astype(BF16), wout_ref[...], preferred_element_type=F32)
    ms = jnp.mean(o * o, axis=-1, keepdims=True)
    out_ref[...] = (o * lax.rsqrt(ms + NORM_EPS)) * fnw_ref[...]


def _tail(yn, ybg, proj, x, w_proj_a, w_glu, b_glu, w_proj_b, w_out, final_norm_w, *, tm):
    m, d = x.shape
    resident = lambda shape: pl.BlockSpec(shape, lambda i: (0, 0), pipeline_mode=pl.Buffered(1))
    return pl.pallas_call(
        _tail_kernel,
        grid=(m // tm,),
        in_specs=[
            pl.BlockSpec((tm, SSD_WIDTH), lambda i: (i, 0)),
            pl.BlockSpec((tm, S5_WIDTH), lambda i: (i, 0)),
            pl.BlockSpec((tm, S5_WIDTH), lambda i: (i, COL_ZB // S5_WIDTH)),
            pl.BlockSpec((tm, d), lambda i: (i, COL_GA // D_MODEL)),
            pl.BlockSpec((tm, d), lambda i: (i, COL_GB // D_MODEL)),
            pl.BlockSpec((tm, d), lambda i: (i, 0)),
            resident((SSD_WIDTH, d)),
            resident((S5_WIDTH, S5_WIDTH)),
            resident((1, S5_WIDTH)),
            resident((S5_WIDTH, d)),
            resident((d, d)),
            resident((1, d)),
        ],
        out_specs=pl.BlockSpec((tm, d), lambda i: (i, 0)),
        out_shape=jax.ShapeDtypeStruct((m, d), F32),
        compiler_params=_params("arbitrary"),
        name="tail",
    )(yn, ybg, proj, proj, proj, x, w_proj_a, w_glu, b_glu, w_proj_b, w_out, final_norm_w)


def _block_diag16(w):
    g, r, c = w.shape
    nb = g // 16
    eye = jnp.eye(16, dtype=w.dtype)
    out = jnp.einsum("jgrc,gh->jgrhc", w.reshape(nb, 16, r, c), eye)
    return out.reshape(nb, 16 * r, 16 * c)


def kernel(x_prompt, x_sample, state_ssd, state_conv, state_s5_re, state_s5_im, meta_tokens, norm_w, w_in,
           conv_w, conv_b, dt_bias, a_log, d_ssd, ssd_norm_w, w_proj_a, lam_re, lam_im, log_dt_s5, b_re, b_im,
           c_re, c_im, d_s5, w_glu, b_glu, w_proj_b, w_out, final_norm_w):
    bsz, seq, d = x_prompt.shape
    nseq = x_sample.shape[0]
    assert d == D_MODEL and seq % CHUNK == 0 and nseq % SUBLANES == 0 and norm_w.shape[0] == 1
    assert meta_tokens.shape[0] == N_META and N_META <= CHUNK

    wi = w_in[0]
    o_xbc = SSD_WIDTH
    o_dt = o_xbc + SSD_XBC
    o_ub = o_dt + SSD_HEADS
    o_zb = o_ub + S5_WIDTH
    o_ga = o_zb + S5_WIDTH
    o_gb = o_ga + D_MODEL
    w_main = jnp.concatenate(
        [wi[:, :o_xbc], wi[:, o_ga:o_gb], wi[:, o_gb:], wi[:, o_xbc:o_dt], wi[:, o_ub:o_zb], wi[:, o_zb:o_ga]],
        axis=1).astype(BF16)
    w_dt = jnp.pad(wi[:, o_dt:o_ub], ((0, 0), (0, LANES - SSD_HEADS))).astype(BF16)
    pad_heads = lambda v: jnp.pad(v.reshape(1, SSD_HEADS), ((0, 0), (0, LANES - SSD_HEADS)))
    dtb = pad_heads(dt_bias[0])
    alog = pad_heads(a_log[0])
    d_exp = jnp.repeat(d_ssd[0], SSD_HEAD_DIM).reshape(1, SSD_WIDTH)
    nw1 = norm_w[0].reshape(1, d)
    ssd_nw = ssd_norm_w[0].reshape(1, SSD_WIDTH)
    convw = conv_w[0]
    convb = conv_b[0].reshape(1, SSD_XBC)
    wpa = w_proj_a[0].astype(BF16)
    wglu = w_glu[0].astype(BF16)
    bglu = b_glu[0].reshape(1, S5_WIDTH)
    wpb = w_proj_b[0].astype(BF16)
    wout = w_out[0].astype(BF16)
    fnw = final_norm_w.reshape(1, d)
    ds5 = d_s5[0].reshape(1, S5_WIDTH)

    ab_re, ab_im, bbt_re, bbt_im = _s5prep(
        lam_re[0], lam_im[0], log_dt_s5[0].reshape(S5_GROUPS, 1),
        jnp.transpose(b_re[0], (0, 2, 1)), jnp.transpose(b_im[0], (0, 2, 1)))
    ab_re = ab_re.reshape(1, S5_NSTATE)
    ab_im = ab_im.reshape(1, S5_NSTATE)
    bt_re = _block_diag16(bbt_re).astype(BF16)
    bt_im = _block_diag16(bbt_im).astype(BF16)
    ct_re = _block_diag16(jnp.transpose(c_re[0], (0, 2, 1))).astype(BF16)
    ct_im = _block_diag16(jnp.transpose(c_im[0], (0, 2, 1))).astype(BF16)

    x_main = x_prompt.reshape(bsz * seq, d)
    x_small = jnp.concatenate(
        [x_sample.reshape(nseq, d), jnp.zeros((CHUNK - N_META, d), x_prompt.dtype),
         meta_tokens.astype(x_prompt.dtype)], axis=0)
    assert nseq % CHUNK == 0
    meta_blk = nseq // CHUNK
    tm_main = 1024 if (bsz * seq) % 1024 == 0 else CHUNK
    proj_m, dt_m = _inproj(x_main, nw1, w_main, w_dt, tm=tm_main)
    proj_s, dt_s = _inproj(x_small, nw1, w_main, w_dt, tm=nseq + CHUNK)

    ssd_args = (convw, convb, dtb, alog, d_exp, ssd_nw)
    pw_re, pw_im = _s5pow(ab_re, ab_im, max(seq, CHUNK) // SUBLANES)
    s5_w = (bt_re, bt_im, ct_re, ct_im)
    s5_v = (ab_re, ab_im, ds5)

    zeros_ht = jnp.zeros((SSD_STATE, SSD_WIDTH), F32)
    zeros_tail = jnp.zeros((SUBLANES, SSD_XBC), F32)
    zeros_s5 = jnp.zeros((1, S5_NSTATE), F32)
    _, _, ht_meta, tail_meta = _ssd(proj_s, dt_s, zeros_ht, zeros_tail, *ssd_args, batch=1, nchunks=1,
                                    row_block0=meta_blk, mask_rows=CHUNK - N_META)
    _, s5r_meta, s5i_meta = _s5(proj_s, zeros_s5, zeros_s5, *s5_w, pw_re, pw_im, *s5_v, batch=1, seq=CHUNK,
                                row_block0=meta_blk)

    yn_m, h_m, _, tail_m = _ssd(proj_m, dt_m, ht_meta[0], tail_meta[0], *ssd_args, batch=bsz,
                                nchunks=seq // CHUNK, row_block0=0, mask_rows=0)
    ybg_m, s5r_m, s5i_m = _s5(proj_m, s5r_meta[0], s5i_meta[0], *s5_w, pw_re, pw_im, *s5_v, batch=bsz, seq=seq,
                              row_block0=0)
    tail_w = (wpa, wglu, bglu, wpb, wout, fnw)
    y_prompt = _tail(yn_m, ybg_m, proj_m, x_main, *tail_w, tm=256)

    yn_s, h_s, cs_s = _ssd_step(proj_s, dt_s, state_conv[0].reshape(nseq, (SSD_CONV - 1) * SSD_XBC),
                                state_ssd[0].reshape(nseq, SSD_WIDTH, SSD_STATE), *ssd_args, nseq=nseq)
    ybg_s, s5r_s, s5i_s = _s5_step(proj_s, state_s5_re[0].reshape(nseq, S5_NSTATE),
                                   state_s5_im[0].reshape(nseq, S5_NSTATE), *s5_w, *s5_v, nseq=nseq)
    y_sample = _tail(yn_s, ybg_s, proj_s, x_sample.reshape(nseq, d), *tail_w, tm=nseq)

    dt_out = x_prompt.dtype
    return (
        y_prompt.reshape(bsz, seq, d),
        y_sample.reshape(nseq, 1, d),
        h_m.reshape(1, bsz, SSD_HEADS, SSD_HEAD_DIM, SSD_STATE).astype(dt_out),
        tail_m[:, SUBLANES - (SSD_CONV - 1):, :].reshape(1, bsz, SSD_CONV - 1, SSD_XBC),
        s5r_m.reshape(1, bsz, S5_GROUPS, S5_STATE).astype(dt_out),
        s5i_m.reshape(1, bsz, S5_GROUPS, S5_STATE).astype(dt_out),
        h_s.reshape(1, nseq, SSD_HEADS, SSD_HEAD_DIM, SSD_STATE).astype(dt_out),
        cs_s.reshape(1, nseq, SSD_CONV - 1, SSD_XBC),
        s5r_s.reshape(1, nseq, S5_GROUPS, S5_STATE).astype(dt_out),
        s5i_s.reshape(1, nseq, S5_GROUPS, S5_STATE).astype(dt_out),
    )
```

```python
import functools

import jax
import jax.numpy as jnp
from jax import lax
from jax.experimental import pallas as pl
from jax.experimental.pallas import tpu as pltpu

F32 = jnp.float32
BF16 = jnp.bfloat16

NORM_EPS = 1e-5
LOG2E = 1.4426950408889634
N_META = 16
D_MODEL = 2048
SSD_HEAD_DIM = 64
SSD_HEADS = 32
SSD_GROUPS = 4
SSD_STATE = 128
SSD_WIDTH = SSD_HEADS * SSD_HEAD_DIM
SSD_XBC = SSD_WIDTH + 2 * SSD_GROUPS * SSD_STATE
SSD_CONV = 4
CHUNK = 128
S5_WIDTH = D_MODEL // 2
S5_GROUP = 16
S5_GROUPS = S5_WIDTH // S5_GROUP
S5_STATE = 64
S5_NSTATE = S5_GROUPS * S5_STATE
S5_BLOCKS = 4
S5_BLK_U = S5_WIDTH // S5_BLOCKS
S5_BLK_N = S5_NSTATE // S5_BLOCKS

LANES = 128
SUBLANES = 8
VMEM_LIMIT = 56 * 1024 * 1024

COL_ZA = 0
COL_GA = 2048
COL_GB = 4096
COL_XBC = 6144
COL_UB = 9216
COL_ZB = 10240
PROJ_COLS = 11264
PROJ_TN = 1024


def _sigmoid(x):
    return 0.5 * jnp.tanh(0.5 * x) + 0.5


def _silu(x):
    h = 0.5 * x
    return h + h * jnp.tanh(h)


def _softplus(x):
    return jnp.maximum(x, 0.0) + jnp.log1p(jnp.exp(-jnp.abs(x)))


def _gelu_tanh(x):
    c = 0.7978845608028654
    return 0.5 * x * (1.0 + jnp.tanh(c * (x + 0.044715 * (x * x * x))))


def _split3(x):
    x1 = x.astype(BF16)
    r1 = x - x1.astype(F32)
    x2 = r1.astype(BF16)
    x3 = (r1 - x2.astype(F32)).astype(BF16)
    return x1, x2, x3


def _params(*sem):
    return pltpu.CompilerParams(dimension_semantics=sem, vmem_limit_bytes=VMEM_LIMIT)


def _s5prep_kernel(lr_ref, li_ref, ldt_ref, btr_ref, bti_ref, abr_ref, abi_ref, bbr_ref, bbi_ref):
    lr = lr_ref[...]
    li = li_ref[...]
    step = jnp.exp(ldt_ref[...])
    mag = jnp.exp(lr * step)
    abr = mag * jnp.cos(li * step)
    abi = mag * jnp.sin(li * step)
    den = lr * lr + li * li
    numr = abr - 1.0
    cr = (numr * lr + abi * li) / den
    ci = (abi * lr - numr * li) / den
    abr_ref[...] = abr
    abi_ref[...] = abi
    btr = btr_ref[...]
    bti = bti_ref[...]
    crb = cr[:, None, :]
    cib = ci[:, None, :]
    bbr_ref[...] = crb * btr - cib * bti
    bbi_ref[...] = crb * bti + cib * btr


def _s5prep(lam_re, lam_im, log_dt, bt_re, bt_im):
    g, n = lam_re.shape
    full2 = pl.BlockSpec((g, n), lambda: (0, 0))
    full3 = pl.BlockSpec((g, S5_GROUP, n), lambda: (0, 0, 0))
    return pl.pallas_call(
        _s5prep_kernel,
        in_specs=[full2, full2, pl.BlockSpec((g, 1), lambda: (0, 0)), full3, full3],
        out_specs=[full2, full2, full3, full3],
        out_shape=[jax.ShapeDtypeStruct((g, n), F32)] * 2
        + [jax.ShapeDtypeStruct((g, S5_GROUP, n), F32)] * 2,
        name="s5prep",
    )(lam_re, lam_im, log_dt, bt_re, bt_im)


def _s5pow_kernel(ar_ref, ai_ref, pr_ref, pi_ref):
    pr_ref[0:1, :] = ar_ref[...]
    pi_ref[0:1, :] = ai_ref[...]
    m = 1
    while m < pr_ref.shape[0]:
        br = pr_ref[m - 1:m, :]
        bi = pi_ref[m - 1:m, :]
        xr = pr_ref[0:m, :]
        xi = pi_ref[0:m, :]
        pr_ref[m:2 * m, :] = xr * br - xi * bi
        pi_ref[m:2 * m, :] = xr * bi + xi * br
        m *= 2


def _s5pow(ab_re, ab_im, npow):
    n = ab_re.shape[1]
    vec = pl.BlockSpec((1, n), lambda: (0, 0))
    tab = pl.BlockSpec((npow, n), lambda: (0, 0))
    return pl.pallas_call(
        _s5pow_kernel,
        in_specs=[vec, vec],
        out_specs=[tab, tab],
        out_shape=[jax.ShapeDtypeStruct((npow, n), F32)] * 2,
        name="s5pow",
    )(ab_re, ab_im)


_NT = (((1,), (1,)), ((), ()))

W_ALIGN = 32
_SRC_ZA = 0
_SRC_XBC = SSD_WIDTH
_SRC_DT = _SRC_XBC + SSD_XBC
_SRC_UB = _SRC_DT + SSD_HEADS
_SRC_ZB = _SRC_UB + S5_WIDTH
_SRC_GA = _SRC_ZB + S5_WIDTH
_SRC_GB = _SRC_GA + D_MODEL
_SEGMENTS = ((_SRC_ZA, SSD_WIDTH), (_SRC_GA, D_MODEL), (_SRC_GB, D_MODEL), (_SRC_XBC, SSD_XBC),
             (_SRC_UB, S5_WIDTH), (_SRC_ZB, S5_WIDTH))
_SRC_TILES = tuple((start + k) // W_ALIGN for start, width in _SEGMENTS for k in range(0, width, PROJ_TN))


def _wprep_kernel(tbl_ref, w_ref, o_ref):
    del tbl_ref
    o_ref[...] = w_ref[...].astype(BF16)


def _wprep_dt_kernel(w_ref, o_ref):
    o_ref[...] = jnp.zeros(o_ref.shape, o_ref.dtype)
    o_ref[0:SSD_HEADS, :] = w_ref[...].astype(BF16)


def _wprep(wt):
    d = wt.shape[1]
    ntiles = len(_SRC_TILES)
    assert ntiles * PROJ_TN == PROJ_COLS and _SRC_DT % SSD_HEADS == 0
    w_main = pl.pallas_call(
        _wprep_kernel,
        grid_spec=pltpu.PrefetchScalarGridSpec(
            num_scalar_prefetch=1,
            grid=(ntiles,),
            in_specs=[pl.BlockSpec((pl.Element(PROJ_TN), pl.Element(d)),
                                   lambda j, tbl: (pl.multiple_of(tbl[j] * W_ALIGN, W_ALIGN), 0))],
            out_specs=pl.BlockSpec((PROJ_TN, d), lambda j, tbl: (j, 0)),
        ),
        out_shape=jax.ShapeDtypeStruct((PROJ_COLS, d), BF16),
        compiler_params=_params("arbitrary"),
        name="wprep",
    )(jnp.asarray(_SRC_TILES, jnp.int32), wt)
    w_dt = pl.pallas_call(
        _wprep_dt_kernel,
        grid=(1,),
        in_specs=[pl.BlockSpec((SSD_HEADS, d), lambda i: (_SRC_DT // SSD_HEADS, 0))],
        out_specs=pl.BlockSpec((LANES, d), lambda i: (0, 0)),
        out_shape=jax.ShapeDtypeStruct((LANES, d), BF16),
        name="wprep_dt",
    )(wt)
    return w_main, w_dt


def _inproj_kernel(x_ref, nw_ref, w_ref, wdt_ref, proj_ref, dt_ref, xn_ref):
    @pl.when(pl.program_id(1) == 0)
    def _():
        x = x_ref[...]
        ms = jnp.mean(x * x, axis=-1, keepdims=True)
        xn = ((x * lax.rsqrt(ms + NORM_EPS)) * nw_ref[...]).astype(BF16)
        xn_ref[...] = xn
        dt_ref[...] = lax.dot_general(xn, wdt_ref[...], _NT, preferred_element_type=F32)

    proj_ref[...] = lax.dot_general(xn_ref[...], w_ref[...], _NT, preferred_element_type=F32)


def _inproj(x, norm_w, w_main, w_dt, tm):
    m, d = x.shape
    n = w_main.shape[0]
    return pl.pallas_call(
        _inproj_kernel,
        grid=(m // tm, n // PROJ_TN),
        in_specs=[
            pl.BlockSpec((tm, d), lambda i, j: (i, 0)),
            pl.BlockSpec((1, d), lambda i, j: (0, 0)),
            pl.BlockSpec((PROJ_TN, d), lambda i, j: (j, 0)),
            pl.BlockSpec((LANES, d), lambda i, j: (0, 0)),
        ],
        out_specs=[
            pl.BlockSpec((tm, PROJ_TN), lambda i, j: (i, j)),
            pl.BlockSpec((tm, LANES), lambda i, j: (i, 0)),
        ],
        out_shape=[jax.ShapeDtypeStruct((m, n), F32), jax.ShapeDtypeStruct((m, LANES), F32)],
        scratch_shapes=[pltpu.VMEM((tm, d), BF16)],
        compiler_params=_params("arbitrary", "arbitrary"),
        name="inproj",
    )(x, norm_w, w_main, w_dt)


def _ssd_kernel(xbc_ref, za_ref, dtr_ref, ht0_ref, tail0_ref, convw_ref, convb_ref, dtb_ref, alog_ref,
                dexp_ref, nw_ref, y_ref, h_ref, ht_ref, tail_ref, ht_scr, ext_scr, *, mask_rows):
    c = pl.program_id(1)
    L = CHUNK
    P2 = 2 * SSD_HEAD_DIM
    GW = SSD_WIDTH // SSD_GROUPS

    @pl.when(c == 0)
    def _():
        ht_scr[...] = ht0_ref[...]
        ext_scr[0:SUBLANES, :] = tail0_ref[...]

    x = xbc_ref[...]
    ext_scr[SUBLANES:SUBLANES + L, :] = x
    w = convw_ref[...]
    conv = convb_ref[...] + ext_scr[SUBLANES - 3:SUBLANES - 3 + L, :] * w[0:1]
    conv = conv + ext_scr[SUBLANES - 2:SUBLANES - 2 + L, :] * w[1:2]
    conv = conv + ext_scr[SUBLANES - 1:SUBLANES - 1 + L, :] * w[2:3]
    conv = conv + x * w[3:4]
    ext_scr[0:SUBLANES, :] = x[L - SUBLANES:L, :]
    xbc = _silu(conv)
    xs = xbc[:, :SSD_WIDTH]
    bmat = xbc[:, SSD_WIDTH:SSD_WIDTH + SSD_GROUPS * SSD_STATE]
    cmat = xbc[:, SSD_WIDTH + SSD_GROUPS * SSD_STATE:]

    rows = lax.broadcasted_iota(jnp.int32, (L, L), 0)
    cols = lax.broadcasted_iota(jnp.int32, (L, L), 1)
    causal = rows >= cols
    lane_lo = cols < SSD_HEAD_DIM

    dt = _softplus(dtr_ref[...] + dtb_ref[...])
    if mask_rows:
        dt = jnp.where(rows < mask_rows, 0.0, dt)
    a = dt * (-jnp.exp(alog_ref[...]))
    tril = jnp.where(causal, 1.0, 0.0).astype(BF16)
    a1, a2, a3 = _split3(a)
    acum = (jnp.dot(tril, a1, preferred_element_type=F32)
            + jnp.dot(tril, a2, preferred_element_type=F32)
            + jnp.dot(tril, a3, preferred_element_type=F32))
    a2 = acum * LOG2E
    e_cum = jnp.exp2(a2)
    w_end = dt * jnp.exp2(a2[L - 1:L, :] - a2)
    a2dt_t = (a2 - jnp.log2(dt)).T

    dexp = dexp_ref[...]
    xs_b = xs.astype(BF16)
    y_parts = []
    for g in range(SSD_GROUPS):
        bg = bmat[:, g * SSD_STATE:(g + 1) * SSD_STATE].astype(BF16)
        cg = cmat[:, g * SSD_STATE:(g + 1) * SSD_STATE].astype(BF16)
        cb = lax.dot_general(cg, bg, (((1,), (1,)), ((), ())), preferred_element_type=F32)
        ht_g = ht_scr[:, g * GW:(g + 1) * GW]
        y_off = jnp.dot(cg, ht_g.astype(BF16), preferred_element_type=F32)
        xw_parts = []
        elast_parts = []
        for jj in range(GW // P2):
            lo = g * GW + jj * P2
            h0 = lo // SSD_HEAD_DIM
            yd, eb, wb = [], [], []
            for h in (h0, h0 + 1):
                colb = jnp.broadcast_to(a2[:, h:h + 1], (L, L))
                m = cb * jnp.exp2(jnp.where(causal, colb - a2dt_t[h:h + 1, :], -jnp.inf))
                yd.append(jnp.dot(m.astype(BF16), xs_b[:, lo:lo + P2], preferred_element_type=F32))
                eb.append(jnp.broadcast_to(e_cum[:, h:h + 1], (L, L)))
                wb.append(jnp.broadcast_to(w_end[:, h:h + 1], (L, L)))
            xs_pair = xs[:, lo:lo + P2]
            y_pair = (jnp.where(lane_lo, yd[0], yd[1])
                      + y_off[:, jj * P2:(jj + 1) * P2] * jnp.where(lane_lo, eb[0], eb[1]))
            y_parts.append(y_pair + dexp[:, lo:lo + P2] * xs_pair)
            xw_parts.append(xs_pair * jnp.where(lane_lo, wb[0], wb[1]))
            elast_parts.append(jnp.where(lane_lo[0:1, :], eb[0][L - 1:L, :], eb[1][L - 1:L, :]))
        xw = jnp.concatenate(xw_parts, axis=1)
        elast = jnp.concatenate(elast_parts, axis=1)
        st = lax.dot_general(bg, xw.astype(BF16), (((0,), (0,)), ((), ())),
                             preferred_element_type=F32)
        ht_scr[:, g * GW:(g + 1) * GW] = ht_g * elast + st

    y = jnp.concatenate(y_parts, axis=1)
    y = y * _silu(za_ref[...])
    nw = nw_ref[...]
    outs = []
    for g in range(SSD_GROUPS):
        yg = y[:, g * GW:(g + 1) * GW]
        ms = jnp.mean(yg * yg, axis=-1, keepdims=True)
        outs.append((yg * lax.rsqrt(ms + NORM_EPS)) * nw[:, g * GW:(g + 1) * GW])
    y_ref[...] = jnp.concatenate(outs, axis=1).astype(y_ref.dtype)

    @pl.when(c == pl.num_programs(1) - 1)
    def _():
        ht = ht_scr[...]
        ht_ref[...] = ht
        h_ref[...] = ht.T
        tail_ref[...] = x[L - SUBLANES:L, :]


def _ssd(proj, dt_raw, ht0, tail0, conv_w, conv_b, dt_bias, a_log, d_exp, norm_w, *, batch, nchunks,
         row_block0, mask_rows):
    L = CHUNK
    rows = batch * nchunks * L
    rb = lambda b, c: row_block0 + b * nchunks + c
    const2 = lambda shape: pl.BlockSpec(shape, lambda b, c: (0, 0))
    return pl.pallas_call(
        functools.partial(_ssd_kernel, mask_rows=mask_rows),
        grid=(batch, nchunks),
        in_specs=[
            pl.BlockSpec((L, SSD_XBC), lambda b, c: (rb(b, c), COL_XBC // SSD_XBC)),
            pl.BlockSpec((L, SSD_WIDTH), lambda b, c: (rb(b, c), COL_ZA // SSD_WIDTH)),
            pl.BlockSpec((L, LANES), lambda b, c: (rb(b, c), 0)),
            const2((SSD_STATE, SSD_WIDTH)),
            const2((SUBLANES, SSD_XBC)),
            const2((SSD_CONV, SSD_XBC)),
            const2((1, SSD_XBC)),
            const2((1, LANES)),
            const2((1, LANES)),
            const2((1, SSD_WIDTH)),
            const2((1, SSD_WIDTH)),
        ],
        out_specs=[
            pl.BlockSpec((L, SSD_WIDTH), lambda b, c: (b * nchunks + c, 0)),
            pl.BlockSpec((None, SSD_WIDTH, SSD_STATE), lambda b, c: (b, 0, 0)),
            pl.BlockSpec((None, SSD_STATE, SSD_WIDTH), lambda b, c: (b, 0, 0)),
            pl.BlockSpec((None, SUBLANES, SSD_XBC), lambda b, c: (b, 0, 0)),
        ],
        out_shape=[
            jax.ShapeDtypeStruct((rows, SSD_WIDTH), BF16),
            jax.ShapeDtypeStruct((batch, SSD_WIDTH, SSD_STATE), F32),
            jax.ShapeDtypeStruct((batch, SSD_STATE, SSD_WIDTH), F32),
            jax.ShapeDtypeStruct((batch, SUBLANES, SSD_XBC), F32),
        ],
        scratch_shapes=[pltpu.VMEM((SSD_STATE, SSD_WIDTH), F32),
                        pltpu.VMEM((SUBLANES + L, SSD_XBC), F32)],
        compiler_params=_params("arbitrary", "arbitrary"),
        name="ssd",
    )(proj, proj, dt_raw, ht0, tail0, conv_w, conv_b, dt_bias, a_log, d_exp, norm_w)


def _ssd_step_kernel(xbc_ref, za_ref, dtr_ref, cs_ref, h_ref, convw_ref, convb_ref, dtb_ref, alog_ref,
                     dexp_ref, nw_ref, y_ref, hout_ref, csout_ref):
    R = SUBLANES
    GW = SSD_WIDTH // SSD_GROUPS
    HPG = SSD_HEADS // SSD_GROUPS
    x = xbc_ref[...]
    w = convw_ref[...]
    s0 = cs_ref[:, 0:SSD_XBC]
    s1 = cs_ref[:, SSD_XBC:2 * SSD_XBC]
    s2 = cs_ref[:, 2 * SSD_XBC:3 * SSD_XBC]
    conv = convb_ref[...] + s0 * w[0:1]
    conv = conv + s1 * w[1:2]
    conv = conv + s2 * w[2:3]
    conv = conv + x * w[3:4]
    csout_ref[:, 0:SSD_XBC] = s1
    csout_ref[:, SSD_XBC:2 * SSD_XBC] = s2
    csout_ref[:, 2 * SSD_XBC:3 * SSD_XBC] = x
    xbc = _silu(conv)
    xs = xbc[:, :SSD_WIDTH]
    bmat = xbc[:, SSD_WIDTH:SSD_WIDTH + SSD_GROUPS * SSD_STATE]
    cmat = xbc[:, SSD_WIDTH + SSD_GROUPS * SSD_STATE:]

    dt = _softplus(dtr_ref[...] + dtb_ref[...])
    da = jnp.exp(dt * (-jnp.exp(alog_ref[...])))
    dt_t = dt.T
    da_t = da.T
    expand = lambda v: jnp.concatenate(
        [jnp.broadcast_to(v[h:h + 1, :], (SSD_HEAD_DIM, R)) for h in range(SSD_HEADS)], axis=0)
    xd_t = xs.T * expand(dt_t)
    da_te = expand(da_t)

    ycols = []
    for i in range(R):
        bexp = jnp.concatenate(
            [jnp.broadcast_to(bmat[i:i + 1, g * SSD_STATE:(g + 1) * SSD_STATE], (GW, SSD_STATE))
             for g in range(SSD_GROUPS)], axis=0)
        cexp = jnp.concatenate(
            [jnp.broadcast_to(cmat[i:i + 1, g * SSD_STATE:(g + 1) * SSD_STATE], (GW, SSD_STATE))
             for g in range(SSD_GROUPS)], axis=0)
        hn = h_ref[i] * da_te[:, i:i + 1] + xd_t[:, i:i + 1] * bexp
        hout_ref[i] = hn
        ycols.append(jnp.sum(hn * cexp, axis=1, keepdims=True))
    y = jnp.concatenate(ycols, axis=1).T
    y = y + dexp_ref[...] * xs
    y = y * _silu(za_ref[...])
    nw = nw_ref[...]
    outs = []
    for g in range(SSD_GROUPS):
        yg = y[:, g * GW:(g + 1) * GW]
        ms = jnp.mean(yg * yg, axis=-1, keepdims=True)
        outs.append((yg * lax.rsqrt(ms + NORM_EPS)) * nw[:, g * GW:(g + 1) * GW])
    y_ref[...] = jnp.concatenate(outs, axis=1).astype(y_ref.dtype)
    del HPG


def _ssd_step(proj, dt_raw, conv_state, ssd_state, conv_w, conv_b, dt_bias, a_log, d_exp, norm_w, *, nseq):
    R = SUBLANES
    const2 = lambda shape: pl.BlockSpec(shape, lambda i: (0, 0))
    return pl.pallas_call(
        _ssd_step_kernel,
        grid=(nseq // R,),
        in_specs=[
            pl.BlockSpec((R, SSD_XBC), lambda i: (i, COL_XBC // SSD_XBC)),
            pl.BlockSpec((R, SSD_WIDTH), lambda i: (i, COL_ZA // SSD_WIDTH)),
            pl.BlockSpec((R, LANES), lambda i: (i, 0)),
            pl.BlockSpec((R, (SSD_CONV - 1) * SSD_XBC), lambda i: (i, 0)),
            pl.BlockSpec((R, SSD_WIDTH, SSD_STATE), lambda i: (i, 0, 0)),
            const2((SSD_CONV, SSD_XBC)),
            const2((1, SSD_XBC)),
            const2((1, LANES)),
            const2((1, LANES)),
            const2((1, SSD_WIDTH)),
            const2((1, SSD_WIDTH)),
        ],
        out_specs=[
            pl.BlockSpec((R, SSD_WIDTH), lambda i: (i, 0)),
            pl.BlockSpec((R, SSD_WIDTH, SSD_STATE), lambda i: (i, 0, 0)),
            pl.BlockSpec((R, (SSD_CONV - 1) * SSD_XBC), lambda i: (i, 0)),
        ],
        out_shape=[
            jax.ShapeDtypeStruct((nseq, SSD_WIDTH), BF16),
            jax.ShapeDtypeStruct((nseq, SSD_WIDTH, SSD_STATE), F32),
            jax.ShapeDtypeStruct((nseq, (SSD_CONV - 1) * SSD_XBC), F32),
        ],
        compiler_params=_params("arbitrary"),
        name="ssd_step",
    )(proj, proj, dt_raw, conv_state, ssd_state, conv_w, conv_b, dt_bias, a_log, d_exp, norm_w)


def _s5_pitch(seg_len):
    return seg_len if (seg_len // SUBLANES) % 2 else seg_len + SUBLANES


def _s5_kernel(u_ref, h0r_ref, h0i_ref, btr_ref, bti_ref, ctr_ref, cti_ref, pr_ref, pi_ref, ar_ref, ai_ref,
               d_ref, y_ref, hr_ref, hi_ref, pad, perm, bur, bui, hb, *, seq):
    NS = SUBLANES
    SL = seq // NS
    pitch = _s5_pitch(SL)
    nk = S5_BLK_U // LANES
    n = S5_BLK_N

    for s in range(NS):
        for k in range(nk):
            pad[k, s * pitch:s * pitch + SL, :] = u_ref[s * SL:(s + 1) * SL, k * LANES:(k + 1) * LANES]

    def gather(j, c):
        r0 = pl.multiple_of(j * NS, NS)
        for k in range(nk):
            perm[pl.ds(r0, NS), k * LANES:(k + 1) * LANES] = pad[k, pl.ds(j, NS, stride=pitch), :]
        return c

    lax.fori_loop(0, SL, gather, 0, unroll=8)
    up = perm[...]
    ub = up.astype(BF16)
    bur[...] = jnp.dot(ub, btr_ref[...], preferred_element_type=F32)
    bui[...] = jnp.dot(ub, bti_ref[...], preferred_element_type=F32)

    ar = jnp.broadcast_to(ar_ref[...], (NS, n))
    ai = jnp.broadcast_to(ai_ref[...], (NS, n))

    def step(carry, r0):
        hr, hi = carry
        nr = ar * hr - ai * hi + bur[pl.ds(r0, NS), :]
        ni = ar * hi + ai * hr + bui[pl.ds(r0, NS), :]
        return nr, ni

    def scan_ends(j, carry):
        return step(carry, pl.multiple_of(j * NS, NS))

    zero = jnp.zeros((NS, n), F32)
    er, ei = lax.fori_loop(0, SL, scan_ends, (zero, zero), unroll=4)

    asr = pr_ref[SUBLANES - 1:SUBLANES, :]
    asi = pi_ref[SUBLANES - 1:SUBLANES, :]
    gr = [h0r_ref[...]]
    gi = [h0i_ref[...]]
    for s in range(NS):
        gr.append(er[s:s + 1, :] + (asr * gr[s] - asi * gi[s]))
        gi.append(ei[s:s + 1, :] + (asr * gi[s] + asi * gr[s]))
    hr_ref[...] = gr[NS]
    hi_ref[...] = gi[NS]

    def scan_store(jp, carry):
        r0 = pl.multiple_of(jp * 2 * NS, 2 * NS)
        c1 = step(carry, r0)
        c2 = step(c1, r0 + NS)
        hb[pl.ds(r0, 2 * NS), 0:n] = jnp.concatenate([c1[0], c2[0]], axis=0).astype(BF16)
        hb[pl.ds(r0, 2 * NS), n:2 * n] = jnp.concatenate([c1[1], c2[1]], axis=0).astype(BF16)
        return c2

    lax.fori_loop(0, SL // 2, scan_store,
                  (jnp.concatenate(gr[:NS], axis=0), jnp.concatenate(gi[:NS], axis=0)), unroll=2)

    y = (jnp.dot(hb[:, 0:n], ctr_ref[...], preferred_element_type=F32)
         - jnp.dot(hb[:, n:2 * n], cti_ref[...], preferred_element_type=F32))
    perm[...] = _gelu_tanh(y + d_ref[...] * up)

    def scatter(j, c):
        r0 = pl.multiple_of(j * NS, NS)
        for k in range(nk):
            pad[k, pl.ds(j, NS, stride=pitch), :] = perm[pl.ds(r0, NS), k * LANES:(k + 1) * LANES]
        return c

    lax.fori_loop(0, SL, scatter, 0, unroll=8)
    for s in range(NS):
        for k in range(nk):
            y_ref[s * SL:(s + 1) * SL, k * LANES:(k + 1) * LANES] = pad[k, s * pitch:s * pitch + SL, :]


def _s5(proj, h0r, h0i, bt_re, bt_im, ct_re, ct_im, pw_re, pw_im, ab_re, ab_im, d_s5, *, batch, seq, row_block0):
    ub0 = COL_UB // S5_BLK_U
    seg = seq // SUBLANES
    vec = lambda width: pl.BlockSpec((1, width), lambda b, j: (0, j))
    wspec = lambda r, c: pl.BlockSpec((None, r, c), lambda b, j: (j, 0, 0))
    pspec = pl.BlockSpec((SUBLANES, S5_BLK_N), lambda b, j: (seg // SUBLANES - 1, j))
    st_out = pl.BlockSpec((None, 1, S5_BLK_N), lambda b, j: (b, 0, j))
    return pl.pallas_call(
        functools.partial(_s5_kernel, seq=seq),
        grid=(batch, S5_BLOCKS),
        in_specs=[
            pl.BlockSpec((seq, S5_BLK_U), lambda b, j: (row_block0 + b, ub0 + j)),
            vec(S5_BLK_N), vec(S5_BLK_N),
            wspec(S5_BLK_U, S5_BLK_N), wspec(S5_BLK_U, S5_BLK_N),
            wspec(S5_BLK_N, S5_BLK_U), wspec(S5_BLK_N, S5_BLK_U),
            pspec, pspec,
            vec(S5_BLK_N), vec(S5_BLK_N), vec(S5_BLK_U),
        ],
        out_specs=[pl.BlockSpec((seq, S5_BLK_U), lambda b, j: (b, j)), st_out, st_out],
        out_shape=[
            jax.ShapeDtypeStruct((batch * seq, S5_WIDTH), F32),
            jax.ShapeDtypeStruct((batch, 1, S5_NSTATE), F32),
            jax.ShapeDtypeStruct((batch, 1, S5_NSTATE), F32),
        ],
        scratch_shapes=[
            pltpu.VMEM((S5_BLK_U // LANES, SUBLANES * _s5_pitch(seg), LANES), F32),
            pltpu.VMEM((seq, S5_BLK_U), F32),
            pltpu.VMEM((seq, S5_BLK_N), F32),
            pltpu.VMEM((seq, S5_BLK_N), F32),
            pltpu.VMEM((seq, 2 * S5_BLK_N), BF16),
        ],
        compiler_params=_params("arbitrary", "arbitrary"),
        name="s5",
    )(proj, h0r, h0i, bt_re, bt_im, ct_re, ct_im, pw_re, pw_im, ab_re, ab_im, d_s5)


def _s5_step_kernel(u_ref, h0r_ref, h0i_ref, btr_ref, bti_ref, ctr_ref, cti_ref, ar_ref, ai_ref, d_ref,
                    y_ref, hr_ref, hi_ref):
    u = u_ref[...]
    ub = u.astype(BF16)
    ar = ar_ref[...]
    ai = ai_ref[...]
    h0r = h0r_ref[...]
    h0i = h0i_ref[...]
    hr = jnp.dot(ub, btr_ref[...], preferred_element_type=F32) + (ar * h0r - ai * h0i)
    hi = jnp.dot(ub, bti_ref[...], preferred_element_type=F32) + (ar * h0i + ai * h0r)
    hr_ref[...] = hr
    hi_ref[...] = hi
    y = (jnp.dot(hr.astype(BF16), ctr_ref[...], preferred_element_type=F32)
         - jnp.dot(hi.astype(BF16), cti_ref[...], preferred_element_type=F32))
    y_ref[...] = _gelu_tanh(y + d_ref[...] * u)


def _s5_step(proj, h0r, h0i, bt_re, bt_im, ct_re, ct_im, ab_re, ab_im, d_s5, *, nseq):
    ub0 = COL_UB // S5_BLK_U
    vec = lambda width: pl.BlockSpec((1, width), lambda j: (0, j))
    wspec = lambda r, c: pl.BlockSpec((None, r, c), lambda j: (j, 0, 0))
    st = pl.BlockSpec((nseq, S5_BLK_N), lambda j: (0, j))
    return pl.pallas_call(
        _s5_step_kernel,
        grid=(S5_BLOCKS,),
        in_specs=[
            pl.BlockSpec((nseq, S5_BLK_U), lambda j: (0, ub0 + j)),
            st, st,
            wspec(S5_BLK_U, S5_BLK_N), wspec(S5_BLK_U, S5_BLK_N),
            wspec(S5_BLK_N, S5_BLK_U), wspec(S5_BLK_N, S5_BLK_U),
            vec(S5_BLK_N), vec(S5_BLK_N), vec(S5_BLK_U),
        ],
        out_specs=[pl.BlockSpec((nseq, S5_BLK_U), lambda j: (0, j)), st, st],
        out_shape=[
            jax.ShapeDtypeStruct((nseq, S5_WIDTH), F32),
            jax.ShapeDtypeStruct((nseq, S5_NSTATE), F32),
            jax.ShapeDtypeStruct((nseq, S5_NSTATE), F32),
        ],
        compiler_params=_params("arbitrary"),
        name="s5_step",
    )(proj, h0r, h0i, bt_re, bt_im, ct_re, ct_im, ab_re, ab_im, d_s5)


def _tail_kernel(yn_ref, ybg_ref, zb_ref, ga_ref, gb_ref, x_ref, wpa_ref, wglu_ref, bglu_ref, wpb_ref,
                 wout_ref, fnw_ref, out_ref):
    ya = jnp.dot(yn_ref[...], wpa_ref[...], preferred_element_type=F32)
    yb = ybg_ref[...]
    glu = jnp.dot(yb.astype(BF16), wglu_ref[...], preferred_element_type=F32) + bglu_ref[...]
    yb = (yb * _sigmoid(glu)) * _silu(zb_ref[...])
    ybp = jnp.dot(yb.astype(BF16), wpb_ref[...], preferred_element_type=F32)
    mixed = _sigmoid(ga_ref[...]) * ya + _sigmoid(gb_ref[...]) * ybp
    o = x_ref[...] + jnp.dot(mixed.astype(BF16), wout_ref[...], preferred_element_type=F32)
    ms = jnp.mean(o * o, axis=-1, keepdims=True)
    out_ref[...] = (o * lax.rsqrt(ms + NORM_EPS)) * fnw_ref[...]


def _tail(yn, ybg, proj, x, w_proj_a, w_glu, b_glu, w_proj_b, w_out, final_norm_w, *, tm):
    m, d = x.shape
    resident = lambda shape: pl.BlockSpec(shape, lambda i: (0, 0), pipeline_mode=pl.Buffered(1))
    return pl.pallas_call(
        _tail_kernel,
        grid=(m // tm,),
        in_specs=[
            pl.BlockSpec((tm, SSD_WIDTH), lambda i: (i, 0)),
            pl.BlockSpec((tm, S5_WIDTH), lambda i: (i, 0)),
            pl.BlockSpec((tm, S5_WIDTH), lambda i: (i, COL_ZB // S5_WIDTH)),
            pl.BlockSpec((tm, d), lambda i: (i, COL_GA // D_MODEL)),
            pl.BlockSpec((tm, d), lambda i: (i, COL_GB // D_MODEL)),
            pl.BlockSpec((tm, d), lambda i: (i, 0)),
            resident((SSD_WIDTH, d)),
            resident((S5_WIDTH, S5_WIDTH)),
            resident((1, S5_WIDTH)),
            resident((S5_WIDTH, d)),
            resident((d, d)),
            resident((1, d)),
        ],
        out_specs=pl.BlockSpec((tm, d), lambda i: (i, 0)),
        out_shape=jax.ShapeDtypeStruct((m, d), F32),
        compiler_params=_params("arbitrary"),
        name="tail",
    )(yn, ybg, proj, proj, proj, x, w_proj_a, w_glu, b_glu, w_proj_b, w_out, final_norm_w)


def _block_diag16(w):
    g, r, c = w.shape
    nb = g // 16
    eye = jnp.eye(16, dtype=w.dtype)
    out = jnp.einsum("jgrc,gh->jgrhc", w.reshape(nb, 16, r, c), eye)
    return out.reshape(nb, 16 * r, 16 * c)


def kernel(x_prompt, x_sample, state_ssd, state_conv, state_s5_re, state_s5_im, meta_tokens, norm_w, w_in,
           conv_w, conv_b, dt_bias, a_log, d_ssd, ssd_norm_w, w_proj_a, lam_re, lam_im, log_dt_s5, b_re, b_im,
           c_re, c_im, d_s5, w_glu, b_glu, w_proj_b, w_out, final_norm_w):
    bsz, seq, d = x_prompt.shape
    nseq = x_sample.shape[0]
    assert d == D_MODEL and seq % CHUNK == 0 and nseq % SUBLANES == 0 and norm_w.shape[0] == 1
    assert meta_tokens.shape[0] == N_META and N_META <= CHUNK

    assert w_in.shape[2] == _SRC_GB + D_MODEL
    w_main, w_dt = _wprep(jnp.transpose(w_in[0]))
    pad_heads = lambda v: jnp.pad(v.reshape(1, SSD_HEADS), ((0, 0), (0, LANES - SSD_HEADS)))
    dtb = pad_heads(dt_bias[0])
    alog = pad_heads(a_log[0])
    d_exp = jnp.repeat(d_ssd[0], SSD_HEAD_DIM).reshape(1, SSD_WIDTH)
    nw1 = norm_w[0].reshape(1, d)
    ssd_nw = ssd_norm_w[0].reshape(1, SSD_WIDTH)
    convw = conv_w[0]
    convb = conv_b[0].reshape(1, SSD_XBC)
    wpa = w_proj_a[0].astype(BF16)
    wglu = w_glu[0].astype(BF16)
    bglu = b_glu[0].reshape(1, S5_WIDTH)
    wpb = w_proj_b[0].astype(BF16)
    wout = w_out[0].astype(BF16)
    fnw = final_norm_w.reshape(1, d)
    ds5 = d_s5[0].reshape(1, S5_WIDTH)

    ab_re, ab_im, bbt_re, bbt_im = _s5prep(
        lam_re[0], lam_im[0], log_dt_s5[0].reshape(S5_GROUPS, 1),
        jnp.transpose(b_re[0], (0, 2, 1)), jnp.transpose(b_im[0], (0, 2, 1)))
    ab_re = ab_re.reshape(1, S5_NSTATE)
    ab_im = ab_im.reshape(1, S5_NSTATE)
    bt_re = _block_diag16(bbt_re).astype(BF16)
    bt_im = _block_diag16(bbt_im).astype(BF16)
    ct_re = _block_diag16(jnp.transpose(c_re[0], (0, 2, 1))).astype(BF16)
    ct_im = _block_diag16(jnp.transpose(c_im[0], (0, 2, 1))).astype(BF16)

    x_main = x_prompt.reshape(bsz * seq, d)
    x_small = jnp.concatenate(
        [x_sample.reshape(nseq, d), jnp.zeros((CHUNK - N_META, d), x_prompt.dtype),
         meta_tokens.astype(x_prompt.dtype)], axis=0)
    assert nseq % CHUNK == 0
    meta_blk = nseq // CHUNK
    tm_main = 1024 if (bsz * seq) % 1024 == 0 else CHUNK
    proj_m, dt_m = _inproj(x_main, nw1, w_main, w_dt, tm=tm_main)
    proj_s, dt_s = _inproj(x_small, nw1, w_main, w_dt, tm=nseq + CHUNK)

    ssd_args = (convw, convb, dtb, alog, d_exp, ssd_nw)
    pw_re, pw_im = _s5pow(ab_re, ab_im, max(seq, CHUNK) // SUBLANES)
    s5_w = (bt_re, bt_im, ct_re, ct_im)
    s5_v = (ab_re, ab_im, ds5)

    zeros_ht = jnp.zeros((SSD_STATE, SSD_WIDTH), F32)
    zeros_tail = jnp.zeros((SUBLANES, SSD_XBC), F32)
    zeros_s5 = jnp.zeros((1, S5_NSTATE), F32)
    _, _, ht_meta, tail_meta = _ssd(proj_s, dt_s, zeros_ht, zeros_tail, *ssd_args, batch=1, nchunks=1,
                                    row_block0=meta_blk, mask_rows=CHUNK - N_META)
    _, s5r_meta, s5i_meta = _s5(proj_s, zeros_s5, zeros_s5, *s5_w, pw_re, pw_im, *s5_v, batch=1, seq=CHUNK,
                                row_block0=meta_blk)

    yn_m, h_m, _, tail_m = _ssd(proj_m, dt_m, ht_meta[0], tail_meta[0], *ssd_args, batch=bsz,
                                nchunks=seq // CHUNK, row_block0=0, mask_rows=0)
    ybg_m, s5r_m, s5i_m = _s5(proj_m, s5r_meta[0], s5i_meta[0], *s5_w, pw_re, pw_im, *s5_v, batch=bsz, seq=seq,
                              row_block0=0)
    tail_w = (wpa, wglu, bglu, wpb, wout, fnw)
    y_prompt = _tail(yn_m, ybg_m, proj_m, x_main, *tail_w, tm=256)

    yn_s, h_s, cs_s = _ssd_step(proj_s, dt_s, state_conv[0].reshape(nseq, (SSD_CONV - 1) * SSD_XBC),
                                state_ssd[0].reshape(nseq, SSD_WIDTH, SSD_STATE), *ssd_args, nseq=nseq)
    ybg_s, s5r_s, s5i_s = _s5_step(proj_s, state_s5_re[0].reshape(nseq, S5_NSTATE),
                                   state_s5_im[0].reshape(nseq, S5_NSTATE), *s5_w, *s5_v, nseq=nseq)
    y_sample = _tail(yn_s, ybg_s, proj_s, x_sample.reshape(nseq, d), *tail_w, tm=nseq)

    dt_out = x_prompt.dtype
    return (
        y_prompt.reshape(bsz, seq, d),
        y_sample.reshape(nseq, 1, d),
        h_m.reshape(1, bsz, SSD_HEADS, SSD_HEAD_DIM, SSD_STATE).astype(dt_out),
        tail_m[:, SUBLANES - (SSD_CONV - 1):, :].reshape(1, bsz, SSD_CONV - 1, SSD_XBC),
        s5r_m.reshape(1, bsz, S5_GROUPS, S5_STATE).astype(dt_out),
        s5i_m.reshape(1, bsz, S5_GROUPS, S5_STATE).astype(dt_out),
        h_s.reshape(1, nseq, SSD_HEADS, SSD_HEAD_DIM, SSD_STATE).astype(dt_out),
        cs_s.reshape(1, nseq, SSD_CONV - 1, SSD_XBC),
        s5r_s.reshape(1, nseq, S5_GROUPS, S5_STATE).astype(dt_out),
        s5i_s.reshape(1, nseq, S5_GROUPS, S5_STATE).astype(dt_out),
    )
```

```python
import functools

import jax
import jax.numpy as jnp
from jax import lax
from jax.experimental import pallas as pl
from jax.experimental.pallas import tpu as pltpu

F32 = jnp.float32
BF16 = jnp.bfloat16

NORM_EPS = 1e-5
LOG2E = 1.4426950408889634
N_META = 16
D_MODEL = 2048
SSD_HEAD_DIM = 64
SSD_HEADS = 32
SSD_GROUPS = 4
SSD_STATE = 128
SSD_WIDTH = SSD_HEADS * SSD_HEAD_DIM
SSD_XBC = SSD_WIDTH + 2 * SSD_GROUPS * SSD_STATE
SSD_CONV = 4
CHUNK = 128
S5_WIDTH = D_MODEL // 2
S5_GROUP = 16
S5_GROUPS = S5_WIDTH // S5_GROUP
S5_STATE = 64
S5_NSTATE = S5_GROUPS * S5_STATE
S5_BLOCKS = 4
S5_BLK_U = S5_WIDTH // S5_BLOCKS
S5_BLK_N = S5_NSTATE // S5_BLOCKS

LANES = 128
SUBLANES = 8
VMEM_LIMIT = 56 * 1024 * 1024
CONV_PITCH = 3

COL_ZA = 0
COL_GA = 2048
COL_GB = 4096
COL_XBC = 6144
COL_UB = 9216
COL_ZB = 10240
PROJ_COLS = 11264
PROJ_TN = 1024


def _sigmoid(x):
    return 0.5 * jnp.tanh(0.5 * x) + 0.5


def _silu(x):
    h = 0.5 * x
    return h + h * jnp.tanh(h)


def _softplus(x):
    return jnp.maximum(x, 0.0) + jnp.log1p(jnp.exp(-jnp.abs(x)))


def _gelu_tanh(x):
    c = 0.7978845608028654
    return 0.5 * x * (1.0 + jnp.tanh(c * (x + 0.044715 * (x * x * x))))


def _split3(x):
    x1 = x.astype(BF16)
    r1 = x - x1.astype(F32)
    x2 = r1.astype(BF16)
    x3 = (r1 - x2.astype(F32)).astype(BF16)
    return x1, x2, x3


def _params(*sem):
    return pltpu.CompilerParams(dimension_semantics=sem, vmem_limit_bytes=VMEM_LIMIT)


def _s5prep_kernel(lr_ref, li_ref, ldt_ref, btr_ref, bti_ref, abr_ref, abi_ref, bbr_ref, bbi_ref):
    lr = lr_ref[...]
    li = li_ref[...]
    step = jnp.exp(ldt_ref[...])
    mag = jnp.exp(lr * step)
    abr = mag * jnp.cos(li * step)
    abi = mag * jnp.sin(li * step)
    den = lr * lr + li * li
    numr = abr - 1.0
    cr = (numr * lr + abi * li) / den
    ci = (abi * lr - numr * li) / den
    abr_ref[...] = abr
    abi_ref[...] = abi
    btr = btr_ref[...]
    bti = bti_ref[...]
    crb = cr[:, None, :]
    cib = ci[:, None, :]
    bbr_ref[...] = crb * btr - cib * bti
    bbi_ref[...] = crb * bti + cib * btr


def _s5prep(lam_re, lam_im, log_dt, bt_re, bt_im):
    g, n = lam_re.shape
    full2 = pl.BlockSpec((g, n), lambda: (0, 0))
    full3 = pl.BlockSpec((g, S5_GROUP, n), lambda: (0, 0, 0))
    return pl.pallas_call(
        _s5prep_kernel,
        in_specs=[full2, full2, pl.BlockSpec((g, 1), lambda: (0, 0)), full3, full3],
        out_specs=[full2, full2, full3, full3],
        out_shape=[jax.ShapeDtypeStruct((g, n), F32)] * 2
        + [jax.ShapeDtypeStruct((g, S5_GROUP, n), F32)] * 2,
        name="s5prep",
    )(lam_re, lam_im, log_dt, bt_re, bt_im)


_NT = (((1,), (1,)), ((), ()))

W_ALIGN = 32
_SRC_ZA = 0
_SRC_XBC = SSD_WIDTH
_SRC_DT = _SRC_XBC + SSD_XBC
_SRC_UB = _SRC_DT + SSD_HEADS
_SRC_ZB = _SRC_UB + S5_WIDTH
_SRC_GA = _SRC_ZB + S5_WIDTH
_SRC_GB = _SRC_GA + D_MODEL
_SEGMENTS = ((_SRC_ZA, SSD_WIDTH), (_SRC_GA, D_MODEL), (_SRC_GB, D_MODEL), (_SRC_XBC, SSD_XBC),
             (_SRC_UB, S5_WIDTH), (_SRC_ZB, S5_WIDTH))
_SRC_TILES = tuple((start + k) // W_ALIGN for start, width in _SEGMENTS for k in range(0, width, PROJ_TN))


def _wprep_kernel(tbl_ref, w_ref, o_ref):
    del tbl_ref
    o_ref[...] = w_ref[...].astype(BF16)


def _wprep_dt_kernel(w_ref, o_ref):
    o_ref[...] = jnp.zeros(o_ref.shape, o_ref.dtype)
    o_ref[0:SSD_HEADS, :] = w_ref[...].astype(BF16)


def _wprep(wt):
    d = wt.shape[1]
    ntiles = len(_SRC_TILES)
    assert ntiles * PROJ_TN == PROJ_COLS and _SRC_DT % SSD_HEADS == 0
    w_main = pl.pallas_call(
        _wprep_kernel,
        grid_spec=pltpu.PrefetchScalarGridSpec(
            num_scalar_prefetch=1,
            grid=(ntiles,),
            in_specs=[pl.BlockSpec((pl.Element(PROJ_TN), pl.Element(d)),
                                   lambda j, tbl: (pl.multiple_of(tbl[j] * W_ALIGN, W_ALIGN), 0))],
            out_specs=pl.BlockSpec((PROJ_TN, d), lambda j, tbl: (j, 0)),
        ),
        out_shape=jax.ShapeDtypeStruct((PROJ_COLS, d), BF16),
        compiler_params=_params("arbitrary"),
        name="wprep",
    )(jnp.asarray(_SRC_TILES, jnp.int32), wt)
    w_dt = pl.pallas_call(
        _wprep_dt_kernel,
        grid=(1,),
        in_specs=[pl.BlockSpec((SSD_HEADS, d), lambda i: (_SRC_DT // SSD_HEADS, 0))],
        out_specs=pl.BlockSpec((LANES, d), lambda i: (0, 0)),
        out_shape=jax.ShapeDtypeStruct((LANES, d), BF16),
        name="wprep_dt",
    )(wt)
    return w_main, w_dt


def _inproj_kernel(x_ref, nw_ref, w_ref, wdt_ref, proj_ref, dt_ref, xn_ref):
    @pl.when(pl.program_id(1) == 0)
    def _():
        x = x_ref[...]
        ms = jnp.mean(x * x, axis=-1, keepdims=True)
        xn = ((x * lax.rsqrt(ms + NORM_EPS)) * nw_ref[...]).astype(BF16)
        xn_ref[...] = xn
        dt_ref[...] = lax.dot_general(xn, wdt_ref[...], _NT, preferred_element_type=F32)

    proj_ref[...] = lax.dot_general(xn_ref[...], w_ref[...], _NT, preferred_element_type=F32)


def _inproj(x, norm_w, w_main, w_dt, tm):
    m, d = x.shape
    n = w_main.shape[0]
    return pl.pallas_call(
        _inproj_kernel,
        grid=(m // tm, n // PROJ_TN),
        in_specs=[
            pl.BlockSpec((tm, d), lambda i, j: (i, 0)),
            pl.BlockSpec((1, d), lambda i, j: (0, 0)),
            pl.BlockSpec((PROJ_TN, d), lambda i, j: (j, 0)),
            pl.BlockSpec((LANES, d), lambda i, j: (0, 0)),
        ],
        out_specs=[
            pl.BlockSpec((tm, PROJ_TN), lambda i, j: (i, j)),
            pl.BlockSpec((tm, LANES), lambda i, j: (i, 0)),
        ],
        out_shape=[jax.ShapeDtypeStruct((m, n), F32), jax.ShapeDtypeStruct((m, LANES), F32)],
        scratch_shapes=[pltpu.VMEM((tm, d), BF16)],
        compiler_params=_params("arbitrary", "arbitrary"),
        name="inproj",
    )(x, norm_w, w_main, w_dt)


def _ssd_kernel(xbc_ref, za_ref, dtr_ref, ht0_ref, tail0_ref, convw_ref, convb_ref, dtb_ref, alog_ref,
                dexp_ref, nw_ref, y_ref, h_ref, ht_ref, tail_ref, ht_scr, ext_scr, *, mask_rows):
    c = pl.program_id(1)
    L = CHUNK
    P2 = 2 * SSD_HEAD_DIM
    GW = SSD_WIDTH // SSD_GROUPS

    time_rows = lambda r0, n: pl.ds(CONV_PITCH * r0, n, stride=CONV_PITCH)

    @pl.when(c == 0)
    def _():
        ht_scr[...] = ht0_ref[...]
        for s in range(SSD_XBC // LANES):
            ext_scr[s, time_rows(0, SUBLANES), :] = tail0_ref[:, s * LANES:(s + 1) * LANES]

    w = convw_ref[...]
    bias = convb_ref[...]
    conv_parts = []
    for s in range(SSD_XBC // LANES):
        ls = slice(s * LANES, (s + 1) * LANES)
        xc = xbc_ref[:, ls]
        ext_scr[s, time_rows(SUBLANES, L), :] = xc
        acc = bias[:, ls]
        for k in range(SSD_CONV - 1):
            acc = acc + ext_scr[s, time_rows(SUBLANES - (SSD_CONV - 1) + k, L), :] * w[k:k + 1, ls]
        conv_parts.append(acc + xc * w[SSD_CONV - 1:SSD_CONV, ls])
        ext_scr[s, time_rows(0, SUBLANES), :] = xc[L - SUBLANES:L, :]
    xbc = _silu(jnp.concatenate(conv_parts, axis=1))
    xs = xbc[:, :SSD_WIDTH]
    bmat = xbc[:, SSD_WIDTH:SSD_WIDTH + SSD_GROUPS * SSD_STATE]
    cmat = xbc[:, SSD_WIDTH + SSD_GROUPS * SSD_STATE:]

    rows = lax.broadcasted_iota(jnp.int32, (L, L), 0)
    cols = lax.broadcasted_iota(jnp.int32, (L, L), 1)
    causal = rows >= cols
    lane_lo = cols < SSD_HEAD_DIM

    dt = _softplus(dtr_ref[...] + dtb_ref[...])
    if mask_rows:
        dt = jnp.where(rows < mask_rows, 0.0, dt)
    a = dt * (-jnp.exp(alog_ref[...]))
    tril = jnp.where(causal, 1.0, 0.0).astype(BF16)
    a1, a2, a3 = _split3(a)
    acum = (jnp.dot(tril, a1, preferred_element_type=F32)
            + jnp.dot(tril, a2, preferred_element_type=F32)
            + jnp.dot(tril, a3, preferred_element_type=F32))
    a2 = acum * LOG2E
    e_cum = jnp.exp2(a2)
    w_end = dt * jnp.exp2(a2[L - 1:L, :] - a2)
    a2dt_t = (a2 - jnp.log2(dt)).T

    dexp = dexp_ref[...]
    xs_b = xs.astype(BF16)
    y_parts = []
    for g in range(SSD_GROUPS):
        bg = bmat[:, g * SSD_STATE:(g + 1) * SSD_STATE].astype(BF16)
        cg = cmat[:, g * SSD_STATE:(g + 1) * SSD_STATE].astype(BF16)
        cb = lax.dot_general(cg, bg, (((1,), (1,)), ((), ())), preferred_element_type=F32)
        ht_g = ht_scr[:, g * GW:(g + 1) * GW]
        y_off = jnp.dot(cg, ht_g.astype(BF16), preferred_element_type=F32)
        xw_parts = []
        elast_parts = []
        for jj in range(GW // P2):
            lo = g * GW + jj * P2
            h0 = lo // SSD_HEAD_DIM
            yd, eb, wb = [], [], []
            for h in (h0, h0 + 1):
                colb = jnp.broadcast_to(a2[:, h:h + 1], (L, L))
                m = cb * jnp.exp2(jnp.where(causal, colb - a2dt_t[h:h + 1, :], -jnp.inf))
                yd.append(jnp.dot(m.astype(BF16), xs_b[:, lo:lo + P2], preferred_element_type=F32))
                eb.append(jnp.broadcast_to(e_cum[:, h:h + 1], (L, L)))
                wb.append(jnp.broadcast_to(w_end[:, h:h + 1], (L, L)))
            xs_pair = xs[:, lo:lo + P2]
            y_pair = (jnp.where(lane_lo, yd[0], yd[1])
                      + y_off[:, jj * P2:(jj + 1) * P2] * jnp.where(lane_lo, eb[0], eb[1]))
            y_parts.append(y_pair + dexp[:, lo:lo + P2] * xs_pair)
            xw_parts.append(xs_pair * jnp.where(lane_lo, wb[0], wb[1]))
            elast_parts.append(jnp.where(lane_lo[0:1, :], eb[0][L - 1:L, :], eb[1][L - 1:L, :]))
        xw = jnp.concatenate(xw_parts, axis=1)
        elast = jnp.concatenate(elast_parts, axis=1)
        st = lax.dot_general(bg, xw.astype(BF16), (((0,), (0,)), ((), ())),
                             preferred_element_type=F32)
        ht_scr[:, g * GW:(g + 1) * GW] = ht_g * elast + st

    y = jnp.concatenate(y_parts, axis=1)
    y = y * _silu(za_ref[...])
    nw = nw_ref[...]
    outs = []
    for g in range(SSD_GROUPS):
        yg = y[:, g * GW:(g + 1) * GW]
        ms = jnp.mean(yg * yg, axis=-1, keepdims=True)
        outs.append((yg * lax.rsqrt(ms + NORM_EPS)) * nw[:, g * GW:(g + 1) * GW])
    y_ref[...] = jnp.concatenate(outs, axis=1).astype(y_ref.dtype)

    @pl.when(c == pl.num_programs(1) - 1)
    def _():
        ht = ht_scr[...]
        ht_ref[...] = ht
        h_ref[...] = ht.T
        tail_ref[...] = xbc_ref[L - SUBLANES:L, :]


def _ssd(proj, dt_raw, ht0, tail0, conv_w, conv_b, dt_bias, a_log, d_exp, norm_w, *, batch, nchunks,
         row_block0, mask_rows):
    L = CHUNK
    rows = batch * nchunks * L
    rb = lambda b, c: row_block0 + b * nchunks + c
    const2 = lambda shape: pl.BlockSpec(shape, lambda b, c: (0, 0))
    return pl.pallas_call(
        functools.partial(_ssd_kernel, mask_rows=mask_rows),
        grid=(batch, nchunks),
        in_specs=[
            pl.BlockSpec((L, SSD_XBC), lambda b, c: (rb(b, c), COL_XBC // SSD_XBC)),
            pl.BlockSpec((L, SSD_WIDTH), lambda b, c: (rb(b, c), COL_ZA // SSD_WIDTH)),
            pl.BlockSpec((L, LANES), lambda b, c: (rb(b, c), 0)),
            const2((SSD_STATE, SSD_WIDTH)),
            const2((SUBLANES, SSD_XBC)),
            const2((SSD_CONV, SSD_XBC)),
            const2((1, SSD_XBC)),
            const2((1, LANES)),
            const2((1, LANES)),
            const2((1, SSD_WIDTH)),
            const2((1, SSD_WIDTH)),
        ],
        out_specs=[
            pl.BlockSpec((L, SSD_WIDTH), lambda b, c: (b * nchunks + c, 0)),
            pl.BlockSpec((None, SSD_WIDTH, SSD_STATE), lambda b, c: (b, 0, 0)),
            pl.BlockSpec((None, SSD_STATE, SSD_WIDTH), lambda b, c: (b, 0, 0)),
            pl.BlockSpec((None, SUBLANES, SSD_XBC), lambda b, c: (b, 0, 0)),
        ],
        out_shape=[
            jax.ShapeDtypeStruct((rows, SSD_WIDTH), BF16),
            jax.ShapeDtypeStruct((batch, SSD_WIDTH, SSD_STATE), F32),
            jax.ShapeDtypeStruct((batch, SSD_STATE, SSD_WIDTH), F32),
            jax.ShapeDtypeStruct((batch, SUBLANES, SSD_XBC), F32),
        ],
        scratch_shapes=[pltpu.VMEM((SSD_STATE, SSD_WIDTH), F32),
                        pltpu.VMEM((SSD_XBC // LANES, CONV_PITCH * (SUBLANES + L), LANES), F32)],
        compiler_params=_params("arbitrary", "arbitrary"),
        name="ssd",
    )(proj, proj, dt_raw, ht0, tail0, conv_w, conv_b, dt_bias, a_log, d_exp, norm_w)


def _ssd_step_kernel(xbc_ref, za_ref, dtr_ref, cs_ref, h_ref, convw_ref, convb_ref, dtb_ref, alog_ref,
                     dexp_ref, nw_ref, y_ref, hout_ref, csout_ref):
    R = SUBLANES
    GW = SSD_WIDTH // SSD_GROUPS
    x = xbc_ref[...]
    w = convw_ref[...]
    s0 = cs_ref[:, 0:SSD_XBC]
    s1 = cs_ref[:, SSD_XBC:2 * SSD_XBC]
    s2 = cs_ref[:, 2 * SSD_XBC:3 * SSD_XBC]
    conv = convb_ref[...] + s0 * w[0:1]
    conv = conv + s1 * w[1:2]
    conv = conv + s2 * w[2:3]
    conv = conv + x * w[3:4]
    csout_ref[:, 0:SSD_XBC] = s1
    csout_ref[:, SSD_XBC:2 * SSD_XBC] = s2
    csout_ref[:, 2 * SSD_XBC:3 * SSD_XBC] = x
    xbc = _silu(conv)
    xs = xbc[:, :SSD_WIDTH]
    bmat = xbc[:, SSD_WIDTH:SSD_WIDTH + SSD_GROUPS * SSD_STATE]
    cmat = xbc[:, SSD_WIDTH + SSD_GROUPS * SSD_STATE:]

    dt = _softplus(dtr_ref[...] + dtb_ref[...])
    da = jnp.exp(dt * (-jnp.exp(alog_ref[...])))
    dt_t = dt.T
    da_t = da.T
    expand = lambda v: jnp.concatenate(
        [jnp.broadcast_to(v[h:h + 1, :], (SSD_HEAD_DIM, R)) for h in range(SSD_HEADS)], axis=0)
    xd_t = xs.T * expand(dt_t)
    da_te = expand(da_t)

    cmat_b = cmat.astype(BF16)
    yrows = []
    for i in range(R):
        bexp = jnp.concatenate(
            [jnp.broadcast_to(bmat[i:i + 1, g * SSD_STATE:(g + 1) * SSD_STATE], (GW, SSD_STATE))
             for g in range(SSD_GROUPS)], axis=0)
        hn = h_ref[i] * da_te[:, i:i + 1] + xd_t[:, i:i + 1] * bexp
        hout_ref[i] = hn
        cg = jnp.concatenate([cmat_b[i:i + 1, g * SSD_STATE:(g + 1) * SSD_STATE] for g in range(SSD_GROUPS)]
                             + [jnp.zeros((R - SSD_GROUPS, SSD_STATE), BF16)], axis=0)
        yg = lax.dot_general(cg, hn.astype(BF16), (((1,), (1,)), ((), ())), preferred_element_type=F32)
        yrows.append(jnp.concatenate([yg[g:g + 1, g * GW:(g + 1) * GW] for g in range(SSD_GROUPS)], axis=1))
    y = jnp.concatenate(yrows, axis=0)
    y = y + dexp_ref[...] * xs
    y = y * _silu(za_ref[...])
    nw = nw_ref[...]
    outs = []
    for g in range(SSD_GROUPS):
        yg = y[:, g * GW:(g + 1) * GW]
        ms = jnp.mean(yg * yg, axis=-1, keepdims=True)
        outs.append((yg * lax.rsqrt(ms + NORM_EPS)) * nw[:, g * GW:(g + 1) * GW])
    y_ref[...] = jnp.concatenate(outs, axis=1).astype(y_ref.dtype)


def _ssd_step(proj, dt_raw, conv_state, ssd_state, conv_w, conv_b, dt_bias, a_log, d_exp, norm_w, *, nseq):
    R = SUBLANES
    const2 = lambda shape: pl.BlockSpec(shape, lambda i: (0, 0))
    return pl.pallas_call(
        _ssd_step_kernel,
        grid=(nseq // R,),
        in_specs=[
            pl.BlockSpec((R, SSD_XBC), lambda i: (i, COL_XBC // SSD_XBC)),
            pl.BlockSpec((R, SSD_WIDTH), lambda i: (i, COL_ZA // SSD_WIDTH)),
            pl.BlockSpec((R, LANES), lambda i: (i, 0)),
            pl.BlockSpec((R, (SSD_CONV - 1) * SSD_XBC), lambda i: (i, 0)),
            pl.BlockSpec((R, SSD_WIDTH, SSD_STATE), lambda i: (i, 0, 0)),
            const2((SSD_CONV, SSD_XBC)),
            const2((1, SSD_XBC)),
            const2((1, LANES)),
            const2((1, LANES)),
            const2((1, SSD_WIDTH)),
            const2((1, SSD_WIDTH)),
        ],
        out_specs=[
            pl.BlockSpec((R, SSD_WIDTH), lambda i: (i, 0)),
            pl.BlockSpec((R, SSD_WIDTH, SSD_STATE), lambda i: (i, 0, 0)),
            pl.BlockSpec((R, (SSD_CONV - 1) * SSD_XBC), lambda i: (i, 0)),
        ],
        out_shape=[
            jax.ShapeDtypeStruct((nseq, SSD_WIDTH), BF16),
            jax.ShapeDtypeStruct((nseq, SSD_WIDTH, SSD_STATE), F32),
            jax.ShapeDtypeStruct((nseq, (SSD_CONV - 1) * SSD_XBC), F32),
        ],
        compiler_params=_params("arbitrary"),
        name="ssd_step",
    )(proj, proj, dt_raw, conv_state, ssd_state, conv_w, conv_b, dt_bias, a_log, d_exp, norm_w)


def _s5_block_diag(w, rows_per_group, cols_per_group):
    ngrp, r, lanes = w.shape
    width = ngrp * cols_per_group
    tiled = jnp.concatenate([w.reshape(ngrp * r, lanes)] * (width // lanes), axis=1)
    row_g = lax.broadcasted_iota(jnp.int32, tiled.shape, 0) // rows_per_group
    col_g = lax.broadcasted_iota(jnp.int32, tiled.shape, 1) // cols_per_group
    return jnp.where(row_g == col_g, tiled, 0.0).astype(BF16)


def _s5_weights(bbr_ref, bbi_ref, ctr_ref, cti_ref):
    return (_s5_block_diag(bbr_ref[...], S5_GROUP, S5_STATE), _s5_block_diag(bbi_ref[...], S5_GROUP, S5_STATE),
            _s5_block_diag(ctr_ref[...], S5_STATE, S5_GROUP), _s5_block_diag(cti_ref[...], S5_STATE, S5_GROUP))


def _s5_pitch(seg_len):
    return seg_len if (seg_len // SUBLANES) % 2 else seg_len + SUBLANES


def _s5_kernel(u_ref, h0r_ref, h0i_ref, bbr_ref, bbi_ref, ctr_ref, cti_ref, ar_ref, ai_ref,
               d_ref, y_ref, hr_ref, hi_ref, pad, perm, bur, bui, hb, *, seq):
    NS = SUBLANES
    SL = seq // NS
    pitch = _s5_pitch(SL)
    nk = S5_BLK_U // LANES
    n = S5_BLK_N
    bt_re, bt_im, ct_re, ct_im = _s5_weights(bbr_ref, bbi_ref, ctr_ref, cti_ref)

    for s in range(NS):
        for k in range(nk):
            pad[k, s * pitch:s * pitch + SL, :] = u_ref[s * SL:(s + 1) * SL, k * LANES:(k + 1) * LANES]

    def gather(j, c):
        r0 = pl.multiple_of(j * NS, NS)
        for k in range(nk):
            perm[pl.ds(r0, NS), k * LANES:(k + 1) * LANES] = pad[k, pl.ds(j, NS, stride=pitch), :]
        return c

    lax.fori_loop(0, SL, gather, 0, unroll=8)
    up = perm[...]
    ub = up.astype(BF16)
    bur[...] = jnp.dot(ub, bt_re, preferred_element_type=F32)
    bui[...] = jnp.dot(ub, bt_im, preferred_element_type=F32)

    ar = jnp.broadcast_to(ar_ref[...], (NS, n))
    ai = jnp.broadcast_to(ai_ref[...], (NS, n))

    def step(carry, r0):
        hr, hi = carry
        nr = ar * hr - ai * hi + bur[pl.ds(r0, NS), :]
        ni = ar * hi + ai * hr + bui[pl.ds(r0, NS), :]
        return nr, ni

    def scan_ends(j, carry):
        return step(carry, pl.multiple_of(j * NS, NS))

    zero = jnp.zeros((NS, n), F32)
    er, ei = lax.fori_loop(0, SL, scan_ends, (zero, zero), unroll=4)

    asr = ar_ref[...]
    asi = ai_ref[...]
    for _ in range(SL.bit_length() - 1):
        asr, asi = asr * asr - asi * asi, 2.0 * (asr * asi)
    gr = [h0r_ref[...]]
    gi = [h0i_ref[...]]
    for s in range(NS):
        gr.append(er[s:s + 1, :] + (asr * gr[s] - asi * gi[s]))
        gi.append(ei[s:s + 1, :] + (asr * gi[s] + asi * gr[s]))
    hr_ref[...] = gr[NS]
    hi_ref[...] = gi[NS]

    def scan_store(jp, carry):
        r0 = pl.multiple_of(jp * 2 * NS, 2 * NS)
        c1 = step(carry, r0)
        c2 = step(c1, r0 + NS)
        hb[pl.ds(r0, 2 * NS), 0:n] = jnp.concatenate([c1[0], c2[0]], axis=0).astype(BF16)
        hb[pl.ds(r0, 2 * NS), n:2 * n] = jnp.concatenate([c1[1], c2[1]], axis=0).astype(BF16)
        return c2

    lax.fori_loop(0, SL // 2, scan_store,
                  (jnp.concatenate(gr[:NS], axis=0), jnp.concatenate(gi[:NS], axis=0)), unroll=2)

    y = (jnp.dot(hb[:, 0:n], ct_re, preferred_element_type=F32)
         - jnp.dot(hb[:, n:2 * n], ct_im, preferred_element_type=F32))
    perm[...] = _gelu_tanh(y + d_ref[...] * up)

    def scatter(j, c):
        r0 = pl.multiple_of(j * NS, NS)
        for k in range(nk):
            pad[k, pl.ds(j, NS, stride=pitch), :] = perm[pl.ds(r0, NS), k * LANES:(k + 1) * LANES]
        return c

    lax.fori_loop(0, SL, scatter, 0, unroll=8)
    for s in range(NS):
        for k in range(nk):
            y_ref[s * SL:(s + 1) * SL, k * LANES:(k + 1) * LANES] = pad[k, s * pitch:s * pitch + SL, :]


def _s5(proj, h0r, h0i, bb_re, bb_im, ct_re, ct_im, ab_re, ab_im, d_s5, *, batch, seq, row_block0):
    ub0 = COL_UB // S5_BLK_U
    seg = seq // SUBLANES
    assert seg & (seg - 1) == 0
    gpb = S5_GROUPS // S5_BLOCKS
    vec = lambda width: pl.BlockSpec((1, width), lambda b, j: (0, j))
    wspec = lambda r: pl.BlockSpec((gpb, r, LANES), lambda b, j: (j, 0, 0))
    st_out = pl.BlockSpec((None, 1, S5_BLK_N), lambda b, j: (b, 0, j))
    return pl.pallas_call(
        functools.partial(_s5_kernel, seq=seq),
        grid=(batch, S5_BLOCKS),
        in_specs=[
            pl.BlockSpec((seq, S5_BLK_U), lambda b, j: (row_block0 + b, ub0 + j)),
            vec(S5_BLK_N), vec(S5_BLK_N),
            wspec(S5_GROUP), wspec(S5_GROUP), wspec(S5_STATE), wspec(S5_STATE),
            vec(S5_BLK_N), vec(S5_BLK_N), vec(S5_BLK_U),
        ],
        out_specs=[pl.BlockSpec((seq, S5_BLK_U), lambda b, j: (b, j)), st_out, st_out],
        out_shape=[
            jax.ShapeDtypeStruct((batch * seq, S5_WIDTH), F32),
            jax.ShapeDtypeStruct((batch, 1, S5_NSTATE), F32),
            jax.ShapeDtypeStruct((batch, 1, S5_NSTATE), F32),
        ],
        scratch_shapes=[
            pltpu.VMEM((S5_BLK_U // LANES, SUBLANES * _s5_pitch(seg), LANES), F32),
            pltpu.VMEM((seq, S5_BLK_U), F32),
            pltpu.VMEM((seq, S5_BLK_N), F32),
            pltpu.VMEM((seq, S5_BLK_N), F32),
            pltpu.VMEM((seq, 2 * S5_BLK_N), BF16),
        ],
        compiler_params=_params("arbitrary", "arbitrary"),
        name="s5",
    )(proj, h0r, h0i, bb_re, bb_im, ct_re, ct_im, ab_re, ab_im, d_s5)


def _s5_step_kernel(u_ref, h0r_ref, h0i_ref, bbr_ref, bbi_ref, ctr_ref, cti_ref, ar_ref, ai_ref, d_ref,
                    y_ref, hr_ref, hi_ref):
    bt_re, bt_im, ct_re, ct_im = _s5_weights(bbr_ref, bbi_ref, ctr_ref, cti_ref)
    u = u_ref[...]
    ub = u.astype(BF16)
    ar = ar_ref[...]
    ai = ai_ref[...]
    h0r = h0r_ref[...]
    h0i = h0i_ref[...]
    hr = jnp.dot(ub, bt_re, preferred_element_type=F32) + (ar * h0r - ai * h0i)
    hi = jnp.dot(ub, bt_im, preferred_element_type=F32) + (ar * h0i + ai * h0r)
    hr_ref[...] = hr
    hi_ref[...] = hi
    y = (jnp.dot(hr.astype(BF16), ct_re, preferred_element_type=F32)
         - jnp.dot(hi.astype(BF16), ct_im, preferred_element_type=F32))
    y_ref[...] = _gelu_tanh(y + d_ref[...] * u)


def _s5_step(proj, h0r, h0i, bb_re, bb_im, ct_re, ct_im, ab_re, ab_im, d_s5, *, nseq):
    ub0 = COL_UB // S5_BLK_U
    gpb = S5_GROUPS // S5_BLOCKS
    vec = lambda width: pl.BlockSpec((1, width), lambda j: (0, j))
    wspec = lambda r: pl.BlockSpec((gpb, r, LANES), lambda j: (j, 0, 0))
    st = pl.BlockSpec((nseq, S5_BLK_N), lambda j: (0, j))
    return pl.pallas_call(
        _s5_step_kernel,
        grid=(S5_BLOCKS,),
        in_specs=[
            pl.BlockSpec((nseq, S5_BLK_U), lambda j: (0, ub0 + j)),
            st, st,
            wspec(S5_GROUP), wspec(S5_GROUP), wspec(S5_STATE), wspec(S5_STATE),
            vec(S5_BLK_N), vec(S5_BLK_N), vec(S5_BLK_U),
        ],
        out_specs=[pl.BlockSpec((nseq, S5_BLK_U), lambda j: (0, j)), st, st],
        out_shape=[
            jax.ShapeDtypeStruct((nseq, S5_WIDTH), F32),
            jax.ShapeDtypeStruct((nseq, S5_NSTATE), F32),
            jax.ShapeDtypeStruct((nseq, S5_NSTATE), F32),
        ],
        compiler_params=_params("arbitrary"),
        name="s5_step",
    )(proj, h0r, h0i, bb_re, bb_im, ct_re, ct_im, ab_re, ab_im, d_s5)


def _tail_kernel(yn_ref, ybg_ref, zb_ref, ga_ref, gb_ref, x_ref, wpa_ref, wglu_ref, bglu_ref, wpb_ref,
                 wout_ref, fnw_ref, out_ref):
    ya = jnp.dot(yn_ref[...], wpa_ref[...], preferred_element_type=F32)
    yb = ybg_ref[...]
    glu = jnp.dot(yb.astype(BF16), wglu_ref[...], preferred_element_type=F32) + bglu_ref[...]
    yb = (yb * _sigmoid(glu)) * _silu(zb_ref[...])
    ybp = jnp.dot(yb.astype(BF16), wpb_ref[...], preferred_element_type=F32)
    mixed = _sigmoid(ga_ref[...]) * ya + _sigmoid(gb_ref[...]) * ybp
    o = x_ref[...] + jnp.dot(mixed.astype(BF16), wout_ref[...], preferred_element_type=F32)
    ms = jnp.mean(o * o, axis=-1, keepdims=True)
    out_ref[...] = (o * lax.rsqrt(ms + NORM_EPS)) * fnw_ref[...]


def _tail(yn, ybg, proj, x, w_proj_a, w_glu, b_glu, w_proj_b, w_out, final_norm_w, *, tm):
    m, d = x.shape
    resident = lambda shape: pl.BlockSpec(shape, lambda i: (0, 0), pipeline_mode=pl.Buffered(1))
    return pl.pallas_call(
        _tail_kernel,
        grid=(m // tm,),
        in_specs=[
            pl.BlockSpec((tm, SSD_WIDTH), lambda i: (i, 0)),
            pl.BlockSpec((tm, S5_WIDTH), lambda i: (i, 0)),
            pl.BlockSpec((tm, S5_WIDTH), lambda i: (i, COL_ZB // S5_WIDTH)),
            pl.BlockSpec((tm, d), lambda i: (i, COL_GA // D_MODEL)),
            pl.BlockSpec((tm, d), lambda i: (i, COL_GB // D_MODEL)),
            pl.BlockSpec((tm, d), lambda i: (i, 0)),
            resident((SSD_WIDTH, d)),
            resident((S5_WIDTH, S5_WIDTH)),
            resident((1, S5_WIDTH)),
            resident((S5_WIDTH, d)),
            resident((d, d)),
            resident((1, d)),
        ],
        out_specs=pl.BlockSpec((tm, d), lambda i: (i, 0)),
        out_shape=jax.ShapeDtypeStruct((m, d), F32),
        compiler_params=_params("arbitrary"),
        name="tail",
    )(yn, ybg, proj, proj, proj, x, w_proj_a, w_glu, b_glu, w_proj_b, w_out, final_norm_w)


def kernel(x_prompt, x_sample, state_ssd, state_conv, state_s5_re, state_s5_im, meta_tokens, norm_w, w_in,
           conv_w, conv_b, dt_bias, a_log, d_ssd, ssd_norm_w, w_proj_a, lam_re, lam_im, log_dt_s5, b_re, b_im,
           c_re, c_im, d_s5, w_glu, b_glu, w_proj_b, w_out, final_norm_w):
    bsz, seq, d = x_prompt.shape
    nseq = x_sample.shape[0]
    assert d == D_MODEL and seq % CHUNK == 0 and nseq % SUBLANES == 0 and norm_w.shape[0] == 1
    assert meta_tokens.shape[0] == N_META and N_META <= CHUNK

    assert w_in.shape[2] == _SRC_GB + D_MODEL
    w_main, w_dt = _wprep(jnp.transpose(w_in[0]))
    pad_heads = lambda v: jnp.pad(v.reshape(1, SSD_HEADS), ((0, 0), (0, LANES - SSD_HEADS)))
    dtb = pad_heads(dt_bias[0])
    alog = pad_heads(a_log[0])
    d_exp = jnp.repeat(d_ssd[0], SSD_HEAD_DIM).reshape(1, SSD_WIDTH)
    nw1 = norm_w[0].reshape(1, d)
    ssd_nw = ssd_norm_w[0].reshape(1, SSD_WIDTH)
    convw = conv_w[0]
    convb = conv_b[0].reshape(1, SSD_XBC)
    wpa = w_proj_a[0].astype(BF16)
    wglu = w_glu[0].astype(BF16)
    bglu = b_glu[0].reshape(1, S5_WIDTH)
    wpb = w_proj_b[0].astype(BF16)
    wout = w_out[0].astype(BF16)
    fnw = final_norm_w.reshape(1, d)
    ds5 = d_s5[0].reshape(1, S5_WIDTH)

    rep = lambda v, k: jnp.concatenate([v] * k, axis=-1)
    lane_rep = LANES // S5_STATE
    ab_re, ab_im, bb_re, bb_im = _s5prep(
        rep(lam_re[0], lane_rep), rep(lam_im[0], lane_rep), log_dt_s5[0].reshape(S5_GROUPS, 1),
        rep(jnp.transpose(b_re[0], (0, 2, 1)), lane_rep), rep(jnp.transpose(b_im[0], (0, 2, 1)), lane_rep))
    ab_re = ab_re[:, :S5_STATE].reshape(1, S5_NSTATE)
    ab_im = ab_im[:, :S5_STATE].reshape(1, S5_NSTATE)
    ct_re = rep(jnp.transpose(c_re[0], (0, 2, 1)), LANES // S5_GROUP)
    ct_im = rep(jnp.transpose(c_im[0], (0, 2, 1)), LANES // S5_GROUP)

    x_main = x_prompt.reshape(bsz * seq, d)
    x_small = jnp.concatenate(
        [x_sample.reshape(nseq, d), jnp.zeros((CHUNK - N_META, d), x_prompt.dtype),
         meta_tokens.astype(x_prompt.dtype)], axis=0)
    assert nseq % CHUNK == 0
    meta_blk = nseq // CHUNK
    tm_main = 1024 if (bsz * seq) % 1024 == 0 else CHUNK
    proj_m, dt_m = _inproj(x_main, nw1, w_main, w_dt, tm=tm_main)
    proj_s, dt_s = _inproj(x_small, nw1, w_main, w_dt, tm=nseq + CHUNK)

    ssd_args = (convw, convb, dtb, alog, d_exp, ssd_nw)
    s5_w = (bb_re, bb_im, ct_re, ct_im)
    s5_v = (ab_re, ab_im, ds5)

    zeros_ht = jnp.zeros((SSD_STATE, SSD_WIDTH), F32)
    zeros_tail = jnp.zeros((SUBLANES, SSD_XBC), F32)
    zeros_s5 = jnp.zeros((1, S5_NSTATE), F32)
    _, _, ht_meta, tail_meta = _ssd(proj_s, dt_s, zeros_ht, zeros_tail, *ssd_args, batch=1, nchunks=1,
                                    row_block0=meta_blk, mask_rows=CHUNK - N_META)
    _, s5r_meta, s5i_meta = _s5(proj_s, zeros_s5, zeros_s5, *s5_w, *s5_v, batch=1, seq=CHUNK,
                                row_block0=meta_blk)

    yn_m, h_m, _, tail_m = _ssd(proj_m, dt_m, ht_meta[0], tail_meta[0], *ssd_args, batch=bsz,
                                nchunks=seq // CHUNK, row_block0=0, mask_rows=0)
    ybg_m, s5r_m, s5i_m = _s5(proj_m, s5r_meta[0], s5i_meta[0], *s5_w, *s5_v, batch=bsz, seq=seq,
                              row_block0=0)
    tail_w = (wpa, wglu, bglu, wpb, wout, fnw)
    y_prompt = _tail(yn_m, ybg_m, proj_m, x_main, *tail_w, tm=256)

    yn_s, h_s, cs_s = _ssd_step(proj_s, dt_s, state_conv[0].reshape(nseq, (SSD_CONV - 1) * SSD_XBC),
                                state_ssd[0].reshape(nseq, SSD_WIDTH, SSD_STATE), *ssd_args, nseq=nseq)
    ybg_s, s5r_s, s5i_s = _s5_step(proj_s, state_s5_re[0].reshape(nseq, S5_NSTATE),
                                   state_s5_im[0].reshape(nseq, S5_NSTATE), *s5_w, *s5_v, nseq=nseq)
    y_sample = _tail(yn_s, ybg_s, proj_s, x_sample.reshape(nseq, d), *tail_w, tm=nseq)

    dt_out = x_prompt.dtype
    return (
        y_prompt.reshape(bsz, seq, d),
        y_sample.reshape(nseq, 1, d),
        h_m.reshape(1, bsz, SSD_HEADS, SSD_HEAD_DIM, SSD_STATE).astype(dt_out),
        tail_m[:, SUBLANES - (SSD_CONV - 1):, :].reshape(1, bsz, SSD_CONV - 1, SSD_XBC),
        s5r_m.reshape(1, bsz, S5_GROUPS, S5_STATE).astype(dt_out),
        s5i_m.reshape(1, bsz, S5_GROUPS, S5_STATE).astype(dt_out),
        h_s.reshape(1, nseq, SSD_HEADS, SSD_HEAD_DIM, SSD_STATE).astype(dt_out),
        cs_s.reshape(1, nseq, SSD_CONV - 1, SSD_XBC),
        s5r_s.reshape(1, nseq, S5_GROUPS, S5_STATE).astype(dt_out),
        s5i_s.reshape(1, nseq, S5_GROUPS, S5_STATE).astype(dt_out),
    )
```

```python
import functools

import jax
import jax.numpy as jnp
from jax import lax
from jax.experimental import pallas as pl
from jax.experimental.pallas import tpu as pltpu

F32 = jnp.float32
BF16 = jnp.bfloat16

NORM_EPS = 1e-5
LOG2E = 1.4426950408889634
N_META = 16
D_MODEL = 2048
SSD_HEAD_DIM = 64
SSD_HEADS = 32
SSD_GROUPS = 4
SSD_STATE = 128
SSD_WIDTH = SSD_HEADS * SSD_HEAD_DIM
SSD_XBC = SSD_WIDTH + 2 * SSD_GROUPS * SSD_STATE
SSD_CONV = 4
CHUNK = 128
S5_WIDTH = D_MODEL // 2
S5_GROUP = 16
S5_GROUPS = S5_WIDTH // S5_GROUP
S5_STATE = 64
S5_NSTATE = S5_GROUPS * S5_STATE
S5_BLOCKS = 4
S5_BLK_U = S5_WIDTH // S5_BLOCKS
S5_BLK_N = S5_NSTATE // S5_BLOCKS

LANES = 128
SUBLANES = 8
VMEM_LIMIT = 56 * 1024 * 1024
CONV_PITCH = 3

COL_ZA = 0
COL_GA = 2048
COL_GB = 4096
COL_XBC = 6144
COL_UB = 9216
COL_ZB = 10240
PROJ_COLS = 11264
PROJ_TN = 1024


_NT = (((1,), (1,)), ((), ()))


def _sigmoid(x):
    return 0.5 * jnp.tanh(0.5 * x) + 0.5


def _silu(x):
    h = 0.5 * x
    return h + h * jnp.tanh(h)


def _softplus(x):
    return jnp.maximum(x, 0.0) + jnp.log1p(jnp.exp(-jnp.abs(x)))


def _gelu_tanh(x):
    c = 0.7978845608028654
    return 0.5 * x * (1.0 + jnp.tanh(c * (x + 0.044715 * (x * x * x))))


def _split3(x):
    x1 = x.astype(BF16)
    r1 = x - x1.astype(F32)
    x2 = r1.astype(BF16)
    x3 = (r1 - x2.astype(F32)).astype(BF16)
    return x1, x2, x3


def _params(*sem):
    return pltpu.CompilerParams(dimension_semantics=sem, vmem_limit_bytes=VMEM_LIMIT)


def _s5prep_kernel(lr_ref, li_ref, ldt_ref, btr_ref, bti_ref, abr_ref, abi_ref, bbr_ref, bbi_ref):
    lr = lr_ref[...]
    li = li_ref[...]
    step = jnp.exp(ldt_ref[...])
    mag = jnp.exp(lr * step)
    abr = mag * jnp.cos(li * step)
    abi = mag * jnp.sin(li * step)
    den = lr * lr + li * li
    numr = abr - 1.0
    cr = (numr * lr + abi * li) / den
    ci = (abi * lr - numr * li) / den
    abr_ref[...] = abr
    abi_ref[...] = abi
    btr = btr_ref[...]
    bti = bti_ref[...]
    crb = cr[:, None, :]
    cib = ci[:, None, :]
    bbr_ref[...] = crb * btr - cib * bti
    bbi_ref[...] = crb * bti + cib * btr


def _s5prep(lam_re, lam_im, log_dt, bt_re, bt_im):
    g, n = lam_re.shape
    full2 = pl.BlockSpec((g, n), lambda: (0, 0))
    full3 = pl.BlockSpec((g, S5_GROUP, n), lambda: (0, 0, 0))
    return pl.pallas_call(
        _s5prep_kernel,
        in_specs=[full2, full2, pl.BlockSpec((g, 1), lambda: (0, 0)), full3, full3],
        out_specs=[full2, full2, full3, full3],
        out_shape=[jax.ShapeDtypeStruct((g, n), F32)] * 2
        + [jax.ShapeDtypeStruct((g, S5_GROUP, n), F32)] * 2,
        name="s5prep",
    )(lam_re, lam_im, log_dt, bt_re, bt_im)


W_ALIGN = 32
_SRC_ZA = 0
_SRC_XBC = SSD_WIDTH
_SRC_DT = _SRC_XBC + SSD_XBC
_SRC_UB = _SRC_DT + SSD_HEADS
_SRC_ZB = _SRC_UB + S5_WIDTH
_SRC_GA = _SRC_ZB + S5_WIDTH
_SRC_GB = _SRC_GA + D_MODEL
_SEGMENTS = ((_SRC_ZA, SSD_WIDTH), (_SRC_GA, D_MODEL), (_SRC_GB, D_MODEL), (_SRC_XBC, SSD_XBC),
             (_SRC_UB, S5_WIDTH), (_SRC_ZB, S5_WIDTH))
_SRC_TILES = tuple((start + k) // W_ALIGN for start, width in _SEGMENTS for k in range(0, width, PROJ_TN))


def _wprep_kernel(tbl_ref, w_ref, o_ref):
    del tbl_ref
    o_ref[...] = w_ref[...].astype(BF16)


def _wprep_dt_kernel(w_ref, o_ref):
    o_ref[...] = jnp.zeros(o_ref.shape, o_ref.dtype)
    o_ref[0:SSD_HEADS, :] = w_ref[...].astype(BF16)


def _wprep(wt):
    d = wt.shape[1]
    ntiles = len(_SRC_TILES)
    assert ntiles * PROJ_TN == PROJ_COLS and _SRC_DT % SSD_HEADS == 0
    w_main = pl.pallas_call(
        _wprep_kernel,
        grid_spec=pltpu.PrefetchScalarGridSpec(
            num_scalar_prefetch=1,
            grid=(ntiles,),
            in_specs=[pl.BlockSpec((pl.Element(PROJ_TN), pl.Element(d)),
                                   lambda j, tbl: (pl.multiple_of(tbl[j] * W_ALIGN, W_ALIGN), 0))],
            out_specs=pl.BlockSpec((PROJ_TN, d), lambda j, tbl: (j, 0)),
        ),
        out_shape=jax.ShapeDtypeStruct((PROJ_COLS, d), BF16),
        compiler_params=_params("arbitrary"),
        name="wprep",
    )(jnp.asarray(_SRC_TILES, jnp.int32), wt)
    w_dt = pl.pallas_call(
        _wprep_dt_kernel,
        grid=(1,),
        in_specs=[pl.BlockSpec((SSD_HEADS, d), lambda i: (_SRC_DT // SSD_HEADS, 0))],
        out_specs=pl.BlockSpec((LANES, d), lambda i: (0, 0)),
        out_shape=jax.ShapeDtypeStruct((LANES, d), BF16),
        name="wprep_dt",
    )(wt)
    return w_main, w_dt


def _inproj_kernel(x_ref, nw_ref, w_ref, wdt_ref, proj_ref, dt_ref, xn_ref):
    @pl.when(pl.program_id(1) == 0)
    def _():
        x = x_ref[...]
        ms = jnp.mean(x * x, axis=-1, keepdims=True)
        xn = ((x * lax.rsqrt(ms + NORM_EPS)) * nw_ref[...]).astype(BF16)
        xn_ref[...] = xn
        dt_ref[...] = lax.dot_general(xn, wdt_ref[...], _NT, preferred_element_type=F32)

    proj_ref[...] = lax.dot_general(xn_ref[...], w_ref[...], _NT, preferred_element_type=F32)


def _inproj(x, norm_w, w_main, w_dt, tm):
    m, d = x.shape
    n = w_main.shape[0]
    return pl.pallas_call(
        _inproj_kernel,
        grid=(m // tm, n // PROJ_TN),
        in_specs=[
            pl.BlockSpec((tm, d), lambda i, j: (i, 0)),
            pl.BlockSpec((1, d), lambda i, j: (0, 0)),
            pl.BlockSpec((PROJ_TN, d), lambda i, j: (j, 0)),
            pl.BlockSpec((LANES, d), lambda i, j: (0, 0)),
        ],
        out_specs=[
            pl.BlockSpec((tm, PROJ_TN), lambda i, j: (i, j)),
            pl.BlockSpec((tm, LANES), lambda i, j: (i, 0)),
        ],
        out_shape=[jax.ShapeDtypeStruct((m, n), F32), jax.ShapeDtypeStruct((m, LANES), F32)],
        scratch_shapes=[pltpu.VMEM((tm, d), BF16)],
        compiler_params=_params("arbitrary", "arbitrary"),
        name="inproj",
    )(x, norm_w, w_main, w_dt)


def _ssd_kernel(xbc_ref, za_ref, dtr_ref, ht0_ref, tail0_ref, convw_ref, convb_ref, dtb_ref, alog_ref,
                dexp_ref, nw_ref, y_ref, h_ref, ht_ref, tail_ref, ht_scr, ext_scr, *, mask_rows):
    c = pl.program_id(1)
    L = CHUNK
    P2 = 2 * SSD_HEAD_DIM
    GW = SSD_WIDTH // SSD_GROUPS

    time_rows = lambda r0, n: pl.ds(CONV_PITCH * r0, n, stride=CONV_PITCH)

    @pl.when(c == 0)
    def _():
        ht_scr[...] = ht0_ref[...]
        for s in range(SSD_XBC // LANES):
            ext_scr[s, time_rows(0, SUBLANES), :] = tail0_ref[:, s * LANES:(s + 1) * LANES]

    w = convw_ref[...]
    bias = convb_ref[...]
    conv_parts = []
    for s in range(SSD_XBC // LANES):
        ls = slice(s * LANES, (s + 1) * LANES)
        xc = xbc_ref[:, ls]
        ext_scr[s, time_rows(SUBLANES, L), :] = xc
        acc = bias[:, ls]
        for k in range(SSD_CONV - 1):
            acc = acc + ext_scr[s, time_rows(SUBLANES - (SSD_CONV - 1) + k, L), :] * w[k:k + 1, ls]
        conv_parts.append(acc + xc * w[SSD_CONV - 1:SSD_CONV, ls])
        ext_scr[s, time_rows(0, SUBLANES), :] = xc[L - SUBLANES:L, :]
    xbc = _silu(jnp.concatenate(conv_parts, axis=1))
    xs = xbc[:, :SSD_WIDTH]
    bmat = xbc[:, SSD_WIDTH:SSD_WIDTH + SSD_GROUPS * SSD_STATE]
    cmat = xbc[:, SSD_WIDTH + SSD_GROUPS * SSD_STATE:]

    rows = lax.broadcasted_iota(jnp.int32, (L, L), 0)
    cols = lax.broadcasted_iota(jnp.int32, (L, L), 1)
    causal = rows >= cols
    lane_lo = cols < SSD_HEAD_DIM

    dt = _softplus(dtr_ref[...] + dtb_ref[...])
    if mask_rows:
        dt = jnp.where(rows < mask_rows, 0.0, dt)
    a = dt * (-jnp.exp(alog_ref[...]))
    tril = jnp.where(causal, 1.0, 0.0).astype(BF16)
    a1, a2, a3 = _split3(a)
    acum = (jnp.dot(tril, a1, preferred_element_type=F32)
            + jnp.dot(tril, a2, preferred_element_type=F32)
            + jnp.dot(tril, a3, preferred_element_type=F32))
    a2 = acum * LOG2E
    e_cum = jnp.exp2(a2)
    w_end = dt * jnp.exp2(a2[L - 1:L, :] - a2)
    a2dt_t = (a2 - jnp.log2(dt)).T

    dexp = dexp_ref[...]
    xs_b = xs.astype(BF16)
    y_parts = []
    for g in range(SSD_GROUPS):
        bg = bmat[:, g * SSD_STATE:(g + 1) * SSD_STATE].astype(BF16)
        cg = cmat[:, g * SSD_STATE:(g + 1) * SSD_STATE].astype(BF16)
        cb = lax.dot_general(cg, bg, (((1,), (1,)), ((), ())), preferred_element_type=F32)
        ht_g = ht_scr[:, g * GW:(g + 1) * GW]
        y_off = jnp.dot(cg, ht_g.astype(BF16), preferred_element_type=F32)
        xw_parts = []
        elast_parts = []
        for jj in range(GW // P2):
            lo = g * GW + jj * P2
            h0 = lo // SSD_HEAD_DIM
            yd, eb, wb = [], [], []
            for h in (h0, h0 + 1):
                colb = jnp.broadcast_to(a2[:, h:h + 1], (L, L))
                m = cb * jnp.exp2(jnp.where(causal, colb - a2dt_t[h:h + 1, :], -jnp.inf))
                yd.append(jnp.dot(m.astype(BF16), xs_b[:, lo:lo + P2], preferred_element_type=F32))
                eb.append(jnp.broadcast_to(e_cum[:, h:h + 1], (L, L)))
                wb.append(jnp.broadcast_to(w_end[:, h:h + 1], (L, L)))
            xs_pair = xs[:, lo:lo + P2]
            y_pair = (jnp.where(lane_lo, yd[0], yd[1])
                      + y_off[:, jj * P2:(jj + 1) * P2] * jnp.where(lane_lo, eb[0], eb[1]))
            y_parts.append(y_pair + dexp[:, lo:lo + P2] * xs_pair)
            xw_parts.append(xs_pair * jnp.where(lane_lo, wb[0], wb[1]))
            elast_parts.append(jnp.where(lane_lo[0:1, :], eb[0][L - 1:L, :], eb[1][L - 1:L, :]))
        xw = jnp.concatenate(xw_parts, axis=1)
        elast = jnp.concatenate(elast_parts, axis=1)
        st = lax.dot_general(bg, xw.astype(BF16), (((0,), (0,)), ((), ())),
                             preferred_element_type=F32)
        ht_scr[:, g * GW:(g + 1) * GW] = ht_g * elast + st

    y = jnp.concatenate(y_parts, axis=1)
    y = y * _silu(za_ref[...])
    nw = nw_ref[...]
    outs = []
    for g in range(SSD_GROUPS):
        yg = y[:, g * GW:(g + 1) * GW]
        ms = jnp.mean(yg * yg, axis=-1, keepdims=True)
        outs.append((yg * lax.rsqrt(ms + NORM_EPS)) * nw[:, g * GW:(g + 1) * GW])
    y_ref[...] = jnp.concatenate(outs, axis=1).astype(y_ref.dtype)

    @pl.when(c == pl.num_programs(1) - 1)
    def _():
        ht = ht_scr[...]
        ht_ref[...] = ht
        h_ref[...] = ht.T
        tail_ref[...] = xbc_ref[L - SUBLANES:L, :]


def _ssd(proj, dt_raw, ht0, tail0, conv_w, conv_b, dt_bias, a_log, d_exp, norm_w, *, batch, nchunks,
         row_block0, mask_rows):
    L = CHUNK
    rows = batch * nchunks * L
    rb = lambda b, c: row_block0 + b * nchunks + c
    const2 = lambda shape: pl.BlockSpec(shape, lambda b, c: (0, 0))
    return pl.pallas_call(
        functools.partial(_ssd_kernel, mask_rows=mask_rows),
        grid=(batch, nchunks),
        in_specs=[
            pl.BlockSpec((L, SSD_XBC), lambda b, c: (rb(b, c), COL_XBC // SSD_XBC)),
            pl.BlockSpec((L, SSD_WIDTH), lambda b, c: (rb(b, c), COL_ZA // SSD_WIDTH)),
            pl.BlockSpec((L, LANES), lambda b, c: (rb(b, c), 0)),
            const2((SSD_STATE, SSD_WIDTH)),
            const2((SUBLANES, SSD_XBC)),
            const2((SSD_CONV, SSD_XBC)),
            const2((1, SSD_XBC)),
            const2((1, LANES)),
            const2((1, LANES)),
            const2((1, SSD_WIDTH)),
            const2((1, SSD_WIDTH)),
        ],
        out_specs=[
            pl.BlockSpec((L, SSD_WIDTH), lambda b, c: (b * nchunks + c, 0)),
            pl.BlockSpec((None, SSD_WIDTH, SSD_STATE), lambda b, c: (b, 0, 0)),
            pl.BlockSpec((None, SSD_STATE, SSD_WIDTH), lambda b, c: (b, 0, 0)),
            pl.BlockSpec((None, SUBLANES, SSD_XBC), lambda b, c: (b, 0, 0)),
        ],
        out_shape=[
            jax.ShapeDtypeStruct((rows, SSD_WIDTH), BF16),
            jax.ShapeDtypeStruct((batch, SSD_WIDTH, SSD_STATE), F32),
            jax.ShapeDtypeStruct((batch, SSD_STATE, SSD_WIDTH), F32),
            jax.ShapeDtypeStruct((batch, SUBLANES, SSD_XBC), F32),
        ],
        scratch_shapes=[pltpu.VMEM((SSD_STATE, SSD_WIDTH), F32),
                        pltpu.VMEM((SSD_XBC // LANES, CONV_PITCH * (SUBLANES + L), LANES), F32)],
        compiler_params=_params("arbitrary", "arbitrary"),
        name="ssd",
    )(proj, proj, dt_raw, ht0, tail0, conv_w, conv_b, dt_bias, a_log, d_exp, norm_w)


def _ssd_step_kernel(xbc_ref, za_ref, dtr_ref, cs_ref, h_ref, convw_ref, convb_ref, dtb_ref, alog_ref,
                     dexp_ref, nw_ref, y_ref, hout_ref, csout_ref):
    R = SUBLANES
    GW = SSD_WIDTH // SSD_GROUPS
    x = xbc_ref[...]
    w = convw_ref[...]
    s0 = cs_ref[0]
    s1 = cs_ref[1]
    s2 = cs_ref[2]
    conv = convb_ref[...] + s0 * w[0:1]
    conv = conv + s1 * w[1:2]
    conv = conv + s2 * w[2:3]
    conv = conv + x * w[3:4]
    csout_ref[0] = s1
    csout_ref[1] = s2
    csout_ref[2] = x
    xbc = _silu(conv)
    xs = xbc[:, :SSD_WIDTH]
    bmat = xbc[:, SSD_WIDTH:SSD_WIDTH + SSD_GROUPS * SSD_STATE]
    cmat = xbc[:, SSD_WIDTH + SSD_GROUPS * SSD_STATE:]

    dt = _softplus(dtr_ref[...] + dtb_ref[...])
    da = jnp.exp(dt * (-jnp.exp(alog_ref[...])))
    dt_t = dt.T
    da_t = da.T
    expand = lambda v: jnp.concatenate(
        [jnp.broadcast_to(v[h:h + 1, :], (SSD_HEAD_DIM, R)) for h in range(SSD_HEADS)], axis=0)
    xd_t = xs.T * expand(dt_t)
    da_te = expand(da_t)

    cmat_b = cmat.astype(BF16)
    yrows = []
    for i in range(R):
        bexp = jnp.concatenate(
            [jnp.broadcast_to(bmat[i:i + 1, g * SSD_STATE:(g + 1) * SSD_STATE], (GW, SSD_STATE))
             for g in range(SSD_GROUPS)], axis=0)
        hn = h_ref[i] * da_te[:, i:i + 1] + xd_t[:, i:i + 1] * bexp
        hout_ref[i] = hn
        cg = jnp.concatenate([cmat_b[i:i + 1, g * SSD_STATE:(g + 1) * SSD_STATE] for g in range(SSD_GROUPS)]
                             + [jnp.zeros((R - SSD_GROUPS, SSD_STATE), BF16)], axis=0)
        yg = lax.dot_general(cg, hn.astype(BF16), (((1,), (1,)), ((), ())), preferred_element_type=F32)
        yrows.append(jnp.concatenate([yg[g:g + 1, g * GW:(g + 1) * GW] for g in range(SSD_GROUPS)], axis=1))
    y = jnp.concatenate(yrows, axis=0)
    y = y + dexp_ref[...] * xs
    y = y * _silu(za_ref[...])
    nw = nw_ref[...]
    outs = []
    for g in range(SSD_GROUPS):
        yg = y[:, g * GW:(g + 1) * GW]
        ms = jnp.mean(yg * yg, axis=-1, keepdims=True)
        outs.append((yg * lax.rsqrt(ms + NORM_EPS)) * nw[:, g * GW:(g + 1) * GW])
    y_ref[...] = jnp.concatenate(outs, axis=1).astype(y_ref.dtype)


def _ssd_step(proj, dt_raw, conv_state, ssd_state, conv_w, conv_b, dt_bias, a_log, d_exp, norm_w, *, nseq):
    R = SUBLANES
    const2 = lambda shape: pl.BlockSpec(shape, lambda i: (0, 0))
    return pl.pallas_call(
        _ssd_step_kernel,
        grid=(nseq // R,),
        in_specs=[
            pl.BlockSpec((R, SSD_XBC), lambda i: (i, COL_XBC // SSD_XBC)),
            pl.BlockSpec((R, SSD_WIDTH), lambda i: (i, COL_ZA // SSD_WIDTH)),
            pl.BlockSpec((R, LANES), lambda i: (i, 0)),
            pl.BlockSpec((SSD_CONV - 1, R, SSD_XBC), lambda i: (0, i, 0)),
            pl.BlockSpec((R, SSD_WIDTH, SSD_STATE), lambda i: (i, 0, 0)),
            const2((SSD_CONV, SSD_XBC)),
            const2((1, SSD_XBC)),
            const2((1, LANES)),
            const2((1, LANES)),
            const2((1, SSD_WIDTH)),
            const2((1, SSD_WIDTH)),
        ],
        out_specs=[
            pl.BlockSpec((R, SSD_WIDTH), lambda i: (i, 0)),
            pl.BlockSpec((R, SSD_WIDTH, SSD_STATE), lambda i: (i, 0, 0)),
            pl.BlockSpec((SSD_CONV - 1, R, SSD_XBC), lambda i: (0, i, 0)),
        ],
        out_shape=[
            jax.ShapeDtypeStruct((nseq, SSD_WIDTH), BF16),
            jax.ShapeDtypeStruct((nseq, SSD_WIDTH, SSD_STATE), F32),
            jax.ShapeDtypeStruct((SSD_CONV - 1, nseq, SSD_XBC), F32),
        ],
        compiler_params=_params("arbitrary"),
        name="ssd_step",
    )(proj, proj, dt_raw, conv_state, ssd_state, conv_w, conv_b, dt_bias, a_log, d_exp, norm_w)


def _s5_block_diag(w, rows_per_group, cols_per_group):
    ngrp, r, lanes = w.shape
    width = ngrp * cols_per_group
    tiled = jnp.concatenate([w.reshape(ngrp * r, lanes)] * (width // lanes), axis=1)
    row_g = lax.broadcasted_iota(jnp.int32, tiled.shape, 0) // rows_per_group
    col_g = lax.broadcasted_iota(jnp.int32, tiled.shape, 1) // cols_per_group
    return jnp.where(row_g == col_g, tiled, 0.0).astype(BF16)


def _s5_weights(bbr_ref, bbi_ref, cr_ref, ci_ref):
    return tuple(_s5_block_diag(r[...], S5_GROUP, S5_STATE) for r in (bbr_ref, bbi_ref, cr_ref, ci_ref))


def _s5_pitch(seg_len):
    return seg_len if (seg_len // SUBLANES) % 2 else seg_len + SUBLANES


def _s5_kernel(u_ref, h0r_ref, h0i_ref, bbr_ref, bbi_ref, ctr_ref, cti_ref, ar_ref, ai_ref,
               d_ref, y_ref, hr_ref, hi_ref, pad, perm, bur, bui, hb, *, seq):
    NS = SUBLANES
    SL = seq // NS
    pitch = _s5_pitch(SL)
    nk = S5_BLK_U // LANES
    n = S5_BLK_N
    bt_re, bt_im, ct_re, ct_im = _s5_weights(bbr_ref, bbi_ref, ctr_ref, cti_ref)

    for s in range(NS):
        for k in range(nk):
            pad[k, s * pitch:s * pitch + SL, :] = u_ref[s * SL:(s + 1) * SL, k * LANES:(k + 1) * LANES]

    def gather(j, c):
        r0 = pl.multiple_of(j * NS, NS)
        for k in range(nk):
            perm[pl.ds(r0, NS), k * LANES:(k + 1) * LANES] = pad[k, pl.ds(j, NS, stride=pitch), :]
        return c

    lax.fori_loop(0, SL, gather, 0, unroll=8)
    up = perm[...]
    ub = up.astype(BF16)
    bur[...] = jnp.dot(ub, bt_re, preferred_element_type=F32)
    bui[...] = jnp.dot(ub, bt_im, preferred_element_type=F32)

    ar = jnp.broadcast_to(ar_ref[...], (NS, n))
    ai = jnp.broadcast_to(ai_ref[...], (NS, n))

    def step(carry, r0):
        hr, hi = carry
        nr = ar * hr - ai * hi + bur[pl.ds(r0, NS), :]
        ni = ar * hi + ai * hr + bui[pl.ds(r0, NS), :]
        return nr, ni

    def scan_ends(j, carry):
        return step(carry, pl.multiple_of(j * NS, NS))

    zero = jnp.zeros((NS, n), F32)
    er, ei = lax.fori_loop(0, SL, scan_ends, (zero, zero), unroll=4)

    asr = ar_ref[...]
    asi = ai_ref[...]
    for _ in range(SL.bit_length() - 1):
        asr, asi = asr * asr - asi * asi, 2.0 * (asr * asi)
    gr = [h0r_ref[...]]
    gi = [h0i_ref[...]]
    for s in range(NS):
        gr.append(er[s:s + 1, :] + (asr * gr[s] - asi * gi[s]))
        gi.append(ei[s:s + 1, :] + (asr * gi[s] + asi * gr[s]))
    hr_ref[...] = gr[NS]
    hi_ref[...] = gi[NS]

    def scan_store(jp, carry):
        r0 = pl.multiple_of(jp * 2 * NS, 2 * NS)
        c1 = step(carry, r0)
        c2 = step(c1, r0 + NS)
        hb[pl.ds(r0, 2 * NS), 0:n] = jnp.concatenate([c1[0], c2[0]], axis=0).astype(BF16)
        hb[pl.ds(r0, 2 * NS), n:2 * n] = jnp.concatenate([c1[1], c2[1]], axis=0).astype(BF16)
        return c2

    lax.fori_loop(0, SL // 2, scan_store,
                  (jnp.concatenate(gr[:NS], axis=0), jnp.concatenate(gi[:NS], axis=0)), unroll=2)

    y = (lax.dot_general(hb[:, 0:n], ct_re, _NT, preferred_element_type=F32)
         - lax.dot_general(hb[:, n:2 * n], ct_im, _NT, preferred_element_type=F32))
    perm[...] = _gelu_tanh(y + d_ref[...] * up)

    def scatter(j, c):
        r0 = pl.multiple_of(j * NS, NS)
        for k in range(nk):
            pad[k, pl.ds(j, NS, stride=pitch), :] = perm[pl.ds(r0, NS), k * LANES:(k + 1) * LANES]
        return c

    lax.fori_loop(0, SL, scatter, 0, unroll=8)
    for s in range(NS):
        for k in range(nk):
            y_ref[s * SL:(s + 1) * SL, k * LANES:(k + 1) * LANES] = pad[k, s * pitch:s * pitch + SL, :]


def _s5(proj, h0r, h0i, bb_re, bb_im, ct_re, ct_im, ab_re, ab_im, d_s5, *, batch, seq, row_block0):
    ub0 = COL_UB // S5_BLK_U
    seg = seq // SUBLANES
    assert seg & (seg - 1) == 0
    gpb = S5_GROUPS // S5_BLOCKS
    vec = lambda width: pl.BlockSpec((1, width), lambda b, j: (0, j))
    wspec = lambda r: pl.BlockSpec((gpb, r, LANES), lambda b, j: (j, 0, 0))
    st_out = pl.BlockSpec((None, 1, S5_BLK_N), lambda b, j: (b, 0, j))
    return pl.pallas_call(
        functools.partial(_s5_kernel, seq=seq),
        grid=(batch, S5_BLOCKS),
        in_specs=[
            pl.BlockSpec((seq, S5_BLK_U), lambda b, j: (row_block0 + b, ub0 + j)),
            vec(S5_BLK_N), vec(S5_BLK_N),
            wspec(S5_GROUP), wspec(S5_GROUP), wspec(S5_GROUP), wspec(S5_GROUP),
            vec(S5_BLK_N), vec(S5_BLK_N), vec(S5_BLK_U),
        ],
        out_specs=[pl.BlockSpec((seq, S5_BLK_U), lambda b, j: (b, j)), st_out, st_out],
        out_shape=[
            jax.ShapeDtypeStruct((batch * seq, S5_WIDTH), F32),
            jax.ShapeDtypeStruct((batch, 1, S5_NSTATE), F32),
            jax.ShapeDtypeStruct((batch, 1, S5_NSTATE), F32),
        ],
        scratch_shapes=[
            pltpu.VMEM((S5_BLK_U // LANES, SUBLANES * _s5_pitch(seg), LANES), F32),
            pltpu.VMEM((seq, S5_BLK_U), F32),
            pltpu.VMEM((seq, S5_BLK_N), F32),
            pltpu.VMEM((seq, S5_BLK_N), F32),
            pltpu.VMEM((seq, 2 * S5_BLK_N), BF16),
        ],
        compiler_params=_params("arbitrary", "arbitrary"),
        name="s5",
    )(proj, h0r, h0i, bb_re, bb_im, ct_re, ct_im, ab_re, ab_im, d_s5)


def _s5_step_kernel(u_ref, h0r_ref, h0i_ref, bbr_ref, bbi_ref, ctr_ref, cti_ref, ar_ref, ai_ref, d_ref,
                    y_ref, hr_ref, hi_ref):
    bt_re, bt_im, ct_re, ct_im = _s5_weights(bbr_ref, bbi_ref, ctr_ref, cti_ref)
    u = u_ref[...]
    ub = u.astype(BF16)
    ar = ar_ref[...]
    ai = ai_ref[...]
    h0r = h0r_ref[...]
    h0i = h0i_ref[...]
    hr = jnp.dot(ub, bt_re, preferred_element_type=F32) + (ar * h0r - ai * h0i)
    hi = jnp.dot(ub, bt_im, preferred_element_type=F32) + (ar * h0i + ai * h0r)
    hr_ref[...] = hr
    hi_ref[...] = hi
    y = (lax.dot_general(hr.astype(BF16), ct_re, _NT, preferred_element_type=F32)
         - lax.dot_general(hi.astype(BF16), ct_im, _NT, preferred_element_type=F32))
    y_ref[...] = _gelu_tanh(y + d_ref[...] * u)


def _s5_step(proj, h0r, h0i, bb_re, bb_im, ct_re, ct_im, ab_re, ab_im, d_s5, *, nseq):
    ub0 = COL_UB // S5_BLK_U
    gpb = S5_GROUPS // S5_BLOCKS
    vec = lambda width: pl.BlockSpec((1, width), lambda j: (0, j))
    wspec = lambda r: pl.BlockSpec((gpb, r, LANES), lambda j: (j, 0, 0))
    st = pl.BlockSpec((nseq, S5_BLK_N), lambda j: (0, j))
    return pl.pallas_call(
        _s5_step_kernel,
        grid=(S5_BLOCKS,),
        in_specs=[
            pl.BlockSpec((nseq, S5_BLK_U), lambda j: (0, ub0 + j)),
            st, st,
            wspec(S5_GROUP), wspec(S5_GROUP), wspec(S5_GROUP), wspec(S5_GROUP),
            vec(S5_BLK_N), vec(S5_BLK_N), vec(S5_BLK_U),
        ],
        out_specs=[pl.BlockSpec((nseq, S5_BLK_U), lambda j: (0, j)), st, st],
        out_shape=[
            jax.ShapeDtypeStruct((nseq, S5_WIDTH), F32),
            jax.ShapeDtypeStruct((nseq, S5_NSTATE), F32),
            jax.ShapeDtypeStruct((nseq, S5_NSTATE), F32),
        ],
        compiler_params=_params("arbitrary"),
        name="s5_step",
    )(proj, h0r, h0i, bb_re, bb_im, ct_re, ct_im, ab_re, ab_im, d_s5)


def _tail_kernel(yn_ref, ybg_ref, zb_ref, ga_ref, gb_ref, x_ref, wpa_ref, wglu_ref, bglu_ref, wpb_ref,
                 wout_ref, fnw_ref, out_ref):
    ya = jnp.dot(yn_ref[...], wpa_ref[...], preferred_element_type=F32)
    yb = ybg_ref[...]
    glu = jnp.dot(yb.astype(BF16), wglu_ref[...], preferred_element_type=F32) + bglu_ref[...]
    yb = (yb * _sigmoid(glu)) * _silu(zb_ref[...])
    ybp = jnp.dot(yb.astype(BF16), wpb_ref[...], preferred_element_type=F32)
    mixed = _sigmoid(ga_ref[...]) * ya + _sigmoid(gb_ref[...]) * ybp
    o = x_ref[...] + jnp.dot(mixed.astype(BF16), wout_ref[...], preferred_element_type=F32)
    ms = jnp.mean(o * o, axis=-1, keepdims=True)
    out_ref[...] = (o * lax.rsqrt(ms + NORM_EPS)) * fnw_ref[...]


def _tail(yn, ybg, proj, x, w_proj_a, w_glu, b_glu, w_proj_b, w_out, final_norm_w, *, tm):
    m, d = x.shape
    resident = lambda shape: pl.BlockSpec(shape, lambda i: (0, 0), pipeline_mode=pl.Buffered(1))
    return pl.pallas_call(
        _tail_kernel,
        grid=(m // tm,),
        in_specs=[
            pl.BlockSpec((tm, SSD_WIDTH), lambda i: (i, 0)),
            pl.BlockSpec((tm, S5_WIDTH), lambda i: (i, 0)),
            pl.BlockSpec((tm, S5_WIDTH), lambda i: (i, COL_ZB // S5_WIDTH)),
            pl.BlockSpec((tm, d), lambda i: (i, COL_GA // D_MODEL)),
            pl.BlockSpec((tm, d), lambda i: (i, COL_GB // D_MODEL)),
            pl.BlockSpec((tm, d), lambda i: (i, 0)),
            resident((SSD_WIDTH, d)),
            resident((S5_WIDTH, S5_WIDTH)),
            resident((1, S5_WIDTH)),
            resident((S5_WIDTH, d)),
            resident((d, d)),
            resident((1, d)),
        ],
        out_specs=pl.BlockSpec((tm, d), lambda i: (i, 0)),
        out_shape=jax.ShapeDtypeStruct((m, d), F32),
        compiler_params=_params("arbitrary"),
        name="tail",
    )(yn, ybg, proj, proj, proj, x, w_proj_a, w_glu, b_glu, w_proj_b, w_out, final_norm_w)


def kernel(x_prompt, x_sample, state_ssd, state_conv, state_s5_re, state_s5_im, meta_tokens, norm_w, w_in,
           conv_w, conv_b, dt_bias, a_log, d_ssd, ssd_norm_w, w_proj_a, lam_re, lam_im, log_dt_s5, b_re, b_im,
           c_re, c_im, d_s5, w_glu, b_glu, w_proj_b, w_out, final_norm_w):
    bsz, seq, d = x_prompt.shape
    nseq = x_sample.shape[0]
    assert d == D_MODEL and seq % CHUNK == 0 and nseq % SUBLANES == 0 and norm_w.shape[0] == 1
    assert meta_tokens.shape[0] == N_META and N_META <= CHUNK

    assert w_in.shape[2] == _SRC_GB + D_MODEL
    w_main, w_dt = _wprep(jnp.transpose(w_in[0]))
    pad_heads = lambda v: jnp.pad(v.reshape(1, SSD_HEADS), ((0, 0), (0, LANES - SSD_HEADS)))
    dtb = pad_heads(dt_bias[0])
    alog = pad_heads(a_log[0])
    d_exp = jnp.repeat(d_ssd[0], SSD_HEAD_DIM).reshape(1, SSD_WIDTH)
    nw1 = norm_w[0].reshape(1, d)
    ssd_nw = ssd_norm_w[0].reshape(1, SSD_WIDTH)
    convw = conv_w[0]
    convb = conv_b[0].reshape(1, SSD_XBC)
    wpa = w_proj_a[0].astype(BF16)
    wglu = w_glu[0].astype(BF16)
    bglu = b_glu[0].reshape(1, S5_WIDTH)
    wpb = w_proj_b[0].astype(BF16)
    wout = w_out[0].astype(BF16)
    fnw = final_norm_w.reshape(1, d)
    ds5 = d_s5[0].reshape(1, S5_WIDTH)

    rep = lambda v, k: jnp.concatenate([v] * k, axis=-1)
    lane_rep = LANES // S5_STATE
    ab_re, ab_im, bb_re, bb_im = _s5prep(
        rep(lam_re[0], lane_rep), rep(lam_im[0], lane_rep), log_dt_s5[0].reshape(S5_GROUPS, 1),
        rep(jnp.transpose(b_re[0], (0, 2, 1)), lane_rep), rep(jnp.transpose(b_im[0], (0, 2, 1)), lane_rep))
    ab_re = ab_re[:, :S5_STATE].reshape(1, S5_NSTATE)
    ab_im = ab_im[:, :S5_STATE].reshape(1, S5_NSTATE)
    ct_re = rep(c_re[0], lane_rep)
    ct_im = rep(c_im[0], lane_rep)

    x_main = x_prompt.reshape(bsz * seq, d)
    x_small = jnp.concatenate(
        [x_sample.reshape(nseq, d), jnp.zeros((CHUNK - N_META, d), x_prompt.dtype),
         meta_tokens.astype(x_prompt.dtype)], axis=0)
    assert nseq % CHUNK == 0
    meta_blk = nseq // CHUNK
    tm_main = 1024 if (bsz * seq) % 1024 == 0 else CHUNK
    proj_m, dt_m = _inproj(x_main, nw1, w_main, w_dt, tm=tm_main)
    proj_s, dt_s = _inproj(x_small, nw1, w_main, w_dt, tm=nseq + CHUNK)

    ssd_args = (convw, convb, dtb, alog, d_exp, ssd_nw)
    s5_w = (bb_re, bb_im, ct_re, ct_im)
    s5_v = (ab_re, ab_im, ds5)

    zeros_ht = jnp.zeros((SSD_STATE, SSD_WIDTH), F32)
    zeros_tail = jnp.zeros((SUBLANES, SSD_XBC), F32)
    zeros_s5 = jnp.zeros((1, S5_NSTATE), F32)
    _, _, ht_meta, tail_meta = _ssd(proj_s, dt_s, zeros_ht, zeros_tail, *ssd_args, batch=1, nchunks=1,
                                    row_block0=meta_blk, mask_rows=CHUNK - N_META)
    _, s5r_meta, s5i_meta = _s5(proj_s, zeros_s5, zeros_s5, *s5_w, *s5_v, batch=1, seq=CHUNK,
                                row_block0=meta_blk)

    yn_m, h_m, _, tail_m = _ssd(proj_m, dt_m, ht_meta[0], tail_meta[0], *ssd_args, batch=bsz,
                                nchunks=seq // CHUNK, row_block0=0, mask_rows=0)
    ybg_m, s5r_m, s5i_m = _s5(proj_m, s5r_meta[0], s5i_meta[0], *s5_w, *s5_v, batch=bsz, seq=seq,
                              row_block0=0)
    tail_w = (wpa, wglu, bglu, wpb, wout, fnw)
    y_prompt = _tail(yn_m, ybg_m, proj_m, x_main, *tail_w, tm=256)

    yn_s, h_s, cs_s = _ssd_step(proj_s, dt_s, jnp.transpose(state_conv[0], (1, 0, 2)),
                                state_ssd[0].reshape(nseq, SSD_WIDTH, SSD_STATE), *ssd_args, nseq=nseq)
    ybg_s, s5r_s, s5i_s = _s5_step(proj_s, state_s5_re[0].reshape(nseq, S5_NSTATE),
                                   state_s5_im[0].reshape(nseq, S5_NSTATE), *s5_w, *s5_v, nseq=nseq)
    y_sample = _tail(yn_s, ybg_s, proj_s, x_sample.reshape(nseq, d), *tail_w, tm=nseq)

    dt_out = x_prompt.dtype
    return (
        y_prompt.reshape(bsz, seq, d),
        y_sample.reshape(nseq, 1, d),
        h_m.reshape(1, bsz, SSD_HEADS, SSD_HEAD_DIM, SSD_STATE).astype(dt_out),
        tail_m[:, SUBLANES - (SSD_CONV - 1):, :].reshape(1, bsz, SSD_CONV - 1, SSD_XBC),
        s5r_m.reshape(1, bsz, S5_GROUPS, S5_STATE).astype(dt_out),
        s5i_m.reshape(1, bsz, S5_GROUPS, S5_STATE).astype(dt_out),
        h_s.reshape(1, nseq, SSD_HEADS, SSD_HEAD_DIM, SSD_STATE).astype(dt_out),
        jnp.transpose(cs_s, (1, 0, 2)).reshape(1, nseq, SSD_CONV - 1, SSD_XBC),
        s5r_s.reshape(1, nseq, S5_GROUPS, S5_STATE).astype(dt_out),
        s5i_s.reshape(1, nseq, S5_GROUPS, S5_STATE).astype(dt_out),
    )
```

```python
import functools

import jax
import jax.numpy as jnp
from jax import lax
from jax.experimental import pallas as pl
from jax.experimental.pallas import tpu as pltpu

F32 = jnp.float32
BF16 = jnp.bfloat16

NORM_EPS = 1e-5
LOG2E = 1.4426950408889634
N_META = 16
D_MODEL = 2048
SSD_HEAD_DIM = 64
SSD_HEADS = 32
SSD_GROUPS = 4
SSD_STATE = 128
SSD_WIDTH = SSD_HEADS * SSD_HEAD_DIM
SSD_XBC = SSD_WIDTH + 2 * SSD_GROUPS * SSD_STATE
SSD_CONV = 4
CHUNK = 128
S5_WIDTH = D_MODEL // 2
S5_GROUP = 16
S5_GROUPS = S5_WIDTH // S5_GROUP
S5_STATE = 64
S5_NSTATE = S5_GROUPS * S5_STATE
S5_BLOCKS = 4
S5_BLK_U = S5_WIDTH // S5_BLOCKS
S5_BLK_N = S5_NSTATE // S5_BLOCKS

LANES = 128
SUBLANES = 8
VMEM_LIMIT = 56 * 1024 * 1024
CONV_PITCH = 3

COL_ZA = 0
COL_GA = 2048
COL_GB = 4096
COL_XBC = 6144
COL_UB = 9216
COL_ZB = 10240
PROJ_COLS = 11264
PROJ_TN = 1024


_NT = (((1,), (1,)), ((), ()))


def _sigmoid(x):
    return 0.5 * jnp.tanh(0.5 * x) + 0.5


def _silu(x):
    h = 0.5 * x
    return h + h * jnp.tanh(h)


def _softplus(x):
    return jnp.maximum(x, 0.0) + jnp.log1p(jnp.exp(-jnp.abs(x)))


def _gelu_tanh(x):
    c = 0.7978845608028654
    return 0.5 * x * (1.0 + jnp.tanh(c * (x + 0.044715 * (x * x * x))))


def _split3(x):
    x1 = x.astype(BF16)
    r1 = x - x1.astype(F32)
    x2 = r1.astype(BF16)
    x3 = (r1 - x2.astype(F32)).astype(BF16)
    return x1, x2, x3


def _params(*sem):
    return pltpu.CompilerParams(dimension_semantics=sem, vmem_limit_bytes=VMEM_LIMIT)


def _s5prep_kernel(lr_ref, li_ref, ldt_ref, btr_ref, bti_ref, abr_ref, abi_ref, bbr_ref, bbi_ref):
    lr = lr_ref[...]
    li = li_ref[...]
    step = jnp.exp(ldt_ref[...])
    mag = jnp.exp(lr * step)
    abr = mag * jnp.cos(li * step)
    abi = mag * jnp.sin(li * step)
    den = lr * lr + li * li
    numr = abr - 1.0
    cr = (numr * lr + abi * li) / den
    ci = (abi * lr - numr * li) / den
    abr_ref[...] = abr
    abi_ref[...] = abi
    btr = btr_ref[...]
    bti = bti_ref[...]
    crb = cr[:, None, :]
    cib = ci[:, None, :]
    bbr_ref[...] = crb * btr - cib * bti
    bbi_ref[...] = crb * bti + cib * btr


def _s5prep(lam_re, lam_im, log_dt, bt_re, bt_im):
    g, n = lam_re.shape
    full2 = pl.BlockSpec((g, n), lambda: (0, 0))
    full3 = pl.BlockSpec((g, S5_GROUP, n), lambda: (0, 0, 0))
    return pl.pallas_call(
        _s5prep_kernel,
        in_specs=[full2, full2, pl.BlockSpec((g, 1), lambda: (0, 0)), full3, full3],
        out_specs=[full2, full2, full3, full3],
        out_shape=[jax.ShapeDtypeStruct((g, n), F32)] * 2
        + [jax.ShapeDtypeStruct((g, S5_GROUP, n), F32)] * 2,
        name="s5prep",
    )(lam_re, lam_im, log_dt, bt_re, bt_im)


W_ALIGN = 32
_SRC_ZA = 0
_SRC_XBC = SSD_WIDTH
_SRC_DT = _SRC_XBC + SSD_XBC
_SRC_UB = _SRC_DT + SSD_HEADS
_SRC_ZB = _SRC_UB + S5_WIDTH
_SRC_GA = _SRC_ZB + S5_WIDTH
_SRC_GB = _SRC_GA + D_MODEL
_SEGMENTS = ((_SRC_ZA, SSD_WIDTH), (_SRC_GA, D_MODEL), (_SRC_GB, D_MODEL), (_SRC_XBC, SSD_XBC),
             (_SRC_UB, S5_WIDTH), (_SRC_ZB, S5_WIDTH))
_SRC_TILES = tuple((start + k) // W_ALIGN for start, width in _SEGMENTS for k in range(0, width, PROJ_TN))


def _wprep_dt_kernel(w_ref, o_ref):
    o_ref[...] = jnp.zeros(o_ref.shape, o_ref.dtype)
    o_ref[0:SSD_HEADS, :] = w_ref[...].astype(BF16)


def _wprep_dt(wt):
    d = wt.shape[1]
    assert _SRC_DT % SSD_HEADS == 0
    return pl.pallas_call(
        _wprep_dt_kernel,
        grid=(1,),
        in_specs=[pl.BlockSpec((SSD_HEADS, d), lambda i: (_SRC_DT // SSD_HEADS, 0))],
        out_specs=pl.BlockSpec((LANES, d), lambda i: (0, 0)),
        out_shape=jax.ShapeDtypeStruct((LANES, d), BF16),
        name="wprep_dt",
    )(wt)


def _norm_rows(x_ref, nw_ref, wdt_ref, xn_ref, dt_ref):
    x = x_ref[...]
    ms = jnp.mean(x * x, axis=-1, keepdims=True)
    xn = ((x * lax.rsqrt(ms + NORM_EPS)) * nw_ref[...]).astype(BF16)
    xn_ref[...] = xn
    dt_ref[...] = lax.dot_general(xn, wdt_ref[...], _NT, preferred_element_type=F32)


def _wprep_inproj_kernel(tbl_ref, x_ref, nw_ref, w_ref, wdt_ref, wout_ref, proj_ref, dt_ref, xn_ref):
    del tbl_ref

    @pl.when(pl.program_id(0) == 0)
    def _():
        _norm_rows(x_ref, nw_ref, wdt_ref, xn_ref, dt_ref)

    wb = w_ref[...].astype(BF16)
    wout_ref[...] = wb
    proj_ref[...] = lax.dot_general(xn_ref[...], wb, _NT, preferred_element_type=F32)


def _wprep_inproj(wt, w_dt, x, norm_w):
    m, d = x.shape
    ntiles = len(_SRC_TILES)
    assert ntiles * PROJ_TN == PROJ_COLS
    return pl.pallas_call(
        _wprep_inproj_kernel,
        grid_spec=pltpu.PrefetchScalarGridSpec(
            num_scalar_prefetch=1,
            grid=(ntiles,),
            in_specs=[
                pl.BlockSpec((m, d), lambda j, tbl: (0, 0)),
                pl.BlockSpec((1, d), lambda j, tbl: (0, 0)),
                pl.BlockSpec((pl.Element(PROJ_TN), pl.Element(d)),
                             lambda j, tbl: (pl.multiple_of(tbl[j] * W_ALIGN, W_ALIGN), 0)),
                pl.BlockSpec((LANES, d), lambda j, tbl: (0, 0)),
            ],
            out_specs=[
                pl.BlockSpec((PROJ_TN, d), lambda j, tbl: (j, 0)),
                pl.BlockSpec((m, PROJ_TN), lambda j, tbl: (0, j)),
                pl.BlockSpec((m, LANES), lambda j, tbl: (0, 0)),
            ],
            scratch_shapes=[pltpu.VMEM((m, d), BF16)],
        ),
        out_shape=[jax.ShapeDtypeStruct((PROJ_COLS, d), BF16), jax.ShapeDtypeStruct((m, PROJ_COLS), F32),
                   jax.ShapeDtypeStruct((m, LANES), F32)],
        compiler_params=_params("arbitrary"),
        name="wprep_inproj",
    )(jnp.asarray(_SRC_TILES, jnp.int32), x, norm_w, wt, w_dt)


def _inproj_kernel(x_ref, nw_ref, w_ref, wdt_ref, proj_ref, dt_ref, xn_ref):
    @pl.when(pl.program_id(1) == 0)
    def _():
        _norm_rows(x_ref, nw_ref, wdt_ref, xn_ref, dt_ref)

    proj_ref[...] = lax.dot_general(xn_ref[...], w_ref[...], _NT, preferred_element_type=F32)


def _inproj(x, norm_w, w_main, w_dt, tm):
    m, d = x.shape
    n = w_main.shape[0]
    return pl.pallas_call(
        _inproj_kernel,
        grid=(m // tm, n // PROJ_TN),
        in_specs=[
            pl.BlockSpec((tm, d), lambda i, j: (i, 0)),
            pl.BlockSpec((1, d), lambda i, j: (0, 0)),
            pl.BlockSpec((PROJ_TN, d), lambda i, j: (j, 0)),
            pl.BlockSpec((LANES, d), lambda i, j: (0, 0)),
        ],
        out_specs=[
            pl.BlockSpec((tm, PROJ_TN), lambda i, j: (i, j)),
            pl.BlockSpec((tm, LANES), lambda i, j: (i, 0)),
        ],
        out_shape=[jax.ShapeDtypeStruct((m, n), F32), jax.ShapeDtypeStruct((m, LANES), F32)],
        scratch_shapes=[pltpu.VMEM((tm, d), BF16)],
        compiler_params=_params("arbitrary", "arbitrary"),
        name="inproj",
    )(x, norm_w, w_main, w_dt)


def _ssd_chunk(r0, xbc_ref, za_ref, dtr_ref, convw_ref, convb_ref, dtb_ref, alog_ref, dexp_ref, nw_ref,
               y_ref, ht_scr, ext_scr, mask_rows):
    L = CHUNK
    P2 = 2 * SSD_HEAD_DIM
    GW = SSD_WIDTH // SSD_GROUPS
    time_rows = lambda t0, n: pl.ds(CONV_PITCH * t0, n, stride=CONV_PITCH)

    w = convw_ref[...]
    bias = convb_ref[...]
    conv_parts = []
    for s in range(SSD_XBC // LANES):
        ls = slice(s * LANES, (s + 1) * LANES)
        xc = xbc_ref[r0:r0 + L, ls]
        ext_scr[s, time_rows(SUBLANES, L), :] = xc
        acc = bias[:, ls]
        for k in range(SSD_CONV - 1):
            acc = acc + ext_scr[s, time_rows(SUBLANES - (SSD_CONV - 1) + k, L), :] * w[k:k + 1, ls]
        conv_parts.append(acc + xc * w[SSD_CONV - 1:SSD_CONV, ls])
        ext_scr[s, time_rows(0, SUBLANES), :] = xc[L - SUBLANES:L, :]
    xbc = _silu(jnp.concatenate(conv_parts, axis=1))
    xs = xbc[:, :SSD_WIDTH]
    bmat = xbc[:, SSD_WIDTH:SSD_WIDTH + SSD_GROUPS * SSD_STATE]
    cmat = xbc[:, SSD_WIDTH + SSD_GROUPS * SSD_STATE:]

    rows = lax.broadcasted_iota(jnp.int32, (L, L), 0)
    cols = lax.broadcasted_iota(jnp.int32, (L, L), 1)
    causal = rows >= cols
    lane_lo = cols < SSD_HEAD_DIM

    dt = _softplus(dtr_ref[r0:r0 + L, :] + dtb_ref[...])
    if mask_rows:
        dt = jnp.where(rows < mask_rows, 0.0, dt)
    a = dt * (-jnp.exp(alog_ref[...]))
    tril = jnp.where(causal, 1.0, 0.0).astype(BF16)
    a1, a2, a3 = _split3(a)
    acum = (jnp.dot(tril, a1, preferred_element_type=F32)
            + jnp.dot(tril, a2, preferred_element_type=F32)
            + jnp.dot(tril, a3, preferred_element_type=F32))
    a2 = acum * LOG2E
    e_cum = jnp.exp2(a2)
    w_end = dt * jnp.exp2(a2[L - 1:L, :] - a2)
    a2dt_t = (a2 - jnp.log2(dt)).T

    dexp = dexp_ref[...]
    xs_b = xs.astype(BF16)
    y_parts = []
    for g in range(SSD_GROUPS):
        bg = bmat[:, g * SSD_STATE:(g + 1) * SSD_STATE].astype(BF16)
        cg = cmat[:, g * SSD_STATE:(g + 1) * SSD_STATE].astype(BF16)
        cb = lax.dot_general(cg, bg, (((1,), (1,)), ((), ())), preferred_element_type=F32)
        ht_g = ht_scr[:, g * GW:(g + 1) * GW]
        y_off = jnp.dot(cg, ht_g.astype(BF16), preferred_element_type=F32)
        xw_parts = []
        elast_parts = []
        for jj in range(GW // P2):
            lo = g * GW + jj * P2
            h0 = lo // SSD_HEAD_DIM
            yd, eb, wb = [], [], []
            for h in (h0, h0 + 1):
                colb = jnp.broadcast_to(a2[:, h:h + 1], (L, L))
                m = cb * jnp.exp2(jnp.where(causal, colb - a2dt_t[h:h + 1, :], -jnp.inf))
                yd.append(jnp.dot(m.astype(BF16), xs_b[:, lo:lo + P2], preferred_element_type=F32))
                eb.append(jnp.broadcast_to(e_cum[:, h:h + 1], (L, L)))
                wb.append(jnp.broadcast_to(w_end[:, h:h + 1], (L, L)))
            xs_pair = xs[:, lo:lo + P2]
            y_pair = (jnp.where(lane_lo, yd[0], yd[1])
                      + y_off[:, jj * P2:(jj + 1) * P2] * jnp.where(lane_lo, eb[0], eb[1]))
            y_parts.append(y_pair + dexp[:, lo:lo + P2] * xs_pair)
            xw_parts.append(xs_pair * jnp.where(lane_lo, wb[0], wb[1]))
            elast_parts.append(jnp.where(lane_lo[0:1, :], eb[0][L - 1:L, :], eb[1][L - 1:L, :]))
        xw = jnp.concatenate(xw_parts, axis=1)
        elast = jnp.concatenate(elast_parts, axis=1)
        st = lax.dot_general(bg, xw.astype(BF16), (((0,), (0,)), ((), ())),
                             preferred_element_type=F32)
        ht_scr[:, g * GW:(g + 1) * GW] = ht_g * elast + st

    y = jnp.concatenate(y_parts, axis=1)
    y = y * _silu(za_ref[r0:r0 + L, :])
    nw = nw_ref[...]
    outs = []
    for g in range(SSD_GROUPS):
        yg = y[:, g * GW:(g + 1) * GW]
        ms = jnp.mean(yg * yg, axis=-1, keepdims=True)
        outs.append((yg * lax.rsqrt(ms + NORM_EPS)) * nw[:, g * GW:(g + 1) * GW])
    y_ref[r0:r0 + L, :] = jnp.concatenate(outs, axis=1).astype(y_ref.dtype)


def _ssd_kernel(xbc_ref, za_ref, dtr_ref, ht0_ref, tail0_ref, convw_ref, convb_ref, dtb_ref, alog_ref,
                dexp_ref, nw_ref, y_ref, h_ref, ht_ref, tail_ref, ht_scr, ext_scr, *, mask_rows):
    c = pl.program_id(1)
    nrows = xbc_ref.shape[0]

    @pl.when(c == 0)
    def _():
        ht_scr[...] = ht0_ref[...]
        for s in range(SSD_XBC // LANES):
            ext_scr[s, pl.ds(0, SUBLANES, stride=CONV_PITCH), :] = tail0_ref[:, s * LANES:(s + 1) * LANES]

    for r0 in range(0, nrows, CHUNK):
        _ssd_chunk(r0, xbc_ref, za_ref, dtr_ref, convw_ref, convb_ref, dtb_ref, alog_ref, dexp_ref, nw_ref,
                   y_ref, ht_scr, ext_scr, mask_rows)

    @pl.when(c == pl.num_programs(1) - 1)
    def _():
        ht = ht_scr[...]
        ht_ref[...] = ht
        h_ref[...] = ht.T
        tail_ref[...] = xbc_ref[nrows - SUBLANES:nrows, :]


def _ssd(proj, dt_raw, ht0, tail0, conv_w, conv_b, dt_bias, a_log, d_exp, norm_w, *, batch, nchunks,
         row_block0, mask_rows):
    L = CHUNK
    rows = batch * nchunks * L
    rb = lambda b, c: row_block0 + b * nchunks + c
    const2 = lambda shape: pl.BlockSpec(shape, lambda b, c: (0, 0))
    return pl.pallas_call(
        functools.partial(_ssd_kernel, mask_rows=mask_rows),
        grid=(batch, nchunks),
        in_specs=[
            pl.BlockSpec((L, SSD_XBC), lambda b, c: (rb(b, c), COL_XBC // SSD_XBC)),
            pl.BlockSpec((L, SSD_WIDTH), lambda b, c: (rb(b, c), COL_ZA // SSD_WIDTH)),
            pl.BlockSpec((L, LANES), lambda b, c: (rb(b, c), 0)),
            const2((SSD_STATE, SSD_WIDTH)),
            const2((SUBLANES, SSD_XBC)),
            const2((SSD_CONV, SSD_XBC)),
            const2((1, SSD_XBC)),
            const2((1, LANES)),
            const2((1, LANES)),
            const2((1, SSD_WIDTH)),
            const2((1, SSD_WIDTH)),
        ],
        out_specs=[
            pl.BlockSpec((L, SSD_WIDTH), lambda b, c: (b * nchunks + c, 0)),
            pl.BlockSpec((None, SSD_WIDTH, SSD_STATE), lambda b, c: (b, 0, 0)),
            pl.BlockSpec((None, SSD_STATE, SSD_WIDTH), lambda b, c: (b, 0, 0)),
            pl.BlockSpec((None, SUBLANES, SSD_XBC), lambda b, c: (b, 0, 0)),
        ],
        out_shape=[
            jax.ShapeDtypeStruct((rows, SSD_WIDTH), BF16),
            jax.ShapeDtypeStruct((batch, SSD_WIDTH, SSD_STATE), F32),
            jax.ShapeDtypeStruct((batch, SSD_STATE, SSD_WIDTH), F32),
            jax.ShapeDtypeStruct((batch, SUBLANES, SSD_XBC), F32),
        ],
        scratch_shapes=[pltpu.VMEM((SSD_STATE, SSD_WIDTH), F32),
                        pltpu.VMEM((SSD_XBC // LANES, CONV_PITCH * (SUBLANES + CHUNK), LANES), F32)],
        compiler_params=_params("arbitrary", "arbitrary"),
        name="ssd",
    )(proj, proj, dt_raw, ht0, tail0, conv_w, conv_b, dt_bias, a_log, d_exp, norm_w)


def _ssd_step_kernel(xbc_ref, za_ref, dtr_ref, cs_ref, h_ref, convw_ref, convb_ref, dtb_ref, alog_ref,
                     dexp_ref, nw_ref, y_ref, hout_ref, csout_ref):
    R = SUBLANES
    GW = SSD_WIDTH // SSD_GROUPS
    x = xbc_ref[...]
    w = convw_ref[...]
    s0 = cs_ref[0]
    s1 = cs_ref[1]
    s2 = cs_ref[2]
    conv = convb_ref[...] + s0 * w[0:1]
    conv = conv + s1 * w[1:2]
    conv = conv + s2 * w[2:3]
    conv = conv + x * w[3:4]
    csout_ref[0] = s1
    csout_ref[1] = s2
    csout_ref[2] = x
    xbc = _silu(conv)
    xs = xbc[:, :SSD_WIDTH]
    bmat = xbc[:, SSD_WIDTH:SSD_WIDTH + SSD_GROUPS * SSD_STATE]
    cmat = xbc[:, SSD_WIDTH + SSD_GROUPS * SSD_STATE:]

    dt = _softplus(dtr_ref[...] + dtb_ref[...])
    da = jnp.exp(dt * (-jnp.exp(alog_ref[...])))
    dt_t = dt.T
    da_t = da.T
    expand = lambda v: jnp.concatenate(
        [jnp.broadcast_to(v[h:h + 1, :], (SSD_HEAD_DIM, R)) for h in range(SSD_HEADS)], axis=0)
    xd_t = xs.T * expand(dt_t)
    da_te = expand(da_t)

    cmat_b = cmat.astype(BF16)
    yrows = []
    for i in range(R):
        bexp = jnp.concatenate(
            [jnp.broadcast_to(bmat[i:i + 1, g * SSD_STATE:(g + 1) * SSD_STATE], (GW, SSD_STATE))
             for g in range(SSD_GROUPS)], axis=0)
        hn = h_ref[i] * da_te[:, i:i + 1] + xd_t[:, i:i + 1] * bexp
        hout_ref[i] = hn
        cg = jnp.concatenate([cmat_b[i:i + 1, g * SSD_STATE:(g + 1) * SSD_STATE] for g in range(SSD_GROUPS)]
                             + [jnp.zeros((R - SSD_GROUPS, SSD_STATE), BF16)], axis=0)
        yg = lax.dot_general(cg, hn.astype(BF16), (((1,), (1,)), ((), ())), preferred_element_type=F32)
        yrows.append(jnp.concatenate([yg[g:g + 1, g * GW:(g + 1) * GW] for g in range(SSD_GROUPS)], axis=1))
    y = jnp.concatenate(yrows, axis=0)
    y = y + dexp_ref[...] * xs
    y = y * _silu(za_ref[...])
    nw = nw_ref[...]
    outs = []
    for g in range(SSD_GROUPS):
        yg = y[:, g * GW:(g + 1) * GW]
        ms = jnp.mean(yg * yg, axis=-1, keepdims=True)
        outs.append((yg * lax.rsqrt(ms + NORM_EPS)) * nw[:, g * GW:(g + 1) * GW])
    y_ref[...] = jnp.concatenate(outs, axis=1).astype(y_ref.dtype)


def _ssd_step(proj, dt_raw, conv_state, ssd_state, conv_w, conv_b, dt_bias, a_log, d_exp, norm_w, *, nseq):
    R = SUBLANES
    const2 = lambda shape: pl.BlockSpec(shape, lambda i: (0, 0))
    return pl.pallas_call(
        _ssd_step_kernel,
        grid=(nseq // R,),
        in_specs=[
            pl.BlockSpec((R, SSD_XBC), lambda i: (i, COL_XBC // SSD_XBC)),
            pl.BlockSpec((R, SSD_WIDTH), lambda i: (i, COL_ZA // SSD_WIDTH)),
            pl.BlockSpec((R, LANES), lambda i: (i, 0)),
            pl.BlockSpec((SSD_CONV - 1, R, SSD_XBC), lambda i: (0, i, 0)),
            pl.BlockSpec((R, SSD_WIDTH, SSD_STATE), lambda i: (i, 0, 0)),
            const2((SSD_CONV, SSD_XBC)),
            const2((1, SSD_XBC)),
            const2((1, LANES)),
            const2((1, LANES)),
            const2((1, SSD_WIDTH)),
            const2((1, SSD_WIDTH)),
        ],
        out_specs=[
            pl.BlockSpec((R, SSD_WIDTH), lambda i: (i, 0)),
            pl.BlockSpec((R, SSD_WIDTH, SSD_STATE), lambda i: (i, 0, 0)),
            pl.BlockSpec((SSD_CONV - 1, R, SSD_XBC), lambda i: (0, i, 0)),
        ],
        out_shape=[
            jax.ShapeDtypeStruct((nseq, SSD_WIDTH), BF16),
            jax.ShapeDtypeStruct((nseq, SSD_WIDTH, SSD_STATE), F32),
            jax.ShapeDtypeStruct((SSD_CONV - 1, nseq, SSD_XBC), F32),
        ],
        compiler_params=_params("arbitrary"),
        name="ssd_step",
    )(proj, proj, dt_raw, conv_state, ssd_state, conv_w, conv_b, dt_bias, a_log, d_exp, norm_w)


def _s5_block_diag(w, rows_per_group, cols_per_group):
    ngrp, r, lanes = w.shape
    width = ngrp * cols_per_group
    tiled = jnp.concatenate([w.reshape(ngrp * r, lanes)] * (width // lanes), axis=1)
    row_g = lax.broadcasted_iota(jnp.int32, tiled.shape, 0) // rows_per_group
    col_g = lax.broadcasted_iota(jnp.int32, tiled.shape, 1) // cols_per_group
    return jnp.where(row_g == col_g, tiled, 0.0)


def _s5_weights(bbr_ref, bbi_ref, cr_ref, ci_ref):
    return tuple(_s5_block_diag(r[...], S5_GROUP, S5_STATE) for r in (bbr_ref, bbi_ref, cr_ref, ci_ref))


def _s5_pitch(seg_len):
    return seg_len if (seg_len // SUBLANES) % 2 else seg_len + SUBLANES


def _s5_kernel(*refs, seq, nconv):
    (u_ref, h0r_ref, h0i_ref, bbr_ref, bbi_ref, ctr_ref, cti_ref, ar_ref, ai_ref, d_ref) = refs[:10]
    w_in_refs = refs[10:10 + nconv]
    y_ref, hr_ref, hi_ref = refs[10 + nconv:13 + nconv]
    w_out_refs = refs[13 + nconv:13 + 2 * nconv]
    pad, pe, po, cr, ci, hb_in, hb_out = refs[13 + 2 * nconv:]
    for wi, wo in zip(w_in_refs, w_out_refs):
        wo[...] = wi[...].astype(BF16)

    NS = SUBLANES
    SL = seq // NS
    H = SL // 2
    pitch = _s5_pitch(SL)
    nk = S5_BLK_U // LANES
    n = S5_BLK_N
    bt_re, bt_im, ct_re, ct_im = _s5_weights(bbr_ref, bbi_ref, ctr_ref, cti_ref)
    a_re = ar_ref[...]
    a_im = ai_ref[...]
    bf = lambda v: v.astype(BF16)
    w2_re = jnp.concatenate([bf(bt_re * a_re - bt_im * a_im), bf(bt_re)], axis=0)
    w2_im = jnp.concatenate([bf(bt_re * a_im + bt_im * a_re), bf(bt_im)], axis=0)
    cta_re = bf(ct_re * a_re - ct_im * a_im)
    cta_im = bf(ct_re * a_im + ct_im * a_re)
    ctb_re = bf(ct_re)
    ctb_im = bf(ct_im)
    k0 = bf(lax.dot_general(bf(bt_re), ctb_re, _NT, preferred_element_type=F32)
            - lax.dot_general(bf(bt_im), ctb_im, _NT, preferred_element_type=F32))

    for s in range(NS):
        for k in range(nk):
            pad[k, s * pitch:s * pitch + SL, :] = u_ref[s * SL:(s + 1) * SL, k * LANES:(k + 1) * LANES]

    def gather(jp, c):
        r0 = pl.multiple_of(jp * NS, NS)
        for k in range(nk):
            pe[pl.ds(r0, NS), k * LANES:(k + 1) * LANES] = pad[k, pl.ds(2 * jp, NS, stride=pitch), :]
            po[pl.ds(r0, NS), k * LANES:(k + 1) * LANES] = pad[k, pl.ds(2 * jp + 1, NS, stride=pitch), :]
        return c

    lax.fori_loop(0, H, gather, 0, unroll=4)
    ue = pe[...]
    uo = po[...]
    ue_b = bf(ue)
    u2 = jnp.concatenate([ue_b, bf(uo)], axis=1)
    cr[...] = jnp.dot(u2, w2_re, preferred_element_type=F32)
    ci[...] = jnp.dot(u2, w2_im, preferred_element_type=F32)

    a2_re = a_re * a_re - a_im * a_im
    a2_im = 2.0 * (a_re * a_im)
    ar = jnp.broadcast_to(a2_re, (NS, n))
    ai = jnp.broadcast_to(a2_im, (NS, n))

    def step(carry, r0):
        hr, hi = carry
        nr = ar * hr - ai * hi + cr[pl.ds(r0, NS), :]
        ni = ar * hi + ai * hr + ci[pl.ds(r0, NS), :]
        return nr, ni

    def scan_ends(j, carry):
        return step(carry, pl.multiple_of(j * NS, NS))

    zero = jnp.zeros((NS, n), F32)
    er, ei = lax.fori_loop(0, H, scan_ends, (zero, zero), unroll=4)

    asr, asi = a_re, a_im
    for _ in range(SL.bit_length() - 1):
        asr, asi = asr * asr - asi * asi, 2.0 * (asr * asi)
    gr = [h0r_ref[...]]
    gi = [h0i_ref[...]]
    for s in range(NS):
        gr.append(er[s:s + 1, :] + (asr * gr[s] - asi * gi[s]))
        gi.append(ei[s:s + 1, :] + (asr * gi[s] + asi * gr[s]))
    hr_ref[...] = gr[NS]
    hi_ref[...] = gi[NS]

    def scan_store(jp, c0):
        r0 = pl.multiple_of(jp * 2 * NS, 2 * NS)
        c1 = step(c0, r0)
        c2 = step(c1, r0 + NS)
        hb_in[pl.ds(r0, 2 * NS), 0:n] = bf(jnp.concatenate([c0[0], c1[0]], axis=0))
        hb_in[pl.ds(r0, 2 * NS), n:2 * n] = bf(jnp.concatenate([c0[1], c1[1]], axis=0))
        hb_out[pl.ds(r0, 2 * NS), 0:n] = bf(jnp.concatenate([c1[0], c2[0]], axis=0))
        hb_out[pl.ds(r0, 2 * NS), n:2 * n] = bf(jnp.concatenate([c1[1], c2[1]], axis=0))
        return c2

    lax.fori_loop(0, H // 2, scan_store,
                  (jnp.concatenate(gr[:NS], axis=0), jnp.concatenate(gi[:NS], axis=0)), unroll=2)

    half = seq // 2
    rb = half // 4
    d = d_ref[...]
    for r in range(0, half, rb):
        rows = slice(r, r + rb)
        y_odd = (lax.dot_general(hb_out[rows, 0:n], ctb_re, _NT, preferred_element_type=F32)
                 - lax.dot_general(hb_out[rows, n:2 * n], ctb_im, _NT, preferred_element_type=F32))
        y_even = (lax.dot_general(hb_in[rows, 0:n], cta_re, _NT, preferred_element_type=F32)
                  - lax.dot_general(hb_in[rows, n:2 * n], cta_im, _NT, preferred_element_type=F32)
                  + jnp.dot(bf(pe[rows, :]), k0, preferred_element_type=F32))
        po[rows, :] = _gelu_tanh(y_odd + d * po[rows, :])
        pe[rows, :] = _gelu_tanh(y_even + d * pe[rows, :])

    def scatter(jp, c):
        r0 = pl.multiple_of(jp * NS, NS)
        for k in range(nk):
            pad[k, pl.ds(2 * jp, NS, stride=pitch), :] = pe[pl.ds(r0, NS), k * LANES:(k + 1) * LANES]
            pad[k, pl.ds(2 * jp + 1, NS, stride=pitch), :] = po[pl.ds(r0, NS), k * LANES:(k + 1) * LANES]
        return c

    lax.fori_loop(0, H, scatter, 0, unroll=4)
    for s in range(NS):
        for k in range(nk):
            y_ref[s * SL:(s + 1) * SL, k * LANES:(k + 1) * LANES] = pad[k, s * pitch:s * pitch + SL, :]


def _s5(proj, h0r, h0i, bb_re, bb_im, ct_re, ct_im, ab_re, ab_im, d_s5, *, batch, seq, row_block0, convert=()):
    ub0 = COL_UB // S5_BLK_U
    nsteps = batch * S5_BLOCKS
    slice_spec = lambda w: pl.BlockSpec((w.shape[0] // nsteps, w.shape[1]), lambda b, j: (b * S5_BLOCKS + j, 0))
    assert all(w.shape[0] % (2 * SUBLANES * nsteps) == 0 for w in convert)
    seg = seq // SUBLANES
    assert seg & (seg - 1) == 0
    gpb = S5_GROUPS // S5_BLOCKS
    vec = lambda width: pl.BlockSpec((1, width), lambda b, j: (0, j))
    wspec = lambda r: pl.BlockSpec((gpb, r, LANES), lambda b, j: (j, 0, 0))
    st_out = pl.BlockSpec((None, 1, S5_BLK_N), lambda b, j: (b, 0, j))
    return pl.pallas_call(
        functools.partial(_s5_kernel, seq=seq, nconv=len(convert)),
        grid=(batch, S5_BLOCKS),
        in_specs=[
            pl.BlockSpec((seq, S5_BLK_U), lambda b, j: (row_block0 + b, ub0 + j)),
            vec(S5_BLK_N), vec(S5_BLK_N),
            wspec(S5_GROUP), wspec(S5_GROUP), wspec(S5_GROUP), wspec(S5_GROUP),
            vec(S5_BLK_N), vec(S5_BLK_N), vec(S5_BLK_U),
        ] + [slice_spec(w) for w in convert],
        out_specs=[pl.BlockSpec((seq, S5_BLK_U), lambda b, j: (b, j)), st_out, st_out]
        + [slice_spec(w) for w in convert],
        out_shape=[
            jax.ShapeDtypeStruct((batch * seq, S5_WIDTH), F32),
            jax.ShapeDtypeStruct((batch, 1, S5_NSTATE), F32),
            jax.ShapeDtypeStruct((batch, 1, S5_NSTATE), F32),
        ] + [jax.ShapeDtypeStruct(w.shape, BF16) for w in convert],
        scratch_shapes=[
            pltpu.VMEM((S5_BLK_U // LANES, SUBLANES * _s5_pitch(seg), LANES), F32),
            pltpu.VMEM((seq // 2, S5_BLK_U), F32),
            pltpu.VMEM((seq // 2, S5_BLK_U), F32),
            pltpu.VMEM((seq // 2, S5_BLK_N), F32),
            pltpu.VMEM((seq // 2, S5_BLK_N), F32),
            pltpu.VMEM((seq // 2, 2 * S5_BLK_N), BF16),
            pltpu.VMEM((seq // 2, 2 * S5_BLK_N), BF16),
        ],
        compiler_params=_params("arbitrary", "arbitrary"),
        name="s5",
    )(proj, h0r, h0i, bb_re, bb_im, ct_re, ct_im, ab_re, ab_im, d_s5, *convert)


def _s5_step_kernel(u_ref, h0r_ref, h0i_ref, bbr_ref, bbi_ref, ctr_ref, cti_ref, ar_ref, ai_ref, d_ref,
                    y_ref, hr_ref, hi_ref):
    bt_re, bt_im, ct_re, ct_im = (w.astype(BF16) for w in _s5_weights(bbr_ref, bbi_ref, ctr_ref, cti_ref))
    u = u_ref[...]
    ub = u.astype(BF16)
    ar = ar_ref[...]
    ai = ai_ref[...]
    h0r = h0r_ref[...].T
    h0i = h0i_ref[...].T
    hr = jnp.dot(ub, bt_re, preferred_element_type=F32) + (ar * h0r - ai * h0i)
    hi = jnp.dot(ub, bt_im, preferred_element_type=F32) + (ar * h0i + ai * h0r)
    hr_ref[...] = hr.T
    hi_ref[...] = hi.T
    y = (lax.dot_general(hr.astype(BF16), ct_re, _NT, preferred_element_type=F32)
         - lax.dot_general(hi.astype(BF16), ct_im, _NT, preferred_element_type=F32))
    y_ref[...] = _gelu_tanh(y + d_ref[...] * u)


def _s5_step(proj, h0r, h0i, bb_re, bb_im, ct_re, ct_im, ab_re, ab_im, d_s5, *, nseq):
    ub0 = COL_UB // S5_BLK_U
    gpb = S5_GROUPS // S5_BLOCKS
    vec = lambda width: pl.BlockSpec((1, width), lambda j: (0, j))
    wspec = lambda r: pl.BlockSpec((gpb, r, LANES), lambda j: (j, 0, 0))
    st = pl.BlockSpec((S5_BLK_N, nseq), lambda j: (j, 0))
    return pl.pallas_call(
        _s5_step_kernel,
        grid=(S5_BLOCKS,),
        in_specs=[
            pl.BlockSpec((nseq, S5_BLK_U), lambda j: (0, ub0 + j)),
            st, st,
            wspec(S5_GROUP), wspec(S5_GROUP), wspec(S5_GROUP), wspec(S5_GROUP),
            vec(S5_BLK_N), vec(S5_BLK_N), vec(S5_BLK_U),
        ],
        out_specs=[pl.BlockSpec((nseq, S5_BLK_U), lambda j: (0, j)), st, st],
        out_shape=[
            jax.ShapeDtypeStruct((nseq, S5_WIDTH), F32),
            jax.ShapeDtypeStruct((S5_NSTATE, nseq), F32),
            jax.ShapeDtypeStruct((S5_NSTATE, nseq), F32),
        ],
        compiler_params=_params("arbitrary"),
        name="s5_step",
    )(proj, h0r, h0i, bb_re, bb_im, ct_re, ct_im, ab_re, ab_im, d_s5)


def _tail_kernel(yn_ref, ybg_ref, zb_ref, ga_ref, gb_ref, x_ref, wpa_ref, wglu_ref, bglu_ref, wpb_ref,
                 wout_ref, fnw_ref, out_ref):
    sg_a = _sigmoid(ga_ref[...])
    sg_b = _sigmoid(gb_ref[...])
    sz_b = _silu(zb_ref[...])
    yb = ybg_ref[...]
    glu = jnp.dot(yb.astype(BF16), wglu_ref[...], preferred_element_type=F32) + bglu_ref[...]
    ya = jnp.dot(yn_ref[...], wpa_ref[...], preferred_element_type=F32)
    yb = (yb * _sigmoid(glu)) * sz_b
    ybp = jnp.dot(yb.astype(BF16), wpb_ref[...], preferred_element_type=F32)
    mixed = sg_a * ya + sg_b * ybp
    o = x_ref[...] + jnp.dot(mixed.astype(BF16), wout_ref[...], preferred_element_type=F32)
    ms = jnp.mean(o * o, axis=-1, keepdims=True)
    out_ref[...] = (o * lax.rsqrt(ms + NORM_EPS)) * fnw_ref[...]


def _tail(yn, ybg, proj, x, w_proj_a, w_glu, b_glu, w_proj_b, w_out, final_norm_w, *, tm):
    m, d = x.shape
    resident = lambda shape: pl.BlockSpec(shape, lambda i: (0, 0), pipeline_mode=pl.Buffered(1))
    return pl.pallas_call(
        _tail_kernel,
        grid=(m // tm,),
        in_specs=[
            pl.BlockSpec((tm, SSD_WIDTH), lambda i: (i, 0)),
            pl.BlockSpec((tm, S5_WIDTH), lambda i: (i, 0)),
            pl.BlockSpec((tm, S5_WIDTH), lambda i: (i, COL_ZB // S5_WIDTH)),
            pl.BlockSpec((tm, d), lambda i: (i, COL_GA // D_MODEL)),
            pl.BlockSpec((tm, d), lambda i: (i, COL_GB // D_MODEL)),
            pl.BlockSpec((tm, d), lambda i: (i, 0)),
            resident((SSD_WIDTH, d)),
            resident((S5_WIDTH, S5_WIDTH)),
            resident((1, S5_WIDTH)),
            resident((S5_WIDTH, d)),
            resident((d, d)),
            resident((1, d)),
        ],
        out_specs=pl.BlockSpec((tm, d), lambda i: (i, 0)),
        out_shape=jax.ShapeDtypeStruct((m, d), F32),
        compiler_params=_params("arbitrary"),
        name="tail",
    )(yn, ybg, proj, proj, proj, x, w_proj_a, w_glu, b_glu, w_proj_b, w_out, final_norm_w)


def kernel(x_prompt, x_sample, state_ssd, state_conv, state_s5_re, state_s5_im, meta_tokens, norm_w, w_in,
           conv_w, conv_b, dt_bias, a_log, d_ssd, ssd_norm_w, w_proj_a, lam_re, lam_im, log_dt_s5, b_re, b_im,
           c_re, c_im, d_s5, w_glu, b_glu, w_proj_b, w_out, final_norm_w):
    bsz, seq, d = x_prompt.shape
    nseq = x_sample.shape[0]
    assert d == D_MODEL and seq % CHUNK == 0 and nseq % SUBLANES == 0 and norm_w.shape[0] == 1
    assert meta_tokens.shape[0] == N_META and N_META <= CHUNK

    assert w_in.shape[2] == _SRC_GB + D_MODEL
    wt = jnp.transpose(w_in[0])
    w_dt = _wprep_dt(wt)
    pad_heads = lambda v: jnp.pad(v.reshape(1, SSD_HEADS), ((0, 0), (0, LANES - SSD_HEADS)))
    dtb = pad_heads(dt_bias[0])
    alog = pad_heads(a_log[0])
    d_exp = jnp.repeat(d_ssd[0], SSD_HEAD_DIM).reshape(1, SSD_WIDTH)
    nw1 = norm_w[0].reshape(1, d)
    ssd_nw = ssd_norm_w[0].reshape(1, SSD_WIDTH)
    convw = conv_w[0]
    convb = conv_b[0].reshape(1, SSD_XBC)
    bglu = b_glu[0].reshape(1, S5_WIDTH)
    fnw = final_norm_w.reshape(1, d)
    ds5 = d_s5[0].reshape(1, S5_WIDTH)

    rep = lambda v, k: jnp.concatenate([v] * k, axis=-1)
    lane_rep = LANES // S5_STATE
    ab_re, ab_im, bb_re, bb_im = _s5prep(
        rep(lam_re[0], lane_rep), rep(lam_im[0], lane_rep), log_dt_s5[0].reshape(S5_GROUPS, 1),
        rep(jnp.transpose(b_re[0], (0, 2, 1)), lane_rep), rep(jnp.transpose(b_im[0], (0, 2, 1)), lane_rep))
    ab_re = ab_re[:, :S5_STATE].reshape(1, S5_NSTATE)
    ab_im = ab_im[:, :S5_STATE].reshape(1, S5_NSTATE)
    ct_re = rep(c_re[0], lane_rep)
    ct_im = rep(c_im[0], lane_rep)

    x_main = x_prompt.reshape(bsz * seq, d)
    x_small = jnp.concatenate(
        [x_sample.reshape(nseq, d), jnp.zeros((CHUNK - N_META, d), x_prompt.dtype),
         meta_tokens.astype(x_prompt.dtype)], axis=0)
    assert nseq % CHUNK == 0
    meta_blk = nseq // CHUNK
    tm_main = 1024 if (bsz * seq) % 1024 == 0 else CHUNK
    w_main, proj_s, dt_s = _wprep_inproj(wt, w_dt, x_small, nw1)
    proj_m, dt_m = _inproj(x_main, nw1, w_main, w_dt, tm=tm_main)

    ssd_args = (convw, convb, dtb, alog, d_exp, ssd_nw)
    s5_w = (bb_re, bb_im, ct_re, ct_im)
    s5_v = (ab_re, ab_im, ds5)

    zeros_ht = jnp.zeros((SSD_STATE, SSD_WIDTH), F32)
    zeros_tail = jnp.zeros((SUBLANES, SSD_XBC), F32)
    zeros_s5 = jnp.zeros((1, S5_NSTATE), F32)
    _, _, ht_meta, tail_meta = _ssd(proj_s, dt_s, zeros_ht, zeros_tail, *ssd_args, batch=1, nchunks=1,
                                    row_block0=meta_blk, mask_rows=CHUNK - N_META)
    _, s5r_meta, s5i_meta = _s5(proj_s, zeros_s5, zeros_s5, *s5_w, *s5_v, batch=1, seq=CHUNK,
                                row_block0=meta_blk)[:3]

    yn_m, h_m, _, tail_m = _ssd(proj_m, dt_m, ht_meta[0], tail_meta[0], *ssd_args, batch=bsz,
                                nchunks=seq // CHUNK, row_block0=0, mask_rows=0)
    ybg_m, s5r_m, s5i_m, wpa, wglu, wpb, wout = _s5(
        proj_m, s5r_meta[0], s5i_meta[0], *s5_w, *s5_v, batch=bsz, seq=seq, row_block0=0,
        convert=(w_proj_a[0], w_glu[0], w_proj_b[0], w_out[0]))
    tail_w = (wpa, wglu, bglu, wpb, wout, fnw)
    y_prompt = _tail(yn_m, ybg_m, proj_m, x_main, *tail_w, tm=256)

    yn_s, h_s, cs_s = _ssd_step(proj_s, dt_s, jnp.transpose(state_conv[0], (1, 0, 2)),
                                state_ssd[0].reshape(nseq, SSD_WIDTH, SSD_STATE), *ssd_args, nseq=nseq)
    seq_minor = lambda v: jnp.transpose(v, (1, 2, 0)).reshape(S5_NSTATE, nseq)
    seq_major = lambda v: jnp.transpose(v.reshape(S5_GROUPS, S5_STATE, nseq), (2, 0, 1))[None]
    ybg_s, s5r_s, s5i_s = _s5_step(proj_s, seq_minor(state_s5_re[0]), seq_minor(state_s5_im[0]),
                                   *s5_w, *s5_v, nseq=nseq)
    y_sample = _tail(yn_s, ybg_s, proj_s, x_sample.reshape(nseq, d), *tail_w, tm=nseq)

    dt_out = x_prompt.dtype
    return (
        y_prompt.reshape(bsz, seq, d),
        y_sample.reshape(nseq, 1, d),
        h_m.reshape(1, bsz, SSD_HEADS, SSD_HEAD_DIM, SSD_STATE).astype(dt_out),
        tail_m[:, SUBLANES - (SSD_CONV - 1):, :].reshape(1, bsz, SSD_CONV - 1, SSD_XBC),
        s5r_m.reshape(1, bsz, S5_GROUPS, S5_STATE).astype(dt_out),
        s5i_m.reshape(1, bsz, S5_GROUPS, S5_STATE).astype(dt_out),
        h_s.reshape(1, nseq, SSD_HEADS, SSD_HEAD_DIM, SSD_STATE).astype(dt_out),
        jnp.transpose(cs_s, (1, 0, 2)).reshape(1, nseq, SSD_CONV - 1, SSD_XBC),
        seq_major(s5r_s).astype(dt_out),
        seq_major(s5i_s).astype(dt_out),
    )
```

```python
import functools

import jax
import jax.numpy as jnp
from jax import lax
from jax.experimental import pallas as pl
from jax.experimental.pallas import tpu as pltpu

F32 = jnp.float32
BF16 = jnp.bfloat16

NORM_EPS = 1e-5
LOG2E = 1.4426950408889634
N_META = 16
D_MODEL = 2048
SSD_HEAD_DIM = 64
SSD_HEADS = 32
SSD_GROUPS = 4
SSD_STATE = 128
SSD_WIDTH = SSD_HEADS * SSD_HEAD_DIM
SSD_XBC = SSD_WIDTH + 2 * SSD_GROUPS * SSD_STATE
SSD_CONV = 4
CHUNK = 128
S5_WIDTH = D_MODEL // 2
S5_GROUP = 16
S5_GROUPS = S5_WIDTH // S5_GROUP
S5_STATE = 64
S5_NSTATE = S5_GROUPS * S5_STATE
S5_BLOCKS = 4
S5_BLK_U = S5_WIDTH // S5_BLOCKS
S5_BLK_N = S5_NSTATE // S5_BLOCKS

LANES = 128
SUBLANES = 8
VMEM_LIMIT = 56 * 1024 * 1024
CONV_PITCH = 3

COL_ZA = 0
COL_GA = 2048
COL_GB = 4096
COL_XBC = 6144
COL_UB = 9216
COL_ZB = 10240
PROJ_COLS = 11264
PROJ_TN = 1024


_NT = (((1,), (1,)), ((), ()))


def _sigmoid(x):
    return 0.5 * jnp.tanh(0.5 * x) + 0.5


def _silu(x):
    h = 0.5 * x
    return h + h * jnp.tanh(h)


def _softplus(x):
    return jnp.maximum(x, 0.0) + jnp.log1p(jnp.exp(-jnp.abs(x)))


def _gelu_tanh(x):
    c = 0.7978845608028654
    return 0.5 * x * (1.0 + jnp.tanh(c * (x + 0.044715 * (x * x * x))))


def _split3(x):
    x1 = x.astype(BF16)
    r1 = x - x1.astype(F32)
    x2 = r1.astype(BF16)
    x3 = (r1 - x2.astype(F32)).astype(BF16)
    return x1, x2, x3


def _params(*sem):
    return pltpu.CompilerParams(dimension_semantics=sem, vmem_limit_bytes=VMEM_LIMIT)


def _s5prep_kernel(lr_ref, li_ref, ldt_ref, btr_ref, bti_ref, abr_ref, abi_ref, bbr_ref, bbi_ref):
    lr = lr_ref[...]
    li = li_ref[...]
    step = jnp.exp(ldt_ref[...])
    mag = jnp.exp(lr * step)
    abr = mag * jnp.cos(li * step)
    abi = mag * jnp.sin(li * step)
    den = lr * lr + li * li
    numr = abr - 1.0
    cr = (numr * lr + abi * li) / den
    ci = (abi * lr - numr * li) / den
    abr_ref[...] = abr
    abi_ref[...] = abi
    btr = btr_ref[...]
    bti = bti_ref[...]
    crb = cr[:, None, :]
    cib = ci[:, None, :]
    bbr_ref[...] = crb * btr - cib * bti
    bbi_ref[...] = crb * bti + cib * btr


def _s5prep(lam_re, lam_im, log_dt, bt_re, bt_im):
    g, n = lam_re.shape
    full2 = pl.BlockSpec((g, n), lambda: (0, 0))
    full3 = pl.BlockSpec((g, S5_GROUP, n), lambda: (0, 0, 0))
    return pl.pallas_call(
        _s5prep_kernel,
        in_specs=[full2, full2, pl.BlockSpec((g, 1), lambda: (0, 0)), full3, full3],
        out_specs=[full2, full2, full3, full3],
        out_shape=[jax.ShapeDtypeStruct((g, n), F32)] * 2
        + [jax.ShapeDtypeStruct((g, S5_GROUP, n), F32)] * 2,
        name="s5prep",
    )(lam_re, lam_im, log_dt, bt_re, bt_im)


W_ALIGN = 32
_SRC_ZA = 0
_SRC_XBC = SSD_WIDTH
_SRC_DT = _SRC_XBC + SSD_XBC
_SRC_UB = _SRC_DT + SSD_HEADS
_SRC_ZB = _SRC_UB + S5_WIDTH
_SRC_GA = _SRC_ZB + S5_WIDTH
_SRC_GB = _SRC_GA + D_MODEL
_SEGMENTS = ((_SRC_ZA, SSD_WIDTH), (_SRC_GA, D_MODEL), (_SRC_GB, D_MODEL), (_SRC_XBC, SSD_XBC),
             (_SRC_UB, S5_WIDTH), (_SRC_ZB, S5_WIDTH))
_SRC_TILES = tuple((start + k) // W_ALIGN for start, width in _SEGMENTS for k in range(0, width, PROJ_TN))


def _wprep_dt_kernel(w_ref, o_ref):
    o_ref[...] = jnp.zeros(o_ref.shape, o_ref.dtype)
    o_ref[0:SSD_HEADS, :] = w_ref[...].astype(BF16)


def _wprep_dt(wt):
    d = wt.shape[1]
    assert _SRC_DT % SSD_HEADS == 0
    return pl.pallas_call(
        _wprep_dt_kernel,
        grid=(1,),
        in_specs=[pl.BlockSpec((SSD_HEADS, d), lambda i: (_SRC_DT // SSD_HEADS, 0))],
        out_specs=pl.BlockSpec((LANES, d), lambda i: (0, 0)),
        out_shape=jax.ShapeDtypeStruct((LANES, d), BF16),
        name="wprep_dt",
    )(wt)


def _norm_rows(x_ref, nw_ref, wdt_ref, xn_ref, dt_ref):
    x = x_ref[...]
    ms = jnp.mean(x * x, axis=-1, keepdims=True)
    xn = ((x * lax.rsqrt(ms + NORM_EPS)) * nw_ref[...]).astype(BF16)
    xn_ref[...] = xn
    dt_ref[...] = lax.dot_general(xn, wdt_ref[...], _NT, preferred_element_type=F32)


def _wprep_inproj_kernel(tbl_ref, x_ref, nw_ref, w_ref, wdt_ref, wout_ref, proj_ref, dt_ref, xn_ref):
    del tbl_ref

    @pl.when(pl.program_id(0) == 0)
    def _():
        _norm_rows(x_ref, nw_ref, wdt_ref, xn_ref, dt_ref)

    wb = w_ref[...].astype(BF16)
    wout_ref[...] = wb
    proj_ref[...] = lax.dot_general(xn_ref[...], wb, _NT, preferred_element_type=F32)


def _wprep_inproj(wt, w_dt, x, norm_w):
    m, d = x.shape
    ntiles = len(_SRC_TILES)
    assert ntiles * PROJ_TN == PROJ_COLS
    return pl.pallas_call(
        _wprep_inproj_kernel,
        grid_spec=pltpu.PrefetchScalarGridSpec(
            num_scalar_prefetch=1,
            grid=(ntiles,),
            in_specs=[
                pl.BlockSpec((m, d), lambda j, tbl: (0, 0)),
                pl.BlockSpec((1, d), lambda j, tbl: (0, 0)),
                pl.BlockSpec((pl.Element(PROJ_TN), pl.Element(d)),
                             lambda j, tbl: (pl.multiple_of(tbl[j] * W_ALIGN, W_ALIGN), 0)),
                pl.BlockSpec((LANES, d), lambda j, tbl: (0, 0)),
            ],
            out_specs=[
                pl.BlockSpec((PROJ_TN, d), lambda j, tbl: (j, 0)),
                pl.BlockSpec((m, PROJ_TN), lambda j, tbl: (0, j)),
                pl.BlockSpec((m, LANES), lambda j, tbl: (0, 0)),
            ],
            scratch_shapes=[pltpu.VMEM((m, d), BF16)],
        ),
        out_shape=[jax.ShapeDtypeStruct((PROJ_COLS, d), BF16), jax.ShapeDtypeStruct((m, PROJ_COLS), F32),
                   jax.ShapeDtypeStruct((m, LANES), F32)],
        compiler_params=_params("arbitrary"),
        name="wprep_inproj",
    )(jnp.asarray(_SRC_TILES, jnp.int32), x, norm_w, wt, w_dt)


def _inproj_kernel(x_ref, nw_ref, w_ref, wdt_ref, proj_ref, dt_ref, xn_ref):
    @pl.when(pl.program_id(1) == 0)
    def _():
        _norm_rows(x_ref, nw_ref, wdt_ref, xn_ref, dt_ref)

    proj_ref[...] = lax.dot_general(xn_ref[...], w_ref[...], _NT, preferred_element_type=F32)


def _inproj(x, norm_w, w_main, w_dt, tm):
    m, d = x.shape
    n = w_main.shape[0]
    return pl.pallas_call(
        _inproj_kernel,
        grid=(m // tm, n // PROJ_TN),
        in_specs=[
            pl.BlockSpec((tm, d), lambda i, j: (i, 0)),
            pl.BlockSpec((1, d), lambda i, j: (0, 0)),
            pl.BlockSpec((PROJ_TN, d), lambda i, j: (j, 0)),
            pl.BlockSpec((LANES, d), lambda i, j: (0, 0)),
        ],
        out_specs=[
            pl.BlockSpec((tm, PROJ_TN), lambda i, j: (i, j)),
            pl.BlockSpec((tm, LANES), lambda i, j: (i, 0)),
        ],
        out_shape=[jax.ShapeDtypeStruct((m, n), F32), jax.ShapeDtypeStruct((m, LANES), F32)],
        scratch_shapes=[pltpu.VMEM((tm, d), BF16)],
        compiler_params=_params("arbitrary", "arbitrary"),
        name="inproj",
    )(x, norm_w, w_main, w_dt)


def _ssd_chunk(r0, xbc_ref, za_ref, dtr_ref, convw_ref, convb_ref, dtb_ref, alog_ref, dexp_ref, nw_ref,
               y_ref, ht_scr, ext_scr, mask_rows, need_y):
    L = CHUNK
    P2 = 2 * SSD_HEAD_DIM
    GW = SSD_WIDTH // SSD_GROUPS
    time_rows = lambda t0, n: pl.ds(CONV_PITCH * t0, n, stride=CONV_PITCH)

    w = convw_ref[...]
    bias = convb_ref[...]
    conv_parts = []
    for s in range(SSD_XBC // LANES):
        ls = slice(s * LANES, (s + 1) * LANES)
        xc = xbc_ref[r0:r0 + L, ls]
        ext_scr[s, time_rows(SUBLANES, L), :] = xc
        acc = bias[:, ls]
        for k in range(SSD_CONV - 1):
            acc = acc + ext_scr[s, time_rows(SUBLANES - (SSD_CONV - 1) + k, L), :] * w[k:k + 1, ls]
        conv_parts.append(acc + xc * w[SSD_CONV - 1:SSD_CONV, ls])
        ext_scr[s, time_rows(0, SUBLANES), :] = xc[L - SUBLANES:L, :]
    xbc = _silu(jnp.concatenate(conv_parts, axis=1))
    xs = xbc[:, :SSD_WIDTH]
    bmat = xbc[:, SSD_WIDTH:SSD_WIDTH + SSD_GROUPS * SSD_STATE]
    cmat = xbc[:, SSD_WIDTH + SSD_GROUPS * SSD_STATE:]

    rows = lax.broadcasted_iota(jnp.int32, (L, L), 0)
    cols = lax.broadcasted_iota(jnp.int32, (L, L), 1)
    causal = rows >= cols
    lane_lo = cols < SSD_HEAD_DIM

    dt = _softplus(dtr_ref[r0:r0 + L, :] + dtb_ref[...])
    if mask_rows:
        dt = jnp.where(rows < mask_rows, 0.0, dt)
    a = dt * (-jnp.exp(alog_ref[...]))
    tril = jnp.where(causal, 1.0, 0.0).astype(BF16)
    a1, a2, a3 = _split3(a)
    acum = (jnp.dot(tril, a1, preferred_element_type=F32)
            + jnp.dot(tril, a2, preferred_element_type=F32)
            + jnp.dot(tril, a3, preferred_element_type=F32))
    a2 = acum * LOG2E
    e_cum = jnp.exp2(a2)
    w_end = dt * jnp.exp2(a2[L - 1:L, :] - a2)
    a2dt_t = (a2 - jnp.log2(dt)).T

    dexp = dexp_ref[...]
    xs_b = xs.astype(BF16)
    y_parts = []
    for g in range(SSD_GROUPS):
        bg = bmat[:, g * SSD_STATE:(g + 1) * SSD_STATE].astype(BF16)
        cg = cmat[:, g * SSD_STATE:(g + 1) * SSD_STATE].astype(BF16)
        ht_g = ht_scr[:, g * GW:(g + 1) * GW]
        if need_y:
            cb = lax.dot_general(cg, bg, (((1,), (1,)), ((), ())), preferred_element_type=F32)
            y_off = jnp.dot(cg, ht_g.astype(BF16), preferred_element_type=F32)
        xw_parts = []
        elast_parts = []
        for jj in range(GW // P2):
            lo = g * GW + jj * P2
            h0 = lo // SSD_HEAD_DIM
            yd, eb, wb = [], [], []
            for h in (h0, h0 + 1):
                if need_y:
                    colb = jnp.broadcast_to(a2[:, h:h + 1], (L, L))
                    m = cb * jnp.exp2(jnp.where(causal, colb - a2dt_t[h:h + 1, :], -jnp.inf))
                    yd.append(jnp.dot(m.astype(BF16), xs_b[:, lo:lo + P2], preferred_element_type=F32))
                eb.append(jnp.broadcast_to(e_cum[:, h:h + 1], (L, L)))
                wb.append(jnp.broadcast_to(w_end[:, h:h + 1], (L, L)))
            xs_pair = xs[:, lo:lo + P2]
            if need_y:
                y_pair = (jnp.where(lane_lo, yd[0], yd[1])
                          + y_off[:, jj * P2:(jj + 1) * P2] * jnp.where(lane_lo, eb[0], eb[1]))
                y_parts.append(y_pair + dexp[:, lo:lo + P2] * xs_pair)
            xw_parts.append(xs_pair * jnp.where(lane_lo, wb[0], wb[1]))
            elast_parts.append(jnp.where(lane_lo[0:1, :], eb[0][L - 1:L, :], eb[1][L - 1:L, :]))
        xw = jnp.concatenate(xw_parts, axis=1)
        elast = jnp.concatenate(elast_parts, axis=1)
        st = lax.dot_general(bg, xw.astype(BF16), (((0,), (0,)), ((), ())),
                             preferred_element_type=F32)
        ht_scr[:, g * GW:(g + 1) * GW] = ht_g * elast + st

    if not need_y:
        y_ref[r0:r0 + L, :] = jnp.zeros((L, SSD_WIDTH), y_ref.dtype)
        return
    y = jnp.concatenate(y_parts, axis=1)
    y = y * _silu(za_ref[r0:r0 + L, :])
    nw = nw_ref[...]
    outs = []
    for g in range(SSD_GROUPS):
        yg = y[:, g * GW:(g + 1) * GW]
        ms = jnp.mean(yg * yg, axis=-1, keepdims=True)
        outs.append((yg * lax.rsqrt(ms + NORM_EPS)) * nw[:, g * GW:(g + 1) * GW])
    y_ref[r0:r0 + L, :] = jnp.concatenate(outs, axis=1).astype(y_ref.dtype)


def _ssd_kernel(xbc_ref, za_ref, dtr_ref, ht0_ref, tail0_ref, convw_ref, convb_ref, dtb_ref, alog_ref,
                dexp_ref, nw_ref, y_ref, h_ref, ht_ref, tail_ref, ht_scr, ext_scr, *, mask_rows, need_y):
    c = pl.program_id(1)
    nrows = xbc_ref.shape[0]

    @pl.when(c == 0)
    def _():
        ht_scr[...] = ht0_ref[...]
        for s in range(SSD_XBC // LANES):
            ext_scr[s, pl.ds(0, SUBLANES, stride=CONV_PITCH), :] = tail0_ref[:, s * LANES:(s + 1) * LANES]

    for r0 in range(0, nrows, CHUNK):
        _ssd_chunk(r0, xbc_ref, za_ref, dtr_ref, convw_ref, convb_ref, dtb_ref, alog_ref, dexp_ref, nw_ref,
                   y_ref, ht_scr, ext_scr, mask_rows, need_y)

    @pl.when(c == pl.num_programs(1) - 1)
    def _():
        ht = ht_scr[...]
        ht_ref[...] = ht
        h_ref[...] = ht.T
        tail_ref[...] = xbc_ref[nrows - SUBLANES:nrows, :]


def _ssd(proj, dt_raw, ht0, tail0, conv_w, conv_b, dt_bias, a_log, d_exp, norm_w, *, batch, nchunks,
         row_block0, mask_rows, need_y=True):
    L = CHUNK
    rows = batch * nchunks * L
    rb = lambda b, c: row_block0 + b * nchunks + c
    const2 = lambda shape: pl.BlockSpec(shape, lambda b, c: (0, 0))
    return pl.pallas_call(
        functools.partial(_ssd_kernel, mask_rows=mask_rows, need_y=need_y),
        grid=(batch, nchunks),
        in_specs=[
            pl.BlockSpec((L, SSD_XBC), lambda b, c: (rb(b, c), COL_XBC // SSD_XBC)),
            pl.BlockSpec((L, SSD_WIDTH), lambda b, c: (rb(b, c), COL_ZA // SSD_WIDTH)),
            pl.BlockSpec((L, LANES), lambda b, c: (rb(b, c), 0)),
            const2((SSD_STATE, SSD_WIDTH)),
            const2((SUBLANES, SSD_XBC)),
            const2((SSD_CONV, SSD_XBC)),
            const2((1, SSD_XBC)),
            const2((1, LANES)),
            const2((1, LANES)),
            const2((1, SSD_WIDTH)),
            const2((1, SSD_WIDTH)),
        ],
        out_specs=[
            pl.BlockSpec((L, SSD_WIDTH), lambda b, c: (b * nchunks + c, 0)),
            pl.BlockSpec((None, SSD_WIDTH, SSD_STATE), lambda b, c: (b, 0, 0)),
            pl.BlockSpec((None, SSD_STATE, SSD_WIDTH), lambda b, c: (b, 0, 0)),
            pl.BlockSpec((None, SUBLANES, SSD_XBC), lambda b, c: (b, 0, 0)),
        ],
        out_shape=[
            jax.ShapeDtypeStruct((rows, SSD_WIDTH), BF16),
            jax.ShapeDtypeStruct((batch, SSD_WIDTH, SSD_STATE), F32),
            jax.ShapeDtypeStruct((batch, SSD_STATE, SSD_WIDTH), F32),
            jax.ShapeDtypeStruct((batch, SUBLANES, SSD_XBC), F32),
        ],
        scratch_shapes=[pltpu.VMEM((SSD_STATE, SSD_WIDTH), F32),
                        pltpu.VMEM((SSD_XBC // LANES, CONV_PITCH * (SUBLANES + CHUNK), LANES), F32)],
        compiler_params=_params("arbitrary", "arbitrary"),
        name="ssd",
    )(proj, proj, dt_raw, ht0, tail0, conv_w, conv_b, dt_bias, a_log, d_exp, norm_w)


def _ssd_step_kernel(xbc_ref, za_ref, dtr_ref, cs_ref, h_ref, convw_ref, convb_ref, dtb_ref, alog_ref,
                     dexp_ref, nw_ref, y_ref, hout_ref, csout_ref):
    R = SUBLANES
    GW = SSD_WIDTH // SSD_GROUPS
    x = xbc_ref[...]
    w = convw_ref[...]
    s0 = cs_ref[0]
    s1 = cs_ref[1]
    s2 = cs_ref[2]
    conv = convb_ref[...] + s0 * w[0:1]
    conv = conv + s1 * w[1:2]
    conv = conv + s2 * w[2:3]
    conv = conv + x * w[3:4]
    csout_ref[0] = s1
    csout_ref[1] = s2
    csout_ref[2] = x
    xbc = _silu(conv)
    xs = xbc[:, :SSD_WIDTH]
    bmat = xbc[:, SSD_WIDTH:SSD_WIDTH + SSD_GROUPS * SSD_STATE]
    cmat = xbc[:, SSD_WIDTH + SSD_GROUPS * SSD_STATE:]

    dt = _softplus(dtr_ref[...] + dtb_ref[...])
    da = jnp.exp(dt * (-jnp.exp(alog_ref[...])))
    dt_t = dt.T
    da_t = da.T
    expand = lambda v: jnp.concatenate(
        [jnp.broadcast_to(v[h:h + 1, :], (SSD_HEAD_DIM, R)) for h in range(SSD_HEADS)], axis=0)
    xd_t = xs.T * expand(dt_t)
    da_te = expand(da_t)

    cmat_b = cmat.astype(BF16)
    yrows = []
    for i in range(R):
        bexp = jnp.concatenate(
            [jnp.broadcast_to(bmat[i:i + 1, g * SSD_STATE:(g + 1) * SSD_STATE], (GW, SSD_STATE))
             for g in range(SSD_GROUPS)], axis=0)
        hn = h_ref[i] * da_te[:, i:i + 1] + xd_t[:, i:i + 1] * bexp
        hout_ref[i] = hn
        cg = jnp.concatenate([cmat_b[i:i + 1, g * SSD_STATE:(g + 1) * SSD_STATE] for g in range(SSD_GROUPS)]
                             + [jnp.zeros((R - SSD_GROUPS, SSD_STATE), BF16)], axis=0)
        yg = lax.dot_general(cg, hn.astype(BF16), (((1,), (1,)), ((), ())), preferred_element_type=F32)
        yrows.append(jnp.concatenate([yg[g:g + 1, g * GW:(g + 1) * GW] for g in range(SSD_GROUPS)], axis=1))
    y = jnp.concatenate(yrows, axis=0)
    y = y + dexp_ref[...] * xs
    y = y * _silu(za_ref[...])
    nw = nw_ref[...]
    outs = []
    for g in range(SSD_GROUPS):
        yg = y[:, g * GW:(g + 1) * GW]
        ms = jnp.mean(yg * yg, axis=-1, keepdims=True)
        outs.append((yg * lax.rsqrt(ms + NORM_EPS)) * nw[:, g * GW:(g + 1) * GW])
    y_ref[...] = jnp.concatenate(outs, axis=1).astype(y_ref.dtype)


def _ssd_step(proj, dt_raw, conv_state, ssd_state, conv_w, conv_b, dt_bias, a_log, d_exp, norm_w, *, nseq):
    R = SUBLANES
    const2 = lambda shape: pl.BlockSpec(shape, lambda i: (0, 0))
    return pl.pallas_call(
        _ssd_step_kernel,
        grid=(nseq // R,),
        in_specs=[
            pl.BlockSpec((R, SSD_XBC), lambda i: (i, COL_XBC // SSD_XBC)),
            pl.BlockSpec((R, SSD_WIDTH), lambda i: (i, COL_ZA // SSD_WIDTH)),
            pl.BlockSpec((R, LANES), lambda i: (i, 0)),
            pl.BlockSpec((SSD_CONV - 1, R, SSD_XBC), lambda i: (0, i, 0)),
            pl.BlockSpec((R, SSD_WIDTH, SSD_STATE), lambda i: (i, 0, 0)),
            const2((SSD_CONV, SSD_XBC)),
            const2((1, SSD_XBC)),
            const2((1, LANES)),
            const2((1, LANES)),
            const2((1, SSD_WIDTH)),
            const2((1, SSD_WIDTH)),
        ],
        out_specs=[
            pl.BlockSpec((R, SSD_WIDTH), lambda i: (i, 0)),
            pl.BlockSpec((R, SSD_WIDTH, SSD_STATE), lambda i: (i, 0, 0)),
            pl.BlockSpec((SSD_CONV - 1, R, SSD_XBC), lambda i: (0, i, 0)),
        ],
        out_shape=[
            jax.ShapeDtypeStruct((nseq, SSD_WIDTH), BF16),
            jax.ShapeDtypeStruct((nseq, SSD_WIDTH, SSD_STATE), F32),
            jax.ShapeDtypeStruct((SSD_CONV - 1, nseq, SSD_XBC), F32),
        ],
        compiler_params=_params("arbitrary"),
        name="ssd_step",
    )(proj, proj, dt_raw, conv_state, ssd_state, conv_w, conv_b, dt_bias, a_log, d_exp, norm_w)


def _s5_block_diag(w, rows_per_group, cols_per_group):
    ngrp, r, lanes = w.shape
    width = ngrp * cols_per_group
    tiled = jnp.concatenate([w.reshape(ngrp * r, lanes)] * (width // lanes), axis=1)
    row_g = lax.broadcasted_iota(jnp.int32, tiled.shape, 0) // rows_per_group
    col_g = lax.broadcasted_iota(jnp.int32, tiled.shape, 1) // cols_per_group
    return jnp.where(row_g == col_g, tiled, 0.0)


def _s5_weights(bbr_ref, bbi_ref, cr_ref, ci_ref):
    return tuple(_s5_block_diag(r[...], S5_GROUP, S5_STATE) for r in (bbr_ref, bbi_ref, cr_ref, ci_ref))


def _s5_pitch(seg_len):
    return seg_len if (seg_len // SUBLANES) % 2 else seg_len + SUBLANES


def _s5_kernel(*refs, seq, nconv, need_y):
    (u_ref, h0r_ref, h0i_ref, bbr_ref, bbi_ref, ctr_ref, cti_ref, ar_ref, ai_ref, d_ref) = refs[:10]
    w_in_refs = refs[10:10 + nconv]
    y_ref, hr_ref, hi_ref = refs[10 + nconv:13 + nconv]
    w_out_refs = refs[13 + nconv:13 + 2 * nconv]
    pad, pe, po, cr, ci, hb_in, hb_out = refs[13 + 2 * nconv:]
    for wi, wo in zip(w_in_refs, w_out_refs):
        wo[...] = wi[...].astype(BF16)

    NS = SUBLANES
    SL = seq // NS
    H = SL // 2
    pitch = _s5_pitch(SL)
    nk = S5_BLK_U // LANES
    n = S5_BLK_N
    bt_re, bt_im, ct_re, ct_im = _s5_weights(bbr_ref, bbi_ref, ctr_ref, cti_ref)
    a_re = ar_ref[...]
    a_im = ai_ref[...]
    bf = lambda v: v.astype(BF16)
    w2_re = jnp.concatenate([bf(bt_re * a_re - bt_im * a_im), bf(bt_re)], axis=0)
    w2_im = jnp.concatenate([bf(bt_re * a_im + bt_im * a_re), bf(bt_im)], axis=0)
    if need_y:
        cta_re = bf(ct_re * a_re - ct_im * a_im)
        cta_im = bf(ct_re * a_im + ct_im * a_re)
        ctb_re = bf(ct_re)
        ctb_im = bf(ct_im)
        k0 = bf(lax.dot_general(bf(bt_re), ctb_re, _NT, preferred_element_type=F32)
                - lax.dot_general(bf(bt_im), ctb_im, _NT, preferred_element_type=F32))

    for s in range(NS):
        for k in range(nk):
            pad[k, s * pitch:s * pitch + SL, :] = u_ref[s * SL:(s + 1) * SL, k * LANES:(k + 1) * LANES]

    def gather(jp, c):
        r0 = pl.multiple_of(jp * NS, NS)
        for k in range(nk):
            pe[pl.ds(r0, NS), k * LANES:(k + 1) * LANES] = pad[k, pl.ds(2 * jp, NS, stride=pitch), :]
            po[pl.ds(r0, NS), k * LANES:(k + 1) * LANES] = pad[k, pl.ds(2 * jp + 1, NS, stride=pitch), :]
        return c

    lax.fori_loop(0, H, gather, 0, unroll=4)
    ue = pe[...]
    uo = po[...]
    ue_b = bf(ue)
    u2 = jnp.concatenate([ue_b, bf(uo)], axis=1)
    cr[...] = jnp.dot(u2, w2_re, preferred_element_type=F32)
    ci[...] = jnp.dot(u2, w2_im, preferred_element_type=F32)

    a2_re = a_re * a_re - a_im * a_im
    a2_im = 2.0 * (a_re * a_im)
    ar = jnp.broadcast_to(a2_re, (NS, n))
    ai = jnp.broadcast_to(a2_im, (NS, n))

    def step(carry, r0):
        hr, hi = carry
        nr = ar * hr - ai * hi + cr[pl.ds(r0, NS), :]
        ni = ar * hi + ai * hr + ci[pl.ds(r0, NS), :]
        return nr, ni

    def scan_ends(j, carry):
        return step(carry, pl.multiple_of(j * NS, NS))

    zero = jnp.zeros((NS, n), F32)
    er, ei = lax.fori_loop(0, H, scan_ends, (zero, zero), unroll=4)

    asr, asi = a_re, a_im
    for _ in range(SL.bit_length() - 1):
        asr, asi = asr * asr - asi * asi, 2.0 * (asr * asi)
    gr = [h0r_ref[...]]
    gi = [h0i_ref[...]]
    for s in range(NS):
        gr.append(er[s:s + 1, :] + (asr * gr[s] - asi * gi[s]))
        gi.append(ei[s:s + 1, :] + (asr * gi[s] + asi * gr[s]))
    hr_ref[...] = gr[NS]
    hi_ref[...] = gi[NS]
    if not need_y:
        y_ref[...] = jnp.zeros(y_ref.shape, y_ref.dtype)
        return

    def scan_store(jp, c0):
        r0 = pl.multiple_of(jp * 2 * NS, 2 * NS)
        c1 = step(c0, r0)
        c2 = step(c1, r0 + NS)
        hb_in[pl.ds(r0, 2 * NS), 0:n] = bf(jnp.concatenate([c0[0], c1[0]], axis=0))
        hb_in[pl.ds(r0, 2 * NS), n:2 * n] = bf(jnp.concatenate([c0[1], c1[1]], axis=0))
        hb_out[pl.ds(r0, 2 * NS), 0:n] = bf(jnp.concatenate([c1[0], c2[0]], axis=0))
        hb_out[pl.ds(r0, 2 * NS), n:2 * n] = bf(jnp.concatenate([c1[1], c2[1]], axis=0))
        return c2

    lax.fori_loop(0, H // 2, scan_store,
                  (jnp.concatenate(gr[:NS], axis=0), jnp.concatenate(gi[:NS], axis=0)), unroll=2)

    half = seq // 2
    rb = half // 4
    d = d_ref[...]
    for r in range(0, half, rb):
        rows = slice(r, r + rb)
        y_odd = (lax.dot_general(hb_out[rows, 0:n], ctb_re, _NT, preferred_element_type=F32)
                 - lax.dot_general(hb_out[rows, n:2 * n], ctb_im, _NT, preferred_element_type=F32))
        y_even = (lax.dot_general(hb_in[rows, 0:n], cta_re, _NT, preferred_element_type=F32)
                  - lax.dot_general(hb_in[rows, n:2 * n], cta_im, _NT, preferred_element_type=F32)
                  + jnp.dot(bf(pe[rows, :]), k0, preferred_element_type=F32))
        po[rows, :] = _gelu_tanh(y_odd + d * po[rows, :])
        pe[rows, :] = _gelu_tanh(y_even + d * pe[rows, :])

    def scatter(jp, c):
        r0 = pl.multiple_of(jp * NS, NS)
        for k in range(nk):
            pad[k, pl.ds(2 * jp, NS, stride=pitch), :] = pe[pl.ds(r0, NS), k * LANES:(k + 1) * LANES]
            pad[k, pl.ds(2 * jp + 1, NS, stride=pitch), :] = po[pl.ds(r0, NS), k * LANES:(k + 1) * LANES]
        return c

    lax.fori_loop(0, H, scatter, 0, unroll=4)
    for s in range(NS):
        for k in range(nk):
            y_ref[s * SL:(s + 1) * SL, k * LANES:(k + 1) * LANES] = pad[k, s * pitch:s * pitch + SL, :]


def _s5(proj, h0r, h0i, bb_re, bb_im, ct_re, ct_im, ab_re, ab_im, d_s5, *, batch, seq, row_block0, need_y=True,
        convert=()):
    ub0 = COL_UB // S5_BLK_U
    nsteps = batch * S5_BLOCKS
    slice_spec = lambda w: pl.BlockSpec((w.shape[0] // nsteps, w.shape[1]), lambda b, j: (b * S5_BLOCKS + j, 0))
    assert all(w.shape[0] % (2 * SUBLANES * nsteps) == 0 for w in convert)
    seg = seq // SUBLANES
    assert seg & (seg - 1) == 0
    gpb = S5_GROUPS // S5_BLOCKS
    vec = lambda width: pl.BlockSpec((1, width), lambda b, j: (0, j))
    wspec = lambda r: pl.BlockSpec((gpb, r, LANES), lambda b, j: (j, 0, 0))
    st_out = pl.BlockSpec((None, 1, S5_BLK_N), lambda b, j: (b, 0, j))
    return pl.pallas_call(
        functools.partial(_s5_kernel, seq=seq, nconv=len(convert), need_y=need_y),
        grid=(batch, S5_BLOCKS),
        in_specs=[
            pl.BlockSpec((seq, S5_BLK_U), lambda b, j: (row_block0 + b, ub0 + j)),
            vec(S5_BLK_N), vec(S5_BLK_N),
            wspec(S5_GROUP), wspec(S5_GROUP), wspec(S5_GROUP), wspec(S5_GROUP),
            vec(S5_BLK_N), vec(S5_BLK_N), vec(S5_BLK_U),
        ] + [slice_spec(w) for w in convert],
        out_specs=[pl.BlockSpec((seq, S5_BLK_U), lambda b, j: (b, j)), st_out, st_out]
        + [slice_spec(w) for w in convert],
        out_shape=[
            jax.ShapeDtypeStruct((batch * seq, S5_WIDTH), F32),
            jax.ShapeDtypeStruct((batch, 1, S5_NSTATE), F32),
            jax.ShapeDtypeStruct((batch, 1, S5_NSTATE), F32),
        ] + [jax.ShapeDtypeStruct(w.shape, BF16) for w in convert],
        scratch_shapes=[
            pltpu.VMEM((S5_BLK_U // LANES, SUBLANES * _s5_pitch(seg), LANES), F32),
            pltpu.VMEM((seq // 2, S5_BLK_U), F32),
            pltpu.VMEM((seq // 2, S5_BLK_U), F32),
            pltpu.VMEM((seq // 2, S5_BLK_N), F32),
            pltpu.VMEM((seq // 2, S5_BLK_N), F32),
            pltpu.VMEM((seq // 2, 2 * S5_BLK_N), BF16),
            pltpu.VMEM((seq // 2, 2 * S5_BLK_N), BF16),
        ],
        compiler_params=_params("arbitrary", "arbitrary"),
        name="s5",
    )(proj, h0r, h0i, bb_re, bb_im, ct_re, ct_im, ab_re, ab_im, d_s5, *convert)


def _s5_step_kernel(u_ref, h0r_ref, h0i_ref, bbr_ref, bbi_ref, ctr_ref, cti_ref, ar_ref, ai_ref, d_ref,
                    y_ref, hr_ref, hi_ref):
    bt_re, bt_im, ct_re, ct_im = (w.astype(BF16) for w in _s5_weights(bbr_ref, bbi_ref, ctr_ref, cti_ref))
    u = u_ref[...]
    ub = u.astype(BF16)
    ar = ar_ref[...]
    ai = ai_ref[...]
    h0r = h0r_ref[...].T
    h0i = h0i_ref[...].T
    hr = jnp.dot(ub, bt_re, preferred_element_type=F32) + (ar * h0r - ai * h0i)
    hi = jnp.dot(ub, bt_im, preferred_element_type=F32) + (ar * h0i + ai * h0r)
    hr_ref[...] = hr.T
    hi_ref[...] = hi.T
    y = (lax.dot_general(hr.astype(BF16), ct_re, _NT, preferred_element_type=F32)
         - lax.dot_general(hi.astype(BF16), ct_im, _NT, preferred_element_type=F32))
    y_ref[...] = _gelu_tanh(y + d_ref[...] * u)


def _s5_step(proj, h0r, h0i, bb_re, bb_im, ct_re, ct_im, ab_re, ab_im, d_s5, *, nseq):
    ub0 = COL_UB // S5_BLK_U
    gpb = S5_GROUPS // S5_BLOCKS
    vec = lambda width: pl.BlockSpec((1, width), lambda j: (0, j))
    wspec = lambda r: pl.BlockSpec((gpb, r, LANES), lambda j: (j, 0, 0))
    st = pl.BlockSpec((S5_BLK_N, nseq), lambda j: (j, 0))
    return pl.pallas_call(
        _s5_step_kernel,
        grid=(S5_BLOCKS,),
        in_specs=[
            pl.BlockSpec((nseq, S5_BLK_U), lambda j: (0, ub0 + j)),
            st, st,
            wspec(S5_GROUP), wspec(S5_GROUP), wspec(S5_GROUP), wspec(S5_GROUP),
            vec(S5_BLK_N), vec(S5_BLK_N), vec(S5_BLK_U),
        ],
        out_specs=[pl.BlockSpec((nseq, S5_BLK_U), lambda j: (0, j)), st, st],
        out_shape=[
            jax.ShapeDtypeStruct((nseq, S5_WIDTH), F32),
            jax.ShapeDtypeStruct((S5_NSTATE, nseq), F32),
            jax.ShapeDtypeStruct((S5_NSTATE, nseq), F32),
        ],
        compiler_params=_params("arbitrary"),
        name="s5_step",
    )(proj, h0r, h0i, bb_re, bb_im, ct_re, ct_im, ab_re, ab_im, d_s5)


def _tail_kernel(yn_ref, ybg_ref, zb_ref, ga_ref, gb_ref, x_ref, wpa_ref, wglu_ref, bglu_ref, wpb_ref,
                 wout_ref, fnw_ref, out_ref):
    sg_a = _sigmoid(ga_ref[...])
    sg_b = _sigmoid(gb_ref[...])
    sz_b = _silu(zb_ref[...])
    yb = ybg_ref[...]
    glu = jnp.dot(yb.astype(BF16), wglu_ref[...], preferred_element_type=F32) + bglu_ref[...]
    ya = jnp.dot(yn_ref[...], wpa_ref[...], preferred_element_type=F32)
    yb = (yb * _sigmoid(glu)) * sz_b
    ybp = jnp.dot(yb.astype(BF16), wpb_ref[...], preferred_element_type=F32)
    mixed = sg_a * ya + sg_b * ybp
    o = x_ref[...] + jnp.dot(mixed.astype(BF16), wout_ref[...], preferred_element_type=F32)
    ms = jnp.mean(o * o, axis=-1, keepdims=True)
    out_ref[...] = (o * lax.rsqrt(ms + NORM_EPS)) * fnw_ref[...]


def _tail(yn, ybg, proj, x, w_proj_a, w_glu, b_glu, w_proj_b, w_out, final_norm_w, *, tm):
    m, d = x.shape
    resident = lambda shape: pl.BlockSpec(shape, lambda i: (0, 0), pipeline_mode=pl.Buffered(1))
    return pl.pallas_call(
        _tail_kernel,
        grid=(m // tm,),
        in_specs=[
            pl.BlockSpec((tm, SSD_WIDTH), lambda i: (i, 0)),
            pl.BlockSpec((tm, S5_WIDTH), lambda i: (i, 0)),
            pl.BlockSpec((tm, S5_WIDTH), lambda i: (i, COL_ZB // S5_WIDTH)),
            pl.BlockSpec((tm, d), lambda i: (i, COL_GA // D_MODEL)),
            pl.BlockSpec((tm, d), lambda i: (i, COL_GB // D_MODEL)),
            pl.BlockSpec((tm, d), lambda i: (i, 0)),
            resident((SSD_WIDTH, d)),
            resident((S5_WIDTH, S5_WIDTH)),
            resident((1, S5_WIDTH)),
            resident((S5_WIDTH, d)),
            resident((d, d)),
            resident((1, d)),
        ],
        out_specs=pl.BlockSpec((tm, d), lambda i: (i, 0)),
        out_shape=jax.ShapeDtypeStruct((m, d), F32),
        compiler_params=_params("arbitrary"),
        name="tail",
    )(yn, ybg, proj, proj, proj, x, w_proj_a, w_glu, b_glu, w_proj_b, w_out, final_norm_w)


def kernel(x_prompt, x_sample, state_ssd, state_conv, state_s5_re, state_s5_im, meta_tokens, norm_w, w_in,
           conv_w, conv_b, dt_bias, a_log, d_ssd, ssd_norm_w, w_proj_a, lam_re, lam_im, log_dt_s5, b_re, b_im,
           c_re, c_im, d_s5, w_glu, b_glu, w_proj_b, w_out, final_norm_w):
    bsz, seq, d = x_prompt.shape
    nseq = x_sample.shape[0]
    assert d == D_MODEL and seq % CHUNK == 0 and nseq % SUBLANES == 0 and norm_w.shape[0] == 1
    assert meta_tokens.shape[0] == N_META and N_META <= CHUNK

    assert w_in.shape[2] == _SRC_GB + D_MODEL
    wt = jnp.transpose(w_in[0])
    w_dt = _wprep_dt(wt)
    pad_heads = lambda v: jnp.pad(v.reshape(1, SSD_HEADS), ((0, 0), (0, LANES - SSD_HEADS)))
    dtb = pad_heads(dt_bias[0])
    alog = pad_heads(a_log[0])
    d_exp = jnp.repeat(d_ssd[0], SSD_HEAD_DIM).reshape(1, SSD_WIDTH)
    nw1 = norm_w[0].reshape(1, d)
    ssd_nw = ssd_norm_w[0].reshape(1, SSD_WIDTH)
    convw = conv_w[0]
    convb = conv_b[0].reshape(1, SSD_XBC)
    bglu = b_glu[0].reshape(1, S5_WIDTH)
    fnw = final_norm_w.reshape(1, d)
    ds5 = d_s5[0].reshape(1, S5_WIDTH)

    rep = lambda v, k: jnp.concatenate([v] * k, axis=-1)
    lane_rep = LANES // S5_STATE
    ab_re, ab_im, bb_re, bb_im = _s5prep(
        rep(lam_re[0], lane_rep), rep(lam_im[0], lane_rep), log_dt_s5[0].reshape(S5_GROUPS, 1),
        rep(jnp.transpose(b_re[0], (0, 2, 1)), lane_rep), rep(jnp.transpose(b_im[0], (0, 2, 1)), lane_rep))
    ab_re = ab_re[:, :S5_STATE].reshape(1, S5_NSTATE)
    ab_im = ab_im[:, :S5_STATE].reshape(1, S5_NSTATE)
    ct_re = rep(c_re[0], lane_rep)
    ct_im = rep(c_im[0], lane_rep)

    x_main = x_prompt.reshape(bsz * seq, d)
    x_small = jnp.concatenate(
        [x_sample.reshape(nseq, d), jnp.zeros((CHUNK - N_META, d), x_prompt.dtype),
         meta_tokens.astype(x_prompt.dtype)], axis=0)
    assert nseq % CHUNK == 0
    meta_blk = nseq // CHUNK
    tm_main = 1024 if (bsz * seq) % 1024 == 0 else CHUNK
    w_main, proj_s, dt_s = _wprep_inproj(wt, w_dt, x_small, nw1)
    proj_m, dt_m = _inproj(x_main, nw1, w_main, w_dt, tm=tm_main)

    ssd_args = (convw, convb, dtb, alog, d_exp, ssd_nw)
    s5_w = (bb_re, bb_im, ct_re, ct_im)
    s5_v = (ab_re, ab_im, ds5)

    zeros_ht = jnp.zeros((SSD_STATE, SSD_WIDTH), F32)
    zeros_tail = jnp.zeros((SUBLANES, SSD_XBC), F32)
    zeros_s5 = jnp.zeros((1, S5_NSTATE), F32)
    _, _, ht_meta, tail_meta = _ssd(proj_s, dt_s, zeros_ht, zeros_tail, *ssd_args, batch=1, nchunks=1,
                                    row_block0=meta_blk, mask_rows=CHUNK - N_META, need_y=False)
    _, s5r_meta, s5i_meta = _s5(proj_s, zeros_s5, zeros_s5, *s5_w, *s5_v, batch=1, seq=CHUNK,
                                row_block0=meta_blk, need_y=False)[:3]

    yn_m, h_m, _, tail_m = _ssd(proj_m, dt_m, ht_meta[0], tail_meta[0], *ssd_args, batch=bsz,
                                nchunks=seq // CHUNK, row_block0=0, mask_rows=0)
    ybg_m, s5r_m, s5i_m, wpa, wglu, wpb, wout = _s5(
        proj_m, s5r_meta[0], s5i_meta[0], *s5_w, *s5_v, batch=bsz, seq=seq, row_block0=0,
        convert=(w_proj_a[0], w_glu[0], w_proj_b[0], w_out[0]))
    tail_w = (wpa, wglu, bglu, wpb, wout, fnw)
    y_prompt = _tail(yn_m, ybg_m, proj_m, x_main, *tail_w, tm=256)

    yn_s, h_s, cs_s = _ssd_step(proj_s, dt_s, jnp.transpose(state_conv[0], (1, 0, 2)),
                                state_ssd[0].reshape(nseq, SSD_WIDTH, SSD_STATE), *ssd_args, nseq=nseq)
    seq_minor = lambda v: jnp.transpose(v, (1, 2, 0)).reshape(S5_NSTATE, nseq)
    seq_major = lambda v: jnp.transpose(v.reshape(S5_GROUPS, S5_STATE, nseq), (2, 0, 1))[None]
    ybg_s, s5r_s, s5i_s = _s5_step(proj_s, seq_minor(state_s5_re[0]), seq_minor(state_s5_im[0]),
                                   *s5_w, *s5_v, nseq=nseq)
    y_sample = _tail(yn_s, ybg_s, proj_s, x_sample.reshape(nseq, d), *tail_w, tm=nseq)

    dt_out = x_prompt.dtype
    return (
        y_prompt.reshape(bsz, seq, d),
        y_sample.reshape(nseq, 1, d),
        h_m.reshape(1, bsz, SSD_HEADS, SSD_HEAD_DIM, SSD_STATE).astype(dt_out),
        tail_m[:, SUBLANES - (SSD_CONV - 1):, :].reshape(1, bsz, SSD_CONV - 1, SSD_XBC),
        s5r_m.reshape(1, bsz, S5_GROUPS, S5_STATE).astype(dt_out),
        s5i_m.reshape(1, bsz, S5_GROUPS, S5_STATE).astype(dt_out),
        h_s.reshape(1, nseq, SSD_HEADS, SSD_HEAD_DIM, SSD_STATE).astype(dt_out),
        jnp.transpose(cs_s, (1, 0, 2)).reshape(1, nseq, SSD_CONV - 1, SSD_XBC),
        seq_major(s5r_s).astype(dt_out),
        seq_major(s5i_s).astype(dt_out),
    )
```

```python
import functools

import jax
import jax.numpy as jnp
from jax import lax
from jax.experimental import pallas as pl
from jax.experimental.pallas import tpu as pltpu

F32 = jnp.float32
BF16 = jnp.bfloat16

NORM_EPS = 1e-5
LOG2E = 1.4426950408889634
N_META = 16
D_MODEL = 2048
SSD_HEAD_DIM = 64
SSD_HEADS = 32
SSD_GROUPS = 4
SSD_STATE = 128
SSD_WIDTH = SSD_HEADS * SSD_HEAD_DIM
SSD_XBC = SSD_WIDTH + 2 * SSD_GROUPS * SSD_STATE
SSD_CONV = 4
CHUNK = 128
S5_WIDTH = D_MODEL // 2
S5_GROUP = 16
S5_GROUPS = S5_WIDTH // S5_GROUP
S5_STATE = 64
S5_NSTATE = S5_GROUPS * S5_STATE
S5_BLOCKS = 4
S5_BLK_U = S5_WIDTH // S5_BLOCKS
S5_BLK_N = S5_NSTATE // S5_BLOCKS

LANES = 128
SUBLANES = 8
VMEM_LIMIT = 56 * 1024 * 1024
CONV_PITCH = 3

COL_ZA = 0
COL_GA = 2048
COL_GB = 4096
COL_XBC = 6144
COL_UB = 9216
COL_ZB = 10240
PROJ_COLS = 11264
PROJ_TN = 1024


_NT = (((1,), (1,)), ((), ()))


def _sigmoid(x):
    return 0.5 * jnp.tanh(0.5 * x) + 0.5


def _silu(x):
    h = 0.5 * x
    return h + h * jnp.tanh(h)


def _softplus(x):
    return jnp.maximum(x, 0.0) + jnp.log1p(jnp.exp(-jnp.abs(x)))


def _gelu_tanh(x):
    c = 0.7978845608028654
    return 0.5 * x * (1.0 + jnp.tanh(c * (x + 0.044715 * (x * x * x))))


def _split3(x):
    x1 = x.astype(BF16)
    r1 = x - x1.astype(F32)
    x2 = r1.astype(BF16)
    x3 = (r1 - x2.astype(F32)).astype(BF16)
    return x1, x2, x3


def _params(*sem):
    return pltpu.CompilerParams(dimension_semantics=sem, vmem_limit_bytes=VMEM_LIMIT)


def _s5prep_kernel(lr_ref, li_ref, ldt_ref, btr_ref, bti_ref, abr_ref, abi_ref, bbr_ref, bbi_ref):
    lr = lr_ref[...]
    li = li_ref[...]
    step = jnp.exp(ldt_ref[...])
    mag = jnp.exp(lr * step)
    abr = mag * jnp.cos(li * step)
    abi = mag * jnp.sin(li * step)
    den = lr * lr + li * li
    numr = abr - 1.0
    cr = (numr * lr + abi * li) / den
    ci = (abi * lr - numr * li) / den
    abr_ref[...] = abr
    abi_ref[...] = abi
    btr = btr_ref[...]
    bti = bti_ref[...]
    crb = cr[:, None, :]
    cib = ci[:, None, :]
    bbr_ref[...] = crb * btr - cib * bti
    bbi_ref[...] = crb * bti + cib * btr


def _s5prep(lam_re, lam_im, log_dt, bt_re, bt_im):
    g, n = lam_re.shape
    full2 = pl.BlockSpec((g, n), lambda: (0, 0))
    full3 = pl.BlockSpec((g, S5_GROUP, n), lambda: (0, 0, 0))
    return pl.pallas_call(
        _s5prep_kernel,
        in_specs=[full2, full2, pl.BlockSpec((g, 1), lambda: (0, 0)), full3, full3],
        out_specs=[full2, full2, full3, full3],
        out_shape=[jax.ShapeDtypeStruct((g, n), F32)] * 2
        + [jax.ShapeDtypeStruct((g, S5_GROUP, n), F32)] * 2,
        name="s5prep",
    )(lam_re, lam_im, log_dt, bt_re, bt_im)


W_ALIGN = 32
_SRC_ZA = 0
_SRC_XBC = SSD_WIDTH
_SRC_DT = _SRC_XBC + SSD_XBC
_SRC_UB = _SRC_DT + SSD_HEADS
_SRC_ZB = _SRC_UB + S5_WIDTH
_SRC_GA = _SRC_ZB + S5_WIDTH
_SRC_GB = _SRC_GA + D_MODEL
_SEGMENTS = ((_SRC_ZA, SSD_WIDTH), (_SRC_GA, D_MODEL), (_SRC_GB, D_MODEL), (_SRC_XBC, SSD_XBC),
             (_SRC_UB, S5_WIDTH), (_SRC_ZB, S5_WIDTH))
_SRC_TILES = tuple((start + k) // W_ALIGN for start, width in _SEGMENTS for k in range(0, width, PROJ_TN))


def _wprep_dt_kernel(w_ref, o_ref):
    o_ref[...] = jnp.zeros(o_ref.shape, o_ref.dtype)
    o_ref[0:SSD_HEADS, :] = w_ref[...].astype(BF16)


def _wprep_dt(wt):
    d = wt.shape[1]
    assert _SRC_DT % SSD_HEADS == 0
    return pl.pallas_call(
        _wprep_dt_kernel,
        grid=(1,),
        in_specs=[pl.BlockSpec((SSD_HEADS, d), lambda i: (_SRC_DT // SSD_HEADS, 0))],
        out_specs=pl.BlockSpec((LANES, d), lambda i: (0, 0)),
        out_shape=jax.ShapeDtypeStruct((LANES, d), BF16),
        name="wprep_dt",
    )(wt)


def _norm_rows(x_ref, nw_ref, wdt_ref, xn_ref, dt_ref):
    x = x_ref[...]
    ms = jnp.mean(x * x, axis=-1, keepdims=True)
    xn = ((x * lax.rsqrt(ms + NORM_EPS)) * nw_ref[...]).astype(BF16)
    xn_ref[...] = xn
    dt_ref[...] = lax.dot_general(xn, wdt_ref[...], _NT, preferred_element_type=F32)


def _wprep_inproj_kernel(tbl_ref, x_ref, nw_ref, w_ref, wdt_ref, wout_ref, proj_ref, dt_ref, xn_ref):
    del tbl_ref

    @pl.when(pl.program_id(0) == 0)
    def _():
        _norm_rows(x_ref, nw_ref, wdt_ref, xn_ref, dt_ref)

    wb = w_ref[...].astype(BF16)
    wout_ref[...] = wb
    proj_ref[...] = lax.dot_general(xn_ref[...], wb, _NT, preferred_element_type=F32)


def _wprep_inproj(wt, w_dt, x, norm_w):
    m, d = x.shape
    ntiles = len(_SRC_TILES)
    assert ntiles * PROJ_TN == PROJ_COLS
    return pl.pallas_call(
        _wprep_inproj_kernel,
        grid_spec=pltpu.PrefetchScalarGridSpec(
            num_scalar_prefetch=1,
            grid=(ntiles,),
            in_specs=[
                pl.BlockSpec((m, d), lambda j, tbl: (0, 0)),
                pl.BlockSpec((1, d), lambda j, tbl: (0, 0)),
                pl.BlockSpec((pl.Element(PROJ_TN), pl.Element(d)),
                             lambda j, tbl: (pl.multiple_of(tbl[j] * W_ALIGN, W_ALIGN), 0)),
                pl.BlockSpec((LANES, d), lambda j, tbl: (0, 0)),
            ],
            out_specs=[
                pl.BlockSpec((PROJ_TN, d), lambda j, tbl: (j, 0)),
                pl.BlockSpec((m, PROJ_TN), lambda j, tbl: (0, j)),
                pl.BlockSpec((m, LANES), lambda j, tbl: (0, 0)),
            ],
            scratch_shapes=[pltpu.VMEM((m, d), BF16)],
        ),
        out_shape=[jax.ShapeDtypeStruct((PROJ_COLS, d), BF16), jax.ShapeDtypeStruct((m, PROJ_COLS), F32),
                   jax.ShapeDtypeStruct((m, LANES), F32)],
        compiler_params=_params("arbitrary"),
        name="wprep_inproj",
    )(jnp.asarray(_SRC_TILES, jnp.int32), x, norm_w, wt, w_dt)


def _inproj_kernel(x_ref, nw_ref, w_ref, wdt_ref, proj_ref, dt_ref, xn_ref):
    @pl.when(pl.program_id(1) == 0)
    def _():
        _norm_rows(x_ref, nw_ref, wdt_ref, xn_ref, dt_ref)

    proj_ref[...] = lax.dot_general(xn_ref[...], w_ref[...], _NT, preferred_element_type=F32)


def _inproj(x, norm_w, w_main, w_dt, tm):
    m, d = x.shape
    n = w_main.shape[0]
    return pl.pallas_call(
        _inproj_kernel,
        grid=(m // tm, n // PROJ_TN),
        in_specs=[
            pl.BlockSpec((tm, d), lambda i, j: (i, 0)),
            pl.BlockSpec((1, d), lambda i, j: (0, 0)),
            pl.BlockSpec((PROJ_TN, d), lambda i, j: (j, 0)),
            pl.BlockSpec((LANES, d), lambda i, j: (0, 0)),
        ],
        out_specs=[
            pl.BlockSpec((tm, PROJ_TN), lambda i, j: (i, j)),
            pl.BlockSpec((tm, LANES), lambda i, j: (i, 0)),
        ],
        out_shape=[jax.ShapeDtypeStruct((m, n), F32), jax.ShapeDtypeStruct((m, LANES), F32)],
        scratch_shapes=[pltpu.VMEM((tm, d), BF16)],
        compiler_params=_params("arbitrary", "arbitrary"),
        name="inproj",
    )(x, norm_w, w_main, w_dt)


def _ssd_chunk(r0, xbc_ref, za_ref, dtr_ref, convw_ref, convb_ref, dtb_ref, alog_ref, dexp_ref, nw_ref,
               y_ref, ht_scr, ext_scr, mask_rows, need_y):
    L = CHUNK
    P2 = 2 * SSD_HEAD_DIM
    GW = SSD_WIDTH // SSD_GROUPS
    time_rows = lambda t0, n: pl.ds(CONV_PITCH * t0, n, stride=CONV_PITCH)

    w = convw_ref[...]
    bias = convb_ref[...]
    conv_parts = []
    for s in range(SSD_XBC // LANES):
        ls = slice(s * LANES, (s + 1) * LANES)
        xc = xbc_ref[r0:r0 + L, ls]
        ext_scr[s, time_rows(SUBLANES, L), :] = xc
        acc = bias[:, ls]
        for k in range(SSD_CONV - 1):
            acc = acc + ext_scr[s, time_rows(SUBLANES - (SSD_CONV - 1) + k, L), :] * w[k:k + 1, ls]
        conv_parts.append(acc + xc * w[SSD_CONV - 1:SSD_CONV, ls])
        ext_scr[s, time_rows(0, SUBLANES), :] = xc[L - SUBLANES:L, :]
    xbc = _silu(jnp.concatenate(conv_parts, axis=1))
    xs = xbc[:, :SSD_WIDTH]
    bmat = xbc[:, SSD_WIDTH:SSD_WIDTH + SSD_GROUPS * SSD_STATE]
    cmat = xbc[:, SSD_WIDTH + SSD_GROUPS * SSD_STATE:]

    rows = lax.broadcasted_iota(jnp.int32, (L, L), 0)
    cols = lax.broadcasted_iota(jnp.int32, (L, L), 1)
    causal = rows >= cols
    lane_lo = cols < SSD_HEAD_DIM

    dt = _softplus(dtr_ref[r0:r0 + L, :] + dtb_ref[...])
    if mask_rows:
        dt = jnp.where(rows < mask_rows, 0.0, dt)
    a = dt * (-jnp.exp(alog_ref[...]))
    tril = jnp.where(causal, 1.0, 0.0).astype(BF16)
    a1, a2, a3 = _split3(a)
    acum = (jnp.dot(tril, a1, preferred_element_type=F32)
            + jnp.dot(tril, a2, preferred_element_type=F32)
            + jnp.dot(tril, a3, preferred_element_type=F32))
    a2 = acum * LOG2E
    e_cum = jnp.exp2(a2)
    w_end = dt * jnp.exp2(a2[L - 1:L, :] - a2)
    a2dt_t = (a2 - jnp.log2(dt)).T

    dexp = dexp_ref[...]
    xs_b = xs.astype(BF16)
    y_parts = []
    for g in range(SSD_GROUPS):
        bg = bmat[:, g * SSD_STATE:(g + 1) * SSD_STATE].astype(BF16)
        cg = cmat[:, g * SSD_STATE:(g + 1) * SSD_STATE].astype(BF16)
        ht_g = ht_scr[:, g * GW:(g + 1) * GW]
        if need_y:
            cb = lax.dot_general(cg, bg, (((1,), (1,)), ((), ())), preferred_element_type=F32)
            y_off = jnp.dot(cg, ht_g.astype(BF16), preferred_element_type=F32)
        xw_parts = []
        elast_parts = []
        for jj in range(GW // P2):
            lo = g * GW + jj * P2
            h0 = lo // SSD_HEAD_DIM
            yd, eb, wb = [], [], []
            for h in (h0, h0 + 1):
                if need_y:
                    colb = jnp.broadcast_to(a2[:, h:h + 1], (L, L))
                    m = cb * jnp.exp2(jnp.where(causal, colb - a2dt_t[h:h + 1, :], -jnp.inf))
                    yd.append(jnp.dot(m.astype(BF16), xs_b[:, lo:lo + P2], preferred_element_type=F32))
                eb.append(jnp.broadcast_to(e_cum[:, h:h + 1], (L, L)))
                wb.append(jnp.broadcast_to(w_end[:, h:h + 1], (L, L)))
            xs_pair = xs[:, lo:lo + P2]
            if need_y:
                y_pair = (jnp.where(lane_lo, yd[0], yd[1])
                          + y_off[:, jj * P2:(jj + 1) * P2] * jnp.where(lane_lo, eb[0], eb[1]))
                y_parts.append(y_pair + dexp[:, lo:lo + P2] * xs_pair)
            xw_parts.append(xs_pair * jnp.where(lane_lo, wb[0], wb[1]))
            elast_parts.append(jnp.where(lane_lo[0:1, :], eb[0][L - 1:L, :], eb[1][L - 1:L, :]))
        xw = jnp.concatenate(xw_parts, axis=1)
        elast = jnp.concatenate(elast_parts, axis=1)
        st = lax.dot_general(bg, xw.astype(BF16), (((0,), (0,)), ((), ())),
                             preferred_element_type=F32)
        ht_scr[:, g * GW:(g + 1) * GW] = ht_g * elast + st

    if not need_y:
        y_ref[r0:r0 + L, :] = jnp.zeros((L, SSD_WIDTH), y_ref.dtype)
        return
    y = jnp.concatenate(y_parts, axis=1)
    y = y * _silu(za_ref[r0:r0 + L, :])
    nw = nw_ref[...]
    outs = []
    for g in range(SSD_GROUPS):
        yg = y[:, g * GW:(g + 1) * GW]
        ms = jnp.mean(yg * yg, axis=-1, keepdims=True)
        outs.append((yg * lax.rsqrt(ms + NORM_EPS)) * nw[:, g * GW:(g + 1) * GW])
    y_ref[r0:r0 + L, :] = jnp.concatenate(outs, axis=1).astype(y_ref.dtype)


def _ssd_kernel(xbc_ref, za_ref, dtr_ref, ht0_ref, tail0_ref, convw_ref, convb_ref, dtb_ref, alog_ref,
                dexp_ref, nw_ref, y_ref, h_ref, ht_ref, tail_ref, ht_scr, ext_scr, *, mask_rows, need_y):
    c = pl.program_id(1)
    nrows = xbc_ref.shape[0]

    @pl.when(c == 0)
    def _():
        ht_scr[...] = ht0_ref[...]
        for s in range(SSD_XBC // LANES):
            ext_scr[s, pl.ds(0, SUBLANES, stride=CONV_PITCH), :] = tail0_ref[:, s * LANES:(s + 1) * LANES]

    for r0 in range(0, nrows, CHUNK):
        _ssd_chunk(r0, xbc_ref, za_ref, dtr_ref, convw_ref, convb_ref, dtb_ref, alog_ref, dexp_ref, nw_ref,
                   y_ref, ht_scr, ext_scr, mask_rows, need_y)

    @pl.when(c == pl.num_programs(1) - 1)
    def _():
        ht = ht_scr[...]
        ht_ref[...] = ht
        h_ref[...] = ht.T
        tail_ref[...] = xbc_ref[nrows - SUBLANES:nrows, :]


def _ssd(proj, dt_raw, ht0, tail0, conv_w, conv_b, dt_bias, a_log, d_exp, norm_w, *, batch, nchunks,
         row_block0, mask_rows, need_y=True):
    L = CHUNK
    rows = batch * nchunks * L
    rb = lambda b, c: row_block0 + b * nchunks + c
    const2 = lambda shape: pl.BlockSpec(shape, lambda b, c: (0, 0))
    return pl.pallas_call(
        functools.partial(_ssd_kernel, mask_rows=mask_rows, need_y=need_y),
        grid=(batch, nchunks),
        in_specs=[
            pl.BlockSpec((L, SSD_XBC), lambda b, c: (rb(b, c), COL_XBC // SSD_XBC)),
            pl.BlockSpec((L, SSD_WIDTH), lambda b, c: (rb(b, c), COL_ZA // SSD_WIDTH)),
            pl.BlockSpec((L, LANES), lambda b, c: (rb(b, c), 0)),
            const2((SSD_STATE, SSD_WIDTH)),
            const2((SUBLANES, SSD_XBC)),
            const2((SSD_CONV, SSD_XBC)),
            const2((1, SSD_XBC)),
            const2((1, LANES)),
            const2((1, LANES)),
            const2((1, SSD_WIDTH)),
            const2((1, SSD_WIDTH)),
        ],
        out_specs=[
            pl.BlockSpec((L, SSD_WIDTH), lambda b, c: (b * nchunks + c, 0)),
            pl.BlockSpec((None, SSD_WIDTH, SSD_STATE), lambda b, c: (b, 0, 0)),
            pl.BlockSpec((None, SSD_STATE, SSD_WIDTH), lambda b, c: (b, 0, 0)),
            pl.BlockSpec((None, SUBLANES, SSD_XBC), lambda b, c: (b, 0, 0)),
        ],
        out_shape=[
            jax.ShapeDtypeStruct((rows, SSD_WIDTH), BF16),
            jax.ShapeDtypeStruct((batch, SSD_WIDTH, SSD_STATE), F32),
            jax.ShapeDtypeStruct((batch, SSD_STATE, SSD_WIDTH), F32),
            jax.ShapeDtypeStruct((batch, SUBLANES, SSD_XBC), F32),
        ],
        scratch_shapes=[pltpu.VMEM((SSD_STATE, SSD_WIDTH), F32),
                        pltpu.VMEM((SSD_XBC // LANES, CONV_PITCH * (SUBLANES + CHUNK), LANES), F32)],
        compiler_params=_params("arbitrary", "arbitrary"),
        name="ssd",
    )(proj, proj, dt_raw, ht0, tail0, conv_w, conv_b, dt_bias, a_log, d_exp, norm_w)


def _ssd_step_kernel(xbc_ref, za_ref, dtr_ref, cs_ref, h_ref, convw_ref, convb_ref, dtb_ref, alog_ref,
                     dexp_ref, nw_ref, y_ref, hout_ref, csout_ref):
    R = SUBLANES
    GW = SSD_WIDTH // SSD_GROUPS
    x = xbc_ref[...]
    w = convw_ref[...]
    s0 = cs_ref[0]
    s1 = cs_ref[1]
    s2 = cs_ref[2]
    conv = convb_ref[...] + s0 * w[0:1]
    conv = conv + s1 * w[1:2]
    conv = conv + s2 * w[2:3]
    conv = conv + x * w[3:4]
    csout_ref[0] = s1
    csout_ref[1] = s2
    csout_ref[2] = x
    xbc = _silu(conv)
    xs = xbc[:, :SSD_WIDTH]
    bmat = xbc[:, SSD_WIDTH:SSD_WIDTH + SSD_GROUPS * SSD_STATE]
    cmat = xbc[:, SSD_WIDTH + SSD_GROUPS * SSD_STATE:]

    dt = _softplus(dtr_ref[...] + dtb_ref[...])
    da = jnp.exp(dt * (-jnp.exp(alog_ref[...])))
    dt_t = dt.T
    da_t = da.T
    expand = lambda v: jnp.concatenate(
        [jnp.broadcast_to(v[h:h + 1, :], (SSD_HEAD_DIM, R)) for h in range(SSD_HEADS)], axis=0)
    xd_t = xs.T * expand(dt_t)
    da_te = expand(da_t)

    cmat_b = cmat.astype(BF16)
    yrows = []
    for i in range(R):
        bexp = jnp.concatenate(
            [jnp.broadcast_to(bmat[i:i + 1, g * SSD_STATE:(g + 1) * SSD_STATE], (GW, SSD_STATE))
             for g in range(SSD_GROUPS)], axis=0)
        hn = h_ref[i] * da_te[:, i:i + 1] + xd_t[:, i:i + 1] * bexp
        hout_ref[i] = hn
        cg = jnp.concatenate([cmat_b[i:i + 1, g * SSD_STATE:(g + 1) * SSD_STATE] for g in range(SSD_GROUPS)]
                             + [jnp.zeros((R - SSD_GROUPS, SSD_STATE), BF16)], axis=0)
        yg = lax.dot_general(cg, hn.astype(BF16), (((1,), (1,)), ((), ())), preferred_element_type=F32)
        yrows.append(jnp.concatenate([yg[g:g + 1, g * GW:(g + 1) * GW] for g in range(SSD_GROUPS)], axis=1))
    y = jnp.concatenate(yrows, axis=0)
    y = y + dexp_ref[...] * xs
    y = y * _silu(za_ref[...])
    nw = nw_ref[...]
    outs = []
    for g in range(SSD_GROUPS):
        yg = y[:, g * GW:(g + 1) * GW]
        ms = jnp.mean(yg * yg, axis=-1, keepdims=True)
        outs.append((yg * lax.rsqrt(ms + NORM_EPS)) * nw[:, g * GW:(g + 1) * GW])
    y_ref[...] = jnp.concatenate(outs, axis=1).astype(y_ref.dtype)


def _ssd_step(proj, dt_raw, conv_state, ssd_state, conv_w, conv_b, dt_bias, a_log, d_exp, norm_w, *, nseq):
    R = SUBLANES
    const2 = lambda shape: pl.BlockSpec(shape, lambda i: (0, 0))
    return pl.pallas_call(
        _ssd_step_kernel,
        grid=(nseq // R,),
        in_specs=[
            pl.BlockSpec((R, SSD_XBC), lambda i: (i, COL_XBC // SSD_XBC)),
            pl.BlockSpec((R, SSD_WIDTH), lambda i: (i, COL_ZA // SSD_WIDTH)),
            pl.BlockSpec((R, LANES), lambda i: (i, 0)),
            pl.BlockSpec((SSD_CONV - 1, R, SSD_XBC), lambda i: (0, i, 0)),
            pl.BlockSpec((R, SSD_WIDTH, SSD_STATE), lambda i: (i, 0, 0)),
            const2((SSD_CONV, SSD_XBC)),
            const2((1, SSD_XBC)),
            const2((1, LANES)),
            const2((1, LANES)),
            const2((1, SSD_WIDTH)),
            const2((1, SSD_WIDTH)),
        ],
        out_specs=[
            pl.BlockSpec((R, SSD_WIDTH), lambda i: (i, 0)),
            pl.BlockSpec((R, SSD_WIDTH, SSD_STATE), lambda i: (i, 0, 0)),
            pl.BlockSpec((SSD_CONV - 1, R, SSD_XBC), lambda i: (0, i, 0)),
        ],
        out_shape=[
            jax.ShapeDtypeStruct((nseq, SSD_WIDTH), BF16),
            jax.ShapeDtypeStruct((nseq, SSD_WIDTH, SSD_STATE), F32),
            jax.ShapeDtypeStruct((SSD_CONV - 1, nseq, SSD_XBC), F32),
        ],
        compiler_params=_params("arbitrary"),
        name="ssd_step",
    )(proj, proj, dt_raw, conv_state, ssd_state, conv_w, conv_b, dt_bias, a_log, d_exp, norm_w)


def _s5_block_diag(w, rows_per_group, cols_per_group):
    ngrp, r, lanes = w.shape
    width = ngrp * cols_per_group
    tiled = jnp.concatenate([w.reshape(ngrp * r, lanes)] * (width // lanes), axis=1)
    row_g = lax.broadcasted_iota(jnp.int32, tiled.shape, 0) // rows_per_group
    col_g = lax.broadcasted_iota(jnp.int32, tiled.shape, 1) // cols_per_group
    return jnp.where(row_g == col_g, tiled, 0.0)


def _s5_weights(bbr_ref, bbi_ref, cr_ref, ci_ref):
    return tuple(_s5_block_diag(r[...], S5_GROUP, S5_STATE) for r in (bbr_ref, bbi_ref, cr_ref, ci_ref))


def _s5_pitch(seg_len):
    return seg_len if (seg_len // SUBLANES) % 2 else seg_len + SUBLANES


def _s5_kernel(*refs, seq, nconv, need_y):
    (u_ref, h0r_ref, h0i_ref, bbr_ref, bbi_ref, ctr_ref, cti_ref, ar_ref, ai_ref, d_ref) = refs[:10]
    w_in_refs = refs[10:10 + nconv]
    y_ref, hr_ref, hi_ref = refs[10 + nconv:13 + nconv]
    w_out_refs = refs[13 + nconv:13 + 2 * nconv]
    pad, pe, po, cr, ci, hb_in, hb_out = refs[13 + 2 * nconv:]
    for wi, wo in zip(w_in_refs, w_out_refs):
        wo[...] = wi[...].astype(BF16)

    NS = SUBLANES
    SL = seq // NS
    H = SL // 2
    pitch = _s5_pitch(SL)
    nk = S5_BLK_U // LANES
    n = S5_BLK_N
    bt_re, bt_im, ct_re, ct_im = _s5_weights(bbr_ref, bbi_ref, ctr_ref, cti_ref)
    a_re = ar_ref[...]
    a_im = ai_ref[...]
    bf = lambda v: v.astype(BF16)
    w2_re = jnp.concatenate([bf(bt_re * a_re - bt_im * a_im), bf(bt_re)], axis=0)
    w2_im = jnp.concatenate([bf(bt_re * a_im + bt_im * a_re), bf(bt_im)], axis=0)
    if need_y:
        cta_re = bf(ct_re * a_re - ct_im * a_im)
        cta_im = bf(ct_re * a_im + ct_im * a_re)
        ctb_re = bf(ct_re)
        ctb_im = bf(ct_im)
        k0 = bf(lax.dot_general(bf(bt_re), ctb_re, _NT, preferred_element_type=F32)
                - lax.dot_general(bf(bt_im), ctb_im, _NT, preferred_element_type=F32))

    for s in range(NS):
        for k in range(nk):
            pad[k, s * pitch:s * pitch + SL, :] = u_ref[s * SL:(s + 1) * SL, k * LANES:(k + 1) * LANES]

    def gather(jp, c):
        r0 = pl.multiple_of(jp * NS, NS)
        for k in range(nk):
            pe[pl.ds(r0, NS), k * LANES:(k + 1) * LANES] = pad[k, pl.ds(2 * jp, NS, stride=pitch), :]
            po[pl.ds(r0, NS), k * LANES:(k + 1) * LANES] = pad[k, pl.ds(2 * jp + 1, NS, stride=pitch), :]
        return c

    lax.fori_loop(0, H, gather, 0, unroll=4)
    ue_b = bf(pe[...])
    uo_b = bf(po[...])
    qc = 4 * S5_GROUP
    qn = 4 * S5_STATE
    for q in range(S5_BLK_U // qc):
        u_q = jnp.concatenate([ue_b[:, q * qc:(q + 1) * qc], uo_b[:, q * qc:(q + 1) * qc]], axis=1)
        rows_q = [slice(q * qc, (q + 1) * qc), slice(S5_BLK_U + q * qc, S5_BLK_U + (q + 1) * qc)]
        cols_q = slice(q * qn, (q + 1) * qn)
        wq_re = jnp.concatenate([w2_re[r, cols_q] for r in rows_q], axis=0)
        wq_im = jnp.concatenate([w2_im[r, cols_q] for r in rows_q], axis=0)
        cr[:, cols_q] = jnp.dot(u_q, wq_re, preferred_element_type=F32)
        ci[:, cols_q] = jnp.dot(u_q, wq_im, preferred_element_type=F32)

    a2_re = a_re * a_re - a_im * a_im
    a2_im = 2.0 * (a_re * a_im)
    ar = jnp.broadcast_to(a2_re, (NS, n))
    ai = jnp.broadcast_to(a2_im, (NS, n))

    def step(carry, r0):
        hr, hi = carry
        nr = ar * hr - ai * hi + cr[pl.ds(r0, NS), :]
        ni = ar * hi + ai * hr + ci[pl.ds(r0, NS), :]
        return nr, ni

    def scan_ends(j, carry):
        return step(carry, pl.multiple_of(j * NS, NS))

    zero = jnp.zeros((NS, n), F32)
    er, ei = lax.fori_loop(0, H, scan_ends, (zero, zero), unroll=4)

    asr, asi = a_re, a_im
    for _ in range(SL.bit_length() - 1):
        asr, asi = asr * asr - asi * asi, 2.0 * (asr * asi)
    gr = [h0r_ref[...]]
    gi = [h0i_ref[...]]
    for s in range(NS):
        gr.append(er[s:s + 1, :] + (asr * gr[s] - asi * gi[s]))
        gi.append(ei[s:s + 1, :] + (asr * gi[s] + asi * gr[s]))
    hr_ref[...] = gr[NS]
    hi_ref[...] = gi[NS]
    if not need_y:
        y_ref[...] = jnp.zeros(y_ref.shape, y_ref.dtype)
        return

    def scan_store(jp, c0):
        r0 = pl.multiple_of(jp * 2 * NS, 2 * NS)
        c1 = step(c0, r0)
        c2 = step(c1, r0 + NS)
        hb_in[pl.ds(r0, 2 * NS), 0:n] = bf(jnp.concatenate([c0[0], c1[0]], axis=0))
        hb_in[pl.ds(r0, 2 * NS), n:2 * n] = bf(jnp.concatenate([c0[1], c1[1]], axis=0))
        hb_out[pl.ds(r0, 2 * NS), 0:n] = bf(jnp.concatenate([c1[0], c2[0]], axis=0))
        hb_out[pl.ds(r0, 2 * NS), n:2 * n] = bf(jnp.concatenate([c1[1], c2[1]], axis=0))
        return c2

    lax.fori_loop(0, H // 2, scan_store,
                  (jnp.concatenate(gr[:NS], axis=0), jnp.concatenate(gi[:NS], axis=0)), unroll=2)

    half = seq // 2
    rb = half // 4
    d = d_ref[...]
    for r in range(0, half, rb):
        rows = slice(r, r + rb)
        y_odd = (lax.dot_general(hb_out[rows, 0:n], ctb_re, _NT, preferred_element_type=F32)
                 - lax.dot_general(hb_out[rows, n:2 * n], ctb_im, _NT, preferred_element_type=F32))
        y_even = (lax.dot_general(hb_in[rows, 0:n], cta_re, _NT, preferred_element_type=F32)
                  - lax.dot_general(hb_in[rows, n:2 * n], cta_im, _NT, preferred_element_type=F32)
                  + jnp.dot(bf(pe[rows, :]), k0, preferred_element_type=F32))
        po[rows, :] = _gelu_tanh(y_odd + d * po[rows, :])
        pe[rows, :] = _gelu_tanh(y_even + d * pe[rows, :])

    def scatter(jp, c):
        r0 = pl.multiple_of(jp * NS, NS)
        for k in range(nk):
            pad[k, pl.ds(2 * jp, NS, stride=pitch), :] = pe[pl.ds(r0, NS), k * LANES:(k + 1) * LANES]
            pad[k, pl.ds(2 * jp + 1, NS, stride=pitch), :] = po[pl.ds(r0, NS), k * LANES:(k + 1) * LANES]
        return c

    lax.fori_loop(0, H, scatter, 0, unroll=4)
    for s in range(NS):
        for k in range(nk):
            y_ref[s * SL:(s + 1) * SL, k * LANES:(k + 1) * LANES] = pad[k, s * pitch:s * pitch + SL, :]


def _s5(proj, h0r, h0i, bb_re, bb_im, ct_re, ct_im, ab_re, ab_im, d_s5, *, batch, seq, row_block0, need_y=True,
        convert=()):
    ub0 = COL_UB // S5_BLK_U
    nsteps = batch * S5_BLOCKS
    slice_spec = lambda w: pl.BlockSpec((w.shape[0] // nsteps, w.shape[1]), lambda b, j: (b * S5_BLOCKS + j, 0))
    assert all(w.shape[0] % (2 * SUBLANES * nsteps) == 0 for w in convert)
    seg = seq // SUBLANES
    assert seg & (seg - 1) == 0
    gpb = S5_GROUPS // S5_BLOCKS
    vec = lambda width: pl.BlockSpec((1, width), lambda b, j: (0, j))
    wspec = lambda r: pl.BlockSpec((gpb, r, LANES), lambda b, j: (j, 0, 0))
    st_out = pl.BlockSpec((None, 1, S5_BLK_N), lambda b, j: (b, 0, j))
    return pl.pallas_call(
        functools.partial(_s5_kernel, seq=seq, nconv=len(convert), need_y=need_y),
        grid=(batch, S5_BLOCKS),
        in_specs=[
            pl.BlockSpec((seq, S5_BLK_U), lambda b, j: (row_block0 + b, ub0 + j)),
            vec(S5_BLK_N), vec(S5_BLK_N),
            wspec(S5_GROUP), wspec(S5_GROUP), wspec(S5_GROUP), wspec(S5_GROUP),
            vec(S5_BLK_N), vec(S5_BLK_N), vec(S5_BLK_U),
        ] + [slice_spec(w) for w in convert],
        out_specs=[pl.BlockSpec((seq, S5_BLK_U), lambda b, j: (b, j)), st_out, st_out]
        + [slice_spec(w) for w in convert],
        out_shape=[
            jax.ShapeDtypeStruct((batch * seq, S5_WIDTH), F32),
            jax.ShapeDtypeStruct((batch, 1, S5_NSTATE), F32),
            jax.ShapeDtypeStruct((batch, 1, S5_NSTATE), F32),
        ] + [jax.ShapeDtypeStruct(w.shape, BF16) for w in convert],
        scratch_shapes=[
            pltpu.VMEM((S5_BLK_U // LANES, SUBLANES * _s5_pitch(seg), LANES), F32),
            pltpu.VMEM((seq // 2, S5_BLK_U), F32),
            pltpu.VMEM((seq // 2, S5_BLK_U), F32),
            pltpu.VMEM((seq // 2, S5_BLK_N), F32),
            pltpu.VMEM((seq // 2, S5_BLK_N), F32),
            pltpu.VMEM((seq // 2, 2 * S5_BLK_N), BF16),
            pltpu.VMEM((seq // 2, 2 * S5_BLK_N), BF16),
        ],
        compiler_params=_params("arbitrary", "arbitrary"),
        name="s5",
    )(proj, h0r, h0i, bb_re, bb_im, ct_re, ct_im, ab_re, ab_im, d_s5, *convert)


def _s5_step_kernel(u_ref, h0r_ref, h0i_ref, bbr_ref, bbi_ref, ctr_ref, cti_ref, ar_ref, ai_ref, d_ref,
                    y_ref, hr_ref, hi_ref):
    bt_re, bt_im, ct_re, ct_im = (w.astype(BF16) for w in _s5_weights(bbr_ref, bbi_ref, ctr_ref, cti_ref))
    u = u_ref[...]
    ub = u.astype(BF16)
    ar = ar_ref[...]
    ai = ai_ref[...]
    h0r = h0r_ref[...].T
    h0i = h0i_ref[...].T
    hr = jnp.dot(ub, bt_re, preferred_element_type=F32) + (ar * h0r - ai * h0i)
    hi = jnp.dot(ub, bt_im, preferred_element_type=F32) + (ar * h0i + ai * h0r)
    hr_ref[...] = hr.T
    hi_ref[...] = hi.T
    y = (lax.dot_general(hr.astype(BF16), ct_re, _NT, preferred_element_type=F32)
         - lax.dot_general(hi.astype(BF16), ct_im, _NT, preferred_element_type=F32))
    y_ref[...] = _gelu_tanh(y + d_ref[...] * u)


def _s5_step(proj, h0r, h0i, bb_re, bb_im, ct_re, ct_im, ab_re, ab_im, d_s5, *, nseq):
    ub0 = COL_UB // S5_BLK_U
    gpb = S5_GROUPS // S5_BLOCKS
    vec = lambda width: pl.BlockSpec((1, width), lambda j: (0, j))
    wspec = lambda r: pl.BlockSpec((gpb, r, LANES), lambda j: (j, 0, 0))
    st = pl.BlockSpec((S5_BLK_N, nseq), lambda j: (j, 0))
    return pl.pallas_call(
        _s5_step_kernel,
        grid=(S5_BLOCKS,),
        in_specs=[
            pl.BlockSpec((nseq, S5_BLK_U), lambda j: (0, ub0 + j)),
            st, st,
            wspec(S5_GROUP), wspec(S5_GROUP), wspec(S5_GROUP), wspec(S5_GROUP),
            vec(S5_BLK_N), vec(S5_BLK_N), vec(S5_BLK_U),
        ],
        out_specs=[pl.BlockSpec((nseq, S5_BLK_U), lambda j: (0, j)), st, st],
        out_shape=[
            jax.ShapeDtypeStruct((nseq, S5_WIDTH), F32),
            jax.ShapeDtypeStruct((S5_NSTATE, nseq), F32),
            jax.ShapeDtypeStruct((S5_NSTATE, nseq), F32),
        ],
        compiler_params=_params("arbitrary"),
        name="s5_step",
    )(proj, h0r, h0i, bb_re, bb_im, ct_re, ct_im, ab_re, ab_im, d_s5)


def _tail_kernel(yn_ref, ybg_ref, zb_ref, ga_ref, gb_ref, x_ref, wpa_ref, wglu_ref, bglu_ref, wpb_ref,
                 wout_ref, fnw_ref, out_ref):
    sg_a = _sigmoid(ga_ref[...])
    sg_b = _sigmoid(gb_ref[...])
    sz_b = _silu(zb_ref[...])
    yb = ybg_ref[...]
    glu = jnp.dot(yb.astype(BF16), wglu_ref[...], preferred_element_type=F32) + bglu_ref[...]
    ya = jnp.dot(yn_ref[...], wpa_ref[...], preferred_element_type=F32)
    yb = (yb * _sigmoid(glu)) * sz_b
    ybp = jnp.dot(yb.astype(BF16), wpb_ref[...], preferred_element_type=F32)
    mixed = sg_a * ya + sg_b * ybp
    o = x_ref[...] + jnp.dot(mixed.astype(BF16), wout_ref[...], preferred_element_type=F32)
    ms = jnp.mean(o * o, axis=-1, keepdims=True)
    out_ref[...] = (o * lax.rsqrt(ms + NORM_EPS)) * fnw_ref[...]


def _tail(yn, ybg, proj, x, w_proj_a, w_glu, b_glu, w_proj_b, w_out, final_norm_w, *, tm):
    m, d = x.shape
    resident = lambda shape: pl.BlockSpec(shape, lambda i: (0, 0), pipeline_mode=pl.Buffered(1))
    return pl.pallas_call(
        _tail_kernel,
        grid=(m // tm,),
        in_specs=[
            pl.BlockSpec((tm, SSD_WIDTH), lambda i: (i, 0)),
            pl.BlockSpec((tm, S5_WIDTH), lambda i: (i, 0)),
            pl.BlockSpec((tm, S5_WIDTH), lambda i: (i, COL_ZB // S5_WIDTH)),
            pl.BlockSpec((tm, d), lambda i: (i, COL_GA // D_MODEL)),
            pl.BlockSpec((tm, d), lambda i: (i, COL_GB // D_MODEL)),
            pl.BlockSpec((tm, d), lambda i: (i, 0)),
            resident((SSD_WIDTH, d)),
            resident((S5_WIDTH, S5_WIDTH)),
            resident((1, S5_WIDTH)),
            resident((S5_WIDTH, d)),
            resident((d, d)),
            resident((1, d)),
        ],
        out_specs=pl.BlockSpec((tm, d), lambda i: (i, 0)),
        out_shape=jax.ShapeDtypeStruct((m, d), F32),
        compiler_params=_params("arbitrary"),
        name="tail",
    )(yn, ybg, proj, proj, proj, x, w_proj_a, w_glu, b_glu, w_proj_b, w_out, final_norm_w)


def kernel(x_prompt, x_sample, state_ssd, state_conv, state_s5_re, state_s5_im, meta_tokens, norm_w, w_in,
           conv_w, conv_b, dt_bias, a_log, d_ssd, ssd_norm_w, w_proj_a, lam_re, lam_im, log_dt_s5, b_re, b_im,
           c_re, c_im, d_s5, w_glu, b_glu, w_proj_b, w_out, final_norm_w):
    bsz, seq, d = x_prompt.shape
    nseq = x_sample.shape[0]
    assert d == D_MODEL and seq % CHUNK == 0 and nseq % SUBLANES == 0 and norm_w.shape[0] == 1
    assert meta_tokens.shape[0] == N_META and N_META <= CHUNK

    assert w_in.shape[2] == _SRC_GB + D_MODEL
    wt = jnp.transpose(w_in[0])
    w_dt = _wprep_dt(wt)
    pad_heads = lambda v: jnp.pad(v.reshape(1, SSD_HEADS), ((0, 0), (0, LANES - SSD_HEADS)))
    dtb = pad_heads(dt_bias[0])
    alog = pad_heads(a_log[0])
    d_exp = jnp.repeat(d_ssd[0], SSD_HEAD_DIM).reshape(1, SSD_WIDTH)
    nw1 = norm_w[0].reshape(1, d)
    ssd_nw = ssd_norm_w[0].reshape(1, SSD_WIDTH)
    convw = conv_w[0]
    convb = conv_b[0].reshape(1, SSD_XBC)
    bglu = b_glu[0].reshape(1, S5_WIDTH)
    fnw = final_norm_w.reshape(1, d)
    ds5 = d_s5[0].reshape(1, S5_WIDTH)

    rep = lambda v, k: jnp.concatenate([v] * k, axis=-1)
    lane_rep = LANES // S5_STATE
    ab_re, ab_im, bb_re, bb_im = _s5prep(
        rep(lam_re[0], lane_rep), rep(lam_im[0], lane_rep), log_dt_s5[0].reshape(S5_GROUPS, 1),
        rep(jnp.transpose(b_re[0], (0, 2, 1)), lane_rep), rep(jnp.transpose(b_im[0], (0, 2, 1)), lane_rep))
    ab_re = ab_re[:, :S5_STATE].reshape(1, S5_NSTATE)
    ab_im = ab_im[:, :S5_STATE].reshape(1, S5_NSTATE)
    ct_re = rep(c_re[0], lane_rep)
    ct_im = rep(c_im[0], lane_rep)

    x_main = x_prompt.reshape(bsz * seq, d)
    x_small = jnp.concatenate(
        [x_sample.reshape(nseq, d), jnp.zeros((CHUNK - N_META, d), x_prompt.dtype),
         meta_tokens.astype(x_prompt.dtype)], axis=0)
    assert nseq % CHUNK == 0
    meta_blk = nseq // CHUNK
    tm_main = 1024 if (bsz * seq) % 1024 == 0 else CHUNK
    w_main, proj_s, dt_s = _wprep_inproj(wt, w_dt, x_small, nw1)
    proj_m, dt_m = _inproj(x_main, nw1, w_main, w_dt, tm=tm_main)

    ssd_args = (convw, convb, dtb, alog, d_exp, ssd_nw)
    s5_w = (bb_re, bb_im, ct_re, ct_im)
    s5_v = (ab_re, ab_im, ds5)

    zeros_ht = jnp.zeros((SSD_STATE, SSD_WIDTH), F32)
    zeros_tail = jnp.zeros((SUBLANES, SSD_XBC), F32)
    zeros_s5 = jnp.zeros((1, S5_NSTATE), F32)
    _, _, ht_meta, tail_meta = _ssd(proj_s, dt_s, zeros_ht, zeros_tail, *ssd_args, batch=1, nchunks=1,
                                    row_block0=meta_blk, mask_rows=CHUNK - N_META, need_y=False)
    _, s5r_meta, s5i_meta = _s5(proj_s, zeros_s5, zeros_s5, *s5_w, *s5_v, batch=1, seq=CHUNK,
                                row_block0=meta_blk, need_y=False)[:3]

    yn_m, h_m, _, tail_m = _ssd(proj_m, dt_m, ht_meta[0], tail_meta[0], *ssd_args, batch=bsz,
                                nchunks=seq // CHUNK, row_block0=0, mask_rows=0)
    ybg_m, s5r_m, s5i_m, wpa, wglu, wpb, wout = _s5(
        proj_m, s5r_meta[0], s5i_meta[0], *s5_w, *s5_v, batch=bsz, seq=seq, row_block0=0,
        convert=(w_proj_a[0], w_glu[0], w_proj_b[0], w_out[0]))
    tail_w = (wpa, wglu, bglu, wpb, wout, fnw)
    y_prompt = _tail(yn_m, ybg_m, proj_m, x_main, *tail_w, tm=256)

    yn_s, h_s, cs_s = _ssd_step(proj_s, dt_s, jnp.transpose(state_conv[0], (1, 0, 2)),
                                state_ssd[0].reshape(nseq, SSD_WIDTH, SSD_STATE), *ssd_args, nseq=nseq)
    seq_minor = lambda v: jnp.transpose(v, (1, 2, 0)).reshape(S5_NSTATE, nseq)
    seq_major = lambda v: jnp.transpose(v.reshape(S5_GROUPS, S5_STATE, nseq), (2, 0, 1))[None]
    ybg_s, s5r_s, s5i_s = _s5_step(proj_s, seq_minor(state_s5_re[0]), seq_minor(state_s5_im[0]),
                                   *s5_w, *s5_v, nseq=nseq)
    y_sample = _tail(yn_s, ybg_s, proj_s, x_sample.reshape(nseq, d), *tail_w, tm=nseq)

    dt_out = x_prompt.dtype
    return (
        y_prompt.reshape(bsz, seq, d),
        y_sample.reshape(nseq, 1, d),
        h_m.reshape(1, bsz, SSD_HEADS, SSD_HEAD_DIM, SSD_STATE).astype(dt_out),
        tail_m[:, SUBLANES - (SSD_CONV - 1):, :].reshape(1, bsz, SSD_CONV - 1, SSD_XBC),
        s5r_m.reshape(1, bsz, S5_GROUPS, S5_STATE).astype(dt_out),
        s5i_m.reshape(1, bsz, S5_GROUPS, S5_STATE).astype(dt_out),
        h_s.reshape(1, nseq, SSD_HEADS, SSD_HEAD_DIM, SSD_STATE).astype(dt_out),
        jnp.transpose(cs_s, (1, 0, 2)).reshape(1, nseq, SSD_CONV - 1, SSD_XBC),
        seq_major(s5r_s).astype(dt_out),
        seq_major(s5i_s).astype(dt_out),
    )
```

```python
import functools

import jax
import jax.numpy as jnp
from jax import lax
from jax.experimental import pallas as pl
from jax.experimental.pallas import tpu as pltpu

F32 = jnp.float32
BF16 = jnp.bfloat16

NORM_EPS = 1e-5
LOG2E = 1.4426950408889634
N_META = 16
D_MODEL = 2048
SSD_HEAD_DIM = 64
SSD_HEADS = 32
SSD_GROUPS = 4
SSD_STATE = 128
SSD_WIDTH = SSD_HEADS * SSD_HEAD_DIM
SSD_XBC = SSD_WIDTH + 2 * SSD_GROUPS * SSD_STATE
SSD_CONV = 4
CHUNK = 128
S5_WIDTH = D_MODEL // 2
S5_GROUP = 16
S5_GROUPS = S5_WIDTH // S5_GROUP
S5_STATE = 64
S5_NSTATE = S5_GROUPS * S5_STATE
S5_BLOCKS = 4
S5_BLK_U = S5_WIDTH // S5_BLOCKS
S5_BLK_N = S5_NSTATE // S5_BLOCKS

LANES = 128
SUBLANES = 8
VMEM_LIMIT = 56 * 1024 * 1024
CONV_PITCH = 3

COL_ZA = 0
COL_GA = 2048
COL_GB = 4096
COL_XBC = 6144
COL_UB = 9216
COL_ZB = 10240
PROJ_COLS = 11264
PROJ_TN = 1024


_NT = (((1,), (1,)), ((), ()))


def _sigmoid(x):
    return 0.5 * jnp.tanh(0.5 * x) + 0.5


def _silu(x):
    h = 0.5 * x
    return h + h * jnp.tanh(h)


def _softplus(x):
    return jnp.maximum(x, 0.0) + jnp.log1p(jnp.exp(-jnp.abs(x)))


def _gelu_tanh(x):
    c = 0.7978845608028654
    return 0.5 * x * (1.0 + jnp.tanh(c * (x + 0.044715 * (x * x * x))))


def _split3(x):
    x1 = x.astype(BF16)
    r1 = x - x1.astype(F32)
    x2 = r1.astype(BF16)
    x3 = (r1 - x2.astype(F32)).astype(BF16)
    return x1, x2, x3


def _params(*sem):
    return pltpu.CompilerParams(dimension_semantics=sem, vmem_limit_bytes=VMEM_LIMIT)


def _s5prep_kernel(lr_ref, li_ref, ldt_ref, btr_ref, bti_ref, abr_ref, abi_ref, bbr_ref, bbi_ref):
    lr = lr_ref[...]
    li = li_ref[...]
    step = jnp.exp(ldt_ref[...])
    mag = jnp.exp(lr * step)
    abr = mag * jnp.cos(li * step)
    abi = mag * jnp.sin(li * step)
    den = lr * lr + li * li
    numr = abr - 1.0
    cr = (numr * lr + abi * li) / den
    ci = (abi * lr - numr * li) / den
    abr_ref[...] = abr
    abi_ref[...] = abi
    btr = btr_ref[...]
    bti = bti_ref[...]
    crb = cr[:, None, :]
    cib = ci[:, None, :]
    bbr_ref[...] = crb * btr - cib * bti
    bbi_ref[...] = crb * bti + cib * btr


def _s5prep(lam_re, lam_im, log_dt, bt_re, bt_im):
    g, n = lam_re.shape
    full2 = pl.BlockSpec((g, n), lambda: (0, 0))
    full3 = pl.BlockSpec((g, S5_GROUP, n), lambda: (0, 0, 0))
    return pl.pallas_call(
        _s5prep_kernel,
        in_specs=[full2, full2, pl.BlockSpec((g, 1), lambda: (0, 0)), full3, full3],
        out_specs=[full2, full2, full3, full3],
        out_shape=[jax.ShapeDtypeStruct((g, n), F32)] * 2
        + [jax.ShapeDtypeStruct((g, S5_GROUP, n), F32)] * 2,
        name="s5prep",
    )(lam_re, lam_im, log_dt, bt_re, bt_im)


W_ALIGN = 32
_SRC_ZA = 0
_SRC_XBC = SSD_WIDTH
_SRC_DT = _SRC_XBC + SSD_XBC
_SRC_UB = _SRC_DT + SSD_HEADS
_SRC_ZB = _SRC_UB + S5_WIDTH
_SRC_GA = _SRC_ZB + S5_WIDTH
_SRC_GB = _SRC_GA + D_MODEL
_SEGMENTS = ((_SRC_ZA, SSD_WIDTH), (_SRC_GA, D_MODEL), (_SRC_GB, D_MODEL), (_SRC_XBC, SSD_XBC),
             (_SRC_UB, S5_WIDTH), (_SRC_ZB, S5_WIDTH))
_SRC_TILES = tuple((start + k) // W_ALIGN for start, width in _SEGMENTS for k in range(0, width, PROJ_TN))


def _wprep_dt_kernel(w_ref, o_ref):
    o_ref[...] = jnp.zeros(o_ref.shape, o_ref.dtype)
    o_ref[0:SSD_HEADS, :] = w_ref[...].astype(BF16)


def _wprep_dt(wt):
    d = wt.shape[1]
    assert _SRC_DT % SSD_HEADS == 0
    return pl.pallas_call(
        _wprep_dt_kernel,
        grid=(1,),
        in_specs=[pl.BlockSpec((SSD_HEADS, d), lambda i: (_SRC_DT // SSD_HEADS, 0))],
        out_specs=pl.BlockSpec((LANES, d), lambda i: (0, 0)),
        out_shape=jax.ShapeDtypeStruct((LANES, d), BF16),
        name="wprep_dt",
    )(wt)


def _norm_rows(x_ref, nw_ref, wdt_ref, xn_ref, dt_ref):
    x = x_ref[...]
    ms = jnp.mean(x * x, axis=-1, keepdims=True)
    xn = ((x * lax.rsqrt(ms + NORM_EPS)) * nw_ref[...]).astype(BF16)
    xn_ref[...] = xn
    dt_ref[...] = lax.dot_general(xn, wdt_ref[...], _NT, preferred_element_type=F32)


def _wprep_inproj_kernel(tbl_ref, x_ref, nw_ref, w_ref, wdt_ref, wout_ref, proj_ref, dt_ref, xn_ref):
    del tbl_ref

    @pl.when(pl.program_id(0) == 0)
    def _():
        _norm_rows(x_ref, nw_ref, wdt_ref, xn_ref, dt_ref)

    wb = w_ref[...].astype(BF16)
    wout_ref[...] = wb
    proj_ref[...] = lax.dot_general(xn_ref[...], wb, _NT, preferred_element_type=F32)


def _wprep_inproj(wt, w_dt, x, norm_w):
    m, d = x.shape
    ntiles = len(_SRC_TILES)
    assert ntiles * PROJ_TN == PROJ_COLS
    return pl.pallas_call(
        _wprep_inproj_kernel,
        grid_spec=pltpu.PrefetchScalarGridSpec(
            num_scalar_prefetch=1,
            grid=(ntiles,),
            in_specs=[
                pl.BlockSpec((m, d), lambda j, tbl: (0, 0)),
                pl.BlockSpec((1, d), lambda j, tbl: (0, 0)),
                pl.BlockSpec((pl.Element(PROJ_TN), pl.Element(d)),
                             lambda j, tbl: (pl.multiple_of(tbl[j] * W_ALIGN, W_ALIGN), 0)),
                pl.BlockSpec((LANES, d), lambda j, tbl: (0, 0)),
            ],
            out_specs=[
                pl.BlockSpec((PROJ_TN, d), lambda j, tbl: (j, 0)),
                pl.BlockSpec((m, PROJ_TN), lambda j, tbl: (0, j)),
                pl.BlockSpec((m, LANES), lambda j, tbl: (0, 0)),
            ],
            scratch_shapes=[pltpu.VMEM((m, d), BF16)],
        ),
        out_shape=[jax.ShapeDtypeStruct((PROJ_COLS, d), BF16), jax.ShapeDtypeStruct((m, PROJ_COLS), F32),
                   jax.ShapeDtypeStruct((m, LANES), F32)],
        compiler_params=_params("arbitrary"),
        name="wprep_inproj",
    )(jnp.asarray(_SRC_TILES, jnp.int32), x, norm_w, wt, w_dt)


def _inproj_kernel(x_ref, nw_ref, w_ref, wdt_ref, proj_ref, dt_ref, xn_ref):
    @pl.when(pl.program_id(1) == 0)
    def _():
        _norm_rows(x_ref, nw_ref, wdt_ref, xn_ref, dt_ref)

    proj_ref[...] = lax.dot_general(xn_ref[...], w_ref[...], _NT, preferred_element_type=F32)


def _inproj(x, norm_w, w_main, w_dt, tm):
    m, d = x.shape
    n = w_main.shape[0]
    return pl.pallas_call(
        _inproj_kernel,
        grid=(m // tm, n // PROJ_TN),
        in_specs=[
            pl.BlockSpec((tm, d), lambda i, j: (i, 0)),
            pl.BlockSpec((1, d), lambda i, j: (0, 0)),
            pl.BlockSpec((PROJ_TN, d), lambda i, j: (j, 0)),
            pl.BlockSpec((LANES, d), lambda i, j: (0, 0)),
        ],
        out_specs=[
            pl.BlockSpec((tm, PROJ_TN), lambda i, j: (i, j)),
            pl.BlockSpec((tm, LANES), lambda i, j: (i, 0)),
        ],
        out_shape=[jax.ShapeDtypeStruct((m, n), F32), jax.ShapeDtypeStruct((m, LANES), F32)],
        scratch_shapes=[pltpu.VMEM((tm, d), BF16)],
        compiler_params=_params("arbitrary", "arbitrary"),
        name="inproj",
    )(x, norm_w, w_main, w_dt)


def _ssd_chunk(r0, xbc_ref, za_ref, dtr_ref, convw_ref, convb_ref, dtb_ref, alog_ref, dexp_ref, nw_ref,
               y_ref, ht_scr, ext_scr, mask_rows, need_y):
    L = CHUNK
    P2 = 2 * SSD_HEAD_DIM
    GW = SSD_WIDTH // SSD_GROUPS
    time_rows = lambda t0, n: pl.ds(CONV_PITCH * t0, n, stride=CONV_PITCH)

    w = convw_ref[...]
    bias = convb_ref[...]
    conv_parts = []
    for s in range(SSD_XBC // LANES):
        ls = slice(s * LANES, (s + 1) * LANES)
        xc = xbc_ref[r0:r0 + L, ls]
        ext_scr[s, time_rows(SUBLANES, L), :] = xc
        acc = bias[:, ls]
        for k in range(SSD_CONV - 1):
            acc = acc + ext_scr[s, time_rows(SUBLANES - (SSD_CONV - 1) + k, L), :] * w[k:k + 1, ls]
        conv_parts.append(acc + xc * w[SSD_CONV - 1:SSD_CONV, ls])
        ext_scr[s, time_rows(0, SUBLANES), :] = xc[L - SUBLANES:L, :]
    xbc = _silu(jnp.concatenate(conv_parts, axis=1))
    xs = xbc[:, :SSD_WIDTH]
    bmat = xbc[:, SSD_WIDTH:SSD_WIDTH + SSD_GROUPS * SSD_STATE]
    cmat = xbc[:, SSD_WIDTH + SSD_GROUPS * SSD_STATE:]

    rows = lax.broadcasted_iota(jnp.int32, (L, L), 0)
    cols = lax.broadcasted_iota(jnp.int32, (L, L), 1)
    causal = rows >= cols
    lane_lo = cols < SSD_HEAD_DIM

    dt = _softplus(dtr_ref[r0:r0 + L, :] + dtb_ref[...])
    if mask_rows:
        dt = jnp.where(rows < mask_rows, 0.0, dt)
    a = dt * (-jnp.exp(alog_ref[...]))
    tril = jnp.where(causal, 1.0, 0.0).astype(BF16)
    a1, a2, a3 = _split3(a)
    acum = (jnp.dot(tril, a1, preferred_element_type=F32)
            + jnp.dot(tril, a2, preferred_element_type=F32)
            + jnp.dot(tril, a3, preferred_element_type=F32))
    a2 = acum * LOG2E
    e_cum = jnp.exp2(a2)
    w_end = dt * jnp.exp2(a2[L - 1:L, :] - a2)
    a2dt_t = (a2 - jnp.log2(dt)).T

    dexp = dexp_ref[...]
    xs_b = xs.astype(BF16)
    y_parts = []
    for g in range(SSD_GROUPS):
        bg = bmat[:, g * SSD_STATE:(g + 1) * SSD_STATE].astype(BF16)
        cg = cmat[:, g * SSD_STATE:(g + 1) * SSD_STATE].astype(BF16)
        ht_g = ht_scr[:, g * GW:(g + 1) * GW]
        if need_y:
            cb = lax.dot_general(cg, bg, (((1,), (1,)), ((), ())), preferred_element_type=F32)
            y_off = jnp.dot(cg, ht_g.astype(BF16), preferred_element_type=F32)
        xw_parts = []
        elast_parts = []
        for jj in range(GW // P2):
            lo = g * GW + jj * P2
            h0 = lo // SSD_HEAD_DIM
            yd, eb, wb = [], [], []
            for h in (h0, h0 + 1):
                if need_y:
                    colb = jnp.broadcast_to(a2[:, h:h + 1], (L, L))
                    m = cb * jnp.exp2(jnp.where(causal, colb - a2dt_t[h:h + 1, :], -jnp.inf))
                    yd.append(jnp.dot(m.astype(BF16), xs_b[:, lo:lo + P2], preferred_element_type=F32))
                eb.append(jnp.broadcast_to(e_cum[:, h:h + 1], (L, L)))
                wb.append(jnp.broadcast_to(w_end[:, h:h + 1], (L, L)))
            xs_pair = xs[:, lo:lo + P2]
            if need_y:
                y_pair = (jnp.where(lane_lo, yd[0], yd[1])
                          + y_off[:, jj * P2:(jj + 1) * P2] * jnp.where(lane_lo, eb[0], eb[1]))
                y_parts.append(y_pair + dexp[:, lo:lo + P2] * xs_pair)
            xw_parts.append(xs_pair * jnp.where(lane_lo, wb[0], wb[1]))
            elast_parts.append(jnp.where(lane_lo[0:1, :], eb[0][L - 1:L, :], eb[1][L - 1:L, :]))
        xw = jnp.concatenate(xw_parts, axis=1)
        elast = jnp.concatenate(elast_parts, axis=1)
        st = lax.dot_general(bg, xw.astype(BF16), (((0,), (0,)), ((), ())),
                             preferred_element_type=F32)
        ht_scr[:, g * GW:(g + 1) * GW] = ht_g * elast + st

    if not need_y:
        y_ref[r0:r0 + L, :] = jnp.zeros((L, SSD_WIDTH), y_ref.dtype)
        return
    y = jnp.concatenate(y_parts, axis=1)
    y = y * _silu(za_ref[r0:r0 + L, :])
    nw = nw_ref[...]
    outs = []
    for g in range(SSD_GROUPS):
        yg = y[:, g * GW:(g + 1) * GW]
        ms = jnp.mean(yg * yg, axis=-1, keepdims=True)
        outs.append((yg * lax.rsqrt(ms + NORM_EPS)) * nw[:, g * GW:(g + 1) * GW])
    y_ref[r0:r0 + L, :] = jnp.concatenate(outs, axis=1).astype(y_ref.dtype)


def _ssd_kernel(xbc_ref, za_ref, dtr_ref, ht0_ref, tail0_ref, convw_ref, convb_ref, dtb_ref, alog_ref,
                dexp_ref, nw_ref, y_ref, h_ref, ht_ref, tail_ref, ht_scr, ext_scr, *, mask_rows, need_y):
    c = pl.program_id(1)
    nrows = xbc_ref.shape[0]

    @pl.when(c == 0)
    def _():
        ht_scr[...] = ht0_ref[...]
        for s in range(SSD_XBC // LANES):
            ext_scr[s, pl.ds(0, SUBLANES, stride=CONV_PITCH), :] = tail0_ref[:, s * LANES:(s + 1) * LANES]

    for r0 in range(0, nrows, CHUNK):
        _ssd_chunk(r0, xbc_ref, za_ref, dtr_ref, convw_ref, convb_ref, dtb_ref, alog_ref, dexp_ref, nw_ref,
                   y_ref, ht_scr, ext_scr, mask_rows, need_y)

    @pl.when(c == pl.num_programs(1) - 1)
    def _():
        ht = ht_scr[...]
        ht_ref[...] = ht
        h_ref[...] = ht.T
        tail_ref[...] = xbc_ref[nrows - SUBLANES:nrows, :]


def _ssd(proj, dt_raw, ht0, tail0, conv_w, conv_b, dt_bias, a_log, d_exp, norm_w, *, batch, nchunks,
         row_block0, mask_rows, need_y=True):
    L = CHUNK
    rows = batch * nchunks * L
    rb = lambda b, c: row_block0 + b * nchunks + c
    const2 = lambda shape: pl.BlockSpec(shape, lambda b, c: (0, 0))
    return pl.pallas_call(
        functools.partial(_ssd_kernel, mask_rows=mask_rows, need_y=need_y),
        grid=(batch, nchunks),
        in_specs=[
            pl.BlockSpec((L, SSD_XBC), lambda b, c: (rb(b, c), COL_XBC // SSD_XBC)),
            pl.BlockSpec((L, SSD_WIDTH), lambda b, c: (rb(b, c), COL_ZA // SSD_WIDTH)),
            pl.BlockSpec((L, LANES), lambda b, c: (rb(b, c), 0)),
            const2((SSD_STATE, SSD_WIDTH)),
            const2((SUBLANES, SSD_XBC)),
            const2((SSD_CONV, SSD_XBC)),
            const2((1, SSD_XBC)),
            const2((1, LANES)),
            const2((1, LANES)),
            const2((1, SSD_WIDTH)),
            const2((1, SSD_WIDTH)),
        ],
        out_specs=[
            pl.BlockSpec((L, SSD_WIDTH), lambda b, c: (b * nchunks + c, 0)),
            pl.BlockSpec((None, SSD_WIDTH, SSD_STATE), lambda b, c: (b, 0, 0)),
            pl.BlockSpec((None, SSD_STATE, SSD_WIDTH), lambda b, c: (b, 0, 0)),
            pl.BlockSpec((None, SUBLANES, SSD_XBC), lambda b, c: (b, 0, 0)),
        ],
        out_shape=[
            jax.ShapeDtypeStruct((rows, SSD_WIDTH), BF16),
            jax.ShapeDtypeStruct((batch, SSD_WIDTH, SSD_STATE), F32),
            jax.ShapeDtypeStruct((batch, SSD_STATE, SSD_WIDTH), F32),
            jax.ShapeDtypeStruct((batch, SUBLANES, SSD_XBC), F32),
        ],
        scratch_shapes=[pltpu.VMEM((SSD_STATE, SSD_WIDTH), F32),
                        pltpu.VMEM((SSD_XBC // LANES, CONV_PITCH * (SUBLANES + CHUNK), LANES), F32)],
        compiler_params=_params("arbitrary", "arbitrary"),
        name="ssd",
    )(proj, proj, dt_raw, ht0, tail0, conv_w, conv_b, dt_bias, a_log, d_exp, norm_w)


def _ssd_step_kernel(xbc_ref, za_ref, dtr_ref, cs_ref, h_ref, convw_ref, convb_ref, dtb_ref, alog_ref,
                     dexp_ref, nw_ref, y_ref, hout_ref, csout_ref):
    R = SUBLANES
    GW = SSD_WIDTH // SSD_GROUPS
    x = xbc_ref[...]
    w = convw_ref[...]
    s0 = cs_ref[0]
    s1 = cs_ref[1]
    s2 = cs_ref[2]
    conv = convb_ref[...] + s0 * w[0:1]
    conv = conv + s1 * w[1:2]
    conv = conv + s2 * w[2:3]
    conv = conv + x * w[3:4]
    csout_ref[0] = s1
    csout_ref[1] = s2
    csout_ref[2] = x
    xbc = _silu(conv)
    xs = xbc[:, :SSD_WIDTH]
    bmat = xbc[:, SSD_WIDTH:SSD_WIDTH + SSD_GROUPS * SSD_STATE]
    cmat = xbc[:, SSD_WIDTH + SSD_GROUPS * SSD_STATE:]

    dt = _softplus(dtr_ref[...] + dtb_ref[...])
    da = jnp.exp(dt * (-jnp.exp(alog_ref[...])))
    dt_t = dt.T
    da_t = da.T
    expand = lambda v: jnp.concatenate(
        [jnp.broadcast_to(v[h:h + 1, :], (SSD_HEAD_DIM, R)) for h in range(SSD_HEADS)], axis=0)
    xd_t = xs.T * expand(dt_t)
    da_te = expand(da_t)

    cmat_b = cmat.astype(BF16)
    yrows = []
    for i in range(R):
        bexp = jnp.concatenate(
            [jnp.broadcast_to(bmat[i:i + 1, g * SSD_STATE:(g + 1) * SSD_STATE], (GW, SSD_STATE))
             for g in range(SSD_GROUPS)], axis=0)
        hn = h_ref[i] * da_te[:, i:i + 1] + xd_t[:, i:i + 1] * bexp
        hout_ref[i] = hn
        cg = jnp.concatenate([cmat_b[i:i + 1, g * SSD_STATE:(g + 1) * SSD_STATE] for g in range(SSD_GROUPS)]
                             + [jnp.zeros((R - SSD_GROUPS, SSD_STATE), BF16)], axis=0)
        yg = lax.dot_general(cg, hn.astype(BF16), (((1,), (1,)), ((), ())), preferred_element_type=F32)
        yrows.append(jnp.concatenate([yg[g:g + 1, g * GW:(g + 1) * GW] for g in range(SSD_GROUPS)], axis=1))
    y = jnp.concatenate(yrows, axis=0)
    y = y + dexp_ref[...] * xs
    y = y * _silu(za_ref[...])
    nw = nw_ref[...]
    outs = []
    for g in range(SSD_GROUPS):
        yg = y[:, g * GW:(g + 1) * GW]
        ms = jnp.mean(yg * yg, axis=-1, keepdims=True)
        outs.append((yg * lax.rsqrt(ms + NORM_EPS)) * nw[:, g * GW:(g + 1) * GW])
    y_ref[...] = jnp.concatenate(outs, axis=1).astype(y_ref.dtype)


def _ssd_step(proj, dt_raw, conv_state, ssd_state, conv_w, conv_b, dt_bias, a_log, d_exp, norm_w, *, nseq):
    R = SUBLANES
    const2 = lambda shape: pl.BlockSpec(shape, lambda i: (0, 0))
    return pl.pallas_call(
        _ssd_step_kernel,
        grid=(nseq // R,),
        in_specs=[
            pl.BlockSpec((R, SSD_XBC), lambda i: (i, COL_XBC // SSD_XBC)),
            pl.BlockSpec((R, SSD_WIDTH), lambda i: (i, COL_ZA // SSD_WIDTH)),
            pl.BlockSpec((R, LANES), lambda i: (i, 0)),
            pl.BlockSpec((SSD_CONV - 1, R, SSD_XBC), lambda i: (0, i, 0)),
            pl.BlockSpec((R, SSD_WIDTH, SSD_STATE), lambda i: (i, 0, 0)),
            const2((SSD_CONV, SSD_XBC)),
            const2((1, SSD_XBC)),
            const2((1, LANES)),
            const2((1, LANES)),
            const2((1, SSD_WIDTH)),
            const2((1, SSD_WIDTH)),
        ],
        out_specs=[
            pl.BlockSpec((R, SSD_WIDTH), lambda i: (i, 0)),
            pl.BlockSpec((R, SSD_WIDTH, SSD_STATE), lambda i: (i, 0, 0)),
            pl.BlockSpec((SSD_CONV - 1, R, SSD_XBC), lambda i: (0, i, 0)),
        ],
        out_shape=[
            jax.ShapeDtypeStruct((nseq, SSD_WIDTH), BF16),
            jax.ShapeDtypeStruct((nseq, SSD_WIDTH, SSD_STATE), F32),
            jax.ShapeDtypeStruct((SSD_CONV - 1, nseq, SSD_XBC), F32),
        ],
        compiler_params=_params("arbitrary"),
        name="ssd_step",
    )(proj, proj, dt_raw, conv_state, ssd_state, conv_w, conv_b, dt_bias, a_log, d_exp, norm_w)


def _s5_block_diag(w, rows_per_group, cols_per_group):
    ngrp, r, lanes = w.shape
    width = ngrp * cols_per_group
    tiled = jnp.concatenate([w.reshape(ngrp * r, lanes)] * (width // lanes), axis=1)
    row_g = lax.broadcasted_iota(jnp.int32, tiled.shape, 0) // rows_per_group
    col_g = lax.broadcasted_iota(jnp.int32, tiled.shape, 1) // cols_per_group
    return jnp.where(row_g == col_g, tiled, 0.0)


def _s5_weights(bbr_ref, bbi_ref, cr_ref, ci_ref):
    return tuple(_s5_block_diag(r[...], S5_GROUP, S5_STATE) for r in (bbr_ref, bbi_ref, cr_ref, ci_ref))


def _s5_pitch(seg_len):
    return seg_len if (seg_len // SUBLANES) % 2 else seg_len + SUBLANES


def _s5_kernel(*refs, seq, nconv, need_y):
    (u_ref, h0r_ref, h0i_ref, bbr_ref, bbi_ref, ctr_ref, cti_ref, ar_ref, ai_ref, d_ref) = refs[:10]
    w_in_refs = refs[10:10 + nconv]
    y_ref, hr_ref, hi_ref = refs[10 + nconv:13 + nconv]
    w_out_refs = refs[13 + nconv:13 + 2 * nconv]
    pad, pe, po, cr, ci, hb_in, hb_out, w2r_scr, w2i_scr, cta_scr, ctb_scr, k0_scr = refs[13 + 2 * nconv:]
    for wi, wo in zip(w_in_refs, w_out_refs):
        wo[...] = wi[...].astype(BF16)

    NS = SUBLANES
    SL = seq // NS
    H = SL // 2
    pitch = _s5_pitch(SL)
    nk = S5_BLK_U // LANES
    n = S5_BLK_N
    a_re = ar_ref[...]
    a_im = ai_ref[...]
    bf = lambda v: v.astype(BF16)

    @pl.when(pl.program_id(1) == 0)
    def _():
        bt_re, bt_im, ct_re, ct_im = _s5_weights(bbr_ref, bbi_ref, ctr_ref, cti_ref)
        w2r_scr[...] = jnp.concatenate([bf(bt_re * a_re - bt_im * a_im), bf(bt_re)], axis=0)
        w2i_scr[...] = jnp.concatenate([bf(bt_re * a_im + bt_im * a_re), bf(bt_im)], axis=0)
        if need_y:
            cta_scr[0] = bf(ct_re * a_re - ct_im * a_im)
            cta_scr[1] = bf(ct_re * a_im + ct_im * a_re)
            ctb_scr[0] = bf(ct_re)
            ctb_scr[1] = bf(ct_im)
            k0_scr[...] = bf(lax.dot_general(bf(bt_re), bf(ct_re), _NT, preferred_element_type=F32)
                             - lax.dot_general(bf(bt_im), bf(ct_im), _NT, preferred_element_type=F32))

    for s in range(NS):
        for k in range(nk):
            pad[k, s * pitch:s * pitch + SL, :] = u_ref[s * SL:(s + 1) * SL, k * LANES:(k + 1) * LANES]

    def gather(jp, c):
        r0 = pl.multiple_of(jp * NS, NS)
        for k in range(nk):
            pe[pl.ds(r0, NS), k * LANES:(k + 1) * LANES] = pad[k, pl.ds(2 * jp, NS, stride=pitch), :]
            po[pl.ds(r0, NS), k * LANES:(k + 1) * LANES] = pad[k, pl.ds(2 * jp + 1, NS, stride=pitch), :]
        return c

    lax.fori_loop(0, H, gather, 0, unroll=4)
    ue_b = bf(pe[...])
    uo_b = bf(po[...])
    qc = 4 * S5_GROUP
    qn = 4 * S5_STATE
    for q in range(S5_BLK_U // qc):
        u_q = jnp.concatenate([ue_b[:, q * qc:(q + 1) * qc], uo_b[:, q * qc:(q + 1) * qc]], axis=1)
        rows_q = [slice(q * qc, (q + 1) * qc), slice(S5_BLK_U + q * qc, S5_BLK_U + (q + 1) * qc)]
        cols_q = slice(q * qn, (q + 1) * qn)
        wq_re = jnp.concatenate([w2r_scr[r, cols_q] for r in rows_q], axis=0)
        wq_im = jnp.concatenate([w2i_scr[r, cols_q] for r in rows_q], axis=0)
        cr[:, cols_q] = jnp.dot(u_q, wq_re, preferred_element_type=F32)
        ci[:, cols_q] = jnp.dot(u_q, wq_im, preferred_element_type=F32)

    a2_re = a_re * a_re - a_im * a_im
    a2_im = 2.0 * (a_re * a_im)
    ar = jnp.broadcast_to(a2_re, (NS, n))
    ai = jnp.broadcast_to(a2_im, (NS, n))

    def step(carry, r0):
        hr, hi = carry
        nr = ar * hr - ai * hi + cr[pl.ds(r0, NS), :]
        ni = ar * hi + ai * hr + ci[pl.ds(r0, NS), :]
        return nr, ni

    def scan_ends(j, carry):
        return step(carry, pl.multiple_of(j * NS, NS))

    zero = jnp.zeros((NS, n), F32)
    er, ei = lax.fori_loop(0, H, scan_ends, (zero, zero), unroll=4)

    asr, asi = a_re, a_im
    for _ in range(SL.bit_length() - 1):
        asr, asi = asr * asr - asi * asi, 2.0 * (asr * asi)
    gr = [h0r_ref[...]]
    gi = [h0i_ref[...]]
    for s in range(NS):
        gr.append(er[s:s + 1, :] + (asr * gr[s] - asi * gi[s]))
        gi.append(ei[s:s + 1, :] + (asr * gi[s] + asi * gr[s]))
    hr_ref[...] = gr[NS]
    hi_ref[...] = gi[NS]
    if not need_y:
        y_ref[...] = jnp.zeros(y_ref.shape, y_ref.dtype)
        return

    def scan_store(jp, c0):
        r0 = pl.multiple_of(jp * 2 * NS, 2 * NS)
        c1 = step(c0, r0)
        c2 = step(c1, r0 + NS)
        hb_in[pl.ds(r0, 2 * NS), 0:n] = bf(jnp.concatenate([c0[0], c1[0]], axis=0))
        hb_in[pl.ds(r0, 2 * NS), n:2 * n] = bf(jnp.concatenate([c0[1], c1[1]], axis=0))
        hb_out[pl.ds(r0, 2 * NS), 0:n] = bf(jnp.concatenate([c1[0], c2[0]], axis=0))
        hb_out[pl.ds(r0, 2 * NS), n:2 * n] = bf(jnp.concatenate([c1[1], c2[1]], axis=0))
        return c2

    lax.fori_loop(0, H // 2, scan_store,
                  (jnp.concatenate(gr[:NS], axis=0), jnp.concatenate(gi[:NS], axis=0)), unroll=2)

    half = seq // 2
    rb = half // 4
    d = d_ref[...]
    cta_re, cta_im = cta_scr[0], cta_scr[1]
    ctb_re, ctb_im = ctb_scr[0], ctb_scr[1]
    k0 = k0_scr[...]
    for r in range(0, half, rb):
        rows = slice(r, r + rb)
        y_odd = (lax.dot_general(hb_out[rows, 0:n], ctb_re, _NT, preferred_element_type=F32)
                 - lax.dot_general(hb_out[rows, n:2 * n], ctb_im, _NT, preferred_element_type=F32))
        y_even = (lax.dot_general(hb_in[rows, 0:n], cta_re, _NT, preferred_element_type=F32)
                  - lax.dot_general(hb_in[rows, n:2 * n], cta_im, _NT, preferred_element_type=F32)
                  + jnp.dot(bf(pe[rows, :]), k0, preferred_element_type=F32))
        po[rows, :] = _gelu_tanh(y_odd + d * po[rows, :])
        pe[rows, :] = _gelu_tanh(y_even + d * pe[rows, :])

    def scatter(jp, c):
        r0 = pl.multiple_of(jp * NS, NS)
        for k in range(nk):
            pad[k, pl.ds(2 * jp, NS, stride=pitch), :] = pe[pl.ds(r0, NS), k * LANES:(k + 1) * LANES]
            pad[k, pl.ds(2 * jp + 1, NS, stride=pitch), :] = po[pl.ds(r0, NS), k * LANES:(k + 1) * LANES]
        return c

    lax.fori_loop(0, H, scatter, 0, unroll=4)
    for s in range(NS):
        for k in range(nk):
            y_ref[s * SL:(s + 1) * SL, k * LANES:(k + 1) * LANES] = pad[k, s * pitch:s * pitch + SL, :]


def _s5(proj, h0r, h0i, bb_re, bb_im, ct_re, ct_im, ab_re, ab_im, d_s5, *, batch, seq, row_block0, need_y=True,
        convert=()):
    ub0 = COL_UB // S5_BLK_U
    nsteps = batch * S5_BLOCKS
    slice_spec = lambda w: pl.BlockSpec((w.shape[0] // nsteps, w.shape[1]), lambda j, b: (j * batch + b, 0))
    assert all(w.shape[0] % (2 * SUBLANES * nsteps) == 0 for w in convert)
    seg = seq // SUBLANES
    assert seg & (seg - 1) == 0
    gpb = S5_GROUPS // S5_BLOCKS
    vec = lambda width: pl.BlockSpec((1, width), lambda j, b: (0, j))
    wspec = lambda r: pl.BlockSpec((gpb, r, LANES), lambda j, b: (j, 0, 0))
    st_out = pl.BlockSpec((None, 1, S5_BLK_N), lambda j, b: (b, 0, j))
    return pl.pallas_call(
        functools.partial(_s5_kernel, seq=seq, nconv=len(convert), need_y=need_y),
        grid=(S5_BLOCKS, batch),
        in_specs=[
            pl.BlockSpec((seq, S5_BLK_U), lambda j, b: (row_block0 + b, ub0 + j)),
            vec(S5_BLK_N), vec(S5_BLK_N),
            wspec(S5_GROUP), wspec(S5_GROUP), wspec(S5_GROUP), wspec(S5_GROUP),
            vec(S5_BLK_N), vec(S5_BLK_N), vec(S5_BLK_U),
        ] + [slice_spec(w) for w in convert],
        out_specs=[pl.BlockSpec((seq, S5_BLK_U), lambda j, b: (b, j)), st_out, st_out]
        + [slice_spec(w) for w in convert],
        out_shape=[
            jax.ShapeDtypeStruct((batch * seq, S5_WIDTH), F32),
            jax.ShapeDtypeStruct((batch, 1, S5_NSTATE), F32),
            jax.ShapeDtypeStruct((batch, 1, S5_NSTATE), F32),
        ] + [jax.ShapeDtypeStruct(w.shape, BF16) for w in convert],
        scratch_shapes=[
            pltpu.VMEM((S5_BLK_U // LANES, SUBLANES * _s5_pitch(seg), LANES), F32),
            pltpu.VMEM((seq // 2, S5_BLK_U), F32),
            pltpu.VMEM((seq // 2, S5_BLK_U), F32),
            pltpu.VMEM((seq // 2, S5_BLK_N), F32),
            pltpu.VMEM((seq // 2, S5_BLK_N), F32),
            pltpu.VMEM((seq // 2, 2 * S5_BLK_N), BF16),
            pltpu.VMEM((seq // 2, 2 * S5_BLK_N), BF16),
            pltpu.VMEM((2 * S5_BLK_U, S5_BLK_N), BF16),
            pltpu.VMEM((2 * S5_BLK_U, S5_BLK_N), BF16),
            pltpu.VMEM((2, S5_BLK_U, S5_BLK_N), BF16),
            pltpu.VMEM((2, S5_BLK_U, S5_BLK_N), BF16),
            pltpu.VMEM((S5_BLK_U, S5_BLK_U), BF16),
        ],
        compiler_params=_params("arbitrary", "arbitrary"),
        name="s5",
    )(proj, h0r, h0i, bb_re, bb_im, ct_re, ct_im, ab_re, ab_im, d_s5, *convert)


def _s5_step_kernel(u_ref, h0r_ref, h0i_ref, bbr_ref, bbi_ref, ctr_ref, cti_ref, ar_ref, ai_ref, d_ref,
                    y_ref, hr_ref, hi_ref):
    bt_re, bt_im, ct_re, ct_im = (w.astype(BF16) for w in _s5_weights(bbr_ref, bbi_ref, ctr_ref, cti_ref))
    u = u_ref[...]
    ub = u.astype(BF16)
    ar = ar_ref[...]
    ai = ai_ref[...]
    h0r = h0r_ref[...].T
    h0i = h0i_ref[...].T
    hr = jnp.dot(ub, bt_re, preferred_element_type=F32) + (ar * h0r - ai * h0i)
    hi = jnp.dot(ub, bt_im, preferred_element_type=F32) + (ar * h0i + ai * h0r)
    hr_ref[...] = hr.T
    hi_ref[...] = hi.T
    y = (lax.dot_general(hr.astype(BF16), ct_re, _NT, preferred_element_type=F32)
         - lax.dot_general(hi.astype(BF16), ct_im, _NT, preferred_element_type=F32))
    y_ref[...] = _gelu_tanh(y + d_ref[...] * u)


def _s5_step(proj, h0r, h0i, bb_re, bb_im, ct_re, ct_im, ab_re, ab_im, d_s5, *, nseq):
    ub0 = COL_UB // S5_BLK_U
    gpb = S5_GROUPS // S5_BLOCKS
    vec = lambda width: pl.BlockSpec((1, width), lambda j: (0, j))
    wspec = lambda r: pl.BlockSpec((gpb, r, LANES), lambda j: (j, 0, 0))
    st = pl.BlockSpec((S5_BLK_N, nseq), lambda j: (j, 0))
    return pl.pallas_call(
        _s5_step_kernel,
        grid=(S5_BLOCKS,),
        in_specs=[
            pl.BlockSpec((nseq, S5_BLK_U), lambda j: (0, ub0 + j)),
            st, st,
            wspec(S5_GROUP), wspec(S5_GROUP), wspec(S5_GROUP), wspec(S5_GROUP),
            vec(S5_BLK_N), vec(S5_BLK_N), vec(S5_BLK_U),
        ],
        out_specs=[pl.BlockSpec((nseq, S5_BLK_U), lambda j: (0, j)), st, st],
        out_shape=[
            jax.ShapeDtypeStruct((nseq, S5_WIDTH), F32),
            jax.ShapeDtypeStruct((S5_NSTATE, nseq), F32),
            jax.ShapeDtypeStruct((S5_NSTATE, nseq), F32),
        ],
        compiler_params=_params("arbitrary"),
        name="s5_step",
    )(proj, h0r, h0i, bb_re, bb_im, ct_re, ct_im, ab_re, ab_im, d_s5)


def _tail_kernel(yn_ref, ybg_ref, zb_ref, ga_ref, gb_ref, x_ref, wpa_ref, wglu_ref, bglu_ref, wpb_ref,
                 wout_ref, fnw_ref, out_ref):
    sg_a = _sigmoid(ga_ref[...])
    sg_b = _sigmoid(gb_ref[...])
    sz_b = _silu(zb_ref[...])
    yb = ybg_ref[...]
    glu = jnp.dot(yb.astype(BF16), wglu_ref[...], preferred_element_type=F32) + bglu_ref[...]
    ya = jnp.dot(yn_ref[...], wpa_ref[...], preferred_element_type=F32)
    yb = (yb * _sigmoid(glu)) * sz_b
    ybp = jnp.dot(yb.astype(BF16), wpb_ref[...], preferred_element_type=F32)
    mixed = sg_a * ya + sg_b * ybp
    o = x_ref[...] + jnp.dot(mixed.astype(BF16), wout_ref[...], preferred_element_type=F32)
    ms = jnp.mean(o * o, axis=-1, keepdims=True)
    out_ref[...] = (o * lax.rsqrt(ms + NORM_EPS)) * fnw_ref[...]


def _tail(yn, ybg, proj, x, w_proj_a, w_glu, b_glu, w_proj_b, w_out, final_norm_w, *, tm):
    m, d = x.shape
    resident = lambda shape: pl.BlockSpec(shape, lambda i: (0, 0), pipeline_mode=pl.Buffered(1))
    return pl.pallas_call(
        _tail_kernel,
        grid=(m // tm,),
        in_specs=[
            pl.BlockSpec((tm, SSD_WIDTH), lambda i: (i, 0)),
            pl.BlockSpec((tm, S5_WIDTH), lambda i: (i, 0)),
            pl.BlockSpec((tm, S5_WIDTH), lambda i: (i, COL_ZB // S5_WIDTH)),
            pl.BlockSpec((tm, d), lambda i: (i, COL_GA // D_MODEL)),
            pl.BlockSpec((tm, d), lambda i: (i, COL_GB // D_MODEL)),
            pl.BlockSpec((tm, d), lambda i: (i, 0)),
            resident((SSD_WIDTH, d)),
            resident((S5_WIDTH, S5_WIDTH)),
            resident((1, S5_WIDTH)),
            resident((S5_WIDTH, d)),
            resident((d, d)),
            resident((1, d)),
        ],
        out_specs=pl.BlockSpec((tm, d), lambda i: (i, 0)),
        out_shape=jax.ShapeDtypeStruct((m, d), F32),
        compiler_params=_params("arbitrary"),
        name="tail",
    )(yn, ybg, proj, proj, proj, x, w_proj_a, w_glu, b_glu, w_proj_b, w_out, final_norm_w)


def kernel(x_prompt, x_sample, state_ssd, state_conv, state_s5_re, state_s5_im, meta_tokens, norm_w, w_in,
           conv_w, conv_b, dt_bias, a_log, d_ssd, ssd_norm_w, w_proj_a, lam_re, lam_im, log_dt_s5, b_re, b_im,
           c_re, c_im, d_s5, w_glu, b_glu, w_proj_b, w_out, final_norm_w):
    bsz, seq, d = x_prompt.shape
    nseq = x_sample.shape[0]
    assert d == D_MODEL and seq % CHUNK == 0 and nseq % SUBLANES == 0 and norm_w.shape[0] == 1
    assert meta_tokens.shape[0] == N_META and N_META <= CHUNK

    assert w_in.shape[2] == _SRC_GB + D_MODEL
    wt = jnp.transpose(w_in[0])
    w_dt = _wprep_dt(wt)
    pad_heads = lambda v: jnp.pad(v.reshape(1, SSD_HEADS), ((0, 0), (0, LANES - SSD_HEADS)))
    dtb = pad_heads(dt_bias[0])
    alog = pad_heads(a_log[0])
    d_exp = jnp.repeat(d_ssd[0], SSD_HEAD_DIM).reshape(1, SSD_WIDTH)
    nw1 = norm_w[0].reshape(1, d)
    ssd_nw = ssd_norm_w[0].reshape(1, SSD_WIDTH)
    convw = conv_w[0]
    convb = conv_b[0].reshape(1, SSD_XBC)
    bglu = b_glu[0].reshape(1, S5_WIDTH)
    fnw = final_norm_w.reshape(1, d)
    ds5 = d_s5[0].reshape(1, S5_WIDTH)

    rep = lambda v, k: jnp.concatenate([v] * k, axis=-1)
    lane_rep = LANES // S5_STATE
    ab_re, ab_im, bb_re, bb_im = _s5prep(
        rep(lam_re[0], lane_rep), rep(lam_im[0], lane_rep), log_dt_s5[0].reshape(S5_GROUPS, 1),
        rep(jnp.transpose(b_re[0], (0, 2, 1)), lane_rep), rep(jnp.transpose(b_im[0], (0, 2, 1)), lane_rep))
    ab_re = ab_re[:, :S5_STATE].reshape(1, S5_NSTATE)
    ab_im = ab_im[:, :S5_STATE].reshape(1, S5_NSTATE)
    ct_re = rep(c_re[0], lane_rep)
    ct_im = rep(c_im[0], lane_rep)

    x_main = x_prompt.reshape(bsz * seq, d)
    x_small = jnp.concatenate(
        [x_sample.reshape(nseq, d), jnp.zeros((CHUNK - N_META, d), x_prompt.dtype),
         meta_tokens.astype(x_prompt.dtype)], axis=0)
    assert nseq % CHUNK == 0
    meta_blk = nseq // CHUNK
    tm_main = 1024 if (bsz * seq) % 1024 == 0 else CHUNK
    w_main, proj_s, dt_s = _wprep_inproj(wt, w_dt, x_small, nw1)
    proj_m, dt_m = _inproj(x_main, nw1, w_main, w_dt, tm=tm_main)

    ssd_args = (convw, convb, dtb, alog, d_exp, ssd_nw)
    s5_w = (bb_re, bb_im, ct_re, ct_im)
    s5_v = (ab_re, ab_im, ds5)

    zeros_ht = jnp.zeros((SSD_STATE, SSD_WIDTH), F32)
    zeros_tail = jnp.zeros((SUBLANES, SSD_XBC), F32)
    zeros_s5 = jnp.zeros((1, S5_NSTATE), F32)
    _, _, ht_meta, tail_meta = _ssd(proj_s, dt_s, zeros_ht, zeros_tail, *ssd_args, batch=1, nchunks=1,
                                    row_block0=meta_blk, mask_rows=CHUNK - N_META, need_y=False)
    _, s5r_meta, s5i_meta = _s5(proj_s, zeros_s5, zeros_s5, *s5_w, *s5_v, batch=1, seq=CHUNK,
                                row_block0=meta_blk, need_y=False)[:3]

    yn_m, h_m, _, tail_m = _ssd(proj_m, dt_m, ht_meta[0], tail_meta[0], *ssd_args, batch=bsz,
                                nchunks=seq // CHUNK, row_block0=0, mask_rows=0)
    ybg_m, s5r_m, s5i_m, wpa, wglu, wpb, wout = _s5(
        proj_m, s5r_meta[0], s5i_meta[0], *s5_w, *s5_v, batch=bsz, seq=seq, row_block0=0,
        convert=(w_proj_a[0], w_glu[0], w_proj_b[0], w_out[0]))
    tail_w = (wpa, wglu, bglu, wpb, wout, fnw)
    y_prompt = _tail(yn_m, ybg_m, proj_m, x_main, *tail_w, tm=256)

    yn_s, h_s, cs_s = _ssd_step(proj_s, dt_s, jnp.transpose(state_conv[0], (1, 0, 2)),
                                state_ssd[0].reshape(nseq, SSD_WIDTH, SSD_STATE), *ssd_args, nseq=nseq)
    seq_minor = lambda v: jnp.transpose(v, (1, 2, 0)).reshape(S5_NSTATE, nseq)
    seq_major = lambda v: jnp.transpose(v.reshape(S5_GROUPS, S5_STATE, nseq), (2, 0, 1))[None]
    ybg_s, s5r_s, s5i_s = _s5_step(proj_s, seq_minor(state_s5_re[0]), seq_minor(state_s5_im[0]),
                                   *s5_w, *s5_v, nseq=nseq)
    y_sample = _tail(yn_s, ybg_s, proj_s, x_sample.reshape(nseq, d), *tail_w, tm=nseq)

    dt_out = x_prompt.dtype
    return (
        y_prompt.reshape(bsz, seq, d),
        y_sample.reshape(nseq, 1, d),
        h_m.reshape(1, bsz, SSD_HEADS, SSD_HEAD_DIM, SSD_STATE).astype(dt_out),
        tail_m[:, SUBLANES - (SSD_CONV - 1):, :].reshape(1, bsz, SSD_CONV - 1, SSD_XBC),
        s5r_m.reshape(1, bsz, S5_GROUPS, S5_STATE).astype(dt_out),
        s5i_m.reshape(1, bsz, S5_GROUPS, S5_STATE).astype(dt_out),
        h_s.reshape(1, nseq, SSD_HEADS, SSD_HEAD_DIM, SSD_STATE).astype(dt_out),
        jnp.transpose(cs_s, (1, 0, 2)).reshape(1, nseq, SSD_CONV - 1, SSD_XBC),
        seq_major(s5r_s).astype(dt_out),
        seq_major(s5i_s).astype(dt_out),
    )
```

```python
import functools

import jax
import jax.numpy as jnp
from jax import lax
from jax.experimental import pallas as pl
from jax.experimental.pallas import tpu as pltpu

F32 = jnp.float32
BF16 = jnp.bfloat16

NORM_EPS = 1e-5
LOG2E = 1.4426950408889634
N_META = 16
D_MODEL = 2048
SSD_HEAD_DIM = 64
SSD_HEADS = 32
SSD_GROUPS = 4
SSD_STATE = 128
SSD_WIDTH = SSD_HEADS * SSD_HEAD_DIM
SSD_XBC = SSD_WIDTH + 2 * SSD_GROUPS * SSD_STATE
SSD_CONV = 4
CHUNK = 128
S5_WIDTH = D_MODEL // 2
S5_GROUP = 16
S5_GROUPS = S5_WIDTH // S5_GROUP
S5_STATE = 64
S5_NSTATE = S5_GROUPS * S5_STATE
S5_BLOCKS = 4
S5_BLK_U = S5_WIDTH // S5_BLOCKS
S5_BLK_N = S5_NSTATE // S5_BLOCKS

LANES = 128
SUBLANES = 8
VMEM_LIMIT = 56 * 1024 * 1024
CONV_PITCH = 3

COL_ZA = 0
COL_GA = COL_ZA + SSD_WIDTH
COL_GB = COL_GA + D_MODEL
COL_XBC = COL_GB + D_MODEL
COL_UB = COL_XBC + SSD_XBC
COL_ZB = COL_UB + S5_WIDTH
PROJ_COLS = COL_ZB + S5_WIDTH
PROJ_TN = 1024


_NT = (((1,), (1,)), ((), ()))


def _sigmoid(x):
    return 0.5 * jnp.tanh(0.5 * x) + 0.5


def _silu(x):
    h = 0.5 * x
    return h + h * jnp.tanh(h)


def _softplus(x):
    return jnp.maximum(x, 0.0) + jnp.log1p(jnp.exp(-jnp.abs(x)))


def _gelu_tanh(x):
    c = 0.7978845608028654
    return 0.5 * x * (1.0 + jnp.tanh(c * (x + 0.044715 * (x * x * x))))


def _split3(x):
    x1 = x.astype(BF16)
    r1 = x - x1.astype(F32)
    x2 = r1.astype(BF16)
    x3 = (r1 - x2.astype(F32)).astype(BF16)
    return x1, x2, x3


def _params(*sem):
    return pltpu.CompilerParams(dimension_semantics=sem, vmem_limit_bytes=VMEM_LIMIT)


def _s5prep_kernel(lr_ref, li_ref, ldt_ref, btr_ref, bti_ref, abr_ref, abi_ref, bbr_ref, bbi_ref):
    lr = lr_ref[...]
    li = li_ref[...]
    step = jnp.exp(ldt_ref[...])
    mag = jnp.exp(lr * step)
    abr = mag * jnp.cos(li * step)
    abi = mag * jnp.sin(li * step)
    den = lr * lr + li * li
    numr = abr - 1.0
    cr = (numr * lr + abi * li) / den
    ci = (abi * lr - numr * li) / den
    abr_ref[...] = abr
    abi_ref[...] = abi
    btr = btr_ref[...]
    bti = bti_ref[...]
    crb = cr[:, None, :]
    cib = ci[:, None, :]
    bbr_ref[...] = crb * btr - cib * bti
    bbi_ref[...] = crb * bti + cib * btr


def _s5prep(lam_re, lam_im, log_dt, bt_re, bt_im):
    g, n = lam_re.shape
    full2 = pl.BlockSpec((g, n), lambda: (0, 0))
    full3 = pl.BlockSpec((g, S5_GROUP, n), lambda: (0, 0, 0))
    return pl.pallas_call(
        _s5prep_kernel,
        in_specs=[full2, full2, pl.BlockSpec((g, 1), lambda: (0, 0)), full3, full3],
        out_specs=[full2, full2, full3, full3],
        out_shape=[jax.ShapeDtypeStruct((g, n), F32)] * 2
        + [jax.ShapeDtypeStruct((g, S5_GROUP, n), F32)] * 2,
        name="s5prep",
    )(lam_re, lam_im, log_dt, bt_re, bt_im)


W_ALIGN = 32
_SRC_ZA = 0
_SRC_XBC = SSD_WIDTH
_SRC_DT = _SRC_XBC + SSD_XBC
_SRC_UB = _SRC_DT + SSD_HEADS
_SRC_ZB = _SRC_UB + S5_WIDTH
_SRC_GA = _SRC_ZB + S5_WIDTH
_SRC_GB = _SRC_GA + D_MODEL
_SEGMENTS = ((_SRC_ZA, SSD_WIDTH), (_SRC_GA, D_MODEL), (_SRC_GB, D_MODEL), (_SRC_XBC, SSD_XBC),
             (_SRC_UB, S5_WIDTH), (_SRC_ZB, S5_WIDTH))
_SRC_TILES = tuple((start + k) // W_ALIGN for start, width in _SEGMENTS for k in range(0, width, PROJ_TN))


def _wprep_dt_kernel(w_ref, o_ref):
    o_ref[...] = jnp.zeros(o_ref.shape, o_ref.dtype)
    o_ref[0:SSD_HEADS, :] = w_ref[...].astype(BF16)


def _wprep_dt(wt):
    d = wt.shape[1]
    assert _SRC_DT % SSD_HEADS == 0
    return pl.pallas_call(
        _wprep_dt_kernel,
        grid=(1,),
        in_specs=[pl.BlockSpec((SSD_HEADS, d), lambda i: (_SRC_DT // SSD_HEADS, 0))],
        out_specs=pl.BlockSpec((LANES, d), lambda i: (0, 0)),
        out_shape=jax.ShapeDtypeStruct((LANES, d), BF16),
        name="wprep_dt",
    )(wt)


def _norm_rows(x_ref, nw_ref, wdt_ref, xn_ref, dt_ref):
    x = x_ref[...]
    ms = jnp.mean(x * x, axis=-1, keepdims=True)
    xn = ((x * lax.rsqrt(ms + NORM_EPS)) * nw_ref[...]).astype(BF16)
    xn_ref[...] = xn
    dt_ref[...] = lax.dot_general(xn, wdt_ref[...], _NT, preferred_element_type=F32)


def _wprep_inproj_kernel(tbl_ref, x_ref, nw_ref, w_ref, wdt_ref, wout_ref, proj_ref, dt_ref, xn_ref):
    del tbl_ref

    @pl.when(pl.program_id(0) == 0)
    def _():
        _norm_rows(x_ref, nw_ref, wdt_ref, xn_ref, dt_ref)

    wb = w_ref[...].astype(BF16)
    wout_ref[...] = wb
    proj_ref[...] = lax.dot_general(xn_ref[...], wb, _NT, preferred_element_type=F32)


def _wprep_inproj(wt, w_dt, x, norm_w):
    m, d = x.shape
    ntiles = len(_SRC_TILES)
    assert ntiles * PROJ_TN == PROJ_COLS
    return pl.pallas_call(
        _wprep_inproj_kernel,
        grid_spec=pltpu.PrefetchScalarGridSpec(
            num_scalar_prefetch=1,
            grid=(ntiles,),
            in_specs=[
                pl.BlockSpec((m, d), lambda j, tbl: (0, 0)),
                pl.BlockSpec((1, d), lambda j, tbl: (0, 0)),
                pl.BlockSpec((pl.Element(PROJ_TN), pl.Element(d)),
                             lambda j, tbl: (pl.multiple_of(tbl[j] * W_ALIGN, W_ALIGN), 0)),
                pl.BlockSpec((LANES, d), lambda j, tbl: (0, 0)),
            ],
            out_specs=[
                pl.BlockSpec((PROJ_TN, d), lambda j, tbl: (j, 0)),
                pl.BlockSpec((m, PROJ_TN), lambda j, tbl: (0, j)),
                pl.BlockSpec((m, LANES), lambda j, tbl: (0, 0)),
            ],
            scratch_shapes=[pltpu.VMEM((m, d), BF16)],
        ),
        out_shape=[jax.ShapeDtypeStruct((PROJ_COLS, d), BF16), jax.ShapeDtypeStruct((m, PROJ_COLS), F32),
                   jax.ShapeDtypeStruct((m, LANES), F32)],
        compiler_params=_params("arbitrary"),
        name="wprep_inproj",
    )(jnp.asarray(_SRC_TILES, jnp.int32), x, norm_w, wt, w_dt)


def _inproj_kernel(x_ref, nw_ref, w_ref, wdt_ref, proj_ref, dt_ref, xn_ref):
    @pl.when(pl.program_id(1) == 0)
    def _():
        _norm_rows(x_ref, nw_ref, wdt_ref, xn_ref, dt_ref)

    proj_ref[...] = lax.dot_general(xn_ref[...], w_ref[...], _NT, preferred_element_type=F32)


def _inproj(x, norm_w, w_main, w_dt, tm):
    m, d = x.shape
    n = w_main.shape[0]
    return pl.pallas_call(
        _inproj_kernel,
        grid=(m // tm, n // PROJ_TN),
        in_specs=[
            pl.BlockSpec((tm, d), lambda i, j: (i, 0)),
            pl.BlockSpec((1, d), lambda i, j: (0, 0)),
            pl.BlockSpec((PROJ_TN, d), lambda i, j: (j, 0)),
            pl.BlockSpec((LANES, d), lambda i, j: (0, 0)),
        ],
        out_specs=[
            pl.BlockSpec((tm, PROJ_TN), lambda i, j: (i, j)),
            pl.BlockSpec((tm, LANES), lambda i, j: (i, 0)),
        ],
        out_shape=[jax.ShapeDtypeStruct((m, n), F32), jax.ShapeDtypeStruct((m, LANES), F32)],
        scratch_shapes=[pltpu.VMEM((tm, d), BF16)],
        compiler_params=_params("arbitrary", "arbitrary"),
        name="inproj",
    )(x, norm_w, w_main, w_dt)


def _ssd_chunk(r0, xbc_ref, za_ref, dtr_ref, convw_ref, convb_ref, dtb_ref, alog_ref, dexp_ref, nw_ref,
               y_ref, ht_scr, ext_scr, mask_rows, need_y):
    L = CHUNK
    P2 = 2 * SSD_HEAD_DIM
    GW = SSD_WIDTH // SSD_GROUPS
    time_rows = lambda t0, n: pl.ds(CONV_PITCH * t0, n, stride=CONV_PITCH)

    w = convw_ref[...]
    bias = convb_ref[...]
    conv_parts = []
    for s in range(SSD_XBC // LANES):
        ls = slice(s * LANES, (s + 1) * LANES)
        xc = xbc_ref[r0:r0 + L, ls]
        ext_scr[s, time_rows(SUBLANES, L), :] = xc
        acc = bias[:, ls]
        for k in range(SSD_CONV - 1):
            acc = acc + ext_scr[s, time_rows(SUBLANES - (SSD_CONV - 1) + k, L), :] * w[k:k + 1, ls]
        conv_parts.append(acc + xc * w[SSD_CONV - 1:SSD_CONV, ls])
        ext_scr[s, time_rows(0, SUBLANES), :] = xc[L - SUBLANES:L, :]
    xbc = _silu(jnp.concatenate(conv_parts, axis=1))
    xs = xbc[:, :SSD_WIDTH]
    bmat = xbc[:, SSD_WIDTH:SSD_WIDTH + SSD_GROUPS * SSD_STATE]
    cmat = xbc[:, SSD_WIDTH + SSD_GROUPS * SSD_STATE:]

    rows = lax.broadcasted_iota(jnp.int32, (L, L), 0)
    cols = lax.broadcasted_iota(jnp.int32, (L, L), 1)
    causal = rows >= cols
    lane_lo = cols < SSD_HEAD_DIM

    dt = _softplus(dtr_ref[r0:r0 + L, :] + dtb_ref[...])
    if mask_rows:
        dt = jnp.where(rows < mask_rows, 0.0, dt)
    a = dt * (-jnp.exp(alog_ref[...]))
    tril = jnp.where(causal, 1.0, 0.0).astype(BF16)
    a1, a2, a3 = _split3(a)
    acum = (jnp.dot(tril, a1, preferred_element_type=F32)
            + jnp.dot(tril, a2, preferred_element_type=F32)
            + jnp.dot(tril, a3, preferred_element_type=F32))
    a2 = acum * LOG2E
    e_cum = jnp.exp2(a2)
    w_end = dt * jnp.exp2(a2[L - 1:L, :] - a2)
    a2dt_t = (a2 - jnp.log2(dt)).T

    dexp = dexp_ref[...]
    xs_b = xs.astype(BF16)
    y_parts = []
    for g in range(SSD_GROUPS):
        bg = bmat[:, g * SSD_STATE:(g + 1) * SSD_STATE].astype(BF16)
        cg = cmat[:, g * SSD_STATE:(g + 1) * SSD_STATE].astype(BF16)
        ht_g = ht_scr[:, g * GW:(g + 1) * GW]
        if need_y:
            cb = lax.dot_general(cg, bg, (((1,), (1,)), ((), ())), preferred_element_type=F32)
            y_off = jnp.dot(cg, ht_g.astype(BF16), preferred_element_type=F32)
        xw_parts = []
        elast_parts = []
        for jj in range(GW // P2):
            lo = g * GW + jj * P2
            h0 = lo // SSD_HEAD_DIM
            yd, eb, wb = [], [], []
            for h in (h0, h0 + 1):
                if need_y:
                    colb = jnp.broadcast_to(a2[:, h:h + 1], (L, L))
                    m = cb * jnp.exp2(jnp.where(causal, colb - a2dt_t[h:h + 1, :], -jnp.inf))
                    yd.append(jnp.dot(m.astype(BF16), xs_b[:, lo:lo + P2], preferred_element_type=F32))
                eb.append(jnp.broadcast_to(e_cum[:, h:h + 1], (L, L)))
                wb.append(jnp.broadcast_to(w_end[:, h:h + 1], (L, L)))
            xs_pair = xs[:, lo:lo + P2]
            if need_y:
                y_pair = (jnp.where(lane_lo, yd[0], yd[1])
                          + y_off[:, jj * P2:(jj + 1) * P2] * jnp.where(lane_lo, eb[0], eb[1]))
                y_parts.append(y_pair + dexp[:, lo:lo + P2] * xs_pair)
            xw_parts.append(xs_pair * jnp.where(lane_lo, wb[0], wb[1]))
            elast_parts.append(jnp.where(lane_lo[0:1, :], eb[0][L - 1:L, :], eb[1][L - 1:L, :]))
        xw = jnp.concatenate(xw_parts, axis=1)
        elast = jnp.concatenate(elast_parts, axis=1)
        st = lax.dot_general(bg, xw.astype(BF16), (((0,), (0,)), ((), ())),
                             preferred_element_type=F32)
        ht_scr[:, g * GW:(g + 1) * GW] = ht_g * elast + st

    if not need_y:
        y_ref[r0:r0 + L, :] = jnp.zeros((L, SSD_WIDTH), y_ref.dtype)
        return
    y = jnp.concatenate(y_parts, axis=1)
    y = y * _silu(za_ref[r0:r0 + L, :])
    nw = nw_ref[...]
    outs = []
    for g in range(SSD_GROUPS):
        yg = y[:, g * GW:(g + 1) * GW]
        ms = jnp.mean(yg * yg, axis=-1, keepdims=True)
        outs.append((yg * lax.rsqrt(ms + NORM_EPS)) * nw[:, g * GW:(g + 1) * GW])
    y_ref[r0:r0 + L, :] = jnp.concatenate(outs, axis=1).astype(y_ref.dtype)


def _ssd_kernel(xbc_ref, za_ref, dtr_ref, ht0_ref, tail0_ref, convw_ref, convb_ref, dtb_ref, alog_ref,
                dexp_ref, nw_ref, y_ref, h_ref, ht_ref, tail_ref, ht_scr, ext_scr, *, mask_rows, need_y):
    c = pl.program_id(1)
    nrows = xbc_ref.shape[0]

    @pl.when(c == 0)
    def _():
        ht_scr[...] = ht0_ref[...]
        for s in range(SSD_XBC // LANES):
            ext_scr[s, pl.ds(0, SUBLANES, stride=CONV_PITCH), :] = tail0_ref[:, s * LANES:(s + 1) * LANES]

    for r0 in range(0, nrows, CHUNK):
        _ssd_chunk(r0, xbc_ref, za_ref, dtr_ref, convw_ref, convb_ref, dtb_ref, alog_ref, dexp_ref, nw_ref,
                   y_ref, ht_scr, ext_scr, mask_rows, need_y)

    @pl.when(c == pl.num_programs(1) - 1)
    def _():
        ht = ht_scr[...]
        ht_ref[...] = ht
        h_ref[...] = ht.T
        tail_ref[...] = xbc_ref[nrows - SUBLANES:nrows, :]


def _ssd(proj, dt_raw, ht0, tail0, conv_w, conv_b, dt_bias, a_log, d_exp, norm_w, *, batch, nchunks,
         row_block0, mask_rows, need_y=True):
    L = CHUNK
    rows = batch * nchunks * L
    rb = lambda b, c: row_block0 + b * nchunks + c
    const2 = lambda shape: pl.BlockSpec(shape, lambda b, c: (0, 0))
    return pl.pallas_call(
        functools.partial(_ssd_kernel, mask_rows=mask_rows, need_y=need_y),
        grid=(batch, nchunks),
        in_specs=[
            pl.BlockSpec((L, SSD_XBC), lambda b, c: (rb(b, c), COL_XBC // SSD_XBC)),
            pl.BlockSpec((L, SSD_WIDTH), lambda b, c: (rb(b, c), COL_ZA // SSD_WIDTH)),
            pl.BlockSpec((L, LANES), lambda b, c: (rb(b, c), 0)),
            const2((SSD_STATE, SSD_WIDTH)),
            const2((SUBLANES, SSD_XBC)),
            const2((SSD_CONV, SSD_XBC)),
            const2((1, SSD_XBC)),
            const2((1, LANES)),
            const2((1, LANES)),
            const2((1, SSD_WIDTH)),
            const2((1, SSD_WIDTH)),
        ],
        out_specs=[
            pl.BlockSpec((L, SSD_WIDTH), lambda b, c: (b * nchunks + c, 0)),
            pl.BlockSpec((None, SSD_WIDTH, SSD_STATE), lambda b, c: (b, 0, 0)),
            pl.BlockSpec((None, SSD_STATE, SSD_WIDTH), lambda b, c: (b, 0, 0)),
            pl.BlockSpec((None, SUBLANES, SSD_XBC), lambda b, c: (b, 0, 0)),
        ],
        out_shape=[
            jax.ShapeDtypeStruct((rows, SSD_WIDTH), BF16),
            jax.ShapeDtypeStruct((batch, SSD_WIDTH, SSD_STATE), F32),
            jax.ShapeDtypeStruct((batch, SSD_STATE, SSD_WIDTH), F32),
            jax.ShapeDtypeStruct((batch, SUBLANES, SSD_XBC), F32),
        ],
        scratch_shapes=[pltpu.VMEM((SSD_STATE, SSD_WIDTH), F32),
                        pltpu.VMEM((SSD_XBC // LANES, CONV_PITCH * (SUBLANES + CHUNK), LANES), F32)],
        compiler_params=_params("arbitrary", "arbitrary"),
        name="ssd",
    )(proj, proj, dt_raw, ht0, tail0, conv_w, conv_b, dt_bias, a_log, d_exp, norm_w)


def _ssd_step_kernel(xbc_ref, za_ref, dtr_ref, cs_ref, h_ref, convw_ref, convb_ref, dtb_ref, alog_ref,
                     dexp_ref, nw_ref, y_ref, hout_ref, csout_ref):
    R = SUBLANES
    GW = SSD_WIDTH // SSD_GROUPS
    x = xbc_ref[...]
    w = convw_ref[...]
    s0 = cs_ref[0]
    s1 = cs_ref[1]
    s2 = cs_ref[2]
    conv = convb_ref[...] + s0 * w[0:1]
    conv = conv + s1 * w[1:2]
    conv = conv + s2 * w[2:3]
    conv = conv + x * w[3:4]
    csout_ref[0] = s1
    csout_ref[1] = s2
    csout_ref[2] = x
    xbc = _silu(conv)
    xs = xbc[:, :SSD_WIDTH]
    bmat = xbc[:, SSD_WIDTH:SSD_WIDTH + SSD_GROUPS * SSD_STATE]
    cmat = xbc[:, SSD_WIDTH + SSD_GROUPS * SSD_STATE:]

    dt = _softplus(dtr_ref[...] + dtb_ref[...])
    da = jnp.exp(dt * (-jnp.exp(alog_ref[...])))
    dt_t = dt.T
    da_t = da.T
    expand = lambda v: jnp.concatenate(
        [jnp.broadcast_to(v[h:h + 1, :], (SSD_HEAD_DIM, R)) for h in range(SSD_HEADS)], axis=0)
    xd_t = xs.T * expand(dt_t)
    da_te = expand(da_t)

    cmat_b = cmat.astype(BF16)
    yrows = []
    for i in range(R):
        bexp = jnp.concatenate(
            [jnp.broadcast_to(bmat[i:i + 1, g * SSD_STATE:(g + 1) * SSD_STATE], (GW, SSD_STATE))
             for g in range(SSD_GROUPS)], axis=0)
        hn = h_ref[i] * da_te[:, i:i + 1] + xd_t[:, i:i + 1] * bexp
        hout_ref[i] = hn
        cg = jnp.concatenate([cmat_b[i:i + 1, g * SSD_STATE:(g + 1) * SSD_STATE] for g in range(SSD_GROUPS)]
                             + [jnp.zeros((R - SSD_GROUPS, SSD_STATE), BF16)], axis=0)
        yg = lax.dot_general(cg, hn.astype(BF16), (((1,), (1,)), ((), ())), preferred_element_type=F32)
        yrows.append(jnp.concatenate([yg[g:g + 1, g * GW:(g + 1) * GW] for g in range(SSD_GROUPS)], axis=1))
    y = jnp.concatenate(yrows, axis=0)
    y = y + dexp_ref[...] * xs
    y = y * _silu(za_ref[...])
    nw = nw_ref[...]
    outs = []
    for g in range(SSD_GROUPS):
        yg = y[:, g * GW:(g + 1) * GW]
        ms = jnp.mean(yg * yg, axis=-1, keepdims=True)
        outs.append((yg * lax.rsqrt(ms + NORM_EPS)) * nw[:, g * GW:(g + 1) * GW])
    y_ref[...] = jnp.concatenate(outs, axis=1).astype(y_ref.dtype)


def _ssd_step(proj, dt_raw, conv_state, ssd_state, conv_w, conv_b, dt_bias, a_log, d_exp, norm_w, *, nseq):
    R = SUBLANES
    const2 = lambda shape: pl.BlockSpec(shape, lambda i: (0, 0))
    return pl.pallas_call(
        _ssd_step_kernel,
        grid=(nseq // R,),
        in_specs=[
            pl.BlockSpec((R, SSD_XBC), lambda i: (i, COL_XBC // SSD_XBC)),
            pl.BlockSpec((R, SSD_WIDTH), lambda i: (i, COL_ZA // SSD_WIDTH)),
            pl.BlockSpec((R, LANES), lambda i: (i, 0)),
            pl.BlockSpec((SSD_CONV - 1, R, SSD_XBC), lambda i: (0, i, 0)),
            pl.BlockSpec((R, SSD_WIDTH, SSD_STATE), lambda i: (i, 0, 0)),
            const2((SSD_CONV, SSD_XBC)),
            const2((1, SSD_XBC)),
            const2((1, LANES)),
            const2((1, LANES)),
            const2((1, SSD_WIDTH)),
            const2((1, SSD_WIDTH)),
        ],
        out_specs=[
            pl.BlockSpec((R, SSD_WIDTH), lambda i: (i, 0)),
            pl.BlockSpec((R, SSD_WIDTH, SSD_STATE), lambda i: (i, 0, 0)),
            pl.BlockSpec((SSD_CONV - 1, R, SSD_XBC), lambda i: (0, i, 0)),
        ],
        out_shape=[
            jax.ShapeDtypeStruct((nseq, SSD_WIDTH), BF16),
            jax.ShapeDtypeStruct((nseq, SSD_WIDTH, SSD_STATE), F32),
            jax.ShapeDtypeStruct((SSD_CONV - 1, nseq, SSD_XBC), F32),
        ],
        compiler_params=_params("arbitrary"),
        name="ssd_step",
    )(proj, proj, dt_raw, conv_state, ssd_state, conv_w, conv_b, dt_bias, a_log, d_exp, norm_w)


def _s5_block_diag(w, rows_per_group, cols_per_group):
    ngrp, r, lanes = w.shape
    width = ngrp * cols_per_group
    tiled = jnp.concatenate([w.reshape(ngrp * r, lanes)] * (width // lanes), axis=1)
    row_g = lax.broadcasted_iota(jnp.int32, tiled.shape, 0) // rows_per_group
    col_g = lax.broadcasted_iota(jnp.int32, tiled.shape, 1) // cols_per_group
    return jnp.where(row_g == col_g, tiled, 0.0)


def _s5_weights(bbr_ref, bbi_ref, cr_ref, ci_ref):
    return tuple(_s5_block_diag(r[...], S5_GROUP, S5_STATE) for r in (bbr_ref, bbi_ref, cr_ref, ci_ref))


def _s5_pitch(seg_len):
    return seg_len if (seg_len // SUBLANES) % 2 else seg_len + SUBLANES


def _s5_kernel(*refs, seq, nconv, need_y):
    (u_ref, h0r_ref, h0i_ref, bbr_ref, bbi_ref, ctr_ref, cti_ref, ar_ref, ai_ref, d_ref) = refs[:10]
    w_in_refs = refs[10:10 + nconv]
    y_ref, hr_ref, hi_ref = refs[10 + nconv:13 + nconv]
    w_out_refs = refs[13 + nconv:13 + 2 * nconv]
    pad, pe, po, cr, ci, hb_in, hb_out, w2r_scr, w2i_scr, cta_scr, ctb_scr, k0_scr = refs[13 + 2 * nconv:]
    for wi, wo in zip(w_in_refs, w_out_refs):
        wo[...] = wi[...].astype(BF16)

    NS = SUBLANES
    SL = seq // NS
    H = SL // 2
    pitch = _s5_pitch(SL)
    nk = S5_BLK_U // LANES
    n = S5_BLK_N
    a_re = ar_ref[...]
    a_im = ai_ref[...]
    bf = lambda v: v.astype(BF16)

    @pl.when(pl.program_id(1) == 0)
    def _():
        bt_re, bt_im, ct_re, ct_im = _s5_weights(bbr_ref, bbi_ref, ctr_ref, cti_ref)
        w2r_scr[...] = jnp.concatenate([bf(bt_re * a_re - bt_im * a_im), bf(bt_re)], axis=0)
        w2i_scr[...] = jnp.concatenate([bf(bt_re * a_im + bt_im * a_re), bf(bt_im)], axis=0)
        if need_y:
            cta_scr[0] = bf(ct_re * a_re - ct_im * a_im)
            cta_scr[1] = bf(ct_re * a_im + ct_im * a_re)
            ctb_scr[0] = bf(ct_re)
            ctb_scr[1] = bf(ct_im)
            k0_scr[...] = bf(lax.dot_general(bf(bt_re), bf(ct_re), _NT, preferred_element_type=F32)
                             - lax.dot_general(bf(bt_im), bf(ct_im), _NT, preferred_element_type=F32))

    for s in range(NS):
        for k in range(nk):
            pad[k, s * pitch:s * pitch + SL, :] = u_ref[s * SL:(s + 1) * SL, k * LANES:(k + 1) * LANES]

    for jp in range(H):
        for k in range(nk):
            pe[jp * NS:(jp + 1) * NS, k * LANES:(k + 1) * LANES] = pad[k, pl.ds(2 * jp, NS, stride=pitch), :]
            po[jp * NS:(jp + 1) * NS, k * LANES:(k + 1) * LANES] = pad[k, pl.ds(2 * jp + 1, NS, stride=pitch), :]
    ue_b = bf(pe[...])
    uo_b = bf(po[...])
    qc = 4 * S5_GROUP
    qn = 4 * S5_STATE
    for q in range(S5_BLK_U // qc):
        u_q = jnp.concatenate([ue_b[:, q * qc:(q + 1) * qc], uo_b[:, q * qc:(q + 1) * qc]], axis=1)
        rows_q = [slice(q * qc, (q + 1) * qc), slice(S5_BLK_U + q * qc, S5_BLK_U + (q + 1) * qc)]
        cols_q = slice(q * qn, (q + 1) * qn)
        wq_re = jnp.concatenate([w2r_scr[r, cols_q] for r in rows_q], axis=0)
        wq_im = jnp.concatenate([w2i_scr[r, cols_q] for r in rows_q], axis=0)
        cr[:, cols_q] = jnp.dot(u_q, wq_re, preferred_element_type=F32)
        ci[:, cols_q] = jnp.dot(u_q, wq_im, preferred_element_type=F32)

    a2_re = a_re * a_re - a_im * a_im
    a2_im = 2.0 * (a_re * a_im)
    ar = jnp.broadcast_to(a2_re, (NS, n))
    ai = jnp.broadcast_to(a2_im, (NS, n))

    def step(carry, r0):
        hr, hi = carry
        nr = ar * hr - ai * hi + cr[pl.ds(r0, NS), :]
        ni = ar * hi + ai * hr + ci[pl.ds(r0, NS), :]
        return nr, ni

    def scan_ends(j, carry):
        return step(carry, pl.multiple_of(j * NS, NS))

    zero = jnp.zeros((NS, n), F32)
    er, ei = lax.fori_loop(0, H, scan_ends, (zero, zero), unroll=4)

    asr, asi = a_re, a_im
    for _ in range(SL.bit_length() - 1):
        asr, asi = asr * asr - asi * asi, 2.0 * (asr * asi)
    gr = [h0r_ref[...]]
    gi = [h0i_ref[...]]
    for s in range(NS):
        gr.append(er[s:s + 1, :] + (asr * gr[s] - asi * gi[s]))
        gi.append(ei[s:s + 1, :] + (asr * gi[s] + asi * gr[s]))
    hr_ref[...] = gr[NS]
    hi_ref[...] = gi[NS]
    if not need_y:
        y_ref[...] = jnp.zeros(y_ref.shape, y_ref.dtype)
        return

    def scan_store(jp, c0):
        r0 = pl.multiple_of(jp * 2 * NS, 2 * NS)
        c1 = step(c0, r0)
        c2 = step(c1, r0 + NS)
        hb_in[pl.ds(r0, 2 * NS), 0:n] = bf(jnp.concatenate([c0[0], c1[0]], axis=0))
        hb_in[pl.ds(r0, 2 * NS), n:2 * n] = bf(jnp.concatenate([c0[1], c1[1]], axis=0))
        hb_out[pl.ds(r0, 2 * NS), 0:n] = bf(jnp.concatenate([c1[0], c2[0]], axis=0))
        hb_out[pl.ds(r0, 2 * NS), n:2 * n] = bf(jnp.concatenate([c1[1], c2[1]], axis=0))
        return c2

    lax.fori_loop(0, H // 2, scan_store,
                  (jnp.concatenate(gr[:NS], axis=0), jnp.concatenate(gi[:NS], axis=0)), unroll=2)

    half = seq // 2
    rb = half // 4
    d = d_ref[...]
    cta_re, cta_im = cta_scr[0], cta_scr[1]
    ctb_re, ctb_im = ctb_scr[0], ctb_scr[1]
    k0 = k0_scr[...]
    for r in range(0, half, rb):
        rows = slice(r, r + rb)
        y_odd = (lax.dot_general(hb_out[rows, 0:n], ctb_re, _NT, preferred_element_type=F32)
                 - lax.dot_general(hb_out[rows, n:2 * n], ctb_im, _NT, preferred_element_type=F32))
        y_even = (lax.dot_general(hb_in[rows, 0:n], cta_re, _NT, preferred_element_type=F32)
                  - lax.dot_general(hb_in[rows, n:2 * n], cta_im, _NT, preferred_element_type=F32)
                  + jnp.dot(bf(pe[rows, :]), k0, preferred_element_type=F32))
        po[rows, :] = _gelu_tanh(y_odd + d * po[rows, :])
        pe[rows, :] = _gelu_tanh(y_even + d * pe[rows, :])

    for jp in range(H):
        for k in range(nk):
            pad[k, pl.ds(2 * jp, NS, stride=pitch), :] = pe[jp * NS:(jp + 1) * NS, k * LANES:(k + 1) * LANES]
            pad[k, pl.ds(2 * jp + 1, NS, stride=pitch), :] = po[jp * NS:(jp + 1) * NS, k * LANES:(k + 1) * LANES]
    for s in range(NS):
        for k in range(nk):
            y_ref[s * SL:(s + 1) * SL, k * LANES:(k + 1) * LANES] = pad[k, s * pitch:s * pitch + SL, :]


def _s5(proj, h0r, h0i, bb_re, bb_im, ct_re, ct_im, ab_re, ab_im, d_s5, *, batch, seq, row_block0, need_y=True,
        convert=()):
    ub0 = COL_UB // S5_BLK_U
    nsteps = batch * S5_BLOCKS
    slice_spec = lambda w: pl.BlockSpec((w.shape[0] // nsteps, w.shape[1]), lambda j, b: (j * batch + b, 0))
    assert all(w.shape[0] % (2 * SUBLANES * nsteps) == 0 for w in convert)
    seg = seq // SUBLANES
    assert seg & (seg - 1) == 0
    gpb = S5_GROUPS // S5_BLOCKS
    vec = lambda width: pl.BlockSpec((1, width), lambda j, b: (0, j))
    wspec = lambda r: pl.BlockSpec((gpb, r, LANES), lambda j, b: (j, 0, 0))
    st_out = pl.BlockSpec((None, 1, S5_BLK_N), lambda j, b: (b, 0, j))
    return pl.pallas_call(
        functools.partial(_s5_kernel, seq=seq, nconv=len(convert), need_y=need_y),
        grid=(S5_BLOCKS, batch),
        in_specs=[
            pl.BlockSpec((seq, S5_BLK_U), lambda j, b: (row_block0 + b, ub0 + j)),
            vec(S5_BLK_N), vec(S5_BLK_N),
            wspec(S5_GROUP), wspec(S5_GROUP), wspec(S5_GROUP), wspec(S5_GROUP),
            vec(S5_BLK_N), vec(S5_BLK_N), vec(S5_BLK_U),
        ] + [slice_spec(w) for w in convert],
        out_specs=[pl.BlockSpec((seq, S5_BLK_U), lambda j, b: (b, j)), st_out, st_out]
        + [slice_spec(w) for w in convert],
        out_shape=[
            jax.ShapeDtypeStruct((batch * seq, S5_WIDTH), F32),
            jax.ShapeDtypeStruct((batch, 1, S5_NSTATE), F32),
            jax.ShapeDtypeStruct((batch, 1, S5_NSTATE), F32),
        ] + [jax.ShapeDtypeStruct(w.shape, BF16) for w in convert],
        scratch_shapes=[
            pltpu.VMEM((S5_BLK_U // LANES, SUBLANES * _s5_pitch(seg), LANES), F32),
            pltpu.VMEM((seq // 2, S5_BLK_U), F32),
            pltpu.VMEM((seq // 2, S5_BLK_U), F32),
            pltpu.VMEM((seq // 2, S5_BLK_N), F32),
            pltpu.VMEM((seq // 2, S5_BLK_N), F32),
            pltpu.VMEM((seq // 2, 2 * S5_BLK_N), BF16),
            pltpu.VMEM((seq // 2, 2 * S5_BLK_N), BF16),
            pltpu.VMEM((2 * S5_BLK_U, S5_BLK_N), BF16),
            pltpu.VMEM((2 * S5_BLK_U, S5_BLK_N), BF16),
            pltpu.VMEM((2, S5_BLK_U, S5_BLK_N), BF16),
            pltpu.VMEM((2, S5_BLK_U, S5_BLK_N), BF16),
            pltpu.VMEM((S5_BLK_U, S5_BLK_U), BF16),
        ],
        compiler_params=_params("arbitrary", "arbitrary"),
        name="s5",
    )(proj, h0r, h0i, bb_re, bb_im, ct_re, ct_im, ab_re, ab_im, d_s5, *convert)


def _s5_step_kernel(u_ref, h0r_ref, h0i_ref, bbr_ref, bbi_ref, ctr_ref, cti_ref, ar_ref, ai_ref, d_ref,
                    y_ref, hr_ref, hi_ref):
    bt_re, bt_im, ct_re, ct_im = (w.astype(BF16) for w in _s5_weights(bbr_ref, bbi_ref, ctr_ref, cti_ref))
    u = u_ref[...]
    ub = u.astype(BF16)
    ar = ar_ref[...]
    ai = ai_ref[...]
    h0r = h0r_ref[...].T
    h0i = h0i_ref[...].T
    hr = jnp.dot(ub, bt_re, preferred_element_type=F32) + (ar * h0r - ai * h0i)
    hi = jnp.dot(ub, bt_im, preferred_element_type=F32) + (ar * h0i + ai * h0r)
    hr_ref[...] = hr.T
    hi_ref[...] = hi.T
    y = (lax.dot_general(hr.astype(BF16), ct_re, _NT, preferred_element_type=F32)
         - lax.dot_general(hi.astype(BF16), ct_im, _NT, preferred_element_type=F32))
    y_ref[...] = _gelu_tanh(y + d_ref[...] * u)


def _s5_step(proj, h0r, h0i, bb_re, bb_im, ct_re, ct_im, ab_re, ab_im, d_s5, *, nseq):
    ub0 = COL_UB // S5_BLK_U
    gpb = S5_GROUPS // S5_BLOCKS
    vec = lambda width: pl.BlockSpec((1, width), lambda j: (0, j))
    wspec = lambda r: pl.BlockSpec((gpb, r, LANES), lambda j: (j, 0, 0))
    st = pl.BlockSpec((S5_BLK_N, nseq), lambda j: (j, 0))
    return pl.pallas_call(
        _s5_step_kernel,
        grid=(S5_BLOCKS,),
        in_specs=[
            pl.BlockSpec((nseq, S5_BLK_U), lambda j: (0, ub0 + j)),
            st, st,
            wspec(S5_GROUP), wspec(S5_GROUP), wspec(S5_GROUP), wspec(S5_GROUP),
            vec(S5_BLK_N), vec(S5_BLK_N), vec(S5_BLK_U),
        ],
        out_specs=[pl.BlockSpec((nseq, S5_BLK_U), lambda j: (0, j)), st, st],
        out_shape=[
            jax.ShapeDtypeStruct((nseq, S5_WIDTH), F32),
            jax.ShapeDtypeStruct((S5_NSTATE, nseq), F32),
            jax.ShapeDtypeStruct((S5_NSTATE, nseq), F32),
        ],
        compiler_params=_params("arbitrary"),
        name="s5_step",
    )(proj, h0r, h0i, bb_re, bb_im, ct_re, ct_im, ab_re, ab_im, d_s5)


def _tail_kernel(yn_ref, ybg_ref, zb_ref, ga_ref, gb_ref, x_ref, wpa_ref, wglu_ref, bglu_ref, wpb_ref,
                 wout_ref, fnw_ref, out_ref):
    sg_a = _sigmoid(ga_ref[...])
    sg_b = _sigmoid(gb_ref[...])
    sz_b = _silu(zb_ref[...])
    yb = ybg_ref[...]
    glu = jnp.dot(yb.astype(BF16), wglu_ref[...], preferred_element_type=F32) + bglu_ref[...]
    ya = jnp.dot(yn_ref[...], wpa_ref[...], preferred_element_type=F32)
    yb = (yb * _sigmoid(glu)) * sz_b
    ybp = jnp.dot(yb.astype(BF16), wpb_ref[...], preferred_element_type=F32)
    mixed = sg_a * ya + sg_b * ybp
    o = x_ref[...] + jnp.dot(mixed.astype(BF16), wout_ref[...], preferred_element_type=F32)
    ms = jnp.mean(o * o, axis=-1, keepdims=True)
    out_ref[...] = (o * lax.rsqrt(ms + NORM_EPS)) * fnw_ref[...]


def _tail(yn, ybg, proj, x, w_proj_a, w_glu, b_glu, w_proj_b, w_out, final_norm_w, *, tm):
    m, d = x.shape
    resident = lambda shape: pl.BlockSpec(shape, lambda i: (0, 0), pipeline_mode=pl.Buffered(1))
    return pl.pallas_call(
        _tail_kernel,
        grid=(m // tm,),
        in_specs=[
            pl.BlockSpec((tm, SSD_WIDTH), lambda i: (i, 0)),
            pl.BlockSpec((tm, S5_WIDTH), lambda i: (i, 0)),
            pl.BlockSpec((tm, S5_WIDTH), lambda i: (i, COL_ZB // S5_WIDTH)),
            pl.BlockSpec((tm, d), lambda i: (i, COL_GA // D_MODEL)),
            pl.BlockSpec((tm, d), lambda i: (i, COL_GB // D_MODEL)),
            pl.BlockSpec((tm, d), lambda i: (i, 0)),
            resident((SSD_WIDTH, d)),
            resident((S5_WIDTH, S5_WIDTH)),
            resident((1, S5_WIDTH)),
            resident((S5_WIDTH, d)),
            resident((d, d)),
            resident((1, d)),
        ],
        out_specs=pl.BlockSpec((tm, d), lambda i: (i, 0)),
        out_shape=jax.ShapeDtypeStruct((m, d), F32),
        compiler_params=_params("arbitrary"),
        name="tail",
    )(yn, ybg, proj, proj, proj, x, w_proj_a, w_glu, b_glu, w_proj_b, w_out, final_norm_w)


def kernel(x_prompt, x_sample, state_ssd, state_conv, state_s5_re, state_s5_im, meta_tokens, norm_w, w_in,
           conv_w, conv_b, dt_bias, a_log, d_ssd, ssd_norm_w, w_proj_a, lam_re, lam_im, log_dt_s5, b_re, b_im,
           c_re, c_im, d_s5, w_glu, b_glu, w_proj_b, w_out, final_norm_w):
    bsz, seq, d = x_prompt.shape
    nseq = x_sample.shape[0]
    assert d == D_MODEL and seq % CHUNK == 0 and nseq % SUBLANES == 0 and norm_w.shape[0] == 1
    assert meta_tokens.shape[0] == N_META and N_META <= CHUNK

    assert w_in.shape[2] == _SRC_GB + D_MODEL
    wt = jnp.transpose(w_in[0])
    w_dt = _wprep_dt(wt)
    pad_heads = lambda v: jnp.pad(v.reshape(1, SSD_HEADS), ((0, 0), (0, LANES - SSD_HEADS)))
    dtb = pad_heads(dt_bias[0])
    alog = pad_heads(a_log[0])
    d_exp = jnp.repeat(d_ssd[0], SSD_HEAD_DIM).reshape(1, SSD_WIDTH)
    nw1 = norm_w[0].reshape(1, d)
    ssd_nw = ssd_norm_w[0].reshape(1, SSD_WIDTH)
    convw = conv_w[0]
    convb = conv_b[0].reshape(1, SSD_XBC)
    bglu = b_glu[0].reshape(1, S5_WIDTH)
    fnw = final_norm_w.reshape(1, d)
    ds5 = d_s5[0].reshape(1, S5_WIDTH)

    rep = lambda v, k: jnp.concatenate([v] * k, axis=-1)
    lane_rep = LANES // S5_STATE
    ab_re, ab_im, bb_re, bb_im = _s5prep(
        rep(lam_re[0], lane_rep), rep(lam_im[0], lane_rep), log_dt_s5[0].reshape(S5_GROUPS, 1),
        rep(jnp.transpose(b_re[0], (0, 2, 1)), lane_rep), rep(jnp.transpose(b_im[0], (0, 2, 1)), lane_rep))
    ab_re = ab_re[:, :S5_STATE].reshape(1, S5_NSTATE)
    ab_im = ab_im[:, :S5_STATE].reshape(1, S5_NSTATE)
    ct_re = rep(c_re[0], lane_rep)
    ct_im = rep(c_im[0], lane_rep)

    x_main = x_prompt.reshape(bsz * seq, d)
    x_small = jnp.concatenate(
        [x_sample.reshape(nseq, d), jnp.zeros((CHUNK - N_META, d), x_prompt.dtype),
         meta_tokens.astype(x_prompt.dtype)], axis=0)
    assert nseq % CHUNK == 0
    meta_blk = nseq // CHUNK
    tm_main = 1024 if (bsz * seq) % 1024 == 0 else CHUNK
    w_main, proj_s, dt_s = _wprep_inproj(wt, w_dt, x_small, nw1)
    proj_m, dt_m = _inproj(x_main, nw1, w_main, w_dt, tm=tm_main)

    ssd_args = (convw, convb, dtb, alog, d_exp, ssd_nw)
    s5_w = (bb_re, bb_im, ct_re, ct_im)
    s5_v = (ab_re, ab_im, ds5)

    zeros_ht = jnp.zeros((SSD_STATE, SSD_WIDTH), F32)
    zeros_tail = jnp.zeros((SUBLANES, SSD_XBC), F32)
    zeros_s5 = jnp.zeros((1, S5_NSTATE), F32)
    _, _, ht_meta, tail_meta = _ssd(proj_s, dt_s, zeros_ht, zeros_tail, *ssd_args, batch=1, nchunks=1,
                                    row_block0=meta_blk, mask_rows=CHUNK - N_META, need_y=False)
    _, s5r_meta, s5i_meta = _s5(proj_s, zeros_s5, zeros_s5, *s5_w, *s5_v, batch=1, seq=CHUNK,
                                row_block0=meta_blk, need_y=False)[:3]

    yn_m, h_m, _, tail_m = _ssd(proj_m, dt_m, ht_meta[0], tail_meta[0], *ssd_args, batch=bsz,
                                nchunks=seq // CHUNK, row_block0=0, mask_rows=0)
    ybg_m, s5r_m, s5i_m, wpa, wglu, wpb, wout = _s5(
        proj_m, s5r_meta[0], s5i_meta[0], *s5_w, *s5_v, batch=bsz, seq=seq, row_block0=0,
        convert=(w_proj_a[0], w_glu[0], w_proj_b[0], w_out[0]))
    tail_w = (wpa, wglu, bglu, wpb, wout, fnw)
    y_prompt = _tail(yn_m, ybg_m, proj_m, x_main, *tail_w, tm=256)

    yn_s, h_s, cs_s = _ssd_step(proj_s, dt_s, jnp.transpose(state_conv[0], (1, 0, 2)),
                                state_ssd[0].reshape(nseq, SSD_WIDTH, SSD_STATE), *ssd_args, nseq=nseq)
    seq_minor = lambda v: jnp.transpose(v, (1, 2, 0)).reshape(S5_NSTATE, nseq)
    seq_major = lambda v: jnp.transpose(v.reshape(S5_GROUPS, S5_STATE, nseq), (2, 0, 1))[None]
    ybg_s, s5r_s, s5i_s = _s5_step(proj_s, seq_minor(state_s5_re[0]), seq_minor(state_s5_im[0]),
                                   *s5_w, *s5_v, nseq=nseq)
    y_sample = _tail(yn_s, ybg_s, proj_s, x_sample.reshape(nseq, d), *tail_w, tm=nseq)

    dt_out = x_prompt.dtype
    return (
        y_prompt.reshape(bsz, seq, d),
        y_sample.reshape(nseq, 1, d),
        h_m.reshape(1, bsz, SSD_HEADS, SSD_HEAD_DIM, SSD_STATE).astype(dt_out),
        tail_m[:, SUBLANES - (SSD_CONV - 1):, :].reshape(1, bsz, SSD_CONV - 1, SSD_XBC),
        s5r_m.reshape(1, bsz, S5_GROUPS, S5_STATE).astype(dt_out),
        s5i_m.reshape(1, bsz, S5_GROUPS, S5_STATE).astype(dt_out),
        h_s.reshape(1, nseq, SSD_HEADS, SSD_HEAD_DIM, SSD_STATE).astype(dt_out),
        jnp.transpose(cs_s, (1, 0, 2)).reshape(1, nseq, SSD_CONV - 1, SSD_XBC),
        seq_major(s5r_s).astype(dt_out),
        seq_major(s5i_s).astype(dt_out),
    )
```

```python
import functools

import jax
import jax.numpy as jnp
from jax import lax
from jax.experimental import pallas as pl
from jax.experimental.pallas import tpu as pltpu

F32 = jnp.float32
BF16 = jnp.bfloat16

NORM_EPS = 1e-5
LOG2E = 1.4426950408889634
N_META = 16
D_MODEL = 2048
SSD_HEAD_DIM = 64
SSD_HEADS = 32
SSD_GROUPS = 4
SSD_STATE = 128
SSD_WIDTH = SSD_HEADS * SSD_HEAD_DIM
SSD_XBC = SSD_WIDTH + 2 * SSD_GROUPS * SSD_STATE
SSD_CONV = 4
CHUNK = 128
S5_WIDTH = D_MODEL // 2
S5_GROUP = 16
S5_GROUPS = S5_WIDTH // S5_GROUP
S5_STATE = 64
S5_NSTATE = S5_GROUPS * S5_STATE
S5_BLOCKS = 4
S5_BLK_U = S5_WIDTH // S5_BLOCKS
S5_BLK_N = S5_NSTATE // S5_BLOCKS

LANES = 128
SUBLANES = 8
VMEM_LIMIT = 56 * 1024 * 1024
CONV_PITCH = 3

COL_ZA = 0
COL_GA = COL_ZA + SSD_WIDTH
COL_GB = COL_GA + D_MODEL
COL_XBC = COL_GB + D_MODEL
COL_UB = COL_XBC + SSD_XBC
COL_ZB = COL_UB + S5_WIDTH
PROJ_COLS = COL_ZB + S5_WIDTH
PROJ_TN = 1024


_NT = (((1,), (1,)), ((), ()))


def _sigmoid(x):
    return 0.5 * jnp.tanh(0.5 * x) + 0.5


def _silu(x):
    h = 0.5 * x
    return h + h * jnp.tanh(h)


def _softplus(x):
    return jnp.maximum(x, 0.0) + jnp.log1p(jnp.exp(-jnp.abs(x)))


def _gelu_tanh(x):
    c = 0.7978845608028654
    return 0.5 * x * (1.0 + jnp.tanh(c * (x + 0.044715 * (x * x * x))))


def _split3(x):
    x1 = x.astype(BF16)
    r1 = x - x1.astype(F32)
    x2 = r1.astype(BF16)
    x3 = (r1 - x2.astype(F32)).astype(BF16)
    return x1, x2, x3


def _params(*sem):
    return pltpu.CompilerParams(dimension_semantics=sem, vmem_limit_bytes=VMEM_LIMIT)


def _s5prep_kernel(lr_ref, li_ref, ldt_ref, btr_ref, bti_ref, abr_ref, abi_ref, bbr_ref, bbi_ref):
    lr = lr_ref[...]
    li = li_ref[...]
    step = jnp.exp(ldt_ref[...])
    mag = jnp.exp(lr * step)
    abr = mag * jnp.cos(li * step)
    abi = mag * jnp.sin(li * step)
    den = lr * lr + li * li
    numr = abr - 1.0
    cr = (numr * lr + abi * li) / den
    ci = (abi * lr - numr * li) / den
    abr_ref[...] = abr
    abi_ref[...] = abi
    btr = btr_ref[...]
    bti = bti_ref[...]
    crb = cr[:, None, :]
    cib = ci[:, None, :]
    bbr_ref[...] = crb * btr - cib * bti
    bbi_ref[...] = crb * bti + cib * btr


def _s5prep(lam_re, lam_im, log_dt, bt_re, bt_im):
    g, n = lam_re.shape
    full2 = pl.BlockSpec((g, n), lambda: (0, 0))
    full3 = pl.BlockSpec((g, S5_GROUP, n), lambda: (0, 0, 0))
    return pl.pallas_call(
        _s5prep_kernel,
        in_specs=[full2, full2, pl.BlockSpec((g, 1), lambda: (0, 0)), full3, full3],
        out_specs=[full2, full2, full3, full3],
        out_shape=[jax.ShapeDtypeStruct((g, n), F32)] * 2
        + [jax.ShapeDtypeStruct((g, S5_GROUP, n), F32)] * 2,
        name="s5prep",
    )(lam_re, lam_im, log_dt, bt_re, bt_im)


W_ALIGN = 32
_SRC_ZA = 0
_SRC_XBC = SSD_WIDTH
_SRC_DT = _SRC_XBC + SSD_XBC
_SRC_UB = _SRC_DT + SSD_HEADS
_SRC_ZB = _SRC_UB + S5_WIDTH
_SRC_GA = _SRC_ZB + S5_WIDTH
_SRC_GB = _SRC_GA + D_MODEL
_SEGMENTS = ((_SRC_ZA, SSD_WIDTH), (_SRC_GA, D_MODEL), (_SRC_GB, D_MODEL), (_SRC_XBC, SSD_XBC),
             (_SRC_UB, S5_WIDTH), (_SRC_ZB, S5_WIDTH))
_SRC_TILES = tuple((start + k) // W_ALIGN for start, width in _SEGMENTS for k in range(0, width, PROJ_TN))


def _wprep_dt_kernel(w_ref, o_ref):
    o_ref[...] = jnp.zeros(o_ref.shape, o_ref.dtype)
    o_ref[0:SSD_HEADS, :] = w_ref[...].astype(BF16)


def _wprep_dt(wt):
    d = wt.shape[1]
    assert _SRC_DT % SSD_HEADS == 0
    return pl.pallas_call(
        _wprep_dt_kernel,
        grid=(1,),
        in_specs=[pl.BlockSpec((SSD_HEADS, d), lambda i: (_SRC_DT // SSD_HEADS, 0))],
        out_specs=pl.BlockSpec((LANES, d), lambda i: (0, 0)),
        out_shape=jax.ShapeDtypeStruct((LANES, d), BF16),
        name="wprep_dt",
    )(wt)


def _norm_rows(x_ref, nw_ref, wdt_ref, xn_ref, dt_ref):
    x = x_ref[...]
    ms = jnp.mean(x * x, axis=-1, keepdims=True)
    xn = ((x * lax.rsqrt(ms + NORM_EPS)) * nw_ref[...]).astype(BF16)
    xn_ref[...] = xn
    dt_ref[...] = lax.dot_general(xn, wdt_ref[...], _NT, preferred_element_type=F32)


def _wprep_inproj_kernel(tbl_ref, x_ref, nw_ref, w_ref, wdt_ref, wout_ref, proj_ref, dt_ref, xn_ref):
    del tbl_ref

    @pl.when(pl.program_id(0) == 0)
    def _():
        _norm_rows(x_ref, nw_ref, wdt_ref, xn_ref, dt_ref)

    wb = w_ref[...].astype(BF16)
    wout_ref[...] = wb
    proj_ref[...] = lax.dot_general(xn_ref[...], wb, _NT, preferred_element_type=F32)


def _wprep_inproj(wt, w_dt, x, norm_w):
    m, d = x.shape
    ntiles = len(_SRC_TILES)
    assert ntiles * PROJ_TN == PROJ_COLS
    return pl.pallas_call(
        _wprep_inproj_kernel,
        grid_spec=pltpu.PrefetchScalarGridSpec(
            num_scalar_prefetch=1,
            grid=(ntiles,),
            in_specs=[
                pl.BlockSpec((m, d), lambda j, tbl: (0, 0)),
                pl.BlockSpec((1, d), lambda j, tbl: (0, 0)),
                pl.BlockSpec((pl.Element(PROJ_TN), pl.Element(d)),
                             lambda j, tbl: (pl.multiple_of(tbl[j] * W_ALIGN, W_ALIGN), 0)),
                pl.BlockSpec((LANES, d), lambda j, tbl: (0, 0)),
            ],
            out_specs=[
                pl.BlockSpec((PROJ_TN, d), lambda j, tbl: (j, 0)),
                pl.BlockSpec((m, PROJ_TN), lambda j, tbl: (0, j)),
                pl.BlockSpec((m, LANES), lambda j, tbl: (0, 0)),
            ],
            scratch_shapes=[pltpu.VMEM((m, d), BF16)],
        ),
        out_shape=[jax.ShapeDtypeStruct((PROJ_COLS, d), BF16), jax.ShapeDtypeStruct((m, PROJ_COLS), F32),
                   jax.ShapeDtypeStruct((m, LANES), F32)],
        compiler_params=_params("arbitrary"),
        name="wprep_inproj",
    )(jnp.asarray(_SRC_TILES, jnp.int32), x, norm_w, wt, w_dt)


def _inproj_kernel(x_ref, nw_ref, w_ref, wdt_ref, proj_ref, dt_ref, xn_ref):
    @pl.when(pl.program_id(1) == 0)
    def _():
        _norm_rows(x_ref, nw_ref, wdt_ref, xn_ref, dt_ref)

    proj_ref[...] = lax.dot_general(xn_ref[...], w_ref[...], _NT, preferred_element_type=F32)


def _inproj(x, norm_w, w_main, w_dt, tm):
    m, d = x.shape
    n = w_main.shape[0]
    return pl.pallas_call(
        _inproj_kernel,
        grid=(m // tm, n // PROJ_TN),
        in_specs=[
            pl.BlockSpec((tm, d), lambda i, j: (i, 0)),
            pl.BlockSpec((1, d), lambda i, j: (0, 0)),
            pl.BlockSpec((PROJ_TN, d), lambda i, j: (j, 0)),
            pl.BlockSpec((LANES, d), lambda i, j: (0, 0)),
        ],
        out_specs=[
            pl.BlockSpec((tm, PROJ_TN), lambda i, j: (i, j)),
            pl.BlockSpec((tm, LANES), lambda i, j: (i, 0)),
        ],
        out_shape=[jax.ShapeDtypeStruct((m, n), F32), jax.ShapeDtypeStruct((m, LANES), F32)],
        scratch_shapes=[pltpu.VMEM((tm, d), BF16)],
        compiler_params=_params("arbitrary", "arbitrary"),
        name="inproj",
    )(x, norm_w, w_main, w_dt)


def _ssd_chunk(r0, xbc_ref, za_ref, dtr_ref, convw_ref, convb_ref, dtb_ref, alog_ref, dexp_ref, nw_ref,
               y_ref, ht_scr, ext_scr, mask_rows, need_y):
    L = CHUNK
    P2 = 2 * SSD_HEAD_DIM
    GW = SSD_WIDTH // SSD_GROUPS
    time_rows = lambda t0, n: pl.ds(CONV_PITCH * t0, n, stride=CONV_PITCH)

    w = convw_ref[...]
    bias = convb_ref[...]
    conv_parts = []
    for s in range(SSD_XBC // LANES):
        ls = slice(s * LANES, (s + 1) * LANES)
        xc = xbc_ref[r0:r0 + L, ls]
        ext_scr[s, time_rows(SUBLANES, L), :] = xc
        acc = bias[:, ls]
        for k in range(SSD_CONV - 1):
            acc = acc + ext_scr[s, time_rows(SUBLANES - (SSD_CONV - 1) + k, L), :] * w[k:k + 1, ls]
        conv_parts.append(acc + xc * w[SSD_CONV - 1:SSD_CONV, ls])
        ext_scr[s, time_rows(0, SUBLANES), :] = xc[L - SUBLANES:L, :]
    xbc = _silu(jnp.concatenate(conv_parts, axis=1))
    xs = xbc[:, :SSD_WIDTH]
    bmat = xbc[:, SSD_WIDTH:SSD_WIDTH + SSD_GROUPS * SSD_STATE]
    cmat = xbc[:, SSD_WIDTH + SSD_GROUPS * SSD_STATE:]

    rows = lax.broadcasted_iota(jnp.int32, (L, L), 0)
    cols = lax.broadcasted_iota(jnp.int32, (L, L), 1)
    causal = rows >= cols
    lane_lo = cols < SSD_HEAD_DIM

    dt = _softplus(dtr_ref[r0:r0 + L, :] + dtb_ref[...])
    if mask_rows:
        dt = jnp.where(rows < mask_rows, 0.0, dt)
    a = dt * (-jnp.exp(alog_ref[...]))
    tril = jnp.where(causal, 1.0, 0.0).astype(BF16)
    a1, a2, a3 = _split3(a)
    acum = (jnp.dot(tril, a1, preferred_element_type=F32)
            + jnp.dot(tril, a2, preferred_element_type=F32)
            + jnp.dot(tril, a3, preferred_element_type=F32))
    a2 = acum * LOG2E
    e_cum = jnp.exp2(a2)
    w_end = dt * jnp.exp2(a2[L - 1:L, :] - a2)
    a2dt_t = (a2 - jnp.log2(dt)).T

    dexp = dexp_ref[...]
    xs_b = xs.astype(BF16)
    y_parts = []
    for g in range(SSD_GROUPS):
        bg = bmat[:, g * SSD_STATE:(g + 1) * SSD_STATE].astype(BF16)
        cg = cmat[:, g * SSD_STATE:(g + 1) * SSD_STATE].astype(BF16)
        ht_g = ht_scr[:, g * GW:(g + 1) * GW]
        if need_y:
            cb = lax.dot_general(cg, bg, (((1,), (1,)), ((), ())), preferred_element_type=F32)
            y_off = jnp.dot(cg, ht_g.astype(BF16), preferred_element_type=F32)
        xw_parts = []
        elast_parts = []
        for jj in range(GW // P2):
            lo = g * GW + jj * P2
            h0 = lo // SSD_HEAD_DIM
            yd, eb, wb = [], [], []
            for h in (h0, h0 + 1):
                if need_y:
                    colb = jnp.broadcast_to(a2[:, h:h + 1], (L, L))
                    m = cb * jnp.exp2(jnp.where(causal, colb - a2dt_t[h:h + 1, :], -jnp.inf))
                    yd.append(jnp.dot(m.astype(BF16), xs_b[:, lo:lo + P2], preferred_element_type=F32))
                eb.append(jnp.broadcast_to(e_cum[:, h:h + 1], (L, L)))
                wb.append(jnp.broadcast_to(w_end[:, h:h + 1], (L, L)))
            xs_pair = xs[:, lo:lo + P2]
            if need_y:
                y_pair = (jnp.where(lane_lo, yd[0], yd[1])
                          + y_off[:, jj * P2:(jj + 1) * P2] * jnp.where(lane_lo, eb[0], eb[1]))
                y_parts.append(y_pair + dexp[:, lo:lo + P2] * xs_pair)
            xw_parts.append(xs_pair * jnp.where(lane_lo, wb[0], wb[1]))
            elast_parts.append(jnp.where(lane_lo[0:1, :], eb[0][L - 1:L, :], eb[1][L - 1:L, :]))
        xw = jnp.concatenate(xw_parts, axis=1)
        elast = jnp.concatenate(elast_parts, axis=1)
        st = lax.dot_general(bg, xw.astype(BF16), (((0,), (0,)), ((), ())),
                             preferred_element_type=F32)
        ht_scr[:, g * GW:(g + 1) * GW] = ht_g * elast + st

    if not need_y:
        y_ref[r0:r0 + L, :] = jnp.zeros((L, SSD_WIDTH), y_ref.dtype)
        return
    y = jnp.concatenate(y_parts, axis=1)
    y = y * _silu(za_ref[r0:r0 + L, :])
    nw = nw_ref[...]
    outs = []
    for g in range(SSD_GROUPS):
        yg = y[:, g * GW:(g + 1) * GW]
        ms = jnp.mean(yg * yg, axis=-1, keepdims=True)
        outs.append((yg * lax.rsqrt(ms + NORM_EPS)) * nw[:, g * GW:(g + 1) * GW])
    y_ref[r0:r0 + L, :] = jnp.concatenate(outs, axis=1).astype(y_ref.dtype)


def _ssd_kernel(xbc_ref, za_ref, dtr_ref, ht0_ref, tail0_ref, convw_ref, convb_ref, dtb_ref, alog_ref,
                dexp_ref, nw_ref, y_ref, h_ref, ht_ref, tail_ref, ht_scr, ext_scr, *, mask_rows, need_y):
    c = pl.program_id(1)
    nrows = xbc_ref.shape[0]

    @pl.when(c == 0)
    def _():
        ht_scr[...] = ht0_ref[...]
        for s in range(SSD_XBC // LANES):
            ext_scr[s, pl.ds(0, SUBLANES, stride=CONV_PITCH), :] = tail0_ref[:, s * LANES:(s + 1) * LANES]

    for r0 in range(0, nrows, CHUNK):
        _ssd_chunk(r0, xbc_ref, za_ref, dtr_ref, convw_ref, convb_ref, dtb_ref, alog_ref, dexp_ref, nw_ref,
                   y_ref, ht_scr, ext_scr, mask_rows, need_y)

    @pl.when(c == pl.num_programs(1) - 1)
    def _():
        ht = ht_scr[...]
        ht_ref[...] = ht
        h_ref[...] = ht.T
        tail_ref[...] = xbc_ref[nrows - SUBLANES:nrows, :]


def _ssd(proj, dt_raw, ht0, tail0, conv_w, conv_b, dt_bias, a_log, d_exp, norm_w, *, batch, nchunks,
         row_block0, mask_rows, need_y=True):
    L = CHUNK
    rows = batch * nchunks * L
    rb = lambda b, c: row_block0 + b * nchunks + c
    const2 = lambda shape: pl.BlockSpec(shape, lambda b, c: (0, 0))
    return pl.pallas_call(
        functools.partial(_ssd_kernel, mask_rows=mask_rows, need_y=need_y),
        grid=(batch, nchunks),
        in_specs=[
            pl.BlockSpec((L, SSD_XBC), lambda b, c: (rb(b, c), COL_XBC // SSD_XBC)),
            pl.BlockSpec((L, SSD_WIDTH), lambda b, c: (rb(b, c), COL_ZA // SSD_WIDTH)),
            pl.BlockSpec((L, LANES), lambda b, c: (rb(b, c), 0)),
            const2((SSD_STATE, SSD_WIDTH)),
            const2((SUBLANES, SSD_XBC)),
            const2((SSD_CONV, SSD_XBC)),
            const2((1, SSD_XBC)),
            const2((1, LANES)),
            const2((1, LANES)),
            const2((1, SSD_WIDTH)),
            const2((1, SSD_WIDTH)),
        ],
        out_specs=[
            pl.BlockSpec((L, SSD_WIDTH), lambda b, c: (b * nchunks + c, 0)),
            pl.BlockSpec((None, SSD_WIDTH, SSD_STATE), lambda b, c: (b, 0, 0)),
            pl.BlockSpec((None, SSD_STATE, SSD_WIDTH), lambda b, c: (b, 0, 0)),
            pl.BlockSpec((None, SUBLANES, SSD_XBC), lambda b, c: (b, 0, 0)),
        ],
        out_shape=[
            jax.ShapeDtypeStruct((rows, SSD_WIDTH), BF16),
            jax.ShapeDtypeStruct((batch, SSD_WIDTH, SSD_STATE), F32),
            jax.ShapeDtypeStruct((batch, SSD_STATE, SSD_WIDTH), F32),
            jax.ShapeDtypeStruct((batch, SUBLANES, SSD_XBC), F32),
        ],
        scratch_shapes=[pltpu.VMEM((SSD_STATE, SSD_WIDTH), F32),
                        pltpu.VMEM((SSD_XBC // LANES, CONV_PITCH * (SUBLANES + CHUNK), LANES), F32)],
        compiler_params=_params("arbitrary", "arbitrary"),
        name="ssd",
    )(proj, proj, dt_raw, ht0, tail0, conv_w, conv_b, dt_bias, a_log, d_exp, norm_w)


def _ssd_step_kernel(xbc_ref, za_ref, dtr_ref, cs_ref, h_ref, convw_ref, convb_ref, dtb_ref, alog_ref,
                     dexp_ref, nw_ref, y_ref, hout_ref, csout_ref):
    R = SUBLANES
    GW = SSD_WIDTH // SSD_GROUPS
    x = xbc_ref[...]
    w = convw_ref[...]
    s0 = cs_ref[0]
    s1 = cs_ref[1]
    s2 = cs_ref[2]
    conv = convb_ref[...] + s0 * w[0:1]
    conv = conv + s1 * w[1:2]
    conv = conv + s2 * w[2:3]
    conv = conv + x * w[3:4]
    csout_ref[0] = s1
    csout_ref[1] = s2
    csout_ref[2] = x
    xbc = _silu(conv)
    xs = xbc[:, :SSD_WIDTH]
    bmat = xbc[:, SSD_WIDTH:SSD_WIDTH + SSD_GROUPS * SSD_STATE]
    cmat = xbc[:, SSD_WIDTH + SSD_GROUPS * SSD_STATE:]

    dt = _softplus(dtr_ref[...] + dtb_ref[...])
    da = jnp.exp(dt * (-jnp.exp(alog_ref[...])))
    dt_t = dt.T
    da_t = da.T
    expand = lambda v: jnp.concatenate(
        [jnp.broadcast_to(v[h:h + 1, :], (SSD_HEAD_DIM, R)) for h in range(SSD_HEADS)], axis=0)
    xd_t = xs.T * expand(dt_t)
    da_te = expand(da_t)

    cmat_b = cmat.astype(BF16)
    yrows = []
    for i in range(R):
        bexp = jnp.concatenate(
            [jnp.broadcast_to(bmat[i:i + 1, g * SSD_STATE:(g + 1) * SSD_STATE], (GW, SSD_STATE))
             for g in range(SSD_GROUPS)], axis=0)
        hn = h_ref[i] * da_te[:, i:i + 1] + xd_t[:, i:i + 1] * bexp
        hout_ref[i] = hn
        cg = jnp.concatenate([cmat_b[i:i + 1, g * SSD_STATE:(g + 1) * SSD_STATE] for g in range(SSD_GROUPS)]
                             + [jnp.zeros((R - SSD_GROUPS, SSD_STATE), BF16)], axis=0)
        yg = lax.dot_general(cg, hn.astype(BF16), (((1,), (1,)), ((), ())), preferred_element_type=F32)
        yrows.append(jnp.concatenate([yg[g:g + 1, g * GW:(g + 1) * GW] for g in range(SSD_GROUPS)], axis=1))
    y = jnp.concatenate(yrows, axis=0)
    y = y + dexp_ref[...] * xs
    y = y * _silu(za_ref[...])
    nw = nw_ref[...]
    outs = []
    for g in range(SSD_GROUPS):
        yg = y[:, g * GW:(g + 1) * GW]
        ms = jnp.mean(yg * yg, axis=-1, keepdims=True)
        outs.append((yg * lax.rsqrt(ms + NORM_EPS)) * nw[:, g * GW:(g + 1) * GW])
    y_ref[...] = jnp.concatenate(outs, axis=1).astype(y_ref.dtype)


def _ssd_step(proj, dt_raw, conv_state, ssd_state, conv_w, conv_b, dt_bias, a_log, d_exp, norm_w, *, nseq):
    R = SUBLANES
    const2 = lambda shape: pl.BlockSpec(shape, lambda i: (0, 0))
    return pl.pallas_call(
        _ssd_step_kernel,
        grid=(nseq // R,),
        in_specs=[
            pl.BlockSpec((R, SSD_XBC), lambda i: (i, COL_XBC // SSD_XBC)),
            pl.BlockSpec((R, SSD_WIDTH), lambda i: (i, COL_ZA // SSD_WIDTH)),
            pl.BlockSpec((R, LANES), lambda i: (i, 0)),
            pl.BlockSpec((SSD_CONV - 1, R, SSD_XBC), lambda i: (0, i, 0)),
            pl.BlockSpec((R, SSD_WIDTH, SSD_STATE), lambda i: (i, 0, 0)),
            const2((SSD_CONV, SSD_XBC)),
            const2((1, SSD_XBC)),
            const2((1, LANES)),
            const2((1, LANES)),
            const2((1, SSD_WIDTH)),
            const2((1, SSD_WIDTH)),
        ],
        out_specs=[
            pl.BlockSpec((R, SSD_WIDTH), lambda i: (i, 0)),
            pl.BlockSpec((R, SSD_WIDTH, SSD_STATE), lambda i: (i, 0, 0)),
            pl.BlockSpec((SSD_CONV - 1, R, SSD_XBC), lambda i: (0, i, 0)),
        ],
        out_shape=[
            jax.ShapeDtypeStruct((nseq, SSD_WIDTH), BF16),
            jax.ShapeDtypeStruct((nseq, SSD_WIDTH, SSD_STATE), F32),
            jax.ShapeDtypeStruct((SSD_CONV - 1, nseq, SSD_XBC), F32),
        ],
        compiler_params=_params("arbitrary"),
        name="ssd_step",
    )(proj, proj, dt_raw, conv_state, ssd_state, conv_w, conv_b, dt_bias, a_log, d_exp, norm_w)


def _s5_block_diag(w, rows_per_group, cols_per_group):
    ngrp, r, lanes = w.shape
    width = ngrp * cols_per_group
    tiled = jnp.concatenate([w.reshape(ngrp * r, lanes)] * (width // lanes), axis=1)
    row_g = lax.broadcasted_iota(jnp.int32, tiled.shape, 0) // rows_per_group
    col_g = lax.broadcasted_iota(jnp.int32, tiled.shape, 1) // cols_per_group
    return jnp.where(row_g == col_g, tiled, 0.0)


def _s5_weights(bbr_ref, bbi_ref, cr_ref, ci_ref):
    return tuple(_s5_block_diag(r[...], S5_GROUP, S5_STATE) for r in (bbr_ref, bbi_ref, cr_ref, ci_ref))


def _s5_pitch(seg_len):
    return seg_len if (seg_len // SUBLANES) % 2 else seg_len + SUBLANES


def _s5_kernel(*refs, seq, nconv, need_y):
    (u_ref, h0r_ref, h0i_ref, bbr_ref, bbi_ref, ctr_ref, cti_ref, ar_ref, ai_ref, d_ref) = refs[:10]
    w_in_refs = refs[10:10 + nconv]
    y_ref, hr_ref, hi_ref = refs[10 + nconv:13 + nconv]
    w_out_refs = refs[13 + nconv:13 + 2 * nconv]
    pad, pe, po, cr, ci, hb_in, hb_out, w2r_scr, w2i_scr, cta_scr, ctb_scr, k0_scr = refs[13 + 2 * nconv:]
    for wi, wo in zip(w_in_refs, w_out_refs):
        wo[...] = wi[...].astype(BF16)

    NS = SUBLANES
    SL = seq // NS
    H = SL // 2
    pitch = _s5_pitch(SL)
    nk = S5_BLK_U // LANES
    n = S5_BLK_N
    a_re = ar_ref[...]
    a_im = ai_ref[...]
    bf = lambda v: v.astype(BF16)

    @pl.when(pl.program_id(1) == 0)
    def _():
        bt_re, bt_im, ct_re, ct_im = _s5_weights(bbr_ref, bbi_ref, ctr_ref, cti_ref)
        w2r_scr[...] = jnp.concatenate([bf(bt_re * a_re - bt_im * a_im), bf(bt_re)], axis=0)
        w2i_scr[...] = jnp.concatenate([bf(bt_re * a_im + bt_im * a_re), bf(bt_im)], axis=0)
        if need_y:
            cta_scr[0] = bf(ct_re * a_re - ct_im * a_im)
            cta_scr[1] = bf(ct_re * a_im + ct_im * a_re)
            ctb_scr[0] = bf(ct_re)
            ctb_scr[1] = bf(ct_im)
            k0_scr[...] = bf(lax.dot_general(bf(bt_re), bf(ct_re), _NT, preferred_element_type=F32)
                             - lax.dot_general(bf(bt_im), bf(ct_im), _NT, preferred_element_type=F32))

    for s in range(NS):
        for k in range(nk):
            pad[k, s * pitch:s * pitch + SL, :] = u_ref[s * SL:(s + 1) * SL, k * LANES:(k + 1) * LANES]

    for jp in range(H):
        for k in range(nk):
            pe[jp * NS:(jp + 1) * NS, k * LANES:(k + 1) * LANES] = pad[k, pl.ds(2 * jp, NS, stride=pitch), :]
            po[jp * NS:(jp + 1) * NS, k * LANES:(k + 1) * LANES] = pad[k, pl.ds(2 * jp + 1, NS, stride=pitch), :]
    half = seq // 2
    NB = 4 if H % 8 == 0 else 1
    rbk = half // NB
    spb = rbk // NS

    qc = 4 * S5_GROUP
    qn = 4 * S5_STATE

    def input_block(kb):
        rows = slice(kb * rbk, (kb + 1) * rbk)
        ue_b = bf(pe[rows, :])
        uo_b = bf(po[rows, :])
        for q in range(S5_BLK_U // qc):
            u_q = jnp.concatenate([ue_b[:, q * qc:(q + 1) * qc], uo_b[:, q * qc:(q + 1) * qc]], axis=1)
            rows_q = [slice(q * qc, (q + 1) * qc), slice(S5_BLK_U + q * qc, S5_BLK_U + (q + 1) * qc)]
            cols_q = slice(q * qn, (q + 1) * qn)
            wq_re = jnp.concatenate([w2r_scr[r, cols_q] for r in rows_q], axis=0)
            wq_im = jnp.concatenate([w2i_scr[r, cols_q] for r in rows_q], axis=0)
            cr[rows, cols_q] = jnp.dot(u_q, wq_re, preferred_element_type=F32)
            ci[rows, cols_q] = jnp.dot(u_q, wq_im, preferred_element_type=F32)

    a2_re = a_re * a_re - a_im * a_im
    a2_im = 2.0 * (a_re * a_im)
    ar = jnp.broadcast_to(a2_re, (NS, n))
    ai = jnp.broadcast_to(a2_im, (NS, n))

    def step(carry, r0):
        hr, hi = carry
        nr = ar * hr - ai * hi + cr[r0:r0 + NS, :]
        ni = ar * hi + ai * hr + ci[r0:r0 + NS, :]
        return nr, ni

    zero = jnp.zeros((NS, n), F32)
    carry = (zero, zero)
    input_block(0)
    for kb in range(NB):
        if kb + 1 < NB:
            input_block(kb + 1)
        for t in range(spb):
            carry = step(carry, (kb * spb + t) * NS)
    er, ei = carry

    asr, asi = a_re, a_im
    for _ in range(SL.bit_length() - 1):
        asr, asi = asr * asr - asi * asi, 2.0 * (asr * asi)
    gr = [h0r_ref[...]]
    gi = [h0i_ref[...]]
    for s in range(NS):
        gr.append(er[s:s + 1, :] + (asr * gr[s] - asi * gi[s]))
        gi.append(ei[s:s + 1, :] + (asr * gi[s] + asi * gr[s]))
    hr_ref[...] = gr[NS]
    hi_ref[...] = gi[NS]
    if not need_y:
        y_ref[...] = jnp.zeros(y_ref.shape, y_ref.dtype)
        return

    def scan_block(kb, c0):
        for p in range(spb // 2):
            r0 = (kb * spb + 2 * p) * NS
            c1 = step(c0, r0)
            c2 = step(c1, r0 + NS)
            hb_in[r0:r0 + 2 * NS, 0:n] = bf(jnp.concatenate([c0[0], c1[0]], axis=0))
            hb_in[r0:r0 + 2 * NS, n:2 * n] = bf(jnp.concatenate([c0[1], c1[1]], axis=0))
            hb_out[r0:r0 + 2 * NS, 0:n] = bf(jnp.concatenate([c1[0], c2[0]], axis=0))
            hb_out[r0:r0 + 2 * NS, n:2 * n] = bf(jnp.concatenate([c1[1], c2[1]], axis=0))
            c0 = c2
        return c0

    d = d_ref[...]
    cta_re, cta_im = cta_scr[0], cta_scr[1]
    ctb_re, ctb_im = ctb_scr[0], ctb_scr[1]
    k0 = k0_scr[...]

    def output_block(kb):
        rows = slice(kb * rbk, (kb + 1) * rbk)
        y_odd = (lax.dot_general(hb_out[rows, 0:n], ctb_re, _NT, preferred_element_type=F32)
                 - lax.dot_general(hb_out[rows, n:2 * n], ctb_im, _NT, preferred_element_type=F32))
        y_even = (lax.dot_general(hb_in[rows, 0:n], cta_re, _NT, preferred_element_type=F32)
                  - lax.dot_general(hb_in[rows, n:2 * n], cta_im, _NT, preferred_element_type=F32)
                  + jnp.dot(bf(pe[rows, :]), k0, preferred_element_type=F32))
        po[rows, :] = _gelu_tanh(y_odd + d * po[rows, :])
        pe[rows, :] = _gelu_tanh(y_even + d * pe[rows, :])

    carry = scan_block(0, (jnp.concatenate(gr[:NS], axis=0), jnp.concatenate(gi[:NS], axis=0)))
    for kb in range(NB):
        if kb + 1 < NB:
            carry = scan_block(kb + 1, carry)
        output_block(kb)

    for jp in range(H):
        for k in range(nk):
            pad[k, pl.ds(2 * jp, NS, stride=pitch), :] = pe[jp * NS:(jp + 1) * NS, k * LANES:(k + 1) * LANES]
            pad[k, pl.ds(2 * jp + 1, NS, stride=pitch), :] = po[jp * NS:(jp + 1) * NS, k * LANES:(k + 1) * LANES]
    for s in range(NS):
        for k in range(nk):
            y_ref[s * SL:(s + 1) * SL, k * LANES:(k + 1) * LANES] = pad[k, s * pitch:s * pitch + SL, :]


def _s5(proj, h0r, h0i, bb_re, bb_im, ct_re, ct_im, ab_re, ab_im, d_s5, *, batch, seq, row_block0, need_y=True,
        convert=()):
    ub0 = COL_UB // S5_BLK_U
    nsteps = batch * S5_BLOCKS
    slice_spec = lambda w: pl.BlockSpec((w.shape[0] // nsteps, w.shape[1]), lambda j, b: (j * batch + b, 0))
    assert all(w.shape[0] % (2 * SUBLANES * nsteps) == 0 for w in convert)
    seg = seq // SUBLANES
    assert seg & (seg - 1) == 0
    gpb = S5_GROUPS // S5_BLOCKS
    vec = lambda width: pl.BlockSpec((1, width), lambda j, b: (0, j))
    wspec = lambda r: pl.BlockSpec((gpb, r, LANES), lambda j, b: (j, 0, 0))
    st_out = pl.BlockSpec((None, 1, S5_BLK_N), lambda j, b: (b, 0, j))
    return pl.pallas_call(
        functools.partial(_s5_kernel, seq=seq, nconv=len(convert), need_y=need_y),
        grid=(S5_BLOCKS, batch),
        in_specs=[
            pl.BlockSpec((seq, S5_BLK_U), lambda j, b: (row_block0 + b, ub0 + j)),
            vec(S5_BLK_N), vec(S5_BLK_N),
            wspec(S5_GROUP), wspec(S5_GROUP), wspec(S5_GROUP), wspec(S5_GROUP),
            vec(S5_BLK_N), vec(S5_BLK_N), vec(S5_BLK_U),
        ] + [slice_spec(w) for w in convert],
        out_specs=[pl.BlockSpec((seq, S5_BLK_U), lambda j, b: (b, j)), st_out, st_out]
        + [slice_spec(w) for w in convert],
        out_shape=[
            jax.ShapeDtypeStruct((batch * seq, S5_WIDTH), F32),
            jax.ShapeDtypeStruct((batch, 1, S5_NSTATE), F32),
            jax.ShapeDtypeStruct((batch, 1, S5_NSTATE), F32),
        ] + [jax.ShapeDtypeStruct(w.shape, BF16) for w in convert],
        scratch_shapes=[
            pltpu.VMEM((S5_BLK_U // LANES, SUBLANES * _s5_pitch(seg), LANES), F32),
            pltpu.VMEM((seq // 2, S5_BLK_U), F32),
            pltpu.VMEM((seq // 2, S5_BLK_U), F32),
            pltpu.VMEM((seq // 2, S5_BLK_N), F32),
            pltpu.VMEM((seq // 2, S5_BLK_N), F32),
            pltpu.VMEM((seq // 2, 2 * S5_BLK_N), BF16),
            pltpu.VMEM((seq // 2, 2 * S5_BLK_N), BF16),
            pltpu.VMEM((2 * S5_BLK_U, S5_BLK_N), BF16),
            pltpu.VMEM((2 * S5_BLK_U, S5_BLK_N), BF16),
            pltpu.VMEM((2, S5_BLK_U, S5_BLK_N), BF16),
            pltpu.VMEM((2, S5_BLK_U, S5_BLK_N), BF16),
            pltpu.VMEM((S5_BLK_U, S5_BLK_U), BF16),
        ],
        compiler_params=_params("arbitrary", "arbitrary"),
        name="s5",
    )(proj, h0r, h0i, bb_re, bb_im, ct_re, ct_im, ab_re, ab_im, d_s5, *convert)


def _s5_step_kernel(u_ref, h0r_ref, h0i_ref, bbr_ref, bbi_ref, ctr_ref, cti_ref, ar_ref, ai_ref, d_ref,
                    y_ref, hr_ref, hi_ref):
    bt_re, bt_im, ct_re, ct_im = (w.astype(BF16) for w in _s5_weights(bbr_ref, bbi_ref, ctr_ref, cti_ref))
    u = u_ref[...]
    ub = u.astype(BF16)
    ar = ar_ref[...]
    ai = ai_ref[...]
    h0r = h0r_ref[...].T
    h0i = h0i_ref[...].T
    hr = jnp.dot(ub, bt_re, preferred_element_type=F32) + (ar * h0r - ai * h0i)
    hi = jnp.dot(ub, bt_im, preferred_element_type=F32) + (ar * h0i + ai * h0r)
    hr_ref[...] = hr.T
    hi_ref[...] = hi.T
    y = (lax.dot_general(hr.astype(BF16), ct_re, _NT, preferred_element_type=F32)
         - lax.dot_general(hi.astype(BF16), ct_im, _NT, preferred_element_type=F32))
    y_ref[...] = _gelu_tanh(y + d_ref[...] * u)


def _s5_step(proj, h0r, h0i, bb_re, bb_im, ct_re, ct_im, ab_re, ab_im, d_s5, *, nseq):
    ub0 = COL_UB // S5_BLK_U
    gpb = S5_GROUPS // S5_BLOCKS
    vec = lambda width: pl.BlockSpec((1, width), lambda j: (0, j))
    wspec = lambda r: pl.BlockSpec((gpb, r, LANES), lambda j: (j, 0, 0))
    st = pl.BlockSpec((S5_BLK_N, nseq), lambda j: (j, 0))
    return pl.pallas_call(
        _s5_step_kernel,
        grid=(S5_BLOCKS,),
        in_specs=[
            pl.BlockSpec((nseq, S5_BLK_U), lambda j: (0, ub0 + j)),
            st, st,
            wspec(S5_GROUP), wspec(S5_GROUP), wspec(S5_GROUP), wspec(S5_GROUP),
            vec(S5_BLK_N), vec(S5_BLK_N), vec(S5_BLK_U),
        ],
        out_specs=[pl.BlockSpec((nseq, S5_BLK_U), lambda j: (0, j)), st, st],
        out_shape=[
            jax.ShapeDtypeStruct((nseq, S5_WIDTH), F32),
            jax.ShapeDtypeStruct((S5_NSTATE, nseq), F32),
            jax.ShapeDtypeStruct((S5_NSTATE, nseq), F32),
        ],
        compiler_params=_params("arbitrary"),
        name="s5_step",
    )(proj, h0r, h0i, bb_re, bb_im, ct_re, ct_im, ab_re, ab_im, d_s5)


def _tail_kernel(yn_ref, ybg_ref, zb_ref, ga_ref, gb_ref, x_ref, wpa_ref, wglu_ref, bglu_ref, wpb_ref,
                 wout_ref, fnw_ref, out_ref):
    sg_a = _sigmoid(ga_ref[...])
    sg_b = _sigmoid(gb_ref[...])
    sz_b = _silu(zb_ref[...])
    yb = ybg_ref[...]
    glu = jnp.dot(yb.astype(BF16), wglu_ref[...], preferred_element_type=F32) + bglu_ref[...]
    ya = jnp.dot(yn_ref[...], wpa_ref[...], preferred_element_type=F32)
    yb = (yb * _sigmoid(glu)) * sz_b
    ybp = jnp.dot(yb.astype(BF16), wpb_ref[...], preferred_element_type=F32)
    mixed = sg_a * ya + sg_b * ybp
    o = x_ref[...] + jnp.dot(mixed.astype(BF16), wout_ref[...], preferred_element_type=F32)
    ms = jnp.mean(o * o, axis=-1, keepdims=True)
    out_ref[...] = (o * lax.rsqrt(ms + NORM_EPS)) * fnw_ref[...]


def _tail(yn, ybg, proj, x, w_proj_a, w_glu, b_glu, w_proj_b, w_out, final_norm_w, *, tm):
    m, d = x.shape
    resident = lambda shape: pl.BlockSpec(shape, lambda i: (0, 0), pipeline_mode=pl.Buffered(1))
    return pl.pallas_call(
        _tail_kernel,
        grid=(m // tm,),
        in_specs=[
            pl.BlockSpec((tm, SSD_WIDTH), lambda i: (i, 0)),
            pl.BlockSpec((tm, S5_WIDTH), lambda i: (i, 0)),
            pl.BlockSpec((tm, S5_WIDTH), lambda i: (i, COL_ZB // S5_WIDTH)),
            pl.BlockSpec((tm, d), lambda i: (i, COL_GA // D_MODEL)),
            pl.BlockSpec((tm, d), lambda i: (i, COL_GB // D_MODEL)),
            pl.BlockSpec((tm, d), lambda i: (i, 0)),
            resident((SSD_WIDTH, d)),
            resident((S5_WIDTH, S5_WIDTH)),
            resident((1, S5_WIDTH)),
            resident((S5_WIDTH, d)),
            resident((d, d)),
            resident((1, d)),
        ],
        out_specs=pl.BlockSpec((tm, d), lambda i: (i, 0)),
        out_shape=jax.ShapeDtypeStruct((m, d), F32),
        compiler_params=_params("arbitrary"),
        name="tail",
    )(yn, ybg, proj, proj, proj, x, w_proj_a, w_glu, b_glu, w_proj_b, w_out, final_norm_w)


def kernel(x_prompt, x_sample, state_ssd, state_conv, state_s5_re, state_s5_im, meta_tokens, norm_w, w_in,
           conv_w, conv_b, dt_bias, a_log, d_ssd, ssd_norm_w, w_proj_a, lam_re, lam_im, log_dt_s5, b_re, b_im,
           c_re, c_im, d_s5, w_glu, b_glu, w_proj_b, w_out, final_norm_w):
    bsz, seq, d = x_prompt.shape
    nseq = x_sample.shape[0]
    assert d == D_MODEL and seq % CHUNK == 0 and nseq % SUBLANES == 0 and norm_w.shape[0] == 1
    assert meta_tokens.shape[0] == N_META and N_META <= CHUNK

    assert w_in.shape[2] == _SRC_GB + D_MODEL
    wt = jnp.transpose(w_in[0])
    w_dt = _wprep_dt(wt)
    pad_heads = lambda v: jnp.pad(v.reshape(1, SSD_HEADS), ((0, 0), (0, LANES - SSD_HEADS)))
    dtb = pad_heads(dt_bias[0])
    alog = pad_heads(a_log[0])
    d_exp = jnp.repeat(d_ssd[0], SSD_HEAD_DIM).reshape(1, SSD_WIDTH)
    nw1 = norm_w[0].reshape(1, d)
    ssd_nw = ssd_norm_w[0].reshape(1, SSD_WIDTH)
    convw = conv_w[0]
    convb = conv_b[0].reshape(1, SSD_XBC)
    bglu = b_glu[0].reshape(1, S5_WIDTH)
    fnw = final_norm_w.reshape(1, d)
    ds5 = d_s5[0].reshape(1, S5_WIDTH)

    rep = lambda v, k: jnp.concatenate([v] * k, axis=-1)
    lane_rep = LANES // S5_STATE
    ab_re, ab_im, bb_re, bb_im = _s5prep(
        rep(lam_re[0], lane_rep), rep(lam_im[0], lane_rep), log_dt_s5[0].reshape(S5_GROUPS, 1),
        rep(jnp.transpose(b_re[0], (0, 2, 1)), lane_rep), rep(jnp.transpose(b_im[0], (0, 2, 1)), lane_rep))
    ab_re = ab_re[:, :S5_STATE].reshape(1, S5_NSTATE)
    ab_im = ab_im[:, :S5_STATE].reshape(1, S5_NSTATE)
    ct_re = rep(c_re[0], lane_rep)
    ct_im = rep(c_im[0], lane_rep)

    x_main = x_prompt.reshape(bsz * seq, d)
    x_small = jnp.concatenate(
        [x_sample.reshape(nseq, d), jnp.zeros((CHUNK - N_META, d), x_prompt.dtype),
         meta_tokens.astype(x_prompt.dtype)], axis=0)
    assert nseq % CHUNK == 0
    meta_blk = nseq // CHUNK
    tm_main = 1024 if (bsz * seq) % 1024 == 0 else CHUNK
    w_main, proj_s, dt_s = _wprep_inproj(wt, w_dt, x_small, nw1)
    proj_m, dt_m = _inproj(x_main, nw1, w_main, w_dt, tm=tm_main)

    ssd_args = (convw, convb, dtb, alog, d_exp, ssd_nw)
    s5_w = (bb_re, bb_im, ct_re, ct_im)
    s5_v = (ab_re, ab_im, ds5)

    zeros_ht = jnp.zeros((SSD_STATE, SSD_WIDTH), F32)
    zeros_tail = jnp.zeros((SUBLANES, SSD_XBC), F32)
    zeros_s5 = jnp.zeros((1, S5_NSTATE), F32)
    _, _, ht_meta, tail_meta = _ssd(proj_s, dt_s, zeros_ht, zeros_tail, *ssd_args, batch=1, nchunks=1,
                                    row_block0=meta_blk, mask_rows=CHUNK - N_META, need_y=False)
    _, s5r_meta, s5i_meta = _s5(proj_s, zeros_s5, zeros_s5, *s5_w, *s5_v, batch=1, seq=CHUNK,
                                row_block0=meta_blk, need_y=False)[:3]

    yn_m, h_m, _, tail_m = _ssd(proj_m, dt_m, ht_meta[0], tail_meta[0], *ssd_args, batch=bsz,
                                nchunks=seq // CHUNK, row_block0=0, mask_rows=0)
    ybg_m, s5r_m, s5i_m, wpa, wglu, wpb, wout = _s5(
        proj_m, s5r_meta[0], s5i_meta[0], *s5_w, *s5_v, batch=bsz, seq=seq, row_block0=0,
        convert=(w_proj_a[0], w_glu[0], w_proj_b[0], w_out[0]))
    tail_w = (wpa, wglu, bglu, wpb, wout, fnw)
    y_prompt = _tail(yn_m, ybg_m, proj_m, x_main, *tail_w, tm=256)

    yn_s, h_s, cs_s = _ssd_step(proj_s, dt_s, jnp.transpose(state_conv[0], (1, 0, 2)),
                                state_ssd[0].reshape(nseq, SSD_WIDTH, SSD_STATE), *ssd_args, nseq=nseq)
    seq_minor = lambda v: jnp.transpose(v, (1, 2, 0)).reshape(S5_NSTATE, nseq)
    seq_major = lambda v: jnp.transpose(v.reshape(S5_GROUPS, S5_STATE, nseq), (2, 0, 1))[None]
    ybg_s, s5r_s, s5i_s = _s5_step(proj_s, seq_minor(state_s5_re[0]), seq_minor(state_s5_im[0]),
                                   *s5_w, *s5_v, nseq=nseq)
    y_sample = _tail(yn_s, ybg_s, proj_s, x_sample.reshape(nseq, d), *tail_w, tm=nseq)

    dt_out = x_prompt.dtype
    return (
        y_prompt.reshape(bsz, seq, d),
        y_sample.reshape(nseq, 1, d),
        h_m.reshape(1, bsz, SSD_HEADS, SSD_HEAD_DIM, SSD_STATE).astype(dt_out),
        tail_m[:, SUBLANES - (SSD_CONV - 1):, :].reshape(1, bsz, SSD_CONV - 1, SSD_XBC),
        s5r_m.reshape(1, bsz, S5_GROUPS, S5_STATE).astype(dt_out),
        s5i_m.reshape(1, bsz, S5_GROUPS, S5_STATE).astype(dt_out),
        h_s.reshape(1, nseq, SSD_HEADS, SSD_HEAD_DIM, SSD_STATE).astype(dt_out),
        jnp.transpose(cs_s, (1, 0, 2)).reshape(1, nseq, SSD_CONV - 1, SSD_XBC),
        seq_major(s5r_s).astype(dt_out),
        seq_major(s5i_s).astype(dt_out),
    )
```

```python
import functools

import jax
import jax.numpy as jnp
from jax import lax
from jax.experimental import pallas as pl
from jax.experimental.pallas import tpu as pltpu

F32 = jnp.float32
BF16 = jnp.bfloat16

NORM_EPS = 1e-5
LOG2E = 1.4426950408889634
N_META = 16
D_MODEL = 2048
SSD_HEAD_DIM = 64
SSD_HEADS = 32
SSD_GROUPS = 4
SSD_STATE = 128
SSD_WIDTH = SSD_HEADS * SSD_HEAD_DIM
SSD_XBC = SSD_WIDTH + 2 * SSD_GROUPS * SSD_STATE
SSD_CONV = 4
CHUNK = 128
S5_WIDTH = D_MODEL // 2
S5_GROUP = 16
S5_GROUPS = S5_WIDTH // S5_GROUP
S5_STATE = 64
S5_NSTATE = S5_GROUPS * S5_STATE
S5_BLOCKS = 4
S5_BLK_U = S5_WIDTH // S5_BLOCKS
S5_BLK_N = S5_NSTATE // S5_BLOCKS

LANES = 128
SUBLANES = 8
VMEM_LIMIT = 56 * 1024 * 1024
CONV_PITCH = 3

COL_ZA = 0
COL_GA = COL_ZA + SSD_WIDTH
COL_GB = COL_GA + D_MODEL
COL_XBC = COL_GB + D_MODEL
COL_UB = COL_XBC + SSD_XBC
COL_ZB = COL_UB + S5_WIDTH
PROJ_COLS = COL_ZB + S5_WIDTH
PROJ_TN = 1024


_NT = (((1,), (1,)), ((), ()))


def _sigmoid(x):
    return 0.5 * jnp.tanh(0.5 * x) + 0.5


def _silu(x):
    h = 0.5 * x
    return h + h * jnp.tanh(h)


def _softplus(x):
    return jnp.maximum(x, 0.0) + jnp.log1p(jnp.exp(-jnp.abs(x)))


def _gelu_tanh(x):
    c = 0.7978845608028654
    return 0.5 * x * (1.0 + jnp.tanh(c * (x + 0.044715 * (x * x * x))))


def _split3(x):
    x1 = x.astype(BF16)
    r1 = x - x1.astype(F32)
    x2 = r1.astype(BF16)
    x3 = (r1 - x2.astype(F32)).astype(BF16)
    return x1, x2, x3


def _params(*sem):
    return pltpu.CompilerParams(dimension_semantics=sem, vmem_limit_bytes=VMEM_LIMIT)


def _s5prep_kernel(lr_ref, li_ref, ldt_ref, btr_ref, bti_ref, abr_ref, abi_ref, bbr_ref, bbi_ref):
    lr = lr_ref[...]
    li = li_ref[...]
    step = jnp.exp(ldt_ref[...])
    mag = jnp.exp(lr * step)
    abr = mag * jnp.cos(li * step)
    abi = mag * jnp.sin(li * step)
    den = lr * lr + li * li
    numr = abr - 1.0
    cr = (numr * lr + abi * li) / den
    ci = (abi * lr - numr * li) / den
    abr_ref[...] = abr
    abi_ref[...] = abi
    btr = btr_ref[...]
    bti = bti_ref[...]
    crb = cr[:, None, :]
    cib = ci[:, None, :]
    bbr_ref[...] = crb * btr - cib * bti
    bbi_ref[...] = crb * bti + cib * btr


def _s5prep(lam_re, lam_im, log_dt, bt_re, bt_im):
    g, n = lam_re.shape
    full2 = pl.BlockSpec((g, n), lambda: (0, 0))
    full3 = pl.BlockSpec((g, S5_GROUP, n), lambda: (0, 0, 0))
    return pl.pallas_call(
        _s5prep_kernel,
        in_specs=[full2, full2, pl.BlockSpec((g, 1), lambda: (0, 0)), full3, full3],
        out_specs=[full2, full2, full3, full3],
        out_shape=[jax.ShapeDtypeStruct((g, n), F32)] * 2
        + [jax.ShapeDtypeStruct((g, S5_GROUP, n), F32)] * 2,
        name="s5prep",
    )(lam_re, lam_im, log_dt, bt_re, bt_im)


W_ALIGN = 32
_SRC_ZA = 0
_SRC_XBC = SSD_WIDTH
_SRC_DT = _SRC_XBC + SSD_XBC
_SRC_UB = _SRC_DT + SSD_HEADS
_SRC_ZB = _SRC_UB + S5_WIDTH
_SRC_GA = _SRC_ZB + S5_WIDTH
_SRC_GB = _SRC_GA + D_MODEL
_SEGMENTS = ((_SRC_ZA, SSD_WIDTH), (_SRC_GA, D_MODEL), (_SRC_GB, D_MODEL), (_SRC_XBC, SSD_XBC),
             (_SRC_UB, S5_WIDTH), (_SRC_ZB, S5_WIDTH))
_SRC_TILES = tuple((start + k) // W_ALIGN for start, width in _SEGMENTS for k in range(0, width, PROJ_TN))


def _wprep_dt_kernel(w_ref, o_ref):
    o_ref[...] = jnp.zeros(o_ref.shape, o_ref.dtype)
    o_ref[0:SSD_HEADS, :] = w_ref[...].astype(BF16)


def _wprep_dt(wt):
    d = wt.shape[1]
    assert _SRC_DT % SSD_HEADS == 0
    return pl.pallas_call(
        _wprep_dt_kernel,
        grid=(1,),
        in_specs=[pl.BlockSpec((SSD_HEADS, d), lambda i: (_SRC_DT // SSD_HEADS, 0))],
        out_specs=pl.BlockSpec((LANES, d), lambda i: (0, 0)),
        out_shape=jax.ShapeDtypeStruct((LANES, d), BF16),
        name="wprep_dt",
    )(wt)


def _norm_rows(x_ref, nw_ref, wdt_ref, xn_ref, dt_ref):
    x = x_ref[...]
    ms = jnp.mean(x * x, axis=-1, keepdims=True)
    xn = ((x * lax.rsqrt(ms + NORM_EPS)) * nw_ref[...]).astype(BF16)
    xn_ref[...] = xn
    dt_ref[...] = lax.dot_general(xn, wdt_ref[...], _NT, preferred_element_type=F32)


def _wprep_inproj_kernel(tbl_ref, x_ref, nw_ref, w_ref, wdt_ref, wout_ref, proj_ref, dt_ref, xn_ref):
    del tbl_ref

    @pl.when(pl.program_id(0) == 0)
    def _():
        _norm_rows(x_ref, nw_ref, wdt_ref, xn_ref, dt_ref)

    wb = w_ref[...].astype(BF16)
    wout_ref[...] = wb
    proj_ref[...] = lax.dot_general(xn_ref[...], wb, _NT, preferred_element_type=F32)


def _wprep_inproj(wt, w_dt, x, norm_w):
    m, d = x.shape
    ntiles = len(_SRC_TILES)
    assert ntiles * PROJ_TN == PROJ_COLS
    return pl.pallas_call(
        _wprep_inproj_kernel,
        grid_spec=pltpu.PrefetchScalarGridSpec(
            num_scalar_prefetch=1,
            grid=(ntiles,),
            in_specs=[
                pl.BlockSpec((m, d), lambda j, tbl: (0, 0)),
                pl.BlockSpec((1, d), lambda j, tbl: (0, 0)),
                pl.BlockSpec((pl.Element(PROJ_TN), pl.Element(d)),
                             lambda j, tbl: (pl.multiple_of(tbl[j] * W_ALIGN, W_ALIGN), 0)),
                pl.BlockSpec((LANES, d), lambda j, tbl: (0, 0)),
            ],
            out_specs=[
                pl.BlockSpec((PROJ_TN, d), lambda j, tbl: (j, 0)),
                pl.BlockSpec((m, PROJ_TN), lambda j, tbl: (0, j)),
                pl.BlockSpec((m, LANES), lambda j, tbl: (0, 0)),
            ],
            scratch_shapes=[pltpu.VMEM((m, d), BF16)],
        ),
        out_shape=[jax.ShapeDtypeStruct((PROJ_COLS, d), BF16), jax.ShapeDtypeStruct((m, PROJ_COLS), F32),
                   jax.ShapeDtypeStruct((m, LANES), F32)],
        compiler_params=_params("arbitrary"),
        name="wprep_inproj",
    )(jnp.asarray(_SRC_TILES, jnp.int32), x, norm_w, wt, w_dt)


def _inproj_kernel(x_ref, nw_ref, w_ref, wdt_ref, proj_ref, dt_ref, xn_ref):
    @pl.when(pl.program_id(1) == 0)
    def _():
        _norm_rows(x_ref, nw_ref, wdt_ref, xn_ref, dt_ref)

    proj_ref[...] = lax.dot_general(xn_ref[...], w_ref[...], _NT, preferred_element_type=F32)


def _inproj(x, norm_w, w_main, w_dt, tm):
    m, d = x.shape
    n = w_main.shape[0]
    return pl.pallas_call(
        _inproj_kernel,
        grid=(m // tm, n // PROJ_TN),
        in_specs=[
            pl.BlockSpec((tm, d), lambda i, j: (i, 0)),
            pl.BlockSpec((1, d), lambda i, j: (0, 0)),
            pl.BlockSpec((PROJ_TN, d), lambda i, j: (j, 0)),
            pl.BlockSpec((LANES, d), lambda i, j: (0, 0)),
        ],
        out_specs=[
            pl.BlockSpec((tm, PROJ_TN), lambda i, j: (i, j)),
            pl.BlockSpec((tm, LANES), lambda i, j: (i, 0)),
        ],
        out_shape=[jax.ShapeDtypeStruct((m, n), F32), jax.ShapeDtypeStruct((m, LANES), F32)],
        scratch_shapes=[pltpu.VMEM((tm, d), BF16)],
        compiler_params=_params("arbitrary", "arbitrary"),
        name="inproj",
    )(x, norm_w, w_main, w_dt)


def _ssd_chunk(r0, xbc_ref, za_ref, dtr_ref, convw_ref, convb_ref, dtb_ref, alog_ref, dexp_ref, nw_ref,
               y_ref, ht_scr, ext_scr, mask_rows, need_y):
    L = CHUNK
    P2 = 2 * SSD_HEAD_DIM
    GW = SSD_WIDTH // SSD_GROUPS
    time_rows = lambda t0, n: pl.ds(CONV_PITCH * t0, n, stride=CONV_PITCH)

    w = convw_ref[...]
    bias = convb_ref[...]
    conv_parts = []
    for s in range(SSD_XBC // LANES):
        ls = slice(s * LANES, (s + 1) * LANES)
        xc = xbc_ref[r0:r0 + L, ls]
        ext_scr[s, time_rows(SUBLANES, L), :] = xc
        acc = bias[:, ls]
        for k in range(SSD_CONV - 1):
            acc = acc + ext_scr[s, time_rows(SUBLANES - (SSD_CONV - 1) + k, L), :] * w[k:k + 1, ls]
        conv_parts.append(acc + xc * w[SSD_CONV - 1:SSD_CONV, ls])
        ext_scr[s, time_rows(0, SUBLANES), :] = xc[L - SUBLANES:L, :]
    xbc = _silu(jnp.concatenate(conv_parts, axis=1))
    xs = xbc[:, :SSD_WIDTH]
    bmat = xbc[:, SSD_WIDTH:SSD_WIDTH + SSD_GROUPS * SSD_STATE]
    cmat = xbc[:, SSD_WIDTH + SSD_GROUPS * SSD_STATE:]

    rows = lax.broadcasted_iota(jnp.int32, (L, L), 0)
    cols = lax.broadcasted_iota(jnp.int32, (L, L), 1)
    causal = rows >= cols
    lane_lo = cols < SSD_HEAD_DIM

    dt = _softplus(dtr_ref[r0:r0 + L, :] + dtb_ref[...])
    if mask_rows:
        dt = jnp.where(rows < mask_rows, 0.0, dt)
    a = dt * (-jnp.exp(alog_ref[...]))
    tril = jnp.where(causal, 1.0, 0.0).astype(BF16)
    a1, a2, a3 = _split3(a)
    acum = (jnp.dot(tril, a1, preferred_element_type=F32)
            + jnp.dot(tril, a2, preferred_element_type=F32)
            + jnp.dot(tril, a3, preferred_element_type=F32))
    a2 = acum * LOG2E
    e_cum = jnp.exp2(a2)
    w_end = dt * jnp.exp2(a2[L - 1:L, :] - a2)
    a2dt_t = (a2 - jnp.log2(dt)).T

    dexp = dexp_ref[...]
    xs_b = xs.astype(BF16)
    y_parts = []
    for g in range(SSD_GROUPS):
        bg = bmat[:, g * SSD_STATE:(g + 1) * SSD_STATE].astype(BF16)
        cg = cmat[:, g * SSD_STATE:(g + 1) * SSD_STATE].astype(BF16)
        ht_g = ht_scr[:, g * GW:(g + 1) * GW]
        if need_y:
            cb = lax.dot_general(cg, bg, (((1,), (1,)), ((), ())), preferred_element_type=F32)
            y_off = jnp.dot(cg, ht_g.astype(BF16), preferred_element_type=F32)
        xw_parts = []
        elast_parts = []
        for jj in range(GW // P2):
            lo = g * GW + jj * P2
            h0 = lo // SSD_HEAD_DIM
            yd, eb, wb = [], [], []
            for h in (h0, h0 + 1):
                if need_y:
                    colb = jnp.broadcast_to(a2[:, h:h + 1], (L, L))
                    m = cb * jnp.exp2(jnp.where(causal, colb - a2dt_t[h:h + 1, :], -jnp.inf))
                    yd.append(jnp.dot(m.astype(BF16), xs_b[:, lo:lo + P2], preferred_element_type=F32))
                eb.append(jnp.broadcast_to(e_cum[:, h:h + 1], (L, L)))
                wb.append(jnp.broadcast_to(w_end[:, h:h + 1], (L, L)))
            xs_pair = xs[:, lo:lo + P2]
            if need_y:
                y_pair = (jnp.where(lane_lo, yd[0], yd[1])
                          + y_off[:, jj * P2:(jj + 1) * P2] * jnp.where(lane_lo, eb[0], eb[1]))
                y_parts.append(y_pair + dexp[:, lo:lo + P2] * xs_pair)
            xw_parts.append(xs_pair * jnp.where(lane_lo, wb[0], wb[1]))
            elast_parts.append(jnp.where(lane_lo[0:1, :], eb[0][L - 1:L, :], eb[1][L - 1:L, :]))
        xw = jnp.concatenate(xw_parts, axis=1)
        elast = jnp.concatenate(elast_parts, axis=1)
        st = lax.dot_general(bg, xw.astype(BF16), (((0,), (0,)), ((), ())),
                             preferred_element_type=F32)
        ht_scr[:, g * GW:(g + 1) * GW] = ht_g * elast + st

    if not need_y:
        y_ref[r0:r0 + L, :] = jnp.zeros((L, SSD_WIDTH), y_ref.dtype)
        return
    y = jnp.concatenate(y_parts, axis=1)
    y = y * _silu(za_ref[r0:r0 + L, :])
    nw = nw_ref[...]
    outs = []
    for g in range(SSD_GROUPS):
        yg = y[:, g * GW:(g + 1) * GW]
        ms = jnp.mean(yg * yg, axis=-1, keepdims=True)
        outs.append((yg * lax.rsqrt(ms + NORM_EPS)) * nw[:, g * GW:(g + 1) * GW])
    y_ref[r0:r0 + L, :] = jnp.concatenate(outs, axis=1).astype(y_ref.dtype)


def _ssd_kernel(xbc_ref, za_ref, dtr_ref, ht0_ref, tail0_ref, convw_ref, convb_ref, dtb_ref, alog_ref,
                dexp_ref, nw_ref, y_ref, h_ref, ht_ref, tail_ref, ht_scr, ext_scr, *, mask_rows, need_y):
    c = pl.program_id(1)
    nrows = xbc_ref.shape[0]

    @pl.when(c == 0)
    def _():
        ht_scr[...] = ht0_ref[...]
        for s in range(SSD_XBC // LANES):
            ext_scr[s, pl.ds(0, SUBLANES, stride=CONV_PITCH), :] = tail0_ref[:, s * LANES:(s + 1) * LANES]

    for r0 in range(0, nrows, CHUNK):
        _ssd_chunk(r0, xbc_ref, za_ref, dtr_ref, convw_ref, convb_ref, dtb_ref, alog_ref, dexp_ref, nw_ref,
                   y_ref, ht_scr, ext_scr, mask_rows, need_y)

    @pl.when(c == pl.num_programs(1) - 1)
    def _():
        ht = ht_scr[...]
        ht_ref[...] = ht
        h_ref[...] = ht.T
        tail_ref[...] = xbc_ref[nrows - SUBLANES:nrows, :]


def _ssd(proj, dt_raw, ht0, tail0, conv_w, conv_b, dt_bias, a_log, d_exp, norm_w, *, batch, nchunks,
         row_block0, mask_rows, need_y=True):
    L = CHUNK
    rows = batch * nchunks * L
    rb = lambda b, c: row_block0 + b * nchunks + c
    const2 = lambda shape: pl.BlockSpec(shape, lambda b, c: (0, 0))
    return pl.pallas_call(
        functools.partial(_ssd_kernel, mask_rows=mask_rows, need_y=need_y),
        grid=(batch, nchunks),
        in_specs=[
            pl.BlockSpec((L, SSD_XBC), lambda b, c: (rb(b, c), COL_XBC // SSD_XBC)),
            pl.BlockSpec((L, SSD_WIDTH), lambda b, c: (rb(b, c), COL_ZA // SSD_WIDTH)),
            pl.BlockSpec((L, LANES), lambda b, c: (rb(b, c), 0)),
            const2((SSD_STATE, SSD_WIDTH)),
            const2((SUBLANES, SSD_XBC)),
            const2((SSD_CONV, SSD_XBC)),
            const2((1, SSD_XBC)),
            const2((1, LANES)),
            const2((1, LANES)),
            const2((1, SSD_WIDTH)),
            const2((1, SSD_WIDTH)),
        ],
        out_specs=[
            pl.BlockSpec((L, SSD_WIDTH), lambda b, c: (b * nchunks + c, 0)),
            pl.BlockSpec((None, SSD_WIDTH, SSD_STATE), lambda b, c: (b, 0, 0)),
            pl.BlockSpec((None, SSD_STATE, SSD_WIDTH), lambda b, c: (b, 0, 0)),
            pl.BlockSpec((None, SUBLANES, SSD_XBC), lambda b, c: (b, 0, 0)),
        ],
        out_shape=[
            jax.ShapeDtypeStruct((rows, SSD_WIDTH), BF16),
            jax.ShapeDtypeStruct((batch, SSD_WIDTH, SSD_STATE), F32),
            jax.ShapeDtypeStruct((batch, SSD_STATE, SSD_WIDTH), F32),
            jax.ShapeDtypeStruct((batch, SUBLANES, SSD_XBC), F32),
        ],
        scratch_shapes=[pltpu.VMEM((SSD_STATE, SSD_WIDTH), F32),
                        pltpu.VMEM((SSD_XBC // LANES, CONV_PITCH * (SUBLANES + CHUNK), LANES), F32)],
        compiler_params=_params("arbitrary", "arbitrary"),
        name="ssd",
    )(proj, proj, dt_raw, ht0, tail0, conv_w, conv_b, dt_bias, a_log, d_exp, norm_w)


def _ssd_step_kernel(xbc_ref, za_ref, dtr_ref, cs_ref, h_ref, convw_ref, convb_ref, dtb_ref, alog_ref,
                     dexp_ref, nw_ref, y_ref, hout_ref, csout_ref):
    R = SUBLANES
    GW = SSD_WIDTH // SSD_GROUPS
    x = xbc_ref[...]
    w = convw_ref[...]
    s0 = cs_ref[0]
    s1 = cs_ref[1]
    s2 = cs_ref[2]
    conv = convb_ref[...] + s0 * w[0:1]
    conv = conv + s1 * w[1:2]
    conv = conv + s2 * w[2:3]
    conv = conv + x * w[3:4]
    csout_ref[0] = s1
    csout_ref[1] = s2
    csout_ref[2] = x
    xbc = _silu(conv)
    xs = xbc[:, :SSD_WIDTH]
    bmat = xbc[:, SSD_WIDTH:SSD_WIDTH + SSD_GROUPS * SSD_STATE]
    cmat = xbc[:, SSD_WIDTH + SSD_GROUPS * SSD_STATE:]

    dt = _softplus(dtr_ref[...] + dtb_ref[...])
    da = jnp.exp(dt * (-jnp.exp(alog_ref[...])))
    dt_t = dt.T
    da_t = da.T
    expand = lambda v: jnp.concatenate(
        [jnp.broadcast_to(v[h:h + 1, :], (SSD_HEAD_DIM, R)) for h in range(SSD_HEADS)], axis=0)
    xd_t = xs.T * expand(dt_t)
    da_te = expand(da_t)

    cmat_b = cmat.astype(BF16)
    yrows = []
    for i in range(R):
        bexp = jnp.concatenate(
            [jnp.broadcast_to(bmat[i:i + 1, g * SSD_STATE:(g + 1) * SSD_STATE], (GW, SSD_STATE))
             for g in range(SSD_GROUPS)], axis=0)
        hn = h_ref[i] * da_te[:, i:i + 1] + xd_t[:, i:i + 1] * bexp
        hout_ref[i] = hn
        cg = jnp.concatenate([cmat_b[i:i + 1, g * SSD_STATE:(g + 1) * SSD_STATE] for g in range(SSD_GROUPS)]
                             + [jnp.zeros((R - SSD_GROUPS, SSD_STATE), BF16)], axis=0)
        yg = lax.dot_general(cg, hn.astype(BF16), (((1,), (1,)), ((), ())), preferred_element_type=F32)
        yrows.append(jnp.concatenate([yg[g:g + 1, g * GW:(g + 1) * GW] for g in range(SSD_GROUPS)], axis=1))
    y = jnp.concatenate(yrows, axis=0)
    y = y + dexp_ref[...] * xs
    y = y * _silu(za_ref[...])
    nw = nw_ref[...]
    outs = []
    for g in range(SSD_GROUPS):
        yg = y[:, g * GW:(g + 1) * GW]
        ms = jnp.mean(yg * yg, axis=-1, keepdims=True)
        outs.append((yg * lax.rsqrt(ms + NORM_EPS)) * nw[:, g * GW:(g + 1) * GW])
    y_ref[...] = jnp.concatenate(outs, axis=1).astype(y_ref.dtype)


def _ssd_step(proj, dt_raw, conv_state, ssd_state, conv_w, conv_b, dt_bias, a_log, d_exp, norm_w, *, nseq):
    R = SUBLANES
    const2 = lambda shape: pl.BlockSpec(shape, lambda i: (0, 0))
    return pl.pallas_call(
        _ssd_step_kernel,
        grid=(nseq // R,),
        in_specs=[
            pl.BlockSpec((R, SSD_XBC), lambda i: (i, COL_XBC // SSD_XBC)),
            pl.BlockSpec((R, SSD_WIDTH), lambda i: (i, COL_ZA // SSD_WIDTH)),
            pl.BlockSpec((R, LANES), lambda i: (i, 0)),
            pl.BlockSpec((SSD_CONV - 1, R, SSD_XBC), lambda i: (0, i, 0)),
            pl.BlockSpec((R, SSD_WIDTH, SSD_STATE), lambda i: (i, 0, 0)),
            const2((SSD_CONV, SSD_XBC)),
            const2((1, SSD_XBC)),
            const2((1, LANES)),
            const2((1, LANES)),
            const2((1, SSD_WIDTH)),
            const2((1, SSD_WIDTH)),
        ],
        out_specs=[
            pl.BlockSpec((R, SSD_WIDTH), lambda i: (i, 0)),
            pl.BlockSpec((R, SSD_WIDTH, SSD_STATE), lambda i: (i, 0, 0)),
            pl.BlockSpec((SSD_CONV - 1, R, SSD_XBC), lambda i: (0, i, 0)),
        ],
        out_shape=[
            jax.ShapeDtypeStruct((nseq, SSD_WIDTH), BF16),
            jax.ShapeDtypeStruct((nseq, SSD_WIDTH, SSD_STATE), F32),
            jax.ShapeDtypeStruct((SSD_CONV - 1, nseq, SSD_XBC), F32),
        ],
        compiler_params=_params("arbitrary"),
        name="ssd_step",
    )(proj, proj, dt_raw, conv_state, ssd_state, conv_w, conv_b, dt_bias, a_log, d_exp, norm_w)


def _s5_block_diag(w, rows_per_group, cols_per_group):
    ngrp, r, lanes = w.shape
    width = ngrp * cols_per_group
    tiled = jnp.concatenate([w.reshape(ngrp * r, lanes)] * (width // lanes), axis=1)
    row_g = lax.broadcasted_iota(jnp.int32, tiled.shape, 0) // rows_per_group
    col_g = lax.broadcasted_iota(jnp.int32, tiled.shape, 1) // cols_per_group
    return jnp.where(row_g == col_g, tiled, 0.0)


def _s5_weights(bbr_ref, bbi_ref, cr_ref, ci_ref):
    return tuple(_s5_block_diag(r[...], S5_GROUP, S5_STATE) for r in (bbr_ref, bbi_ref, cr_ref, ci_ref))


def _s5_pitch(seg_len):
    return seg_len if (seg_len // SUBLANES) % 2 else seg_len + SUBLANES


def _s5_kernel(*refs, seq, nconv, need_y):
    (u_ref, h0r_ref, h0i_ref, bbr_ref, bbi_ref, ctr_ref, cti_ref, ar_ref, ai_ref, d_ref) = refs[:10]
    w_in_refs = refs[10:10 + nconv]
    y_ref, hr_ref, hi_ref = refs[10 + nconv:13 + nconv]
    w_out_refs = refs[13 + nconv:13 + 2 * nconv]
    pad, pe, po, cr, ci, yeh, hb_out, w2r_scr, w2i_scr, cta_scr, ctb_scr, k0_scr = refs[13 + 2 * nconv:]
    for wi, wo in zip(w_in_refs, w_out_refs):
        wo[...] = wi[...].astype(BF16)

    NS = SUBLANES
    SL = seq // NS
    H = SL // 2
    pitch = _s5_pitch(SL)
    nk = S5_BLK_U // LANES
    n = S5_BLK_N
    a_re = ar_ref[...]
    a_im = ai_ref[...]
    bf = lambda v: v.astype(BF16)

    @pl.when(pl.program_id(1) == 0)
    def _():
        bt_re, bt_im, ct_re, ct_im = _s5_weights(bbr_ref, bbi_ref, ctr_ref, cti_ref)
        w2r_scr[...] = jnp.concatenate([bf(bt_re * a_re - bt_im * a_im), bf(bt_re)], axis=0)
        w2i_scr[...] = jnp.concatenate([bf(bt_re * a_im + bt_im * a_re), bf(bt_im)], axis=0)
        if need_y:
            cta_scr[0] = bf(ct_re * a_re - ct_im * a_im)
            cta_scr[1] = bf(ct_re * a_im + ct_im * a_re)
            ctb_scr[0] = bf(ct_re)
            ctb_scr[1] = bf(ct_im)
            k0_scr[...] = bf(lax.dot_general(bf(bt_re), bf(ct_re), _NT, preferred_element_type=F32)
                             - lax.dot_general(bf(bt_im), bf(ct_im), _NT, preferred_element_type=F32))

    for s in range(NS):
        for k in range(nk):
            pad[k, s * pitch:s * pitch + SL, :] = u_ref[s * SL:(s + 1) * SL, k * LANES:(k + 1) * LANES]

    for jp in range(H):
        for k in range(nk):
            pe[jp * NS:(jp + 1) * NS, k * LANES:(k + 1) * LANES] = pad[k, pl.ds(2 * jp, NS, stride=pitch), :]
            po[jp * NS:(jp + 1) * NS, k * LANES:(k + 1) * LANES] = pad[k, pl.ds(2 * jp + 1, NS, stride=pitch), :]
    half = seq // 2
    NB = 4 if H % 8 == 0 else 1
    rbk = half // NB
    spb = rbk // NS

    qc = 4 * S5_GROUP
    qn = 4 * S5_STATE

    def input_block(kb):
        rows = slice(kb * rbk, (kb + 1) * rbk)
        ue_b = bf(pe[rows, :])
        uo_b = bf(po[rows, :])
        for q in range(S5_BLK_U // qc):
            u_q = jnp.concatenate([ue_b[:, q * qc:(q + 1) * qc], uo_b[:, q * qc:(q + 1) * qc]], axis=1)
            rows_q = [slice(q * qc, (q + 1) * qc), slice(S5_BLK_U + q * qc, S5_BLK_U + (q + 1) * qc)]
            cols_q = slice(q * qn, (q + 1) * qn)
            wq_re = jnp.concatenate([w2r_scr[r, cols_q] for r in rows_q], axis=0)
            wq_im = jnp.concatenate([w2i_scr[r, cols_q] for r in rows_q], axis=0)
            cr[rows, cols_q] = jnp.dot(u_q, wq_re, preferred_element_type=F32)
            ci[rows, cols_q] = jnp.dot(u_q, wq_im, preferred_element_type=F32)

    a2_re = a_re * a_re - a_im * a_im
    a2_im = 2.0 * (a_re * a_im)
    ar = jnp.broadcast_to(a2_re, (NS, n))
    ai = jnp.broadcast_to(a2_im, (NS, n))

    def step(carry, r0):
        hr, hi = carry
        nr = ar * hr - ai * hi + cr[r0:r0 + NS, :]
        ni = ar * hi + ai * hr + ci[r0:r0 + NS, :]
        return nr, ni

    zero = jnp.zeros((NS, n), F32)
    carry = (zero, zero)
    input_block(0)
    for kb in range(NB):
        if kb + 1 < NB:
            input_block(kb + 1)
        for t in range(spb):
            carry = step(carry, (kb * spb + t) * NS)
    er, ei = carry

    asr, asi = a_re, a_im
    for _ in range(SL.bit_length() - 1):
        asr, asi = asr * asr - asi * asi, 2.0 * (asr * asi)
    gr = [h0r_ref[...]]
    gi = [h0i_ref[...]]
    for s in range(NS):
        gr.append(er[s:s + 1, :] + (asr * gr[s] - asi * gi[s]))
        gi.append(ei[s:s + 1, :] + (asr * gi[s] + asi * gr[s]))
    hr_ref[...] = gr[NS]
    hi_ref[...] = gi[NS]
    if not need_y:
        y_ref[...] = jnp.zeros(y_ref.shape, y_ref.dtype)
        return

    def scan_block(kb, c0):
        for p in range(spb // 2):
            r0 = (kb * spb + 2 * p) * NS
            c1 = step(c0, r0)
            c2 = step(c1, r0 + NS)
            hb_out[r0:r0 + 2 * NS, 0:n] = bf(jnp.concatenate([c1[0], c2[0]], axis=0))
            hb_out[r0:r0 + 2 * NS, n:2 * n] = bf(jnp.concatenate([c1[1], c2[1]], axis=0))
            c0 = c2
        return c0

    d = d_ref[...]
    cta_re, cta_im = cta_scr[0], cta_scr[1]
    ctb_re, ctb_im = ctb_scr[0], ctb_scr[1]
    k0 = k0_scr[...]

    def z_weights(q, part):
        sl = (slice(q * qc, (q + 1) * qc), slice(q * qn, (q + 1) * qn))
        return jnp.concatenate([ctb_scr[part][sl], cta_scr[part][sl]], axis=0)

    g_re = bf(jnp.concatenate(gr[:NS], axis=0))
    g_im = bf(jnp.concatenate(gi[:NS], axis=0))
    yeh[0:NS, :] = (lax.dot_general(g_re, cta_re, _NT, preferred_element_type=F32)
                    - lax.dot_general(g_im, cta_im, _NT, preferred_element_type=F32))

    def output_block(kb):
        rows = slice(kb * rbk, (kb + 1) * rbk)
        zs = []
        for q in range(S5_BLK_U // qc):
            cols_q = slice(q * qn, (q + 1) * qn)
            zs.append(lax.dot_general(hb_out[rows, cols_q], z_weights(q, 0), _NT, preferred_element_type=F32)
                      - lax.dot_general(hb_out[rows, n + q * qn:n + (q + 1) * qn], z_weights(q, 1), _NT,
                                        preferred_element_type=F32))
        y_odd = jnp.concatenate([z[:, 0:qc] for z in zs], axis=1)
        yeh[NS + kb * rbk:NS + (kb + 1) * rbk, :] = jnp.concatenate([z[:, qc:2 * qc] for z in zs], axis=1)
        y_even = yeh[rows, :] + jnp.dot(bf(pe[rows, :]), k0, preferred_element_type=F32)
        po[rows, :] = _gelu_tanh(y_odd + d * po[rows, :])
        pe[rows, :] = _gelu_tanh(y_even + d * pe[rows, :])

    carry = scan_block(0, (jnp.concatenate(gr[:NS], axis=0), jnp.concatenate(gi[:NS], axis=0)))
    for kb in range(NB):
        if kb + 1 < NB:
            carry = scan_block(kb + 1, carry)
        output_block(kb)

    for jp in range(H):
        for k in range(nk):
            pad[k, pl.ds(2 * jp, NS, stride=pitch), :] = pe[jp * NS:(jp + 1) * NS, k * LANES:(k + 1) * LANES]
            pad[k, pl.ds(2 * jp + 1, NS, stride=pitch), :] = po[jp * NS:(jp + 1) * NS, k * LANES:(k + 1) * LANES]
    for s in range(NS):
        for k in range(nk):
            y_ref[s * SL:(s + 1) * SL, k * LANES:(k + 1) * LANES] = pad[k, s * pitch:s * pitch + SL, :]


def _s5(proj, h0r, h0i, bb_re, bb_im, ct_re, ct_im, ab_re, ab_im, d_s5, *, batch, seq, row_block0, need_y=True,
        convert=()):
    ub0 = COL_UB // S5_BLK_U
    nsteps = batch * S5_BLOCKS
    slice_spec = lambda w: pl.BlockSpec((w.shape[0] // nsteps, w.shape[1]), lambda j, b: (j * batch + b, 0))
    assert all(w.shape[0] % (2 * SUBLANES * nsteps) == 0 for w in convert)
    seg = seq // SUBLANES
    assert seg & (seg - 1) == 0
    gpb = S5_GROUPS // S5_BLOCKS
    vec = lambda width: pl.BlockSpec((1, width), lambda j, b: (0, j))
    wspec = lambda r: pl.BlockSpec((gpb, r, LANES), lambda j, b: (j, 0, 0))
    st_out = pl.BlockSpec((None, 1, S5_BLK_N), lambda j, b: (b, 0, j))
    return pl.pallas_call(
        functools.partial(_s5_kernel, seq=seq, nconv=len(convert), need_y=need_y),
        grid=(S5_BLOCKS, batch),
        in_specs=[
            pl.BlockSpec((seq, S5_BLK_U), lambda j, b: (row_block0 + b, ub0 + j)),
            vec(S5_BLK_N), vec(S5_BLK_N),
            wspec(S5_GROUP), wspec(S5_GROUP), wspec(S5_GROUP), wspec(S5_GROUP),
            vec(S5_BLK_N), vec(S5_BLK_N), vec(S5_BLK_U),
        ] + [slice_spec(w) for w in convert],
        out_specs=[pl.BlockSpec((seq, S5_BLK_U), lambda j, b: (b, j)), st_out, st_out]
        + [slice_spec(w) for w in convert],
        out_shape=[
            jax.ShapeDtypeStruct((batch * seq, S5_WIDTH), F32),
            jax.ShapeDtypeStruct((batch, 1, S5_NSTATE), F32),
            jax.ShapeDtypeStruct((batch, 1, S5_NSTATE), F32),
        ] + [jax.ShapeDtypeStruct(w.shape, BF16) for w in convert],
        scratch_shapes=[
            pltpu.VMEM((S5_BLK_U // LANES, SUBLANES * _s5_pitch(seg), LANES), F32),
            pltpu.VMEM((seq // 2, S5_BLK_U), F32),
            pltpu.VMEM((seq // 2, S5_BLK_U), F32),
            pltpu.VMEM((seq // 2, S5_BLK_N), F32),
            pltpu.VMEM((seq // 2, S5_BLK_N), F32),
            pltpu.VMEM((seq // 2 + SUBLANES, S5_BLK_U), F32),
            pltpu.VMEM((seq // 2, 2 * S5_BLK_N), BF16),
            pltpu.VMEM((2 * S5_BLK_U, S5_BLK_N), BF16),
            pltpu.VMEM((2 * S5_BLK_U, S5_BLK_N), BF16),
            pltpu.VMEM((2, S5_BLK_U, S5_BLK_N), BF16),
            pltpu.VMEM((2, S5_BLK_U, S5_BLK_N), BF16),
            pltpu.VMEM((S5_BLK_U, S5_BLK_U), BF16),
        ],
        compiler_params=_params("arbitrary", "arbitrary"),
        name="s5",
    )(proj, h0r, h0i, bb_re, bb_im, ct_re, ct_im, ab_re, ab_im, d_s5, *convert)


def _s5_step_kernel(u_ref, h0r_ref, h0i_ref, bbr_ref, bbi_ref, ctr_ref, cti_ref, ar_ref, ai_ref, d_ref,
                    y_ref, hr_ref, hi_ref):
    bt_re, bt_im, ct_re, ct_im = (w.astype(BF16) for w in _s5_weights(bbr_ref, bbi_ref, ctr_ref, cti_ref))
    u = u_ref[...]
    ub = u.astype(BF16)
    ar = ar_ref[...]
    ai = ai_ref[...]
    h0r = h0r_ref[...].T
    h0i = h0i_ref[...].T
    hr = jnp.dot(ub, bt_re, preferred_element_type=F32) + (ar * h0r - ai * h0i)
    hi = jnp.dot(ub, bt_im, preferred_element_type=F32) + (ar * h0i + ai * h0r)
    hr_ref[...] = hr.T
    hi_ref[...] = hi.T
    y = (lax.dot_general(hr.astype(BF16), ct_re, _NT, preferred_element_type=F32)
         - lax.dot_general(hi.astype(BF16), ct_im, _NT, preferred_element_type=F32))
    y_ref[...] = _gelu_tanh(y + d_ref[...] * u)


def _s5_step(proj, h0r, h0i, bb_re, bb_im, ct_re, ct_im, ab_re, ab_im, d_s5, *, nseq):
    ub0 = COL_UB // S5_BLK_U
    gpb = S5_GROUPS // S5_BLOCKS
    vec = lambda width: pl.BlockSpec((1, width), lambda j: (0, j))
    wspec = lambda r: pl.BlockSpec((gpb, r, LANES), lambda j: (j, 0, 0))
    st = pl.BlockSpec((S5_BLK_N, nseq), lambda j: (j, 0))
    return pl.pallas_call(
        _s5_step_kernel,
        grid=(S5_BLOCKS,),
        in_specs=[
            pl.BlockSpec((nseq, S5_BLK_U), lambda j: (0, ub0 + j)),
            st, st,
            wspec(S5_GROUP), wspec(S5_GROUP), wspec(S5_GROUP), wspec(S5_GROUP),
            vec(S5_BLK_N), vec(S5_BLK_N), vec(S5_BLK_U),
        ],
        out_specs=[pl.BlockSpec((nseq, S5_BLK_U), lambda j: (0, j)), st, st],
        out_shape=[
            jax.ShapeDtypeStruct((nseq, S5_WIDTH), F32),
            jax.ShapeDtypeStruct((S5_NSTATE, nseq), F32),
            jax.ShapeDtypeStruct((S5_NSTATE, nseq), F32),
        ],
        compiler_params=_params("arbitrary"),
        name="s5_step",
    )(proj, h0r, h0i, bb_re, bb_im, ct_re, ct_im, ab_re, ab_im, d_s5)


def _tail_kernel(yn_ref, ybg_ref, zb_ref, ga_ref, gb_ref, x_ref, wpa_ref, wglu_ref, bglu_ref, wpb_ref,
                 wout_ref, fnw_ref, out_ref):
    sg_a = _sigmoid(ga_ref[...])
    sg_b = _sigmoid(gb_ref[...])
    sz_b = _silu(zb_ref[...])
    yb = ybg_ref[...]
    glu = jnp.dot(yb.astype(BF16), wglu_ref[...], preferred_element_type=F32) + bglu_ref[...]
    ya = jnp.dot(yn_ref[...], wpa_ref[...], preferred_element_type=F32)
    yb = (yb * _sigmoid(glu)) * sz_b
    ybp = jnp.dot(yb.astype(BF16), wpb_ref[...], preferred_element_type=F32)
    mixed = sg_a * ya + sg_b * ybp
    o = x_ref[...] + jnp.dot(mixed.astype(BF16), wout_ref[...], preferred_element_type=F32)
    ms = jnp.mean(o * o, axis=-1, keepdims=True)
    out_ref[...] = (o * lax.rsqrt(ms + NORM_EPS)) * fnw_ref[...]


def _tail(yn, ybg, proj, x, w_proj_a, w_glu, b_glu, w_proj_b, w_out, final_norm_w, *, tm):
    m, d = x.shape
    resident = lambda shape: pl.BlockSpec(shape, lambda i: (0, 0), pipeline_mode=pl.Buffered(1))
    return pl.pallas_call(
        _tail_kernel,
        grid=(m // tm,),
        in_specs=[
            pl.BlockSpec((tm, SSD_WIDTH), lambda i: (i, 0)),
            pl.BlockSpec((tm, S5_WIDTH), lambda i: (i, 0)),
            pl.BlockSpec((tm, S5_WIDTH), lambda i: (i, COL_ZB // S5_WIDTH)),
            pl.BlockSpec((tm, d), lambda i: (i, COL_GA // D_MODEL)),
            pl.BlockSpec((tm, d), lambda i: (i, COL_GB // D_MODEL)),
            pl.BlockSpec((tm, d), lambda i: (i, 0)),
            resident((SSD_WIDTH, d)),
            resident((S5_WIDTH, S5_WIDTH)),
            resident((1, S5_WIDTH)),
            resident((S5_WIDTH, d)),
            resident((d, d)),
            resident((1, d)),
        ],
        out_specs=pl.BlockSpec((tm, d), lambda i: (i, 0)),
        out_shape=jax.ShapeDtypeStruct((m, d), F32),
        compiler_params=_params("arbitrary"),
        name="tail",
    )(yn, ybg, proj, proj, proj, x, w_proj_a, w_glu, b_glu, w_proj_b, w_out, final_norm_w)


def kernel(x_prompt, x_sample, state_ssd, state_conv, state_s5_re, state_s5_im, meta_tokens, norm_w, w_in,
           conv_w, conv_b, dt_bias, a_log, d_ssd, ssd_norm_w, w_proj_a, lam_re, lam_im, log_dt_s5, b_re, b_im,
           c_re, c_im, d_s5, w_glu, b_glu, w_proj_b, w_out, final_norm_w):
    bsz, seq, d = x_prompt.shape
    nseq = x_sample.shape[0]
    assert d == D_MODEL and seq % CHUNK == 0 and nseq % SUBLANES == 0 and norm_w.shape[0] == 1
    assert meta_tokens.shape[0] == N_META and N_META <= CHUNK

    assert w_in.shape[2] == _SRC_GB + D_MODEL
    wt = jnp.transpose(w_in[0])
    w_dt = _wprep_dt(wt)
    pad_heads = lambda v: jnp.pad(v.reshape(1, SSD_HEADS), ((0, 0), (0, LANES - SSD_HEADS)))
    dtb = pad_heads(dt_bias[0])
    alog = pad_heads(a_log[0])
    d_exp = jnp.repeat(d_ssd[0], SSD_HEAD_DIM).reshape(1, SSD_WIDTH)
    nw1 = norm_w[0].reshape(1, d)
    ssd_nw = ssd_norm_w[0].reshape(1, SSD_WIDTH)
    convw = conv_w[0]
    convb = conv_b[0].reshape(1, SSD_XBC)
    bglu = b_glu[0].reshape(1, S5_WIDTH)
    fnw = final_norm_w.reshape(1, d)
    ds5 = d_s5[0].reshape(1, S5_WIDTH)

    rep = lambda v, k: jnp.concatenate([v] * k, axis=-1)
    lane_rep = LANES // S5_STATE
    ab_re, ab_im, bb_re, bb_im = _s5prep(
        rep(lam_re[0], lane_rep), rep(lam_im[0], lane_rep), log_dt_s5[0].reshape(S5_GROUPS, 1),
        rep(jnp.transpose(b_re[0], (0, 2, 1)), lane_rep), rep(jnp.transpose(b_im[0], (0, 2, 1)), lane_rep))
    ab_re = ab_re[:, :S5_STATE].reshape(1, S5_NSTATE)
    ab_im = ab_im[:, :S5_STATE].reshape(1, S5_NSTATE)
    ct_re = rep(c_re[0], lane_rep)
    ct_im = rep(c_im[0], lane_rep)

    x_main = x_prompt.reshape(bsz * seq, d)
    x_small = jnp.concatenate(
        [x_sample.reshape(nseq, d), jnp.zeros((CHUNK - N_META, d), x_prompt.dtype),
         meta_tokens.astype(x_prompt.dtype)], axis=0)
    assert nseq % CHUNK == 0
    meta_blk = nseq // CHUNK
    tm_main = 1024 if (bsz * seq) % 1024 == 0 else CHUNK
    w_main, proj_s, dt_s = _wprep_inproj(wt, w_dt, x_small, nw1)
    proj_m, dt_m = _inproj(x_main, nw1, w_main, w_dt, tm=tm_main)

    ssd_args = (convw, convb, dtb, alog, d_exp, ssd_nw)
    s5_w = (bb_re, bb_im, ct_re, ct_im)
    s5_v = (ab_re, ab_im, ds5)

    zeros_ht = jnp.zeros((SSD_STATE, SSD_WIDTH), F32)
    zeros_tail = jnp.zeros((SUBLANES, SSD_XBC), F32)
    zeros_s5 = jnp.zeros((1, S5_NSTATE), F32)
    _, _, ht_meta, tail_meta = _ssd(proj_s, dt_s, zeros_ht, zeros_tail, *ssd_args, batch=1, nchunks=1,
                                    row_block0=meta_blk, mask_rows=CHUNK - N_META, need_y=False)
    _, s5r_meta, s5i_meta = _s5(proj_s, zeros_s5, zeros_s5, *s5_w, *s5_v, batch=1, seq=CHUNK,
                                row_block0=meta_blk, need_y=False)[:3]

    yn_m, h_m, _, tail_m = _ssd(proj_m, dt_m, ht_meta[0], tail_meta[0], *ssd_args, batch=bsz,
                                nchunks=seq // CHUNK, row_block0=0, mask_rows=0)
    ybg_m, s5r_m, s5i_m, wpa, wglu, wpb, wout = _s5(
        proj_m, s5r_meta[0], s5i_meta[0], *s5_w, *s5_v, batch=bsz, seq=seq, row_block0=0,
        convert=(w_proj_a[0], w_glu[0], w_proj_b[0], w_out[0]))
    tail_w = (wpa, wglu, bglu, wpb, wout, fnw)
    y_prompt = _tail(yn_m, ybg_m, proj_m, x_main, *tail_w, tm=256)

    yn_s, h_s, cs_s = _ssd_step(proj_s, dt_s, jnp.transpose(state_conv[0], (1, 0, 2)),
                                state_ssd[0].reshape(nseq, SSD_WIDTH, SSD_STATE), *ssd_args, nseq=nseq)
    seq_minor = lambda v: jnp.transpose(v, (1, 2, 0)).reshape(S5_NSTATE, nseq)
    seq_major = lambda v: jnp.transpose(v.reshape(S5_GROUPS, S5_STATE, nseq), (2, 0, 1))[None]
    ybg_s, s5r_s, s5i_s = _s5_step(proj_s, seq_minor(state_s5_re[0]), seq_minor(state_s5_im[0]),
                                   *s5_w, *s5_v, nseq=nseq)
    y_sample = _tail(yn_s, ybg_s, proj_s, x_sample.reshape(nseq, d), *tail_w, tm=nseq)

    dt_out = x_prompt.dtype
    return (
        y_prompt.reshape(bsz, seq, d),
        y_sample.reshape(nseq, 1, d),
        h_m.reshape(1, bsz, SSD_HEADS, SSD_HEAD_DIM, SSD_STATE).astype(dt_out),
        tail_m[:, SUBLANES - (SSD_CONV - 1):, :].reshape(1, bsz, SSD_CONV - 1, SSD_XBC),
        s5r_m.reshape(1, bsz, S5_GROUPS, S5_STATE).astype(dt_out),
        s5i_m.reshape(1, bsz, S5_GROUPS, S5_STATE).astype(dt_out),
        h_s.reshape(1, nseq, SSD_HEADS, SSD_HEAD_DIM, SSD_STATE).astype(dt_out),
        jnp.transpose(cs_s, (1, 0, 2)).reshape(1, nseq, SSD_CONV - 1, SSD_XBC),
        seq_major(s5r_s).astype(dt_out),
        seq_major(s5i_s).astype(dt_out),
    )
```

```python
import functools

import jax
import jax.numpy as jnp
from jax import lax
from jax.experimental import pallas as pl
from jax.experimental.pallas import tpu as pltpu

F32 = jnp.float32
BF16 = jnp.bfloat16

NORM_EPS = 1e-5
LOG2E = 1.4426950408889634
N_META = 16
D_MODEL = 2048
SSD_HEAD_DIM = 64
SSD_HEADS = 32
SSD_GROUPS = 4
SSD_STATE = 128
SSD_WIDTH = SSD_HEADS * SSD_HEAD_DIM
SSD_XBC = SSD_WIDTH + 2 * SSD_GROUPS * SSD_STATE
SSD_CONV = 4
CHUNK = 128
S5_WIDTH = D_MODEL // 2
S5_GROUP = 16
S5_GROUPS = S5_WIDTH // S5_GROUP
S5_STATE = 64
S5_NSTATE = S5_GROUPS * S5_STATE
S5_BLOCKS = 4
S5_BLK_U = S5_WIDTH // S5_BLOCKS
S5_BLK_N = S5_NSTATE // S5_BLOCKS

LANES = 128
SUBLANES = 8
VMEM_LIMIT = 56 * 1024 * 1024
CONV_PITCH = 3

COL_ZA = 0
COL_GA = COL_ZA + SSD_WIDTH
COL_GB = COL_GA + D_MODEL
COL_XBC = COL_GB + D_MODEL
COL_UB = COL_XBC + SSD_XBC
COL_ZB = COL_UB + S5_WIDTH
PROJ_COLS = COL_ZB + S5_WIDTH
PROJ_TN = 1024


_NT = (((1,), (1,)), ((), ()))


def _sigmoid(x):
    return 0.5 * jnp.tanh(0.5 * x) + 0.5


def _silu(x):
    h = 0.5 * x
    return h + h * jnp.tanh(h)


def _softplus(x):
    return jnp.maximum(x, 0.0) + jnp.log1p(jnp.exp(-jnp.abs(x)))


def _gelu_tanh(x):
    c = 0.7978845608028654
    return 0.5 * x * (1.0 + jnp.tanh(c * (x + 0.044715 * (x * x * x))))


def _split3(x):
    x1 = x.astype(BF16)
    r1 = x - x1.astype(F32)
    x2 = r1.astype(BF16)
    x3 = (r1 - x2.astype(F32)).astype(BF16)
    return x1, x2, x3


def _params(*sem):
    return pltpu.CompilerParams(dimension_semantics=sem, vmem_limit_bytes=VMEM_LIMIT)


def _s5prep_kernel(lr_ref, li_ref, ldt_ref, btr_ref, bti_ref, abr_ref, abi_ref, bbr_ref, bbi_ref):
    lr = lr_ref[...]
    li = li_ref[...]
    step = jnp.exp(ldt_ref[...])
    mag = jnp.exp(lr * step)
    abr = mag * jnp.cos(li * step)
    abi = mag * jnp.sin(li * step)
    den = lr * lr + li * li
    numr = abr - 1.0
    cr = (numr * lr + abi * li) / den
    ci = (abi * lr - numr * li) / den
    abr_ref[...] = abr
    abi_ref[...] = abi
    btr = btr_ref[...]
    bti = bti_ref[...]
    crb = cr[:, None, :]
    cib = ci[:, None, :]
    bbr_ref[...] = crb * btr - cib * bti
    bbi_ref[...] = crb * bti + cib * btr


def _s5prep(lam_re, lam_im, log_dt, bt_re, bt_im):
    g, n = lam_re.shape
    full2 = pl.BlockSpec((g, n), lambda: (0, 0))
    full3 = pl.BlockSpec((g, S5_GROUP, n), lambda: (0, 0, 0))
    return pl.pallas_call(
        _s5prep_kernel,
        in_specs=[full2, full2, pl.BlockSpec((g, 1), lambda: (0, 0)), full3, full3],
        out_specs=[full2, full2, full3, full3],
        out_shape=[jax.ShapeDtypeStruct((g, n), F32)] * 2
        + [jax.ShapeDtypeStruct((g, S5_GROUP, n), F32)] * 2,
        name="s5prep",
    )(lam_re, lam_im, log_dt, bt_re, bt_im)


W_ALIGN = 32
_SRC_ZA = 0
_SRC_XBC = SSD_WIDTH
_SRC_DT = _SRC_XBC + SSD_XBC
_SRC_UB = _SRC_DT + SSD_HEADS
_SRC_ZB = _SRC_UB + S5_WIDTH
_SRC_GA = _SRC_ZB + S5_WIDTH
_SRC_GB = _SRC_GA + D_MODEL
_SEGMENTS = ((_SRC_ZA, SSD_WIDTH), (_SRC_GA, D_MODEL), (_SRC_GB, D_MODEL), (_SRC_XBC, SSD_XBC),
             (_SRC_UB, S5_WIDTH), (_SRC_ZB, S5_WIDTH))
_SRC_TILES = tuple((start + k) // W_ALIGN for start, width in _SEGMENTS for k in range(0, width, PROJ_TN))


def _wprep_dt_kernel(w_ref, o_ref):
    o_ref[...] = jnp.zeros(o_ref.shape, o_ref.dtype)
    o_ref[0:SSD_HEADS, :] = w_ref[...].astype(BF16)


def _wprep_dt(wt):
    d = wt.shape[1]
    assert _SRC_DT % SSD_HEADS == 0
    return pl.pallas_call(
        _wprep_dt_kernel,
        grid=(1,),
        in_specs=[pl.BlockSpec((SSD_HEADS, d), lambda i: (_SRC_DT // SSD_HEADS, 0))],
        out_specs=pl.BlockSpec((LANES, d), lambda i: (0, 0)),
        out_shape=jax.ShapeDtypeStruct((LANES, d), BF16),
        name="wprep_dt",
    )(wt)


def _norm_rows(x_ref, nw_ref, wdt_ref, xn_ref, dt_ref):
    x = x_ref[...]
    ms = jnp.mean(x * x, axis=-1, keepdims=True)
    xn = ((x * lax.rsqrt(ms + NORM_EPS)) * nw_ref[...]).astype(BF16)
    xn_ref[...] = xn
    dt_ref[...] = lax.dot_general(xn, wdt_ref[...], _NT, preferred_element_type=F32)


def _wprep_inproj_kernel(tbl_ref, x_ref, nw_ref, w_ref, wdt_ref, wout_ref, proj_ref, dt_ref, xn_ref):
    del tbl_ref

    @pl.when(pl.program_id(0) == 0)
    def _():
        _norm_rows(x_ref, nw_ref, wdt_ref, xn_ref, dt_ref)

    wb = w_ref[...].astype(BF16)
    wout_ref[...] = wb
    proj_ref[...] = lax.dot_general(xn_ref[...], wb, _NT, preferred_element_type=F32)


def _wprep_inproj(wt, w_dt, x, norm_w):
    m, d = x.shape
    ntiles = len(_SRC_TILES)
    assert ntiles * PROJ_TN == PROJ_COLS
    return pl.pallas_call(
        _wprep_inproj_kernel,
        grid_spec=pltpu.PrefetchScalarGridSpec(
            num_scalar_prefetch=1,
            grid=(ntiles,),
            in_specs=[
                pl.BlockSpec((m, d), lambda j, tbl: (0, 0)),
                pl.BlockSpec((1, d), lambda j, tbl: (0, 0)),
                pl.BlockSpec((pl.Element(PROJ_TN), pl.Element(d)),
                             lambda j, tbl: (pl.multiple_of(tbl[j] * W_ALIGN, W_ALIGN), 0)),
                pl.BlockSpec((LANES, d), lambda j, tbl: (0, 0)),
            ],
            out_specs=[
                pl.BlockSpec((PROJ_TN, d), lambda j, tbl: (j, 0)),
                pl.BlockSpec((m, PROJ_TN), lambda j, tbl: (0, j)),
                pl.BlockSpec((m, LANES), lambda j, tbl: (0, 0)),
            ],
            scratch_shapes=[pltpu.VMEM((m, d), BF16)],
        ),
        out_shape=[jax.ShapeDtypeStruct((PROJ_COLS, d), BF16), jax.ShapeDtypeStruct((m, PROJ_COLS), F32),
                   jax.ShapeDtypeStruct((m, LANES), F32)],
        compiler_params=_params("arbitrary"),
        name="wprep_inproj",
    )(jnp.asarray(_SRC_TILES, jnp.int32), x, norm_w, wt, w_dt)


def _inproj_kernel(x_ref, nw_ref, w_ref, wdt_ref, proj_ref, dt_ref, xn_ref):
    @pl.when(pl.program_id(1) == 0)
    def _():
        _norm_rows(x_ref, nw_ref, wdt_ref, xn_ref, dt_ref)

    proj_ref[...] = lax.dot_general(xn_ref[...], w_ref[...], _NT, preferred_element_type=F32)


def _inproj(x, norm_w, w_main, w_dt, tm):
    m, d = x.shape
    n = w_main.shape[0]
    return pl.pallas_call(
        _inproj_kernel,
        grid=(m // tm, n // PROJ_TN),
        in_specs=[
            pl.BlockSpec((tm, d), lambda i, j: (i, 0)),
            pl.BlockSpec((1, d), lambda i, j: (0, 0)),
            pl.BlockSpec((PROJ_TN, d), lambda i, j: (j, 0)),
            pl.BlockSpec((LANES, d), lambda i, j: (0, 0)),
        ],
        out_specs=[
            pl.BlockSpec((tm, PROJ_TN), lambda i, j: (i, j)),
            pl.BlockSpec((tm, LANES), lambda i, j: (i, 0)),
        ],
        out_shape=[jax.ShapeDtypeStruct((m, n), F32), jax.ShapeDtypeStruct((m, LANES), F32)],
        scratch_shapes=[pltpu.VMEM((tm, d), BF16)],
        compiler_params=_params("arbitrary", "arbitrary"),
        name="inproj",
    )(x, norm_w, w_main, w_dt)


def _ssd_chunk(r0, xbc_ref, za_ref, dtr_ref, convw_ref, convb_ref, dtb_ref, alog_ref, dexp_ref, nw_ref,
               y_ref, ht_scr, ext_scr, mask_rows, need_y):
    L = CHUNK
    P2 = 2 * SSD_HEAD_DIM
    GW = SSD_WIDTH // SSD_GROUPS
    time_rows = lambda t0, n: pl.ds(CONV_PITCH * t0, n, stride=CONV_PITCH)

    w = convw_ref[...]
    bias = convb_ref[...]
    conv_parts = []
    for s in range(SSD_XBC // LANES):
        ls = slice(s * LANES, (s + 1) * LANES)
        xc = xbc_ref[r0:r0 + L, ls]
        ext_scr[s, time_rows(SUBLANES, L), :] = xc
        acc = bias[:, ls]
        for k in range(SSD_CONV - 1):
            acc = acc + ext_scr[s, time_rows(SUBLANES - (SSD_CONV - 1) + k, L), :] * w[k:k + 1, ls]
        conv_parts.append(acc + xc * w[SSD_CONV - 1:SSD_CONV, ls])
        ext_scr[s, time_rows(0, SUBLANES), :] = xc[L - SUBLANES:L, :]
    xbc = _silu(jnp.concatenate(conv_parts, axis=1))
    xs = xbc[:, :SSD_WIDTH]
    bmat = xbc[:, SSD_WIDTH:SSD_WIDTH + SSD_GROUPS * SSD_STATE]
    cmat = xbc[:, SSD_WIDTH + SSD_GROUPS * SSD_STATE:]

    rows = lax.broadcasted_iota(jnp.int32, (L, L), 0)
    cols = lax.broadcasted_iota(jnp.int32, (L, L), 1)
    causal = rows >= cols
    lane_lo = cols < SSD_HEAD_DIM

    dt = _softplus(dtr_ref[r0:r0 + L, :] + dtb_ref[...])
    if mask_rows:
        dt = jnp.where(rows < mask_rows, 0.0, dt)
    a = dt * (-jnp.exp(alog_ref[...]))
    tril = jnp.where(causal, 1.0, 0.0).astype(BF16)
    a1, a2, a3 = _split3(a)
    acum = (jnp.dot(tril, a1, preferred_element_type=F32)
            + jnp.dot(tril, a2, preferred_element_type=F32)
            + jnp.dot(tril, a3, preferred_element_type=F32))
    a2 = acum * LOG2E
    e_cum = jnp.exp2(a2)
    w_end = dt * jnp.exp2(a2[L - 1:L, :] - a2)
    a2dt_t = (a2 - jnp.log2(dt)).T

    dexp = dexp_ref[...]
    xs_b = xs.astype(BF16)
    y_parts = []
    for g in range(SSD_GROUPS):
        bg = bmat[:, g * SSD_STATE:(g + 1) * SSD_STATE].astype(BF16)
        cg = cmat[:, g * SSD_STATE:(g + 1) * SSD_STATE].astype(BF16)
        ht_g = ht_scr[:, g * GW:(g + 1) * GW]
        if need_y:
            cb = lax.dot_general(cg, bg, (((1,), (1,)), ((), ())), preferred_element_type=F32)
            y_off = jnp.dot(cg, ht_g.astype(BF16), preferred_element_type=F32)
        xw_parts = []
        elast_parts = []
        for jj in range(GW // P2):
            lo = g * GW + jj * P2
            h0 = lo // SSD_HEAD_DIM
            yd, eb, wb = [], [], []
            for h in (h0, h0 + 1):
                if need_y:
                    colb = jnp.broadcast_to(a2[:, h:h + 1], (L, L))
                    m = cb * jnp.exp2(jnp.where(causal, colb - a2dt_t[h:h + 1, :], -jnp.inf))
                    yd.append(jnp.dot(m.astype(BF16), xs_b[:, lo:lo + P2], preferred_element_type=F32))
                eb.append(jnp.broadcast_to(e_cum[:, h:h + 1], (L, L)))
                wb.append(jnp.broadcast_to(w_end[:, h:h + 1], (L, L)))
            xs_pair = xs[:, lo:lo + P2]
            if need_y:
                y_pair = (jnp.where(lane_lo, yd[0], yd[1])
                          + y_off[:, jj * P2:(jj + 1) * P2] * jnp.where(lane_lo, eb[0], eb[1]))
                y_parts.append(y_pair + dexp[:, lo:lo + P2] * xs_pair)
            xw_parts.append(xs_pair * jnp.where(lane_lo, wb[0], wb[1]))
            elast_parts.append(jnp.where(lane_lo[0:1, :], eb[0][L - 1:L, :], eb[1][L - 1:L, :]))
        xw = jnp.concatenate(xw_parts, axis=1)
        elast = jnp.concatenate(elast_parts, axis=1)
        st = lax.dot_general(bg, xw.astype(BF16), (((0,), (0,)), ((), ())),
                             preferred_element_type=F32)
        ht_scr[:, g * GW:(g + 1) * GW] = ht_g * elast + st

    if not need_y:
        y_ref[r0:r0 + L, :] = jnp.zeros((L, SSD_WIDTH), y_ref.dtype)
        return
    y = jnp.concatenate(y_parts, axis=1)
    y = y * _silu(za_ref[r0:r0 + L, :])
    nw = nw_ref[...]
    outs = []
    for g in range(SSD_GROUPS):
        yg = y[:, g * GW:(g + 1) * GW]
        ms = jnp.mean(yg * yg, axis=-1, keepdims=True)
        outs.append((yg * lax.rsqrt(ms + NORM_EPS)) * nw[:, g * GW:(g + 1) * GW])
    y_ref[r0:r0 + L, :] = jnp.concatenate(outs, axis=1).astype(y_ref.dtype)


def _ssd_kernel(xbc_ref, za_ref, dtr_ref, ht0_ref, tail0_ref, convw_ref, convb_ref, dtb_ref, alog_ref,
                dexp_ref, nw_ref, y_ref, h_ref, ht_ref, tail_ref, ht_scr, ext_scr, *, mask_rows, need_y):
    c = pl.program_id(1)
    nrows = xbc_ref.shape[0]

    @pl.when(c == 0)
    def _():
        ht_scr[...] = ht0_ref[...]
        for s in range(SSD_XBC // LANES):
            ext_scr[s, pl.ds(0, SUBLANES, stride=CONV_PITCH), :] = tail0_ref[:, s * LANES:(s + 1) * LANES]

    for r0 in range(0, nrows, CHUNK):
        _ssd_chunk(r0, xbc_ref, za_ref, dtr_ref, convw_ref, convb_ref, dtb_ref, alog_ref, dexp_ref, nw_ref,
                   y_ref, ht_scr, ext_scr, mask_rows, need_y)

    @pl.when(c == pl.num_programs(1) - 1)
    def _():
        ht = ht_scr[...]
        ht_ref[...] = ht
        h_ref[...] = ht.T
        tail_ref[...] = xbc_ref[nrows - SUBLANES:nrows, :]


def _ssd(proj, dt_raw, ht0, tail0, conv_w, conv_b, dt_bias, a_log, d_exp, norm_w, *, batch, nchunks,
         row_block0, mask_rows, need_y=True):
    L = CHUNK
    rows = batch * nchunks * L
    rb = lambda b, c: row_block0 + b * nchunks + c
    const2 = lambda shape: pl.BlockSpec(shape, lambda b, c: (0, 0))
    return pl.pallas_call(
        functools.partial(_ssd_kernel, mask_rows=mask_rows, need_y=need_y),
        grid=(batch, nchunks),
        in_specs=[
            pl.BlockSpec((L, SSD_XBC), lambda b, c: (rb(b, c), COL_XBC // SSD_XBC)),
            pl.BlockSpec((L, SSD_WIDTH), lambda b, c: (rb(b, c), COL_ZA // SSD_WIDTH)),
            pl.BlockSpec((L, LANES), lambda b, c: (rb(b, c), 0)),
            const2((SSD_STATE, SSD_WIDTH)),
            const2((SUBLANES, SSD_XBC)),
            const2((SSD_CONV, SSD_XBC)),
            const2((1, SSD_XBC)),
            const2((1, LANES)),
            const2((1, LANES)),
            const2((1, SSD_WIDTH)),
            const2((1, SSD_WIDTH)),
        ],
        out_specs=[
            pl.BlockSpec((L, SSD_WIDTH), lambda b, c: (b * nchunks + c, 0)),
            pl.BlockSpec((None, SSD_WIDTH, SSD_STATE), lambda b, c: (b, 0, 0)),
            pl.BlockSpec((None, SSD_STATE, SSD_WIDTH), lambda b, c: (b, 0, 0)),
            pl.BlockSpec((None, SUBLANES, SSD_XBC), lambda b, c: (b, 0, 0)),
        ],
        out_shape=[
            jax.ShapeDtypeStruct((rows, SSD_WIDTH), BF16),
            jax.ShapeDtypeStruct((batch, SSD_WIDTH, SSD_STATE), F32),
            jax.ShapeDtypeStruct((batch, SSD_STATE, SSD_WIDTH), F32),
            jax.ShapeDtypeStruct((batch, SUBLANES, SSD_XBC), F32),
        ],
        scratch_shapes=[pltpu.VMEM((SSD_STATE, SSD_WIDTH), F32),
                        pltpu.VMEM((SSD_XBC // LANES, CONV_PITCH * (SUBLANES + CHUNK), LANES), F32)],
        compiler_params=_params("arbitrary", "arbitrary"),
        name="ssd",
    )(proj, proj, dt_raw, ht0, tail0, conv_w, conv_b, dt_bias, a_log, d_exp, norm_w)


def _ssd_step_kernel(xbc_ref, za_ref, dtr_ref, cs_ref, h_ref, convw_ref, convb_ref, dtb_ref, alog_ref,
                     dexp_ref, nw_ref, y_ref, hout_ref, csout_ref):
    R = SUBLANES
    GW = SSD_WIDTH // SSD_GROUPS
    x = xbc_ref[...]
    w = convw_ref[...]
    s0 = cs_ref[0]
    s1 = cs_ref[1]
    s2 = cs_ref[2]
    conv = convb_ref[...] + s0 * w[0:1]
    conv = conv + s1 * w[1:2]
    conv = conv + s2 * w[2:3]
    conv = conv + x * w[3:4]
    csout_ref[0] = s1
    csout_ref[1] = s2
    csout_ref[2] = x
    xbc = _silu(conv)
    xs = xbc[:, :SSD_WIDTH]
    bmat = xbc[:, SSD_WIDTH:SSD_WIDTH + SSD_GROUPS * SSD_STATE]
    cmat = xbc[:, SSD_WIDTH + SSD_GROUPS * SSD_STATE:]

    dt = _softplus(dtr_ref[...] + dtb_ref[...])
    da = jnp.exp(dt * (-jnp.exp(alog_ref[...])))
    dt_t = dt.T
    da_t = da.T
    expand = lambda v: jnp.concatenate(
        [jnp.broadcast_to(v[h:h + 1, :], (SSD_HEAD_DIM, R)) for h in range(SSD_HEADS)], axis=0)
    xd_t = xs.T * expand(dt_t)
    da_te = expand(da_t)

    cmat_b = cmat.astype(BF16)
    yrows = []
    for i in range(R):
        bexp = jnp.concatenate(
            [jnp.broadcast_to(bmat[i:i + 1, g * SSD_STATE:(g + 1) * SSD_STATE], (GW, SSD_STATE))
             for g in range(SSD_GROUPS)], axis=0)
        hn = h_ref[i] * da_te[:, i:i + 1] + xd_t[:, i:i + 1] * bexp
        hout_ref[i] = hn
        cg = jnp.concatenate([cmat_b[i:i + 1, g * SSD_STATE:(g + 1) * SSD_STATE] for g in range(SSD_GROUPS)]
                             + [jnp.zeros((R - SSD_GROUPS, SSD_STATE), BF16)], axis=0)
        yg = lax.dot_general(cg, hn.astype(BF16), (((1,), (1,)), ((), ())), preferred_element_type=F32)
        yrows.append(jnp.concatenate([yg[g:g + 1, g * GW:(g + 1) * GW] for g in range(SSD_GROUPS)], axis=1))
    y = jnp.concatenate(yrows, axis=0)
    y = y + dexp_ref[...] * xs
    y = y * _silu(za_ref[...])
    nw = nw_ref[...]
    outs = []
    for g in range(SSD_GROUPS):
        yg = y[:, g * GW:(g + 1) * GW]
        ms = jnp.mean(yg * yg, axis=-1, keepdims=True)
        outs.append((yg * lax.rsqrt(ms + NORM_EPS)) * nw[:, g * GW:(g + 1) * GW])
    y_ref[...] = jnp.concatenate(outs, axis=1).astype(y_ref.dtype)


def _ssd_step(proj, dt_raw, conv_state, ssd_state, conv_w, conv_b, dt_bias, a_log, d_exp, norm_w, *, nseq):
    R = SUBLANES
    const2 = lambda shape: pl.BlockSpec(shape, lambda i: (0, 0))
    return pl.pallas_call(
        _ssd_step_kernel,
        grid=(nseq // R,),
        in_specs=[
            pl.BlockSpec((R, SSD_XBC), lambda i: (i, COL_XBC // SSD_XBC)),
            pl.BlockSpec((R, SSD_WIDTH), lambda i: (i, COL_ZA // SSD_WIDTH)),
            pl.BlockSpec((R, LANES), lambda i: (i, 0)),
            pl.BlockSpec((SSD_CONV - 1, R, SSD_XBC), lambda i: (0, i, 0)),
            pl.BlockSpec((R, SSD_WIDTH, SSD_STATE), lambda i: (i, 0, 0)),
            const2((SSD_CONV, SSD_XBC)),
            const2((1, SSD_XBC)),
            const2((1, LANES)),
            const2((1, LANES)),
            const2((1, SSD_WIDTH)),
            const2((1, SSD_WIDTH)),
        ],
        out_specs=[
            pl.BlockSpec((R, SSD_WIDTH), lambda i: (i, 0)),
            pl.BlockSpec((R, SSD_WIDTH, SSD_STATE), lambda i: (i, 0, 0)),
            pl.BlockSpec((SSD_CONV - 1, R, SSD_XBC), lambda i: (0, i, 0)),
        ],
        out_shape=[
            jax.ShapeDtypeStruct((nseq, SSD_WIDTH), BF16),
            jax.ShapeDtypeStruct((nseq, SSD_WIDTH, SSD_STATE), F32),
            jax.ShapeDtypeStruct((SSD_CONV - 1, nseq, SSD_XBC), F32),
        ],
        compiler_params=_params("arbitrary"),
        name="ssd_step",
    )(proj, proj, dt_raw, conv_state, ssd_state, conv_w, conv_b, dt_bias, a_log, d_exp, norm_w)


def _s5_block_diag(w, rows_per_group, cols_per_group):
    ngrp, r, lanes = w.shape
    width = ngrp * cols_per_group
    tiled = jnp.concatenate([w.reshape(ngrp * r, lanes)] * (width // lanes), axis=1)
    row_g = lax.broadcasted_iota(jnp.int32, tiled.shape, 0) // rows_per_group
    col_g = lax.broadcasted_iota(jnp.int32, tiled.shape, 1) // cols_per_group
    return jnp.where(row_g == col_g, tiled, 0.0)


def _s5_weights(bbr_ref, bbi_ref, cr_ref, ci_ref):
    return tuple(_s5_block_diag(r[...], S5_GROUP, S5_STATE) for r in (bbr_ref, bbi_ref, cr_ref, ci_ref))


def _s5_pitch(seg_len):
    return seg_len if (seg_len // SUBLANES) % 2 else seg_len + SUBLANES


def _s5_kernel(*refs, seq, nconv, need_y):
    (u_ref, h0r_ref, h0i_ref, bbr_ref, bbi_ref, ctr_ref, cti_ref, ar_ref, ai_ref, d_ref) = refs[:10]
    w_in_refs = refs[10:10 + nconv]
    y_ref, hr_ref, hi_ref = refs[10 + nconv:13 + nconv]
    w_out_refs = refs[13 + nconv:13 + 2 * nconv]
    pad, pe, po, cr, ci, yeh, hb_out, w2r_scr, w2i_scr, cta_scr, ctb_scr, k0_scr = refs[13 + 2 * nconv:]
    for wi, wo in zip(w_in_refs, w_out_refs):
        wo[...] = wi[...].astype(BF16)

    NS = SUBLANES
    SL = seq // NS
    H = SL // 2
    pitch = _s5_pitch(SL)
    nk = S5_BLK_U // LANES
    n = S5_BLK_N
    a_re = ar_ref[...]
    a_im = ai_ref[...]
    bf = lambda v: v.astype(BF16)

    @pl.when(pl.program_id(1) == 0)
    def _():
        bt_re, bt_im, ct_re, ct_im = _s5_weights(bbr_ref, bbi_ref, ctr_ref, cti_ref)
        w2r_scr[...] = jnp.concatenate([bf(bt_re * a_re - bt_im * a_im), bf(bt_re)], axis=0)
        w2i_scr[...] = jnp.concatenate([bf(bt_re * a_im + bt_im * a_re), bf(bt_im)], axis=0)
        if need_y:
            cta_scr[0] = bf(ct_re * a_re - ct_im * a_im)
            cta_scr[1] = bf(ct_re * a_im + ct_im * a_re)
            ctb_scr[0] = bf(ct_re)
            ctb_scr[1] = bf(ct_im)
            k0_scr[...] = bf(lax.dot_general(bf(bt_re), bf(ct_re), _NT, preferred_element_type=F32)
                             - lax.dot_general(bf(bt_im), bf(ct_im), _NT, preferred_element_type=F32))

    for s in range(NS):
        for k in range(nk):
            pad[k, s * pitch:s * pitch + SL, :] = u_ref[s * SL:(s + 1) * SL, k * LANES:(k + 1) * LANES]

    for jp in range(H):
        for k in range(nk):
            pe[jp * NS:(jp + 1) * NS, k * LANES:(k + 1) * LANES] = pad[k, pl.ds(2 * jp, NS, stride=pitch), :]
            po[jp * NS:(jp + 1) * NS, k * LANES:(k + 1) * LANES] = pad[k, pl.ds(2 * jp + 1, NS, stride=pitch), :]
    half = seq // 2
    NB = 4 if H % 8 == 0 else 1
    rbk = half // NB
    spb = rbk // NS

    qc = 4 * S5_GROUP
    qn = 4 * S5_STATE

    def input_block(kb):
        rows = slice(kb * rbk, (kb + 1) * rbk)
        ue_b = bf(pe[rows, :])
        uo_b = bf(po[rows, :])
        for q in range(S5_BLK_U // qc):
            u_q = jnp.concatenate([ue_b[:, q * qc:(q + 1) * qc], uo_b[:, q * qc:(q + 1) * qc]], axis=1)
            rows_q = [slice(q * qc, (q + 1) * qc), slice(S5_BLK_U + q * qc, S5_BLK_U + (q + 1) * qc)]
            cols_q = slice(q * qn, (q + 1) * qn)
            wq_re = jnp.concatenate([w2r_scr[r, cols_q] for r in rows_q], axis=0)
            wq_im = jnp.concatenate([w2i_scr[r, cols_q] for r in rows_q], axis=0)
            cr[rows, cols_q] = jnp.dot(u_q, wq_re, preferred_element_type=F32)
            ci[rows, cols_q] = jnp.dot(u_q, wq_im, preferred_element_type=F32)

    a2_re = a_re * a_re - a_im * a_im
    a2_im = 2.0 * (a_re * a_im)
    ar = jnp.broadcast_to(a2_re, (NS, n))
    ai = jnp.broadcast_to(a2_im, (NS, n))

    def step(carry, r0):
        hr, hi = carry
        nr = ar * hr - ai * hi + cr[r0:r0 + NS, :]
        ni = ar * hi + ai * hr + ci[r0:r0 + NS, :]
        return nr, ni

    zero = jnp.zeros((NS, n), F32)
    carry = (zero, zero)
    input_block(0)
    for kb in range(NB):
        if kb + 1 < NB:
            input_block(kb + 1)
        for t in range(spb):
            carry = step(carry, (kb * spb + t) * NS)
    er, ei = carry

    asr, asi = a_re, a_im
    for _ in range(SL.bit_length() - 1):
        asr, asi = asr * asr - asi * asi, 2.0 * (asr * asi)
    gr = [h0r_ref[...]]
    gi = [h0i_ref[...]]
    for s in range(NS):
        gr.append(er[s:s + 1, :] + (asr * gr[s] - asi * gi[s]))
        gi.append(ei[s:s + 1, :] + (asr * gi[s] + asi * gr[s]))
    hr_ref[...] = gr[NS]
    hi_ref[...] = gi[NS]
    if not need_y:
        y_ref[...] = jnp.zeros(y_ref.shape, y_ref.dtype)
        return

    def scan_block(kb, c0):
        for p in range(spb // 2):
            r0 = (kb * spb + 2 * p) * NS
            c1 = step(c0, r0)
            c2 = step(c1, r0 + NS)
            hb_out[r0:r0 + 2 * NS, 0:n] = bf(jnp.concatenate([c1[0], c2[0]], axis=0))
            hb_out[r0:r0 + 2 * NS, n:2 * n] = bf(jnp.concatenate([c1[1], c2[1]], axis=0))
            c0 = c2
        return c0

    d = d_ref[...]
    cta_re, cta_im = cta_scr[0], cta_scr[1]
    ctb_re, ctb_im = ctb_scr[0], ctb_scr[1]
    k0 = k0_scr[...]

    def z_weights(q, part):
        sl = (slice(q * qc, (q + 1) * qc), slice(q * qn, (q + 1) * qn))
        return jnp.concatenate([ctb_scr[part][sl], cta_scr[part][sl]], axis=0)

    g_re = bf(jnp.concatenate(gr[:NS], axis=0))
    g_im = bf(jnp.concatenate(gi[:NS], axis=0))
    yeh[0:NS, :] = (lax.dot_general(g_re, cta_re, _NT, preferred_element_type=F32)
                    - lax.dot_general(g_im, cta_im, _NT, preferred_element_type=F32))

    def output_block(kb):
        rows = slice(kb * rbk, (kb + 1) * rbk)
        zs = []
        for q in range(S5_BLK_U // qc):
            cols_q = slice(q * qn, (q + 1) * qn)
            zs.append(lax.dot_general(hb_out[rows, cols_q], z_weights(q, 0), _NT, preferred_element_type=F32)
                      - lax.dot_general(hb_out[rows, n + q * qn:n + (q + 1) * qn], z_weights(q, 1), _NT,
                                        preferred_element_type=F32))
        y_odd = jnp.concatenate([z[:, 0:qc] for z in zs], axis=1)
        yeh[NS + kb * rbk:NS + (kb + 1) * rbk, :] = jnp.concatenate([z[:, qc:2 * qc] for z in zs], axis=1)
        y_even = yeh[rows, :] + jnp.dot(bf(pe[rows, :]), k0, preferred_element_type=F32)
        po[rows, :] = _gelu_tanh(y_odd + d * po[rows, :])
        pe[rows, :] = _gelu_tanh(y_even + d * pe[rows, :])

    carry = scan_block(0, (jnp.concatenate(gr[:NS], axis=0), jnp.concatenate(gi[:NS], axis=0)))
    for kb in range(NB):
        if kb + 1 < NB:
            carry = scan_block(kb + 1, carry)
        output_block(kb)

    for jp in range(H):
        for k in range(nk):
            pad[k, pl.ds(2 * jp, NS, stride=pitch), :] = pe[jp * NS:(jp + 1) * NS, k * LANES:(k + 1) * LANES]
            pad[k, pl.ds(2 * jp + 1, NS, stride=pitch), :] = po[jp * NS:(jp + 1) * NS, k * LANES:(k + 1) * LANES]
    for s in range(NS):
        for k in range(nk):
            y_ref[s * SL:(s + 1) * SL, k * LANES:(k + 1) * LANES] = pad[k, s * pitch:s * pitch + SL, :]


def _s5(proj, h0r, h0i, bb_re, bb_im, ct_re, ct_im, ab_re, ab_im, d_s5, *, batch, seq, row_block0, need_y=True,
        convert=()):
    ub0 = COL_UB // S5_BLK_U
    nsteps = batch * S5_BLOCKS
    slice_spec = lambda w: pl.BlockSpec((w.shape[0] // nsteps, w.shape[1]), lambda j, b: (j * batch + b, 0))
    assert all(w.shape[0] % (2 * SUBLANES * nsteps) == 0 for w in convert)
    seg = seq // SUBLANES
    assert seg & (seg - 1) == 0
    gpb = S5_GROUPS // S5_BLOCKS
    vec = lambda width: pl.BlockSpec((1, width), lambda j, b: (0, j))
    wspec = lambda r: pl.BlockSpec((gpb, r, LANES), lambda j, b: (j, 0, 0))
    st_out = pl.BlockSpec((None, 1, S5_BLK_N), lambda j, b: (b, 0, j))
    return pl.pallas_call(
        functools.partial(_s5_kernel, seq=seq, nconv=len(convert), need_y=need_y),
        grid=(S5_BLOCKS, batch),
        in_specs=[
            pl.BlockSpec((seq, S5_BLK_U), lambda j, b: (row_block0 + b, ub0 + j)),
            vec(S5_BLK_N), vec(S5_BLK_N),
            wspec(S5_GROUP), wspec(S5_GROUP), wspec(S5_GROUP), wspec(S5_GROUP),
            vec(S5_BLK_N), vec(S5_BLK_N), vec(S5_BLK_U),
        ] + [slice_spec(w) for w in convert],
        out_specs=[pl.BlockSpec((seq, S5_BLK_U), lambda j, b: (b, j)), st_out, st_out]
        + [slice_spec(w) for w in convert],
        out_shape=[
            jax.ShapeDtypeStruct((batch * seq, S5_WIDTH), F32),
            jax.ShapeDtypeStruct((batch, 1, S5_NSTATE), F32),
            jax.ShapeDtypeStruct((batch, 1, S5_NSTATE), F32),
        ] + [jax.ShapeDtypeStruct(w.shape, BF16) for w in convert],
        scratch_shapes=[
            pltpu.VMEM((S5_BLK_U // LANES, SUBLANES * _s5_pitch(seg), LANES), F32),
            pltpu.VMEM((seq // 2, S5_BLK_U), F32),
            pltpu.VMEM((seq // 2, S5_BLK_U), F32),
            pltpu.VMEM((seq // 2, S5_BLK_N), F32),
            pltpu.VMEM((seq // 2, S5_BLK_N), F32),
            pltpu.VMEM((seq // 2 + SUBLANES, S5_BLK_U), F32),
            pltpu.VMEM((seq // 2, 2 * S5_BLK_N), BF16),
            pltpu.VMEM((2 * S5_BLK_U, S5_BLK_N), BF16),
            pltpu.VMEM((2 * S5_BLK_U, S5_BLK_N), BF16),
            pltpu.VMEM((2, S5_BLK_U, S5_BLK_N), BF16),
            pltpu.VMEM((2, S5_BLK_U, S5_BLK_N), BF16),
            pltpu.VMEM((S5_BLK_U, S5_BLK_U), BF16),
        ],
        compiler_params=_params("arbitrary", "arbitrary"),
        name="s5",
    )(proj, h0r, h0i, bb_re, bb_im, ct_re, ct_im, ab_re, ab_im, d_s5, *convert)


def _s5_step_kernel(u_ref, h0r_ref, h0i_ref, bbr_ref, bbi_ref, ctr_ref, cti_ref, ar_ref, ai_ref, d_ref,
                    y_ref, hr_ref, hi_ref):
    bt_re, bt_im, ct_re, ct_im = (w.astype(BF16) for w in _s5_weights(bbr_ref, bbi_ref, ctr_ref, cti_ref))
    u = u_ref[...]
    ub = u.astype(BF16)
    ar = ar_ref[...]
    ai = ai_ref[...]
    h0r = h0r_ref[...].T
    h0i = h0i_ref[...].T
    hr = jnp.dot(ub, bt_re, preferred_element_type=F32) + (ar * h0r - ai * h0i)
    hi = jnp.dot(ub, bt_im, preferred_element_type=F32) + (ar * h0i + ai * h0r)
    hr_ref[...] = hr.T
    hi_ref[...] = hi.T
    y = (lax.dot_general(hr.astype(BF16), ct_re, _NT, preferred_element_type=F32)
         - lax.dot_general(hi.astype(BF16), ct_im, _NT, preferred_element_type=F32))
    y_ref[...] = _gelu_tanh(y + d_ref[...] * u)


def _s5_step(proj, h0r, h0i, bb_re, bb_im, ct_re, ct_im, ab_re, ab_im, d_s5, *, nseq):
    ub0 = COL_UB // S5_BLK_U
    gpb = S5_GROUPS // S5_BLOCKS
    vec = lambda width: pl.BlockSpec((1, width), lambda j: (0, j))
    wspec = lambda r: pl.BlockSpec((gpb, r, LANES), lambda j: (j, 0, 0))
    st = pl.BlockSpec((S5_BLK_N, nseq), lambda j: (j, 0))
    return pl.pallas_call(
        _s5_step_kernel,
        grid=(S5_BLOCKS,),
        in_specs=[
            pl.BlockSpec((nseq, S5_BLK_U), lambda j: (0, ub0 + j)),
            st, st,
            wspec(S5_GROUP), wspec(S5_GROUP), wspec(S5_GROUP), wspec(S5_GROUP),
            vec(S5_BLK_N), vec(S5_BLK_N), vec(S5_BLK_U),
        ],
        out_specs=[pl.BlockSpec((nseq, S5_BLK_U), lambda j: (0, j)), st, st],
        out_shape=[
            jax.ShapeDtypeStruct((nseq, S5_WIDTH), F32),
            jax.ShapeDtypeStruct((S5_NSTATE, nseq), F32),
            jax.ShapeDtypeStruct((S5_NSTATE, nseq), F32),
        ],
        compiler_params=_params("arbitrary"),
        name="s5_step",
    )(proj, h0r, h0i, bb_re, bb_im, ct_re, ct_im, ab_re, ab_im, d_s5)


def _tail_rows(yn_ref, ybg_ref, zb_ref, ga_ref, gb_ref, x_ref, wpa_ref, wglu_ref, bglu_ref, wpb_ref,
               wout_ref, fnw_ref, out_ref):
    sg_a = _sigmoid(ga_ref[...])
    sg_b = _sigmoid(gb_ref[...])
    sz_b = _silu(zb_ref[...])
    yb = ybg_ref[...]
    glu = jnp.dot(yb.astype(BF16), wglu_ref[...], preferred_element_type=F32) + bglu_ref[...]
    ya = jnp.dot(yn_ref[...], wpa_ref[...], preferred_element_type=F32)
    yb = (yb * _sigmoid(glu)) * sz_b
    ybp = jnp.dot(yb.astype(BF16), wpb_ref[...], preferred_element_type=F32)
    mixed = sg_a * ya + sg_b * ybp
    o = x_ref[...] + jnp.dot(mixed.astype(BF16), wout_ref[...], preferred_element_type=F32)
    ms = jnp.mean(o * o, axis=-1, keepdims=True)
    out_ref[...] = (o * lax.rsqrt(ms + NORM_EPS)) * fnw_ref[...]


def _tail_kernel(*refs, n_main):
    main_in, samp_in, weights, (out_m_ref, out_s_ref) = refs[0:6], refs[6:12], refs[12:18], refs[18:20]
    i = pl.program_id(0)

    @pl.when(i < n_main)
    def _():
        _tail_rows(*main_in, *weights, out_m_ref)

    @pl.when(i == n_main)
    def _():
        _tail_rows(*samp_in, *weights, out_s_ref)


def _tail(main, samp, w_proj_a, w_glu, b_glu, w_proj_b, w_out, final_norm_w, *, tm):
    m, d = main[3].shape
    ms = samp[3].shape[0]
    n_main = m // tm
    resident = lambda shape: pl.BlockSpec(shape, lambda i: (0, 0), pipeline_mode=pl.Buffered(1))
    row = lambda i: jnp.minimum(i, n_main - 1)
    once = lambda width, col: pl.BlockSpec((ms, width), lambda i: (0, col), pipeline_mode=pl.Buffered(1))
    expand = lambda t: (t[0], t[1], t[2], t[2], t[2], t[3])
    return pl.pallas_call(
        functools.partial(_tail_kernel, n_main=n_main),
        grid=(n_main + 1,),
        in_specs=[
            pl.BlockSpec((tm, SSD_WIDTH), lambda i: (row(i), 0)),
            pl.BlockSpec((tm, S5_WIDTH), lambda i: (row(i), 0)),
            pl.BlockSpec((tm, S5_WIDTH), lambda i: (row(i), COL_ZB // S5_WIDTH)),
            pl.BlockSpec((tm, d), lambda i: (row(i), COL_GA // D_MODEL)),
            pl.BlockSpec((tm, d), lambda i: (row(i), COL_GB // D_MODEL)),
            pl.BlockSpec((tm, d), lambda i: (row(i), 0)),
            once(SSD_WIDTH, 0),
            once(S5_WIDTH, 0),
            once(S5_WIDTH, COL_ZB // S5_WIDTH),
            once(d, COL_GA // D_MODEL),
            once(d, COL_GB // D_MODEL),
            once(d, 0),
            resident((SSD_WIDTH, d)),
            resident((S5_WIDTH, S5_WIDTH)),
            resident((1, S5_WIDTH)),
            resident((S5_WIDTH, d)),
            resident((d, d)),
            resident((1, d)),
        ],
        out_specs=[
            pl.BlockSpec((tm, d), lambda i: (row(i), 0)),
            pl.BlockSpec((ms, d), lambda i: (0, 0)),
        ],
        out_shape=[jax.ShapeDtypeStruct((m, d), F32), jax.ShapeDtypeStruct((ms, d), F32)],
        compiler_params=_params("arbitrary"),
        name="tail",
    )(*expand(main), *expand(samp), w_proj_a, w_glu, b_glu, w_proj_b, w_out, final_norm_w)


def kernel(x_prompt, x_sample, state_ssd, state_conv, state_s5_re, state_s5_im, meta_tokens, norm_w, w_in,
           conv_w, conv_b, dt_bias, a_log, d_ssd, ssd_norm_w, w_proj_a, lam_re, lam_im, log_dt_s5, b_re, b_im,
           c_re, c_im, d_s5, w_glu, b_glu, w_proj_b, w_out, final_norm_w):
    bsz, seq, d = x_prompt.shape
    nseq = x_sample.shape[0]
    assert d == D_MODEL and seq % CHUNK == 0 and nseq % SUBLANES == 0 and norm_w.shape[0] == 1
    assert meta_tokens.shape[0] == N_META and N_META <= CHUNK

    assert w_in.shape[2] == _SRC_GB + D_MODEL
    wt = jnp.transpose(w_in[0])
    w_dt = _wprep_dt(wt)
    pad_heads = lambda v: jnp.pad(v.reshape(1, SSD_HEADS), ((0, 0), (0, LANES - SSD_HEADS)))
    dtb = pad_heads(dt_bias[0])
    alog = pad_heads(a_log[0])
    d_exp = jnp.repeat(d_ssd[0], SSD_HEAD_DIM).reshape(1, SSD_WIDTH)
    nw1 = norm_w[0].reshape(1, d)
    ssd_nw = ssd_norm_w[0].reshape(1, SSD_WIDTH)
    convw = conv_w[0]
    convb = conv_b[0].reshape(1, SSD_XBC)
    bglu = b_glu[0].reshape(1, S5_WIDTH)
    fnw = final_norm_w.reshape(1, d)
    ds5 = d_s5[0].reshape(1, S5_WIDTH)

    rep = lambda v, k: jnp.concatenate([v] * k, axis=-1)
    lane_rep = LANES // S5_STATE
    ab_re, ab_im, bb_re, bb_im = _s5prep(
        rep(lam_re[0], lane_rep), rep(lam_im[0], lane_rep), log_dt_s5[0].reshape(S5_GROUPS, 1),
        rep(jnp.transpose(b_re[0], (0, 2, 1)), lane_rep), rep(jnp.transpose(b_im[0], (0, 2, 1)), lane_rep))
    ab_re = ab_re[:, :S5_STATE].reshape(1, S5_NSTATE)
    ab_im = ab_im[:, :S5_STATE].reshape(1, S5_NSTATE)
    ct_re = rep(c_re[0], lane_rep)
    ct_im = rep(c_im[0], lane_rep)

    x_main = x_prompt.reshape(bsz * seq, d)
    x_small = jnp.concatenate(
        [x_sample.reshape(nseq, d), jnp.zeros((CHUNK - N_META, d), x_prompt.dtype),
         meta_tokens.astype(x_prompt.dtype)], axis=0)
    assert nseq % CHUNK == 0
    meta_blk = nseq // CHUNK
    tm_main = 1024 if (bsz * seq) % 1024 == 0 else CHUNK
    w_main, proj_s, dt_s = _wprep_inproj(wt, w_dt, x_small, nw1)
    proj_m, dt_m = _inproj(x_main, nw1, w_main, w_dt, tm=tm_main)

    ssd_args = (convw, convb, dtb, alog, d_exp, ssd_nw)
    s5_w = (bb_re, bb_im, ct_re, ct_im)
    s5_v = (ab_re, ab_im, ds5)

    zeros_ht = jnp.zeros((SSD_STATE, SSD_WIDTH), F32)
    zeros_tail = jnp.zeros((SUBLANES, SSD_XBC), F32)
    zeros_s5 = jnp.zeros((1, S5_NSTATE), F32)
    _, _, ht_meta, tail_meta = _ssd(proj_s, dt_s, zeros_ht, zeros_tail, *ssd_args, batch=1, nchunks=1,
                                    row_block0=meta_blk, mask_rows=CHUNK - N_META, need_y=False)
    _, s5r_meta, s5i_meta = _s5(proj_s, zeros_s5, zeros_s5, *s5_w, *s5_v, batch=1, seq=CHUNK,
                                row_block0=meta_blk, need_y=False)[:3]

    yn_m, h_m, _, tail_m = _ssd(proj_m, dt_m, ht_meta[0], tail_meta[0], *ssd_args, batch=bsz,
                                nchunks=seq // CHUNK, row_block0=0, mask_rows=0)
    ybg_m, s5r_m, s5i_m, wpa, wglu, wpb, wout = _s5(
        proj_m, s5r_meta[0], s5i_meta[0], *s5_w, *s5_v, batch=bsz, seq=seq, row_block0=0,
        convert=(w_proj_a[0], w_glu[0], w_proj_b[0], w_out[0]))
    tail_w = (wpa, wglu, bglu, wpb, wout, fnw)

    yn_s, h_s, cs_s = _ssd_step(proj_s, dt_s, jnp.transpose(state_conv[0], (1, 0, 2)),
                                state_ssd[0].reshape(nseq, SSD_WIDTH, SSD_STATE), *ssd_args, nseq=nseq)
    seq_minor = lambda v: jnp.transpose(v, (1, 2, 0)).reshape(S5_NSTATE, nseq)
    seq_major = lambda v: jnp.transpose(v.reshape(S5_GROUPS, S5_STATE, nseq), (2, 0, 1))[None]
    ybg_s, s5r_s, s5i_s = _s5_step(proj_s, seq_minor(state_s5_re[0]), seq_minor(state_s5_im[0]),
                                   *s5_w, *s5_v, nseq=nseq)

    y_prompt, y_sample = _tail((yn_m, ybg_m, proj_m, x_main), (yn_s, ybg_s, proj_s, x_sample.reshape(nseq, d)),
                               *tail_w, tm=256)

    dt_out = x_prompt.dtype
    return (
        y_prompt.reshape(bsz, seq, d),
        y_sample.reshape(nseq, 1, d),
        h_m.reshape(1, bsz, SSD_HEADS, SSD_HEAD_DIM, SSD_STATE).astype(dt_out),
        tail_m[:, SUBLANES - (SSD_CONV - 1):, :].reshape(1, bsz, SSD_CONV - 1, SSD_XBC),
        s5r_m.reshape(1, bsz, S5_GROUPS, S5_STATE).astype(dt_out),
        s5i_m.reshape(1, bsz, S5_GROUPS, S5_STATE).astype(dt_out),
        h_s.reshape(1, nseq, SSD_HEADS, SSD_HEAD_DIM, SSD_STATE).astype(dt_out),
        jnp.transpose(cs_s, (1, 0, 2)).reshape(1, nseq, SSD_CONV - 1, SSD_XBC),
        seq_major(s5r_s).astype(dt_out),
        seq_major(s5i_s).astype(dt_out),
    )
```

```python
import functools

import jax
import jax.numpy as jnp
from jax import lax
from jax.experimental import pallas as pl
from jax.experimental.pallas import tpu as pltpu

F32 = jnp.float32
BF16 = jnp.bfloat16

NORM_EPS = 1e-5
LOG2E = 1.4426950408889634
N_META = 16
D_MODEL = 2048
SSD_HEAD_DIM = 64
SSD_HEADS = 32
SSD_GROUPS = 4
SSD_STATE = 128
SSD_WIDTH = SSD_HEADS * SSD_HEAD_DIM
SSD_XBC = SSD_WIDTH + 2 * SSD_GROUPS * SSD_STATE
SSD_CONV = 4
CHUNK = 128
S5_WIDTH = D_MODEL // 2
S5_GROUP = 16
S5_GROUPS = S5_WIDTH // S5_GROUP
S5_STATE = 64
S5_NSTATE = S5_GROUPS * S5_STATE
S5_BLOCKS = 4
S5_BLK_U = S5_WIDTH // S5_BLOCKS
S5_BLK_N = S5_NSTATE // S5_BLOCKS

LANES = 128
SUBLANES = 8
VMEM_LIMIT = 56 * 1024 * 1024
CONV_PITCH = 3

COL_ZA = 0
COL_GA = COL_ZA + SSD_WIDTH
COL_GB = COL_GA + D_MODEL
COL_XBC = COL_GB + D_MODEL
COL_UB = COL_XBC + SSD_XBC
COL_ZB = COL_UB + S5_WIDTH
PROJ_COLS = COL_ZB + S5_WIDTH
PROJ_TN = 1024


_NT = (((1,), (1,)), ((), ()))


def _sigmoid(x):
    return 0.5 * jnp.tanh(0.5 * x) + 0.5


def _silu(x):
    h = 0.5 * x
    return h + h * jnp.tanh(h)


def _softplus(x):
    return jnp.maximum(x, 0.0) + jnp.log1p(jnp.exp(-jnp.abs(x)))


def _gelu_tanh(x):
    c = 0.7978845608028654
    return 0.5 * x * (1.0 + jnp.tanh(c * (x + 0.044715 * (x * x * x))))


def _split3(x):
    x1 = x.astype(BF16)
    r1 = x - x1.astype(F32)
    x2 = r1.astype(BF16)
    x3 = (r1 - x2.astype(F32)).astype(BF16)
    return x1, x2, x3


def _params(*sem, vmem=VMEM_LIMIT):
    return pltpu.CompilerParams(dimension_semantics=sem, vmem_limit_bytes=vmem)


def _s5prep_kernel(lr_ref, li_ref, ldt_ref, btr_ref, bti_ref, abr_ref, abi_ref, bbr_ref, bbi_ref):
    lr = lr_ref[...]
    li = li_ref[...]
    step = jnp.exp(ldt_ref[...])
    mag = jnp.exp(lr * step)
    abr = mag * jnp.cos(li * step)
    abi = mag * jnp.sin(li * step)
    den = lr * lr + li * li
    numr = abr - 1.0
    cr = (numr * lr + abi * li) / den
    ci = (abi * lr - numr * li) / den
    abr_ref[...] = abr
    abi_ref[...] = abi
    btr = btr_ref[...]
    bti = bti_ref[...]
    crb = cr[:, None, :]
    cib = ci[:, None, :]
    bbr_ref[...] = crb * btr - cib * bti
    bbi_ref[...] = crb * bti + cib * btr


def _s5prep(lam_re, lam_im, log_dt, bt_re, bt_im):
    g, n = lam_re.shape
    full2 = pl.BlockSpec((g, n), lambda: (0, 0))
    full3 = pl.BlockSpec((g, S5_GROUP, n), lambda: (0, 0, 0))
    return pl.pallas_call(
        _s5prep_kernel,
        in_specs=[full2, full2, pl.BlockSpec((g, 1), lambda: (0, 0)), full3, full3],
        out_specs=[full2, full2, full3, full3],
        out_shape=[jax.ShapeDtypeStruct((g, n), F32)] * 2
        + [jax.ShapeDtypeStruct((g, S5_GROUP, n), F32)] * 2,
        name="s5prep",
    )(lam_re, lam_im, log_dt, bt_re, bt_im)


W_ALIGN = 32
_SRC_ZA = 0
_SRC_XBC = SSD_WIDTH
_SRC_DT = _SRC_XBC + SSD_XBC
_SRC_UB = _SRC_DT + SSD_HEADS
_SRC_ZB = _SRC_UB + S5_WIDTH
_SRC_GA = _SRC_ZB + S5_WIDTH
_SRC_GB = _SRC_GA + D_MODEL
_SEGMENTS = ((_SRC_ZA, SSD_WIDTH), (_SRC_GA, D_MODEL), (_SRC_GB, D_MODEL), (_SRC_XBC, SSD_XBC),
             (_SRC_UB, S5_WIDTH), (_SRC_ZB, S5_WIDTH))
_SRC_TILES = tuple((start + k) // W_ALIGN for start, width in _SEGMENTS for k in range(0, width, PROJ_TN))


def _wprep_dt_kernel(w_ref, o_ref):
    o_ref[...] = jnp.zeros(o_ref.shape, o_ref.dtype)
    o_ref[0:SSD_HEADS, :] = w_ref[...].astype(BF16)


def _wprep_dt(wt):
    d = wt.shape[1]
    assert _SRC_DT % SSD_HEADS == 0
    return pl.pallas_call(
        _wprep_dt_kernel,
        grid=(1,),
        in_specs=[pl.BlockSpec((SSD_HEADS, d), lambda i: (_SRC_DT // SSD_HEADS, 0))],
        out_specs=pl.BlockSpec((LANES, d), lambda i: (0, 0)),
        out_shape=jax.ShapeDtypeStruct((LANES, d), BF16),
        name="wprep_dt",
    )(wt)


def _norm_rows(x_ref, nw_ref, wdt_ref, xn_ref, dt_ref):
    x = x_ref[...]
    ms = jnp.mean(x * x, axis=-1, keepdims=True)
    xn = ((x * lax.rsqrt(ms + NORM_EPS)) * nw_ref[...]).astype(BF16)
    xn_ref[...] = xn
    dt_ref[...] = lax.dot_general(xn, wdt_ref[...], _NT, preferred_element_type=F32)


def _wprep_inproj_kernel(tbl_ref, x_ref, nw_ref, w_ref, wdt_ref, wout_ref, proj_ref, dt_ref, xn_ref):
    del tbl_ref

    @pl.when(pl.program_id(0) == 0)
    def _():
        _norm_rows(x_ref, nw_ref, wdt_ref, xn_ref, dt_ref)

    wb = w_ref[...].astype(BF16)
    wout_ref[...] = wb
    proj_ref[...] = lax.dot_general(xn_ref[...], wb, _NT, preferred_element_type=F32)


def _wprep_inproj(wt, w_dt, x, norm_w):
    m, d = x.shape
    ntiles = len(_SRC_TILES)
    assert ntiles * PROJ_TN == PROJ_COLS
    return pl.pallas_call(
        _wprep_inproj_kernel,
        grid_spec=pltpu.PrefetchScalarGridSpec(
            num_scalar_prefetch=1,
            grid=(ntiles,),
            in_specs=[
                pl.BlockSpec((m, d), lambda j, tbl: (0, 0)),
                pl.BlockSpec((1, d), lambda j, tbl: (0, 0)),
                pl.BlockSpec((pl.Element(PROJ_TN), pl.Element(d)),
                             lambda j, tbl: (pl.multiple_of(tbl[j] * W_ALIGN, W_ALIGN), 0)),
                pl.BlockSpec((LANES, d), lambda j, tbl: (0, 0)),
            ],
            out_specs=[
                pl.BlockSpec((PROJ_TN, d), lambda j, tbl: (j, 0)),
                pl.BlockSpec((m, PROJ_TN), lambda j, tbl: (0, j)),
                pl.BlockSpec((m, LANES), lambda j, tbl: (0, 0)),
            ],
            scratch_shapes=[pltpu.VMEM((m, d), BF16)],
        ),
        out_shape=[jax.ShapeDtypeStruct((PROJ_COLS, d), BF16), jax.ShapeDtypeStruct((m, PROJ_COLS), F32),
                   jax.ShapeDtypeStruct((m, LANES), F32)],
        compiler_params=_params("arbitrary"),
        name="wprep_inproj",
    )(jnp.asarray(_SRC_TILES, jnp.int32), x, norm_w, wt, w_dt)


def _inproj_kernel(x_ref, nw_ref, w_ref, wdt_ref, proj_ref, dt_ref, xn_ref):
    @pl.when(pl.program_id(1) == 0)
    def _():
        _norm_rows(x_ref, nw_ref, wdt_ref, xn_ref, dt_ref)

    proj_ref[...] = lax.dot_general(xn_ref[...], w_ref[...], _NT, preferred_element_type=F32)


def _inproj(x, norm_w, w_main, w_dt, tm):
    m, d = x.shape
    n = w_main.shape[0]
    return pl.pallas_call(
        _inproj_kernel,
        grid=(m // tm, n // PROJ_TN),
        in_specs=[
            pl.BlockSpec((tm, d), lambda i, j: (i, 0)),
            pl.BlockSpec((1, d), lambda i, j: (0, 0)),
            pl.BlockSpec((PROJ_TN, d), lambda i, j: (j, 0)),
            pl.BlockSpec((LANES, d), lambda i, j: (0, 0)),
        ],
        out_specs=[
            pl.BlockSpec((tm, PROJ_TN), lambda i, j: (i, j)),
            pl.BlockSpec((tm, LANES), lambda i, j: (i, 0)),
        ],
        out_shape=[jax.ShapeDtypeStruct((m, n), F32), jax.ShapeDtypeStruct((m, LANES), F32)],
        scratch_shapes=[pltpu.VMEM((tm, d), BF16)],
        compiler_params=_params("arbitrary", "arbitrary"),
        name="inproj",
    )(x, norm_w, w_main, w_dt)


def _ssd_chunk(r0, xbc_ref, za_ref, dtr_ref, convw_ref, convb_ref, dtb_ref, alog_ref, dexp_ref, nw_ref,
               y_ref, ht_scr, ext_scr, mask_rows, need_y):
    L = CHUNK
    P2 = 2 * SSD_HEAD_DIM
    GW = SSD_WIDTH // SSD_GROUPS
    time_rows = lambda t0, n: pl.ds(CONV_PITCH * t0, n, stride=CONV_PITCH)

    w = convw_ref[...]
    bias = convb_ref[...]
    conv_parts = []
    for s in range(SSD_XBC // LANES):
        ls = slice(s * LANES, (s + 1) * LANES)
        xc = xbc_ref[r0:r0 + L, ls]
        ext_scr[s, time_rows(SUBLANES, L), :] = xc
        acc = bias[:, ls]
        for k in range(SSD_CONV - 1):
            acc = acc + ext_scr[s, time_rows(SUBLANES - (SSD_CONV - 1) + k, L), :] * w[k:k + 1, ls]
        conv_parts.append(acc + xc * w[SSD_CONV - 1:SSD_CONV, ls])
        ext_scr[s, time_rows(0, SUBLANES), :] = xc[L - SUBLANES:L, :]
    xbc = _silu(jnp.concatenate(conv_parts, axis=1))
    xs = xbc[:, :SSD_WIDTH]
    bmat = xbc[:, SSD_WIDTH:SSD_WIDTH + SSD_GROUPS * SSD_STATE]
    cmat = xbc[:, SSD_WIDTH + SSD_GROUPS * SSD_STATE:]

    rows = lax.broadcasted_iota(jnp.int32, (L, L), 0)
    cols = lax.broadcasted_iota(jnp.int32, (L, L), 1)
    causal = rows >= cols
    lane_lo = cols < SSD_HEAD_DIM

    dt = _softplus(dtr_ref[r0:r0 + L, :] + dtb_ref[...])
    if mask_rows:
        dt = jnp.where(rows < mask_rows, 0.0, dt)
    a = dt * (-jnp.exp(alog_ref[...]))
    tril = jnp.where(causal, 1.0, 0.0).astype(BF16)
    a1, a2, a3 = _split3(a)
    acum = (jnp.dot(tril, a1, preferred_element_type=F32)
            + jnp.dot(tril, a2, preferred_element_type=F32)
            + jnp.dot(tril, a3, preferred_element_type=F32))
    a2 = acum * LOG2E
    e_cum = jnp.exp2(a2)
    w_end = dt * jnp.exp2(a2[L - 1:L, :] - a2)
    a2dt_t = (a2 - jnp.log2(dt)).T

    dexp = dexp_ref[...]
    xs_b = xs.astype(BF16)
    y_parts = []
    for g in range(SSD_GROUPS):
        bg = bmat[:, g * SSD_STATE:(g + 1) * SSD_STATE].astype(BF16)
        cg = cmat[:, g * SSD_STATE:(g + 1) * SSD_STATE].astype(BF16)
        ht_g = ht_scr[:, g * GW:(g + 1) * GW]
        if need_y:
            cb = lax.dot_general(cg, bg, (((1,), (1,)), ((), ())), preferred_element_type=F32)
            y_off = jnp.dot(cg, ht_g.astype(BF16), preferred_element_type=F32)
        xw_parts = []
        elast_parts = []
        for jj in range(GW // P2):
            lo = g * GW + jj * P2
            h0 = lo // SSD_HEAD_DIM
            yd, eb, wb = [], [], []
            for h in (h0, h0 + 1):
                if need_y:
                    colb = jnp.broadcast_to(a2[:, h:h + 1], (L, L))
                    m = cb * jnp.exp2(jnp.where(causal, colb - a2dt_t[h:h + 1, :], -jnp.inf))
                    yd.append(jnp.dot(m.astype(BF16), xs_b[:, lo:lo + P2], preferred_element_type=F32))
                eb.append(jnp.broadcast_to(e_cum[:, h:h + 1], (L, L)))
                wb.append(jnp.broadcast_to(w_end[:, h:h + 1], (L, L)))
            xs_pair = xs[:, lo:lo + P2]
            if need_y:
                y_pair = (jnp.where(lane_lo, yd[0], yd[1])
                          + y_off[:, jj * P2:(jj + 1) * P2] * jnp.where(lane_lo, eb[0], eb[1]))
                y_parts.append(y_pair + dexp[:, lo:lo + P2] * xs_pair)
            xw_parts.append(xs_pair * jnp.where(lane_lo, wb[0], wb[1]))
            elast_parts.append(jnp.where(lane_lo[0:1, :], eb[0][L - 1:L, :], eb[1][L - 1:L, :]))
        xw = jnp.concatenate(xw_parts, axis=1)
        elast = jnp.concatenate(elast_parts, axis=1)
        st = lax.dot_general(bg, xw.astype(BF16), (((0,), (0,)), ((), ())),
                             preferred_element_type=F32)
        ht_scr[:, g * GW:(g + 1) * GW] = ht_g * elast + st

    if not need_y:
        y_ref[r0:r0 + L, :] = jnp.zeros((L, SSD_WIDTH), y_ref.dtype)
        return
    y = jnp.concatenate(y_parts, axis=1)
    y = y * _silu(za_ref[r0:r0 + L, :])
    nw = nw_ref[...]
    outs = []
    for g in range(SSD_GROUPS):
        yg = y[:, g * GW:(g + 1) * GW]
        ms = jnp.mean(yg * yg, axis=-1, keepdims=True)
        outs.append((yg * lax.rsqrt(ms + NORM_EPS)) * nw[:, g * GW:(g + 1) * GW])
    y_ref[r0:r0 + L, :] = jnp.concatenate(outs, axis=1).astype(y_ref.dtype)


def _ssd_kernel(xbc_ref, za_ref, dtr_ref, ht0_ref, tail0_ref, convw_ref, convb_ref, dtb_ref, alog_ref,
                dexp_ref, nw_ref, y_ref, h_ref, ht_ref, tail_ref, ht_scr, ext_scr, *, mask_rows, need_y):
    c = pl.program_id(1)
    nrows = xbc_ref.shape[0]

    @pl.when(c == 0)
    def _():
        ht_scr[...] = ht0_ref[...]
        for s in range(SSD_XBC // LANES):
            ext_scr[s, pl.ds(0, SUBLANES, stride=CONV_PITCH), :] = tail0_ref[:, s * LANES:(s + 1) * LANES]

    for r0 in range(0, nrows, CHUNK):
        _ssd_chunk(r0, xbc_ref, za_ref, dtr_ref, convw_ref, convb_ref, dtb_ref, alog_ref, dexp_ref, nw_ref,
                   y_ref, ht_scr, ext_scr, mask_rows, need_y)

    @pl.when(c == pl.num_programs(1) - 1)
    def _():
        ht = ht_scr[...]
        ht_ref[...] = ht
        h_ref[...] = ht.T
        tail_ref[...] = xbc_ref[nrows - SUBLANES:nrows, :]


def _ssd(proj, dt_raw, ht0, tail0, conv_w, conv_b, dt_bias, a_log, d_exp, norm_w, *, batch, nchunks,
         row_block0, mask_rows, need_y=True):
    L = CHUNK
    rows = batch * nchunks * L
    rb = lambda b, c: row_block0 + b * nchunks + c
    const2 = lambda shape: pl.BlockSpec(shape, lambda b, c: (0, 0))
    return pl.pallas_call(
        functools.partial(_ssd_kernel, mask_rows=mask_rows, need_y=need_y),
        grid=(batch, nchunks),
        in_specs=[
            pl.BlockSpec((L, SSD_XBC), lambda b, c: (rb(b, c), COL_XBC // SSD_XBC)),
            pl.BlockSpec((L, SSD_WIDTH), lambda b, c: (rb(b, c), COL_ZA // SSD_WIDTH)),
            pl.BlockSpec((L, LANES), lambda b, c: (rb(b, c), 0)),
            const2((SSD_STATE, SSD_WIDTH)),
            const2((SUBLANES, SSD_XBC)),
            const2((SSD_CONV, SSD_XBC)),
            const2((1, SSD_XBC)),
            const2((1, LANES)),
            const2((1, LANES)),
            const2((1, SSD_WIDTH)),
            const2((1, SSD_WIDTH)),
        ],
        out_specs=[
            pl.BlockSpec((L, SSD_WIDTH), lambda b, c: (b * nchunks + c, 0)),
            pl.BlockSpec((None, SSD_WIDTH, SSD_STATE), lambda b, c: (b, 0, 0)),
            pl.BlockSpec((None, SSD_STATE, SSD_WIDTH), lambda b, c: (b, 0, 0)),
            pl.BlockSpec((None, SUBLANES, SSD_XBC), lambda b, c: (b, 0, 0)),
        ],
        out_shape=[
            jax.ShapeDtypeStruct((rows, SSD_WIDTH), BF16),
            jax.ShapeDtypeStruct((batch, SSD_WIDTH, SSD_STATE), F32),
            jax.ShapeDtypeStruct((batch, SSD_STATE, SSD_WIDTH), F32),
            jax.ShapeDtypeStruct((batch, SUBLANES, SSD_XBC), F32),
        ],
        scratch_shapes=[pltpu.VMEM((SSD_STATE, SSD_WIDTH), F32),
                        pltpu.VMEM((SSD_XBC // LANES, CONV_PITCH * (SUBLANES + CHUNK), LANES), F32)],
        compiler_params=_params("arbitrary", "arbitrary"),
        name="ssd",
    )(proj, proj, dt_raw, ht0, tail0, conv_w, conv_b, dt_bias, a_log, d_exp, norm_w)


def _ssd_step_kernel(xbc_ref, za_ref, dtr_ref, cs_ref, h_ref, convw_ref, convb_ref, dtb_ref, alog_ref,
                     dexp_ref, nw_ref, y_ref, hout_ref, csout_ref):
    R = SUBLANES
    GW = SSD_WIDTH // SSD_GROUPS
    x = xbc_ref[...]
    w = convw_ref[...]
    s0 = cs_ref[0]
    s1 = cs_ref[1]
    s2 = cs_ref[2]
    conv = convb_ref[...] + s0 * w[0:1]
    conv = conv + s1 * w[1:2]
    conv = conv + s2 * w[2:3]
    conv = conv + x * w[3:4]
    csout_ref[0] = s1
    csout_ref[1] = s2
    csout_ref[2] = x
    xbc = _silu(conv)
    xs = xbc[:, :SSD_WIDTH]
    bmat = xbc[:, SSD_WIDTH:SSD_WIDTH + SSD_GROUPS * SSD_STATE]
    cmat = xbc[:, SSD_WIDTH + SSD_GROUPS * SSD_STATE:]

    dt = _softplus(dtr_ref[...] + dtb_ref[...])
    da = jnp.exp(dt * (-jnp.exp(alog_ref[...])))
    dt_t = dt.T
    da_t = da.T
    expand = lambda v: jnp.concatenate(
        [jnp.broadcast_to(v[h:h + 1, :], (SSD_HEAD_DIM, R)) for h in range(SSD_HEADS)], axis=0)
    xd_t = xs.T * expand(dt_t)
    da_te = expand(da_t)

    cmat_b = cmat.astype(BF16)
    yrows = []
    for i in range(R):
        bexp = jnp.concatenate(
            [jnp.broadcast_to(bmat[i:i + 1, g * SSD_STATE:(g + 1) * SSD_STATE], (GW, SSD_STATE))
             for g in range(SSD_GROUPS)], axis=0)
        hn = h_ref[i] * da_te[:, i:i + 1] + xd_t[:, i:i + 1] * bexp
        hout_ref[i] = hn
        cg = jnp.concatenate([cmat_b[i:i + 1, g * SSD_STATE:(g + 1) * SSD_STATE] for g in range(SSD_GROUPS)]
                             + [jnp.zeros((R - SSD_GROUPS, SSD_STATE), BF16)], axis=0)
        yg = lax.dot_general(cg, hn.astype(BF16), (((1,), (1,)), ((), ())), preferred_element_type=F32)
        yrows.append(jnp.concatenate([yg[g:g + 1, g * GW:(g + 1) * GW] for g in range(SSD_GROUPS)], axis=1))
    y = jnp.concatenate(yrows, axis=0)
    y = y + dexp_ref[...] * xs
    y = y * _silu(za_ref[...])
    nw = nw_ref[...]
    outs = []
    for g in range(SSD_GROUPS):
        yg = y[:, g * GW:(g + 1) * GW]
        ms = jnp.mean(yg * yg, axis=-1, keepdims=True)
        outs.append((yg * lax.rsqrt(ms + NORM_EPS)) * nw[:, g * GW:(g + 1) * GW])
    y_ref[...] = jnp.concatenate(outs, axis=1).astype(y_ref.dtype)


def _ssd_step(proj, dt_raw, conv_state, ssd_state, conv_w, conv_b, dt_bias, a_log, d_exp, norm_w, *, nseq):
    R = SUBLANES
    const2 = lambda shape: pl.BlockSpec(shape, lambda i: (0, 0))
    return pl.pallas_call(
        _ssd_step_kernel,
        grid=(nseq // R,),
        in_specs=[
            pl.BlockSpec((R, SSD_XBC), lambda i: (i, COL_XBC // SSD_XBC)),
            pl.BlockSpec((R, SSD_WIDTH), lambda i: (i, COL_ZA // SSD_WIDTH)),
            pl.BlockSpec((R, LANES), lambda i: (i, 0)),
            pl.BlockSpec((SSD_CONV - 1, R, SSD_XBC), lambda i: (0, i, 0)),
            pl.BlockSpec((R, SSD_WIDTH, SSD_STATE), lambda i: (i, 0, 0)),
            const2((SSD_CONV, SSD_XBC)),
            const2((1, SSD_XBC)),
            const2((1, LANES)),
            const2((1, LANES)),
            const2((1, SSD_WIDTH)),
            const2((1, SSD_WIDTH)),
        ],
        out_specs=[
            pl.BlockSpec((R, SSD_WIDTH), lambda i: (i, 0)),
            pl.BlockSpec((R, SSD_WIDTH, SSD_STATE), lambda i: (i, 0, 0)),
            pl.BlockSpec((SSD_CONV - 1, R, SSD_XBC), lambda i: (0, i, 0)),
        ],
        out_shape=[
            jax.ShapeDtypeStruct((nseq, SSD_WIDTH), BF16),
            jax.ShapeDtypeStruct((nseq, SSD_WIDTH, SSD_STATE), F32),
            jax.ShapeDtypeStruct((SSD_CONV - 1, nseq, SSD_XBC), F32),
        ],
        compiler_params=_params("arbitrary"),
        name="ssd_step",
    )(proj, proj, dt_raw, conv_state, ssd_state, conv_w, conv_b, dt_bias, a_log, d_exp, norm_w)


def _s5_block_diag(w, rows_per_group, cols_per_group):
    ngrp, r, lanes = w.shape
    width = ngrp * cols_per_group
    tiled = jnp.concatenate([w.reshape(ngrp * r, lanes)] * (width // lanes), axis=1)
    row_g = lax.broadcasted_iota(jnp.int32, tiled.shape, 0) // rows_per_group
    col_g = lax.broadcasted_iota(jnp.int32, tiled.shape, 1) // cols_per_group
    return jnp.where(row_g == col_g, tiled, 0.0)


def _s5_weights(bbr_ref, bbi_ref, cr_ref, ci_ref):
    return tuple(_s5_block_diag(r[...], S5_GROUP, S5_STATE) for r in (bbr_ref, bbi_ref, cr_ref, ci_ref))


def _s5_pitch(seg_len):
    return seg_len if (seg_len // SUBLANES) % 2 else seg_len + SUBLANES


def _s5_kernel(*refs, seq, nconv, need_y):
    (u_ref, h0r_ref, h0i_ref, bbr_ref, bbi_ref, ctr_ref, cti_ref, ar_ref, ai_ref, d_ref) = refs[:10]
    w_in_refs = refs[10:10 + nconv]
    y_ref, hr_ref, hi_ref = refs[10 + nconv:13 + nconv]
    w_out_refs = refs[13 + nconv:13 + 2 * nconv]
    pad, pe, po, cr, ci, yeh, hb_out, w2r_scr, w2i_scr, cta_scr, ctb_scr, k0_scr = refs[13 + 2 * nconv:]
    for wi, wo in zip(w_in_refs, w_out_refs):
        wo[...] = wi[...].astype(BF16)

    NS = SUBLANES
    SL = seq // NS
    H = SL // 2
    pitch = _s5_pitch(SL)
    nk = S5_BLK_U // LANES
    n = S5_BLK_N
    a_re = ar_ref[...]
    a_im = ai_ref[...]
    bf = lambda v: v.astype(BF16)

    @pl.when(pl.program_id(1) == 0)
    def _():
        bt_re, bt_im, ct_re, ct_im = _s5_weights(bbr_ref, bbi_ref, ctr_ref, cti_ref)
        w2r_scr[...] = jnp.concatenate([bf(bt_re * a_re - bt_im * a_im), bf(bt_re)], axis=0)
        w2i_scr[...] = jnp.concatenate([bf(bt_re * a_im + bt_im * a_re), bf(bt_im)], axis=0)
        if need_y:
            cta_scr[0] = bf(ct_re * a_re - ct_im * a_im)
            cta_scr[1] = bf(ct_re * a_im + ct_im * a_re)
            ctb_scr[0] = bf(ct_re)
            ctb_scr[1] = bf(ct_im)
            k0_scr[...] = bf(lax.dot_general(bf(bt_re), bf(ct_re), _NT, preferred_element_type=F32)
                             - lax.dot_general(bf(bt_im), bf(ct_im), _NT, preferred_element_type=F32))

    for s in range(NS):
        for k in range(nk):
            pad[k, s * pitch:s * pitch + SL, :] = u_ref[s * SL:(s + 1) * SL, k * LANES:(k + 1) * LANES]

    for jp in range(H):
        for k in range(nk):
            pe[jp * NS:(jp + 1) * NS, k * LANES:(k + 1) * LANES] = pad[k, pl.ds(2 * jp, NS, stride=pitch), :]
            po[jp * NS:(jp + 1) * NS, k * LANES:(k + 1) * LANES] = pad[k, pl.ds(2 * jp + 1, NS, stride=pitch), :]
    half = seq // 2
    NB = 4 if H % 8 == 0 else 1
    rbk = half // NB
    spb = rbk // NS

    qc = 4 * S5_GROUP
    qn = 4 * S5_STATE

    def input_block(kb):
        rows = slice(kb * rbk, (kb + 1) * rbk)
        ue_b = bf(pe[rows, :])
        uo_b = bf(po[rows, :])
        for q in range(S5_BLK_U // qc):
            u_q = jnp.concatenate([ue_b[:, q * qc:(q + 1) * qc], uo_b[:, q * qc:(q + 1) * qc]], axis=1)
            rows_q = [slice(q * qc, (q + 1) * qc), slice(S5_BLK_U + q * qc, S5_BLK_U + (q + 1) * qc)]
            cols_q = slice(q * qn, (q + 1) * qn)
            wq_re = jnp.concatenate([w2r_scr[r, cols_q] for r in rows_q], axis=0)
            wq_im = jnp.concatenate([w2i_scr[r, cols_q] for r in rows_q], axis=0)
            cr[rows, cols_q] = jnp.dot(u_q, wq_re, preferred_element_type=F32)
            ci[rows, cols_q] = jnp.dot(u_q, wq_im, preferred_element_type=F32)

    a2_re = a_re * a_re - a_im * a_im
    a2_im = 2.0 * (a_re * a_im)
    ar = jnp.broadcast_to(a2_re, (NS, n))
    ai = jnp.broadcast_to(a2_im, (NS, n))

    def step(carry, r0):
        hr, hi = carry
        nr = ar * hr - ai * hi + cr[r0:r0 + NS, :]
        ni = ar * hi + ai * hr + ci[r0:r0 + NS, :]
        return nr, ni

    zero = jnp.zeros((NS, n), F32)
    carry = (zero, zero)
    input_block(0)
    for kb in range(NB):
        if kb + 1 < NB:
            input_block(kb + 1)
        for t in range(spb):
            carry = step(carry, (kb * spb + t) * NS)
    er, ei = carry

    asr, asi = a_re, a_im
    for _ in range(SL.bit_length() - 1):
        asr, asi = asr * asr - asi * asi, 2.0 * (asr * asi)
    gr = [h0r_ref[...]]
    gi = [h0i_ref[...]]
    for s in range(NS):
        gr.append(er[s:s + 1, :] + (asr * gr[s] - asi * gi[s]))
        gi.append(ei[s:s + 1, :] + (asr * gi[s] + asi * gr[s]))
    hr_ref[...] = gr[NS]
    hi_ref[...] = gi[NS]
    if not need_y:
        y_ref[...] = jnp.zeros(y_ref.shape, y_ref.dtype)
        return

    def scan_block(kb, c0):
        for p in range(spb // 2):
            r0 = (kb * spb + 2 * p) * NS
            c1 = step(c0, r0)
            c2 = step(c1, r0 + NS)
            hb_out[r0:r0 + 2 * NS, 0:n] = bf(jnp.concatenate([c1[0], c2[0]], axis=0))
            hb_out[r0:r0 + 2 * NS, n:2 * n] = bf(jnp.concatenate([c1[1], c2[1]], axis=0))
            c0 = c2
        return c0

    d = d_ref[...]
    cta_re, cta_im = cta_scr[0], cta_scr[1]
    ctb_re, ctb_im = ctb_scr[0], ctb_scr[1]
    k0 = k0_scr[...]

    def z_weights(q, part):
        sl = (slice(q * qc, (q + 1) * qc), slice(q * qn, (q + 1) * qn))
        return jnp.concatenate([ctb_scr[part][sl], cta_scr[part][sl]], axis=0)

    g_re = bf(jnp.concatenate(gr[:NS], axis=0))
    g_im = bf(jnp.concatenate(gi[:NS], axis=0))
    yeh[0:NS, :] = (lax.dot_general(g_re, cta_re, _NT, preferred_element_type=F32)
                    - lax.dot_general(g_im, cta_im, _NT, preferred_element_type=F32))

    def output_block(kb):
        rows = slice(kb * rbk, (kb + 1) * rbk)
        zs = []
        for q in range(S5_BLK_U // qc):
            cols_q = slice(q * qn, (q + 1) * qn)
            zs.append(lax.dot_general(hb_out[rows, cols_q], z_weights(q, 0), _NT, preferred_element_type=F32)
                      - lax.dot_general(hb_out[rows, n + q * qn:n + (q + 1) * qn], z_weights(q, 1), _NT,
                                        preferred_element_type=F32))
        y_odd = jnp.concatenate([z[:, 0:qc] for z in zs], axis=1)
        yeh[NS + kb * rbk:NS + (kb + 1) * rbk, :] = jnp.concatenate([z[:, qc:2 * qc] for z in zs], axis=1)
        y_even = yeh[rows, :] + jnp.dot(bf(pe[rows, :]), k0, preferred_element_type=F32)
        po[rows, :] = _gelu_tanh(y_odd + d * po[rows, :])
        pe[rows, :] = _gelu_tanh(y_even + d * pe[rows, :])

    carry = scan_block(0, (jnp.concatenate(gr[:NS], axis=0), jnp.concatenate(gi[:NS], axis=0)))
    for kb in range(NB):
        if kb + 1 < NB:
            carry = scan_block(kb + 1, carry)
        output_block(kb)

    for jp in range(H):
        for k in range(nk):
            pad[k, pl.ds(2 * jp, NS, stride=pitch), :] = pe[jp * NS:(jp + 1) * NS, k * LANES:(k + 1) * LANES]
            pad[k, pl.ds(2 * jp + 1, NS, stride=pitch), :] = po[jp * NS:(jp + 1) * NS, k * LANES:(k + 1) * LANES]
    for s in range(NS):
        for k in range(nk):
            y_ref[s * SL:(s + 1) * SL, k * LANES:(k + 1) * LANES] = pad[k, s * pitch:s * pitch + SL, :]


def _s5(proj, h0r, h0i, bb_re, bb_im, ct_re, ct_im, ab_re, ab_im, d_s5, *, batch, seq, row_block0, need_y=True,
        convert=()):
    ub0 = COL_UB // S5_BLK_U
    nsteps = batch * S5_BLOCKS
    slice_spec = lambda w: pl.BlockSpec((w.shape[0] // nsteps, w.shape[1]), lambda j, b: (j * batch + b, 0))
    assert all(w.shape[0] % (2 * SUBLANES * nsteps) == 0 for w in convert)
    seg = seq // SUBLANES
    assert seg & (seg - 1) == 0
    gpb = S5_GROUPS // S5_BLOCKS
    vec = lambda width: pl.BlockSpec((1, width), lambda j, b: (0, j))
    wspec = lambda r: pl.BlockSpec((gpb, r, LANES), lambda j, b: (j, 0, 0))
    st_out = pl.BlockSpec((None, 1, S5_BLK_N), lambda j, b: (b, 0, j))
    return pl.pallas_call(
        functools.partial(_s5_kernel, seq=seq, nconv=len(convert), need_y=need_y),
        grid=(S5_BLOCKS, batch),
        in_specs=[
            pl.BlockSpec((seq, S5_BLK_U), lambda j, b: (row_block0 + b, ub0 + j)),
            vec(S5_BLK_N), vec(S5_BLK_N),
            wspec(S5_GROUP), wspec(S5_GROUP), wspec(S5_GROUP), wspec(S5_GROUP),
            vec(S5_BLK_N), vec(S5_BLK_N), vec(S5_BLK_U),
        ] + [slice_spec(w) for w in convert],
        out_specs=[pl.BlockSpec((seq, S5_BLK_U), lambda j, b: (b, j)), st_out, st_out]
        + [slice_spec(w) for w in convert],
        out_shape=[
            jax.ShapeDtypeStruct((batch * seq, S5_WIDTH), F32),
            jax.ShapeDtypeStruct((batch, 1, S5_NSTATE), F32),
            jax.ShapeDtypeStruct((batch, 1, S5_NSTATE), F32),
        ] + [jax.ShapeDtypeStruct(w.shape, BF16) for w in convert],
        scratch_shapes=[
            pltpu.VMEM((S5_BLK_U // LANES, SUBLANES * _s5_pitch(seg), LANES), F32),
            pltpu.VMEM((seq // 2, S5_BLK_U), F32),
            pltpu.VMEM((seq // 2, S5_BLK_U), F32),
            pltpu.VMEM((seq // 2, S5_BLK_N), F32),
            pltpu.VMEM((seq // 2, S5_BLK_N), F32),
            pltpu.VMEM((seq // 2 + SUBLANES, S5_BLK_U), F32),
            pltpu.VMEM((seq // 2, 2 * S5_BLK_N), BF16),
            pltpu.VMEM((2 * S5_BLK_U, S5_BLK_N), BF16),
            pltpu.VMEM((2 * S5_BLK_U, S5_BLK_N), BF16),
            pltpu.VMEM((2, S5_BLK_U, S5_BLK_N), BF16),
            pltpu.VMEM((2, S5_BLK_U, S5_BLK_N), BF16),
            pltpu.VMEM((S5_BLK_U, S5_BLK_U), BF16),
        ],
        compiler_params=_params("arbitrary", "arbitrary"),
        name="s5",
    )(proj, h0r, h0i, bb_re, bb_im, ct_re, ct_im, ab_re, ab_im, d_s5, *convert)


def _s5_step_kernel(u_ref, h0r_ref, h0i_ref, bbr_ref, bbi_ref, ctr_ref, cti_ref, ar_ref, ai_ref, d_ref,
                    y_ref, hr_ref, hi_ref):
    bt_re, bt_im, ct_re, ct_im = (w.astype(BF16) for w in _s5_weights(bbr_ref, bbi_ref, ctr_ref, cti_ref))
    u = u_ref[...]
    ub = u.astype(BF16)
    ar = ar_ref[...]
    ai = ai_ref[...]
    h0r = h0r_ref[...].T
    h0i = h0i_ref[...].T
    hr = jnp.dot(ub, bt_re, preferred_element_type=F32) + (ar * h0r - ai * h0i)
    hi = jnp.dot(ub, bt_im, preferred_element_type=F32) + (ar * h0i + ai * h0r)
    hr_ref[...] = hr.T
    hi_ref[...] = hi.T
    y = (lax.dot_general(hr.astype(BF16), ct_re, _NT, preferred_element_type=F32)
         - lax.dot_general(hi.astype(BF16), ct_im, _NT, preferred_element_type=F32))
    y_ref[...] = _gelu_tanh(y + d_ref[...] * u)


def _s5_step(proj, h0r, h0i, bb_re, bb_im, ct_re, ct_im, ab_re, ab_im, d_s5, *, nseq):
    ub0 = COL_UB // S5_BLK_U
    gpb = S5_GROUPS // S5_BLOCKS
    vec = lambda width: pl.BlockSpec((1, width), lambda j: (0, j))
    wspec = lambda r: pl.BlockSpec((gpb, r, LANES), lambda j: (j, 0, 0))
    st = pl.BlockSpec((S5_BLK_N, nseq), lambda j: (j, 0))
    return pl.pallas_call(
        _s5_step_kernel,
        grid=(S5_BLOCKS,),
        in_specs=[
            pl.BlockSpec((nseq, S5_BLK_U), lambda j: (0, ub0 + j)),
            st, st,
            wspec(S5_GROUP), wspec(S5_GROUP), wspec(S5_GROUP), wspec(S5_GROUP),
            vec(S5_BLK_N), vec(S5_BLK_N), vec(S5_BLK_U),
        ],
        out_specs=[pl.BlockSpec((nseq, S5_BLK_U), lambda j: (0, j)), st, st],
        out_shape=[
            jax.ShapeDtypeStruct((nseq, S5_WIDTH), F32),
            jax.ShapeDtypeStruct((S5_NSTATE, nseq), F32),
            jax.ShapeDtypeStruct((S5_NSTATE, nseq), F32),
        ],
        compiler_params=_params("arbitrary"),
        name="s5_step",
    )(proj, h0r, h0i, bb_re, bb_im, ct_re, ct_im, ab_re, ab_im, d_s5)


def _tail_rows(yn_ref, ybg_ref, zb_ref, ga_ref, gb_ref, x_ref, wpa_ref, wglu_ref, bglu_ref, wpb_ref,
               wout_ref, fnw_ref, out_ref, arrived=lambda name: None):
    sg_a = _sigmoid(ga_ref[...])
    sg_b = _sigmoid(gb_ref[...])
    sz_b = _silu(zb_ref[...])
    yb = ybg_ref[...]
    arrived("glu")
    glu = jnp.dot(yb.astype(BF16), wglu_ref[...], preferred_element_type=F32) + bglu_ref[...]
    arrived("proj_a")
    ya = jnp.dot(yn_ref[...], wpa_ref[...], preferred_element_type=F32)
    yb = (yb * _sigmoid(glu)) * sz_b
    arrived("proj_b")
    ybp = jnp.dot(yb.astype(BF16), wpb_ref[...], preferred_element_type=F32)
    mixed = sg_a * ya + sg_b * ybp
    arrived("out")
    o = x_ref[...] + jnp.dot(mixed.astype(BF16), wout_ref[...], preferred_element_type=F32)
    ms = jnp.mean(o * o, axis=-1, keepdims=True)
    out_ref[...] = (o * lax.rsqrt(ms + NORM_EPS)) * fnw_ref[...]


def _tail_kernel(*refs, n_main):
    main_in, samp_in, (out_m_ref, out_s_ref) = refs[0:6], refs[6:12], refs[18:20]
    wpa_hbm, wglu_hbm, bglu_ref, wpb_hbm, wout_hbm, fnw_ref = refs[12:18]
    wpa_ref, wglu_ref, wpb_ref, wout_ref, sem = refs[20:25]
    weights = (wpa_ref, wglu_ref, bglu_ref, wpb_ref, wout_ref, fnw_ref)
    i = pl.program_id(0)

    copies = {
        "glu": pltpu.make_async_copy(wglu_hbm, wglu_ref, sem.at[0]),
        "proj_a": pltpu.make_async_copy(wpa_hbm, wpa_ref, sem.at[1]),
        "proj_b": pltpu.make_async_copy(wpb_hbm, wpb_ref, sem.at[2]),
        "out": pltpu.make_async_copy(wout_hbm, wout_ref, sem.at[3]),
    }

    @pl.when(i == 0)
    def _():
        for name in ("glu", "proj_a", "proj_b", "out"):
            copies[name].start()
        _tail_rows(*main_in, *weights, out_m_ref, arrived=lambda name: copies[name].wait())

    @pl.when((i > 0) & (i < n_main))
    def _():
        _tail_rows(*main_in, *weights, out_m_ref)

    @pl.when(i == n_main)
    def _():
        _tail_rows(*samp_in, *weights, out_s_ref)


def _tail(main, samp, w_proj_a, w_glu, b_glu, w_proj_b, w_out, final_norm_w, *, tm):
    m, d = main[3].shape
    ms = samp[3].shape[0]
    n_main = m // tm
    resident = lambda shape: pl.BlockSpec(shape, lambda i: (0, 0), pipeline_mode=pl.Buffered(1))
    row = lambda i: jnp.minimum(i, n_main - 1)
    once = lambda width, col: pl.BlockSpec((ms, width), lambda i: (0, col), pipeline_mode=pl.Buffered(1))
    expand = lambda t: (t[0], t[1], t[2], t[2], t[2], t[3])
    in_hbm = pl.BlockSpec(memory_space=pl.ANY)
    assert n_main >= 1 and w_proj_a.dtype == BF16 and w_out.dtype == BF16
    return pl.pallas_call(
        functools.partial(_tail_kernel, n_main=n_main),
        grid=(n_main + 1,),
        in_specs=[
            pl.BlockSpec((tm, SSD_WIDTH), lambda i: (row(i), 0)),
            pl.BlockSpec((tm, S5_WIDTH), lambda i: (row(i), 0)),
            pl.BlockSpec((tm, S5_WIDTH), lambda i: (row(i), COL_ZB // S5_WIDTH)),
            pl.BlockSpec((tm, d), lambda i: (row(i), COL_GA // D_MODEL)),
            pl.BlockSpec((tm, d), lambda i: (row(i), COL_GB // D_MODEL)),
            pl.BlockSpec((tm, d), lambda i: (row(i), 0)),
            once(SSD_WIDTH, 0),
            once(S5_WIDTH, 0),
            once(S5_WIDTH, COL_ZB // S5_WIDTH),
            once(d, COL_GA // D_MODEL),
            once(d, COL_GB // D_MODEL),
            once(d, 0),
            in_hbm,
            in_hbm,
            resident((1, S5_WIDTH)),
            in_hbm,
            in_hbm,
            resident((1, d)),
        ],
        out_specs=[
            pl.BlockSpec((tm, d), lambda i: (row(i), 0)),
            pl.BlockSpec((ms, d), lambda i: (0, 0)),
        ],
        out_shape=[jax.ShapeDtypeStruct((m, d), F32), jax.ShapeDtypeStruct((ms, d), F32)],
        scratch_shapes=[
            pltpu.VMEM((SSD_WIDTH, d), BF16),
            pltpu.VMEM((S5_WIDTH, S5_WIDTH), BF16),
            pltpu.VMEM((S5_WIDTH, d), BF16),
            pltpu.VMEM((d, d), BF16),
            pltpu.SemaphoreType.DMA((4,)),
        ],
        compiler_params=_params("arbitrary", vmem=60 * 1024 * 1024),
        name="tail",
    )(*expand(main), *expand(samp), w_proj_a, w_glu, b_glu, w_proj_b, w_out, final_norm_w)


def kernel(x_prompt, x_sample, state_ssd, state_conv, state_s5_re, state_s5_im, meta_tokens, norm_w, w_in,
           conv_w, conv_b, dt_bias, a_log, d_ssd, ssd_norm_w, w_proj_a, lam_re, lam_im, log_dt_s5, b_re, b_im,
           c_re, c_im, d_s5, w_glu, b_glu, w_proj_b, w_out, final_norm_w):
    bsz, seq, d = x_prompt.shape
    nseq = x_sample.shape[0]
    assert d == D_MODEL and seq % CHUNK == 0 and nseq % SUBLANES == 0 and norm_w.shape[0] == 1
    assert meta_tokens.shape[0] == N_META and N_META <= CHUNK

    assert w_in.shape[2] == _SRC_GB + D_MODEL
    wt = jnp.transpose(w_in[0])
    w_dt = _wprep_dt(wt)
    pad_heads = lambda v: jnp.pad(v.reshape(1, SSD_HEADS), ((0, 0), (0, LANES - SSD_HEADS)))
    dtb = pad_heads(dt_bias[0])
    alog = pad_heads(a_log[0])
    d_exp = jnp.repeat(d_ssd[0], SSD_HEAD_DIM).reshape(1, SSD_WIDTH)
    nw1 = norm_w[0].reshape(1, d)
    ssd_nw = ssd_norm_w[0].reshape(1, SSD_WIDTH)
    convw = conv_w[0]
    convb = conv_b[0].reshape(1, SSD_XBC)
    bglu = b_glu[0].reshape(1, S5_WIDTH)
    fnw = final_norm_w.reshape(1, d)
    ds5 = d_s5[0].reshape(1, S5_WIDTH)

    rep = lambda v, k: jnp.concatenate([v] * k, axis=-1)
    lane_rep = LANES // S5_STATE
    ab_re, ab_im, bb_re, bb_im = _s5prep(
        rep(lam_re[0], lane_rep), rep(lam_im[0], lane_rep), log_dt_s5[0].reshape(S5_GROUPS, 1),
        rep(jnp.transpose(b_re[0], (0, 2, 1)), lane_rep), rep(jnp.transpose(b_im[0], (0, 2, 1)), lane_rep))
    ab_re = ab_re[:, :S5_STATE].reshape(1, S5_NSTATE)
    ab_im = ab_im[:, :S5_STATE].reshape(1, S5_NSTATE)
    ct_re = rep(c_re[0], lane_rep)
    ct_im = rep(c_im[0], lane_rep)

    x_main = x_prompt.reshape(bsz * seq, d)
    x_small = jnp.concatenate(
        [x_sample.reshape(nseq, d), jnp.zeros((CHUNK - N_META, d), x_prompt.dtype),
         meta_tokens.astype(x_prompt.dtype)], axis=0)
    assert nseq % CHUNK == 0
    meta_blk = nseq // CHUNK
    tm_main = 1024 if (bsz * seq) % 1024 == 0 else CHUNK
    w_main, proj_s, dt_s = _wprep_inproj(wt, w_dt, x_small, nw1)
    proj_m, dt_m = _inproj(x_main, nw1, w_main, w_dt, tm=tm_main)

    ssd_args = (convw, convb, dtb, alog, d_exp, ssd_nw)
    s5_w = (bb_re, bb_im, ct_re, ct_im)
    s5_v = (ab_re, ab_im, ds5)

    zeros_ht = jnp.zeros((SSD_STATE, SSD_WIDTH), F32)
    zeros_tail = jnp.zeros((SUBLANES, SSD_XBC), F32)
    zeros_s5 = jnp.zeros((1, S5_NSTATE), F32)
    _, _, ht_meta, tail_meta = _ssd(proj_s, dt_s, zeros_ht, zeros_tail, *ssd_args, batch=1, nchunks=1,
                                    row_block0=meta_blk, mask_rows=CHUNK - N_META, need_y=False)
    _, s5r_meta, s5i_meta = _s5(proj_s, zeros_s5, zeros_s5, *s5_w, *s5_v, batch=1, seq=CHUNK,
                                row_block0=meta_blk, need_y=False)[:3]

    yn_m, h_m, _, tail_m = _ssd(proj_m, dt_m, ht_meta[0], tail_meta[0], *ssd_args, batch=bsz,
                                nchunks=seq // CHUNK, row_block0=0, mask_rows=0)
    ybg_m, s5r_m, s5i_m, wpa, wglu, wpb, wout = _s5(
        proj_m, s5r_meta[0], s5i_meta[0], *s5_w, *s5_v, batch=bsz, seq=seq, row_block0=0,
        convert=(w_proj_a[0], w_glu[0], w_proj_b[0], w_out[0]))
    tail_w = (wpa, wglu, bglu, wpb, wout, fnw)

    yn_s, h_s, cs_s = _ssd_step(proj_s, dt_s, jnp.transpose(state_conv[0], (1, 0, 2)),
                                state_ssd[0].reshape(nseq, SSD_WIDTH, SSD_STATE), *ssd_args, nseq=nseq)
    seq_minor = lambda v: jnp.transpose(v, (1, 2, 0)).reshape(S5_NSTATE, nseq)
    seq_major = lambda v: jnp.transpose(v.reshape(S5_GROUPS, S5_STATE, nseq), (2, 0, 1))[None]
    ybg_s, s5r_s, s5i_s = _s5_step(proj_s, seq_minor(state_s5_re[0]), seq_minor(state_s5_im[0]),
                                   *s5_w, *s5_v, nseq=nseq)

    y_prompt, y_sample = _tail((yn_m, ybg_m, proj_m, x_main), (yn_s, ybg_s, proj_s, x_sample.reshape(nseq, d)),
                               *tail_w, tm=256)

    dt_out = x_prompt.dtype
    return (
        y_prompt.reshape(bsz, seq, d),
        y_sample.reshape(nseq, 1, d),
        h_m.reshape(1, bsz, SSD_HEADS, SSD_HEAD_DIM, SSD_STATE).astype(dt_out),
        tail_m[:, SUBLANES - (SSD_CONV - 1):, :].reshape(1, bsz, SSD_CONV - 1, SSD_XBC),
        s5r_m.reshape(1, bsz, S5_GROUPS, S5_STATE).astype(dt_out),
        s5i_m.reshape(1, bsz, S5_GROUPS, S5_STATE).astype(dt_out),
        h_s.reshape(1, nseq, SSD_HEADS, SSD_HEAD_DIM, SSD_STATE).astype(dt_out),
        jnp.transpose(cs_s, (1, 0, 2)).reshape(1, nseq, SSD_CONV - 1, SSD_XBC),
        seq_major(s5r_s).astype(dt_out),
        seq_major(s5i_s).astype(dt_out),
    )
```

```python
import functools

import jax
import jax.numpy as jnp
from jax import lax
from jax.experimental import pallas as pl
from jax.experimental.pallas import tpu as pltpu

F32 = jnp.float32
BF16 = jnp.bfloat16

NORM_EPS = 1e-5
LOG2E = 1.4426950408889634
N_META = 16
D_MODEL = 2048
SSD_HEAD_DIM = 64
SSD_HEADS = 32
SSD_GROUPS = 4
SSD_STATE = 128
SSD_WIDTH = SSD_HEADS * SSD_HEAD_DIM
SSD_XBC = SSD_WIDTH + 2 * SSD_GROUPS * SSD_STATE
SSD_CONV = 4
CHUNK = 128
S5_WIDTH = D_MODEL // 2
S5_GROUP = 16
S5_GROUPS = S5_WIDTH // S5_GROUP
S5_STATE = 64
S5_NSTATE = S5_GROUPS * S5_STATE
S5_BLOCKS = 4
S5_BLK_U = S5_WIDTH // S5_BLOCKS
S5_BLK_N = S5_NSTATE // S5_BLOCKS

LANES = 128
SUBLANES = 8
VMEM_LIMIT = 56 * 1024 * 1024
CONV_PITCH = 3

COL_ZA = 0
COL_GA = COL_ZA + SSD_WIDTH
COL_GB = COL_GA + D_MODEL
COL_XBC = COL_GB + D_MODEL
COL_UB = COL_XBC + SSD_XBC
COL_ZB = COL_UB + S5_WIDTH
PROJ_COLS = COL_ZB + S5_WIDTH
PROJ_TN = 1024


_NT = (((1,), (1,)), ((), ()))


def _sigmoid(x):
    return 0.5 * jnp.tanh(0.5 * x) + 0.5


def _silu(x):
    h = 0.5 * x
    return h + h * jnp.tanh(h)


def _softplus(x):
    return jnp.maximum(x, 0.0) + jnp.log1p(jnp.exp(-jnp.abs(x)))


def _gelu_tanh(x):
    c = 0.7978845608028654
    return 0.5 * x * (1.0 + jnp.tanh(c * (x + 0.044715 * (x * x * x))))


def _split3(x):
    x1 = x.astype(BF16)
    r1 = x - x1.astype(F32)
    x2 = r1.astype(BF16)
    x3 = (r1 - x2.astype(F32)).astype(BF16)
    return x1, x2, x3


def _params(*sem, vmem=VMEM_LIMIT):
    return pltpu.CompilerParams(dimension_semantics=sem, vmem_limit_bytes=vmem)


def _s5prep_kernel(lr_ref, li_ref, ldt_ref, btr_ref, bti_ref, abr_ref, abi_ref, bbr_ref, bbi_ref):
    lr = lr_ref[...]
    li = li_ref[...]
    step = jnp.exp(ldt_ref[...])
    mag = jnp.exp(lr * step)
    abr = mag * jnp.cos(li * step)
    abi = mag * jnp.sin(li * step)
    den = lr * lr + li * li
    numr = abr - 1.0
    cr = (numr * lr + abi * li) / den
    ci = (abi * lr - numr * li) / den
    abr_ref[...] = abr
    abi_ref[...] = abi
    btr = btr_ref[...]
    bti = bti_ref[...]
    crb = cr[:, None, :]
    cib = ci[:, None, :]
    bbr_ref[...] = crb * btr - cib * bti
    bbi_ref[...] = crb * bti + cib * btr


def _s5prep(lam_re, lam_im, log_dt, bt_re, bt_im):
    g, n = lam_re.shape
    full2 = pl.BlockSpec((g, n), lambda: (0, 0))
    full3 = pl.BlockSpec((g, S5_GROUP, n), lambda: (0, 0, 0))
    return pl.pallas_call(
        _s5prep_kernel,
        in_specs=[full2, full2, pl.BlockSpec((g, 1), lambda: (0, 0)), full3, full3],
        out_specs=[full2, full2, full3, full3],
        out_shape=[jax.ShapeDtypeStruct((g, n), F32)] * 2
        + [jax.ShapeDtypeStruct((g, S5_GROUP, n), F32)] * 2,
        name="s5prep",
    )(lam_re, lam_im, log_dt, bt_re, bt_im)


W_ALIGN = 32
_SRC_ZA = 0
_SRC_XBC = SSD_WIDTH
_SRC_DT = _SRC_XBC + SSD_XBC
_SRC_UB = _SRC_DT + SSD_HEADS
_SRC_ZB = _SRC_UB + S5_WIDTH
_SRC_GA = _SRC_ZB + S5_WIDTH
_SRC_GB = _SRC_GA + D_MODEL
_SEGMENTS = ((_SRC_ZA, SSD_WIDTH), (_SRC_GA, D_MODEL), (_SRC_GB, D_MODEL), (_SRC_XBC, SSD_XBC),
             (_SRC_UB, S5_WIDTH), (_SRC_ZB, S5_WIDTH))
_SRC_TILES = tuple((start + k) // W_ALIGN for start, width in _SEGMENTS for k in range(0, width, PROJ_TN))


def _wprep_dt_kernel(w_ref, o_ref):
    o_ref[...] = jnp.zeros(o_ref.shape, o_ref.dtype)
    o_ref[0:SSD_HEADS, :] = w_ref[...].astype(BF16)


def _wprep_dt(wt):
    d = wt.shape[1]
    assert _SRC_DT % SSD_HEADS == 0
    return pl.pallas_call(
        _wprep_dt_kernel,
        grid=(1,),
        in_specs=[pl.BlockSpec((SSD_HEADS, d), lambda i: (_SRC_DT // SSD_HEADS, 0))],
        out_specs=pl.BlockSpec((LANES, d), lambda i: (0, 0)),
        out_shape=jax.ShapeDtypeStruct((LANES, d), BF16),
        name="wprep_dt",
    )(wt)


def _norm_rows(x_ref, nw_ref, wdt_ref, xn_ref, dt_ref):
    x = x_ref[...]
    ms = jnp.mean(x * x, axis=-1, keepdims=True)
    xn = ((x * lax.rsqrt(ms + NORM_EPS)) * nw_ref[...]).astype(BF16)
    xn_ref[...] = xn
    dt_ref[...] = lax.dot_general(xn, wdt_ref[...], _NT, preferred_element_type=F32)


def _wprep_inproj_kernel(tbl_ref, x_ref, nw_ref, w_ref, wdt_ref, wout_ref, proj_ref, dt_ref, xn_ref):
    del tbl_ref

    @pl.when(pl.program_id(0) == 0)
    def _():
        _norm_rows(x_ref, nw_ref, wdt_ref, xn_ref, dt_ref)

    wb = w_ref[...].astype(BF16)
    wout_ref[...] = wb
    proj_ref[...] = lax.dot_general(xn_ref[...], wb, _NT, preferred_element_type=F32)


def _wprep_inproj(wt, w_dt, x, norm_w):
    m, d = x.shape
    ntiles = len(_SRC_TILES)
    assert ntiles * PROJ_TN == PROJ_COLS
    return pl.pallas_call(
        _wprep_inproj_kernel,
        grid_spec=pltpu.PrefetchScalarGridSpec(
            num_scalar_prefetch=1,
            grid=(ntiles,),
            in_specs=[
                pl.BlockSpec((m, d), lambda j, tbl: (0, 0)),
                pl.BlockSpec((1, d), lambda j, tbl: (0, 0)),
                pl.BlockSpec((pl.Element(PROJ_TN), pl.Element(d)),
                             lambda j, tbl: (pl.multiple_of(tbl[j] * W_ALIGN, W_ALIGN), 0)),
                pl.BlockSpec((LANES, d), lambda j, tbl: (0, 0)),
            ],
            out_specs=[
                pl.BlockSpec((PROJ_TN, d), lambda j, tbl: (j, 0)),
                pl.BlockSpec((m, PROJ_TN), lambda j, tbl: (0, j)),
                pl.BlockSpec((m, LANES), lambda j, tbl: (0, 0)),
            ],
            scratch_shapes=[pltpu.VMEM((m, d), BF16)],
        ),
        out_shape=[jax.ShapeDtypeStruct((PROJ_COLS, d), BF16), jax.ShapeDtypeStruct((m, PROJ_COLS), F32),
                   jax.ShapeDtypeStruct((m, LANES), F32)],
        compiler_params=_params("arbitrary"),
        name="wprep_inproj",
    )(jnp.asarray(_SRC_TILES, jnp.int32), x, norm_w, wt, w_dt)


def _inproj_kernel(x_ref, nw_ref, w_ref, wdt_ref, proj_ref, dt_ref, xn_ref):
    @pl.when(pl.program_id(1) == 0)
    def _():
        _norm_rows(x_ref, nw_ref, wdt_ref, xn_ref, dt_ref)

    proj_ref[...] = lax.dot_general(xn_ref[...], w_ref[...], _NT, preferred_element_type=F32)


def _inproj(x, norm_w, w_main, w_dt, tm):
    m, d = x.shape
    n = w_main.shape[0]
    return pl.pallas_call(
        _inproj_kernel,
        grid=(m // tm, n // PROJ_TN),
        in_specs=[
            pl.BlockSpec((tm, d), lambda i, j: (i, 0)),
            pl.BlockSpec((1, d), lambda i, j: (0, 0)),
            pl.BlockSpec((PROJ_TN, d), lambda i, j: (j, 0)),
            pl.BlockSpec((LANES, d), lambda i, j: (0, 0)),
        ],
        out_specs=[
            pl.BlockSpec((tm, PROJ_TN), lambda i, j: (i, j)),
            pl.BlockSpec((tm, LANES), lambda i, j: (i, 0)),
        ],
        out_shape=[jax.ShapeDtypeStruct((m, n), F32), jax.ShapeDtypeStruct((m, LANES), F32)],
        scratch_shapes=[pltpu.VMEM((tm, d), BF16)],
        compiler_params=_params("arbitrary", "arbitrary"),
        name="inproj",
    )(x, norm_w, w_main, w_dt)


def _ssd_chunk(r0, xbc_ref, za_ref, dtr_ref, convw_ref, convb_ref, dtb_ref, alog_ref, dexp_ref, nw_ref,
               y_ref, ht_scr, ext_scr, mask_rows, need_y):
    L = CHUNK
    P2 = 2 * SSD_HEAD_DIM
    GW = SSD_WIDTH // SSD_GROUPS
    time_rows = lambda t0, n: pl.ds(CONV_PITCH * t0, n, stride=CONV_PITCH)

    w = convw_ref[...]
    bias = convb_ref[...]
    conv_parts = []
    for s in range(SSD_XBC // LANES):
        ls = slice(s * LANES, (s + 1) * LANES)
        xc = xbc_ref[r0:r0 + L, ls]
        ext_scr[s, time_rows(SUBLANES, L), :] = xc
        acc = bias[:, ls]
        for k in range(SSD_CONV - 1):
            acc = acc + ext_scr[s, time_rows(SUBLANES - (SSD_CONV - 1) + k, L), :] * w[k:k + 1, ls]
        conv_parts.append(acc + xc * w[SSD_CONV - 1:SSD_CONV, ls])
        ext_scr[s, time_rows(0, SUBLANES), :] = xc[L - SUBLANES:L, :]
    xbc = _silu(jnp.concatenate(conv_parts, axis=1))
    xs = xbc[:, :SSD_WIDTH]
    bmat = xbc[:, SSD_WIDTH:SSD_WIDTH + SSD_GROUPS * SSD_STATE]
    cmat = xbc[:, SSD_WIDTH + SSD_GROUPS * SSD_STATE:]

    rows = lax.broadcasted_iota(jnp.int32, (L, L), 0)
    cols = lax.broadcasted_iota(jnp.int32, (L, L), 1)
    causal = rows >= cols
    lane_lo = cols < SSD_HEAD_DIM

    dt = _softplus(dtr_ref[r0:r0 + L, :] + dtb_ref[...])
    if mask_rows:
        dt = jnp.where(rows < mask_rows, 0.0, dt)
    a = dt * (-jnp.exp(alog_ref[...]))
    tril = jnp.where(causal, 1.0, 0.0).astype(BF16)
    a1, a2, a3 = _split3(a)
    acum = (jnp.dot(tril, a1, preferred_element_type=F32)
            + jnp.dot(tril, a2, preferred_element_type=F32)
            + jnp.dot(tril, a3, preferred_element_type=F32))
    a2 = acum * LOG2E
    e_cum = jnp.exp2(a2)
    w_end = dt * jnp.exp2(a2[L - 1:L, :] - a2)
    a2dt_t = (a2 - jnp.log2(dt)).T

    dexp = dexp_ref[...]
    xs_b = xs.astype(BF16)
    y_parts = []
    for g in range(SSD_GROUPS):
        bg = bmat[:, g * SSD_STATE:(g + 1) * SSD_STATE].astype(BF16)
        cg = cmat[:, g * SSD_STATE:(g + 1) * SSD_STATE].astype(BF16)
        ht_g = ht_scr[:, g * GW:(g + 1) * GW]
        if need_y:
            cb = lax.dot_general(cg, bg, (((1,), (1,)), ((), ())), preferred_element_type=F32)
            y_off = jnp.dot(cg, ht_g.astype(BF16), preferred_element_type=F32)
        xw_parts = []
        elast_parts = []
        for jj in range(GW // P2):
            lo = g * GW + jj * P2
            h0 = lo // SSD_HEAD_DIM
            yd, eb, wb = [], [], []
            for h in (h0, h0 + 1):
                if need_y:
                    colb = jnp.broadcast_to(a2[:, h:h + 1], (L, L))
                    m = cb * jnp.exp2(jnp.where(causal, colb - a2dt_t[h:h + 1, :], -jnp.inf))
                    yd.append(jnp.dot(m.astype(BF16), xs_b[:, lo:lo + P2], preferred_element_type=F32))
                eb.append(jnp.broadcast_to(e_cum[:, h:h + 1], (L, L)))
                wb.append(jnp.broadcast_to(w_end[:, h:h + 1], (L, L)))
            xs_pair = xs[:, lo:lo + P2]
            if need_y:
                y_pair = (jnp.where(lane_lo, yd[0], yd[1])
                          + y_off[:, jj * P2:(jj + 1) * P2] * jnp.where(lane_lo, eb[0], eb[1]))
                y_parts.append(y_pair + dexp[:, lo:lo + P2] * xs_pair)
            xw_parts.append(xs_pair * jnp.where(lane_lo, wb[0], wb[1]))
            elast_parts.append(jnp.where(lane_lo[0:1, :], eb[0][L - 1:L, :], eb[1][L - 1:L, :]))
        xw = jnp.concatenate(xw_parts, axis=1)
        elast = jnp.concatenate(elast_parts, axis=1)
        st = lax.dot_general(bg, xw.astype(BF16), (((0,), (0,)), ((), ())),
                             preferred_element_type=F32)
        ht_scr[:, g * GW:(g + 1) * GW] = ht_g * elast + st

    if not need_y:
        y_ref[r0:r0 + L, :] = jnp.zeros((L, SSD_WIDTH), y_ref.dtype)
        return
    y = jnp.concatenate(y_parts, axis=1)
    y = y * _silu(za_ref[r0:r0 + L, :])
    nw = nw_ref[...]
    outs = []
    for g in range(SSD_GROUPS):
        yg = y[:, g * GW:(g + 1) * GW]
        ms = jnp.mean(yg * yg, axis=-1, keepdims=True)
        outs.append((yg * lax.rsqrt(ms + NORM_EPS)) * nw[:, g * GW:(g + 1) * GW])
    y_ref[r0:r0 + L, :] = jnp.concatenate(outs, axis=1).astype(y_ref.dtype)


def _ssd_kernel(xbc_ref, za_ref, dtr_ref, ht0_ref, tail0_ref, convw_ref, convb_ref, dtb_ref, alog_ref,
                dexp_ref, nw_ref, y_ref, h_ref, ht_ref, tail_ref, ht_scr, ext_scr, *, mask_rows, need_y):
    c = pl.program_id(1)
    nrows = xbc_ref.shape[0]

    @pl.when(c == 0)
    def _():
        ht_scr[...] = ht0_ref[...]
        for s in range(SSD_XBC // LANES):
            ext_scr[s, pl.ds(0, SUBLANES, stride=CONV_PITCH), :] = tail0_ref[:, s * LANES:(s + 1) * LANES]

    for r0 in range(0, nrows, CHUNK):
        _ssd_chunk(r0, xbc_ref, za_ref, dtr_ref, convw_ref, convb_ref, dtb_ref, alog_ref, dexp_ref, nw_ref,
                   y_ref, ht_scr, ext_scr, mask_rows, need_y)

    @pl.when(c == pl.num_programs(1) - 1)
    def _():
        ht = ht_scr[...]
        ht_ref[...] = ht
        h_ref[...] = ht.T
        tail_ref[...] = xbc_ref[nrows - SUBLANES:nrows, :]


def _ssd(proj, dt_raw, ht0, tail0, conv_w, conv_b, dt_bias, a_log, d_exp, norm_w, *, batch, nchunks,
         row_block0, mask_rows, need_y=True):
    L = CHUNK
    rows = batch * nchunks * L
    rb = lambda b, c: row_block0 + b * nchunks + c
    const2 = lambda shape: pl.BlockSpec(shape, lambda b, c: (0, 0))
    return pl.pallas_call(
        functools.partial(_ssd_kernel, mask_rows=mask_rows, need_y=need_y),
        grid=(batch, nchunks),
        in_specs=[
            pl.BlockSpec((L, SSD_XBC), lambda b, c: (rb(b, c), COL_XBC // SSD_XBC)),
            pl.BlockSpec((L, SSD_WIDTH), lambda b, c: (rb(b, c), COL_ZA // SSD_WIDTH)),
            pl.BlockSpec((L, LANES), lambda b, c: (rb(b, c), 0)),
            const2((SSD_STATE, SSD_WIDTH)),
            const2((SUBLANES, SSD_XBC)),
            const2((SSD_CONV, SSD_XBC)),
            const2((1, SSD_XBC)),
            const2((1, LANES)),
            const2((1, LANES)),
            const2((1, SSD_WIDTH)),
            const2((1, SSD_WIDTH)),
        ],
        out_specs=[
            pl.BlockSpec((L, SSD_WIDTH), lambda b, c: (b * nchunks + c, 0)),
            pl.BlockSpec((None, SSD_WIDTH, SSD_STATE), lambda b, c: (b, 0, 0)),
            pl.BlockSpec((None, SSD_STATE, SSD_WIDTH), lambda b, c: (b, 0, 0)),
            pl.BlockSpec((None, SUBLANES, SSD_XBC), lambda b, c: (b, 0, 0)),
        ],
        out_shape=[
            jax.ShapeDtypeStruct((rows, SSD_WIDTH), BF16),
            jax.ShapeDtypeStruct((batch, SSD_WIDTH, SSD_STATE), F32),
            jax.ShapeDtypeStruct((batch, SSD_STATE, SSD_WIDTH), F32),
            jax.ShapeDtypeStruct((batch, SUBLANES, SSD_XBC), F32),
        ],
        scratch_shapes=[pltpu.VMEM((SSD_STATE, SSD_WIDTH), F32),
                        pltpu.VMEM((SSD_XBC // LANES, CONV_PITCH * (SUBLANES + CHUNK), LANES), F32)],
        compiler_params=_params("arbitrary", "arbitrary"),
        name="ssd",
    )(proj, proj, dt_raw, ht0, tail0, conv_w, conv_b, dt_bias, a_log, d_exp, norm_w)


def _ssd_step_kernel(xbc_ref, za_ref, dtr_ref, cs_ref, h_ref, convw_ref, convb_ref, dtb_ref, alog_ref,
                     dexp_ref, nw_ref, y_ref, hout_ref, csout_ref):
    R = SUBLANES
    GW = SSD_WIDTH // SSD_GROUPS
    x = xbc_ref[...]
    w = convw_ref[...]
    s0 = cs_ref[0]
    s1 = cs_ref[1]
    s2 = cs_ref[2]
    conv = convb_ref[...] + s0 * w[0:1]
    conv = conv + s1 * w[1:2]
    conv = conv + s2 * w[2:3]
    conv = conv + x * w[3:4]
    csout_ref[0] = s1
    csout_ref[1] = s2
    csout_ref[2] = x
    xbc = _silu(conv)
    xs = xbc[:, :SSD_WIDTH]
    bmat = xbc[:, SSD_WIDTH:SSD_WIDTH + SSD_GROUPS * SSD_STATE]
    cmat = xbc[:, SSD_WIDTH + SSD_GROUPS * SSD_STATE:]

    dt = _softplus(dtr_ref[...] + dtb_ref[...])
    da = jnp.exp(dt * (-jnp.exp(alog_ref[...])))
    dt_t = dt.T
    da_t = da.T
    expand = lambda v: jnp.concatenate(
        [jnp.broadcast_to(v[h:h + 1, :], (SSD_HEAD_DIM, R)) for h in range(SSD_HEADS)], axis=0)
    xd_t = xs.T * expand(dt_t)
    da_te = expand(da_t)

    cmat_b = cmat.astype(BF16)
    yrows = []
    for i in range(R):
        bexp = jnp.concatenate(
            [jnp.broadcast_to(bmat[i:i + 1, g * SSD_STATE:(g + 1) * SSD_STATE], (GW, SSD_STATE))
             for g in range(SSD_GROUPS)], axis=0)
        hn = h_ref[i] * da_te[:, i:i + 1] + xd_t[:, i:i + 1] * bexp
        hout_ref[i] = hn
        cg = jnp.concatenate([cmat_b[i:i + 1, g * SSD_STATE:(g + 1) * SSD_STATE] for g in range(SSD_GROUPS)]
                             + [jnp.zeros((R - SSD_GROUPS, SSD_STATE), BF16)], axis=0)
        yg = lax.dot_general(cg, hn.astype(BF16), (((1,), (1,)), ((), ())), preferred_element_type=F32)
        yrows.append(jnp.concatenate([yg[g:g + 1, g * GW:(g + 1) * GW] for g in range(SSD_GROUPS)], axis=1))
    y = jnp.concatenate(yrows, axis=0)
    y = y + dexp_ref[...] * xs
    y = y * _silu(za_ref[...])
    nw = nw_ref[...]
    outs = []
    for g in range(SSD_GROUPS):
        yg = y[:, g * GW:(g + 1) * GW]
        ms = jnp.mean(yg * yg, axis=-1, keepdims=True)
        outs.append((yg * lax.rsqrt(ms + NORM_EPS)) * nw[:, g * GW:(g + 1) * GW])
    y_ref[...] = jnp.concatenate(outs, axis=1).astype(y_ref.dtype)


def _ssd_step(proj, dt_raw, conv_state, ssd_state, conv_w, conv_b, dt_bias, a_log, d_exp, norm_w, *, nseq):
    R = SUBLANES
    const2 = lambda shape: pl.BlockSpec(shape, lambda i: (0, 0))
    return pl.pallas_call(
        _ssd_step_kernel,
        grid=(nseq // R,),
        in_specs=[
            pl.BlockSpec((R, SSD_XBC), lambda i: (i, COL_XBC // SSD_XBC)),
            pl.BlockSpec((R, SSD_WIDTH), lambda i: (i, COL_ZA // SSD_WIDTH)),
            pl.BlockSpec((R, LANES), lambda i: (i, 0)),
            pl.BlockSpec((SSD_CONV - 1, R, SSD_XBC), lambda i: (0, i, 0)),
            pl.BlockSpec((R, SSD_WIDTH, SSD_STATE), lambda i: (i, 0, 0)),
            const2((SSD_CONV, SSD_XBC)),
            const2((1, SSD_XBC)),
            const2((1, LANES)),
            const2((1, LANES)),
            const2((1, SSD_WIDTH)),
            const2((1, SSD_WIDTH)),
        ],
        out_specs=[
            pl.BlockSpec((R, SSD_WIDTH), lambda i: (i, 0)),
            pl.BlockSpec((R, SSD_WIDTH, SSD_STATE), lambda i: (i, 0, 0)),
            pl.BlockSpec((SSD_CONV - 1, R, SSD_XBC), lambda i: (0, i, 0)),
        ],
        out_shape=[
            jax.ShapeDtypeStruct((nseq, SSD_WIDTH), BF16),
            jax.ShapeDtypeStruct((nseq, SSD_WIDTH, SSD_STATE), F32),
            jax.ShapeDtypeStruct((SSD_CONV - 1, nseq, SSD_XBC), F32),
        ],
        compiler_params=_params("arbitrary"),
        name="ssd_step",
    )(proj, proj, dt_raw, conv_state, ssd_state, conv_w, conv_b, dt_bias, a_log, d_exp, norm_w)


def _s5_block_diag(w, rows_per_group, cols_per_group):
    ngrp, r, lanes = w.shape
    width = ngrp * cols_per_group
    tiled = jnp.concatenate([w.reshape(ngrp * r, lanes)] * (width // lanes), axis=1)
    row_g = lax.broadcasted_iota(jnp.int32, tiled.shape, 0) // rows_per_group
    col_g = lax.broadcasted_iota(jnp.int32, tiled.shape, 1) // cols_per_group
    return jnp.where(row_g == col_g, tiled, 0.0)


def _s5_weights(bbr_ref, bbi_ref, cr_ref, ci_ref):
    return tuple(_s5_block_diag(r[...], S5_GROUP, S5_STATE) for r in (bbr_ref, bbi_ref, cr_ref, ci_ref))


def _s5_pitch(seg_len):
    return seg_len if (seg_len // SUBLANES) % 2 else seg_len + SUBLANES


def _s5_kernel(*refs, seq, nconv, need_y):
    (u_ref, h0r_ref, h0i_ref, bbr_ref, bbi_ref, ctr_ref, cti_ref, ar_ref, ai_ref, d_ref) = refs[:10]
    w_in_refs = refs[10:10 + nconv]
    y_ref, hr_ref, hi_ref = refs[10 + nconv:13 + nconv]
    w_out_refs = refs[13 + nconv:13 + 2 * nconv]
    pad, pe, po, cr, ci, yeh, hb_out, w2r_scr, w2i_scr, cta_scr, ctb_scr, k0_scr = refs[13 + 2 * nconv:]
    for wi, wo in zip(w_in_refs, w_out_refs):
        wo[...] = wi[...].astype(BF16)

    NS = SUBLANES
    SL = seq // NS
    H = SL // 2
    pitch = _s5_pitch(SL)
    nk = S5_BLK_U // LANES
    n = S5_BLK_N
    a_re = ar_ref[...]
    a_im = ai_ref[...]
    bf = lambda v: v.astype(BF16)

    @pl.when(pl.program_id(1) == 0)
    def _():
        bt_re, bt_im, ct_re, ct_im = _s5_weights(bbr_ref, bbi_ref, ctr_ref, cti_ref)
        w2r_scr[...] = jnp.concatenate([bf(bt_re * a_re - bt_im * a_im), bf(bt_re)], axis=0)
        w2i_scr[...] = jnp.concatenate([bf(bt_re * a_im + bt_im * a_re), bf(bt_im)], axis=0)
        if need_y:
            cta_scr[0] = bf(ct_re * a_re - ct_im * a_im)
            cta_scr[1] = bf(ct_re * a_im + ct_im * a_re)
            ctb_scr[0] = bf(ct_re)
            ctb_scr[1] = bf(ct_im)
            k0_scr[...] = bf(lax.dot_general(bf(bt_re), bf(ct_re), _NT, preferred_element_type=F32)
                             - lax.dot_general(bf(bt_im), bf(ct_im), _NT, preferred_element_type=F32))

    for s in range(NS):
        for k in range(nk):
            pad[k, s * pitch:s * pitch + SL, :] = u_ref[s * SL:(s + 1) * SL, k * LANES:(k + 1) * LANES]

    for jp in range(H):
        for k in range(nk):
            pe[jp * NS:(jp + 1) * NS, k * LANES:(k + 1) * LANES] = pad[k, pl.ds(2 * jp, NS, stride=pitch), :]
            po[jp * NS:(jp + 1) * NS, k * LANES:(k + 1) * LANES] = pad[k, pl.ds(2 * jp + 1, NS, stride=pitch), :]
    half = seq // 2
    NB = 4 if H % 8 == 0 else 1
    rbk = half // NB
    spb = rbk // NS

    qc = 4 * S5_GROUP
    qn = 4 * S5_STATE

    def input_block(kb):
        rows = slice(kb * rbk, (kb + 1) * rbk)
        ue_b = bf(pe[rows, :])
        uo_b = bf(po[rows, :])
        for q in range(S5_BLK_U // qc):
            u_q = jnp.concatenate([ue_b[:, q * qc:(q + 1) * qc], uo_b[:, q * qc:(q + 1) * qc]], axis=1)
            rows_q = [slice(q * qc, (q + 1) * qc), slice(S5_BLK_U + q * qc, S5_BLK_U + (q + 1) * qc)]
            cols_q = slice(q * qn, (q + 1) * qn)
            wq_re = jnp.concatenate([w2r_scr[r, cols_q] for r in rows_q], axis=0)
            wq_im = jnp.concatenate([w2i_scr[r, cols_q] for r in rows_q], axis=0)
            cr[rows, cols_q] = jnp.dot(u_q, wq_re, preferred_element_type=F32)
            ci[rows, cols_q] = jnp.dot(u_q, wq_im, preferred_element_type=F32)

    a2_re = a_re * a_re - a_im * a_im
    a2_im = 2.0 * (a_re * a_im)
    ar = jnp.broadcast_to(a2_re, (NS, n))
    ai = jnp.broadcast_to(a2_im, (NS, n))

    def step(carry, r0):
        hr, hi = carry
        nr = ar * hr - ai * hi + cr[r0:r0 + NS, :]
        ni = ar * hi + ai * hr + ci[r0:r0 + NS, :]
        return nr, ni

    zero = jnp.zeros((NS, n), F32)
    carry = (zero, zero)
    input_block(0)
    for kb in range(NB):
        if kb + 1 < NB:
            input_block(kb + 1)
        for t in range(spb):
            carry = step(carry, (kb * spb + t) * NS)
    er, ei = carry

    asr, asi = a_re, a_im
    for _ in range(SL.bit_length() - 1):
        asr, asi = asr * asr - asi * asi, 2.0 * (asr * asi)
    gr = [h0r_ref[...]]
    gi = [h0i_ref[...]]
    for s in range(NS):
        gr.append(er[s:s + 1, :] + (asr * gr[s] - asi * gi[s]))
        gi.append(ei[s:s + 1, :] + (asr * gi[s] + asi * gr[s]))
    hr_ref[...] = gr[NS]
    hi_ref[...] = gi[NS]
    if not need_y:
        y_ref[...] = jnp.zeros(y_ref.shape, y_ref.dtype)
        return

    def scan_block(kb, c0):
        for p in range(spb // 2):
            r0 = (kb * spb + 2 * p) * NS
            c1 = step(c0, r0)
            c2 = step(c1, r0 + NS)
            hb_out[r0:r0 + 2 * NS, 0:n] = bf(jnp.concatenate([c1[0], c2[0]], axis=0))
            hb_out[r0:r0 + 2 * NS, n:2 * n] = bf(jnp.concatenate([c1[1], c2[1]], axis=0))
            c0 = c2
        return c0

    d = d_ref[...]
    cta_re, cta_im = cta_scr[0], cta_scr[1]
    ctb_re, ctb_im = ctb_scr[0], ctb_scr[1]
    k0 = k0_scr[...]

    def z_weights(q, part):
        sl = (slice(q * qc, (q + 1) * qc), slice(q * qn, (q + 1) * qn))
        return jnp.concatenate([ctb_scr[part][sl], cta_scr[part][sl]], axis=0)

    g_re = bf(jnp.concatenate(gr[:NS], axis=0))
    g_im = bf(jnp.concatenate(gi[:NS], axis=0))
    yeh[0:NS, :] = (lax.dot_general(g_re, cta_re, _NT, preferred_element_type=F32)
                    - lax.dot_general(g_im, cta_im, _NT, preferred_element_type=F32))

    def output_block(kb):
        rows = slice(kb * rbk, (kb + 1) * rbk)
        zs = []
        for q in range(S5_BLK_U // qc):
            cols_q = slice(q * qn, (q + 1) * qn)
            zs.append(lax.dot_general(hb_out[rows, cols_q], z_weights(q, 0), _NT, preferred_element_type=F32)
                      - lax.dot_general(hb_out[rows, n + q * qn:n + (q + 1) * qn], z_weights(q, 1), _NT,
                                        preferred_element_type=F32))
        y_odd = jnp.concatenate([z[:, 0:qc] for z in zs], axis=1)
        yeh[NS + kb * rbk:NS + (kb + 1) * rbk, :] = jnp.concatenate([z[:, qc:2 * qc] for z in zs], axis=1)
        y_even = yeh[rows, :] + jnp.dot(bf(pe[rows, :]), k0, preferred_element_type=F32)
        po[rows, :] = _gelu_tanh(y_odd + d * po[rows, :])
        pe[rows, :] = _gelu_tanh(y_even + d * pe[rows, :])

    carry = scan_block(0, (jnp.concatenate(gr[:NS], axis=0), jnp.concatenate(gi[:NS], axis=0)))
    for kb in range(NB):
        if kb + 1 < NB:
            carry = scan_block(kb + 1, carry)
        output_block(kb)

    for jp in range(H):
        for k in range(nk):
            pad[k, pl.ds(2 * jp, NS, stride=pitch), :] = pe[jp * NS:(jp + 1) * NS, k * LANES:(k + 1) * LANES]
            pad[k, pl.ds(2 * jp + 1, NS, stride=pitch), :] = po[jp * NS:(jp + 1) * NS, k * LANES:(k + 1) * LANES]
    for s in range(NS):
        for k in range(nk):
            y_ref[s * SL:(s + 1) * SL, k * LANES:(k + 1) * LANES] = pad[k, s * pitch:s * pitch + SL, :]


def _s5(proj, h0r, h0i, bb_re, bb_im, ct_re, ct_im, ab_re, ab_im, d_s5, *, batch, seq, row_block0, need_y=True,
        convert=()):
    ub0 = COL_UB // S5_BLK_U
    nsteps = batch * S5_BLOCKS
    slice_spec = lambda w: pl.BlockSpec((w.shape[0] // nsteps, w.shape[1]), lambda j, b: (j * batch + b, 0))
    assert all(w.shape[0] % (2 * SUBLANES * nsteps) == 0 for w in convert)
    seg = seq // SUBLANES
    assert seg & (seg - 1) == 0
    gpb = S5_GROUPS // S5_BLOCKS
    vec = lambda width: pl.BlockSpec((1, width), lambda j, b: (0, j))
    wspec = lambda r: pl.BlockSpec((gpb, r, LANES), lambda j, b: (j, 0, 0))
    st_out = pl.BlockSpec((None, 1, S5_BLK_N), lambda j, b: (b, 0, j))
    return pl.pallas_call(
        functools.partial(_s5_kernel, seq=seq, nconv=len(convert), need_y=need_y),
        grid=(S5_BLOCKS, batch),
        in_specs=[
            pl.BlockSpec((seq, S5_BLK_U), lambda j, b: (row_block0 + b, ub0 + j)),
            vec(S5_BLK_N), vec(S5_BLK_N),
            wspec(S5_GROUP), wspec(S5_GROUP), wspec(S5_GROUP), wspec(S5_GROUP),
            vec(S5_BLK_N), vec(S5_BLK_N), vec(S5_BLK_U),
        ] + [slice_spec(w) for w in convert],
        out_specs=[pl.BlockSpec((seq, S5_BLK_U), lambda j, b: (b, j)), st_out, st_out]
        + [slice_spec(w) for w in convert],
        out_shape=[
            jax.ShapeDtypeStruct((batch * seq, S5_WIDTH), F32),
            jax.ShapeDtypeStruct((batch, 1, S5_NSTATE), F32),
            jax.ShapeDtypeStruct((batch, 1, S5_NSTATE), F32),
        ] + [jax.ShapeDtypeStruct(w.shape, BF16) for w in convert],
        scratch_shapes=[
            pltpu.VMEM((S5_BLK_U // LANES, SUBLANES * _s5_pitch(seg), LANES), F32),
            pltpu.VMEM((seq // 2, S5_BLK_U), F32),
            pltpu.VMEM((seq // 2, S5_BLK_U), F32),
            pltpu.VMEM((seq // 2, S5_BLK_N), F32),
            pltpu.VMEM((seq // 2, S5_BLK_N), F32),
            pltpu.VMEM((seq // 2 + SUBLANES, S5_BLK_U), F32),
            pltpu.VMEM((seq // 2, 2 * S5_BLK_N), BF16),
            pltpu.VMEM((2 * S5_BLK_U, S5_BLK_N), BF16),
            pltpu.VMEM((2 * S5_BLK_U, S5_BLK_N), BF16),
            pltpu.VMEM((2, S5_BLK_U, S5_BLK_N), BF16),
            pltpu.VMEM((2, S5_BLK_U, S5_BLK_N), BF16),
            pltpu.VMEM((S5_BLK_U, S5_BLK_U), BF16),
        ],
        compiler_params=_params("arbitrary", "arbitrary"),
        name="s5",
    )(proj, h0r, h0i, bb_re, bb_im, ct_re, ct_im, ab_re, ab_im, d_s5, *convert)


def _s5_step_kernel(u_ref, h0r_ref, h0i_ref, bbr_ref, bbi_ref, ctr_ref, cti_ref, ar_ref, ai_ref, d_ref,
                    y_ref, hr_ref, hi_ref):
    bt_re, bt_im, ct_re, ct_im = (w.astype(BF16) for w in _s5_weights(bbr_ref, bbi_ref, ctr_ref, cti_ref))
    u = u_ref[...]
    ub = u.astype(BF16)
    ar = ar_ref[...]
    ai = ai_ref[...]
    h0r = h0r_ref[...].T
    h0i = h0i_ref[...].T
    hr = jnp.dot(ub, bt_re, preferred_element_type=F32) + (ar * h0r - ai * h0i)
    hi = jnp.dot(ub, bt_im, preferred_element_type=F32) + (ar * h0i + ai * h0r)
    hr_ref[...] = hr.T
    hi_ref[...] = hi.T
    y = (lax.dot_general(hr.astype(BF16), ct_re, _NT, preferred_element_type=F32)
         - lax.dot_general(hi.astype(BF16), ct_im, _NT, preferred_element_type=F32))
    y_ref[...] = _gelu_tanh(y + d_ref[...] * u)


def _s5_step(proj, h0r, h0i, bb_re, bb_im, ct_re, ct_im, ab_re, ab_im, d_s5, *, nseq):
    ub0 = COL_UB // S5_BLK_U
    gpb = S5_GROUPS // S5_BLOCKS
    vec = lambda width: pl.BlockSpec((1, width), lambda j: (0, j))
    wspec = lambda r: pl.BlockSpec((gpb, r, LANES), lambda j: (j, 0, 0))
    st = pl.BlockSpec((S5_BLK_N, nseq), lambda j: (j, 0))
    return pl.pallas_call(
        _s5_step_kernel,
        grid=(S5_BLOCKS,),
        in_specs=[
            pl.BlockSpec((nseq, S5_BLK_U), lambda j: (0, ub0 + j)),
            st, st,
            wspec(S5_GROUP), wspec(S5_GROUP), wspec(S5_GROUP), wspec(S5_GROUP),
            vec(S5_BLK_N), vec(S5_BLK_N), vec(S5_BLK_U),
        ],
        out_specs=[pl.BlockSpec((nseq, S5_BLK_U), lambda j: (0, j)), st, st],
        out_shape=[
            jax.ShapeDtypeStruct((nseq, S5_WIDTH), F32),
            jax.ShapeDtypeStruct((S5_NSTATE, nseq), F32),
            jax.ShapeDtypeStruct((S5_NSTATE, nseq), F32),
        ],
        compiler_params=_params("arbitrary"),
        name="s5_step",
    )(proj, h0r, h0i, bb_re, bb_im, ct_re, ct_im, ab_re, ab_im, d_s5)


def _tail_rows(yn_ref, ybg_ref, zb_ref, ga_ref, gb_ref, x_ref, wpa_ref, wglu_ref, bglu_ref, wpb_ref,
               wout_ref, fnw_ref, out_ref, arrived=lambda name: None):
    sg_a = _sigmoid(ga_ref[...])
    sg_b = _sigmoid(gb_ref[...])
    sz_b = _silu(zb_ref[...])
    yb = ybg_ref[...]
    arrived("glu")
    glu = jnp.dot(yb.astype(BF16), wglu_ref[...], preferred_element_type=F32) + bglu_ref[...]
    arrived("proj_a")
    ya = jnp.dot(yn_ref[...], wpa_ref[...], preferred_element_type=F32)
    yb = (yb * _sigmoid(glu)) * sz_b
    arrived("proj_b")
    ybp = jnp.dot(yb.astype(BF16), wpb_ref[...], preferred_element_type=F32)
    mixed = sg_a * ya + sg_b * ybp
    arrived("out")
    o = x_ref[...] + jnp.dot(mixed.astype(BF16), wout_ref[...], preferred_element_type=F32)
    ms = jnp.mean(o * o, axis=-1, keepdims=True)
    out_ref[...] = (o * lax.rsqrt(ms + NORM_EPS)) * fnw_ref[...]


def _tail_kernel(*refs, n_main):
    main_in, samp_hbm, (out_m_ref, out_s_ref) = refs[0:6], refs[6:12], refs[18:20]
    wpa_hbm, wglu_hbm, bglu_ref, wpb_hbm, wout_hbm, fnw_ref = refs[12:18]
    wpa_ref, wglu_ref, wpb_ref, wout_ref, sem = refs[20:25]
    samp_in, samp_sem = refs[25:31], refs[31]
    ms = out_s_ref.shape[0]
    samp_cols = (None, None, COL_ZB, COL_GA, COL_GB, None)
    samp_copies = [
        pltpu.make_async_copy(
            src if col is None else src.at[pl.ds(0, ms), pl.ds(col, dst.shape[1])], dst, samp_sem.at[k])
        for k, (src, dst, col) in enumerate(zip(samp_hbm, samp_in, samp_cols))]
    weights = (wpa_ref, wglu_ref, bglu_ref, wpb_ref, wout_ref, fnw_ref)
    i = pl.program_id(0)

    copies = {
        "glu": pltpu.make_async_copy(wglu_hbm, wglu_ref, sem.at[0]),
        "proj_a": pltpu.make_async_copy(wpa_hbm, wpa_ref, sem.at[1]),
        "proj_b": pltpu.make_async_copy(wpb_hbm, wpb_ref, sem.at[2]),
        "out": pltpu.make_async_copy(wout_hbm, wout_ref, sem.at[3]),
    }

    @pl.when(i == 0)
    def _():
        for name in ("glu", "proj_a", "proj_b", "out"):
            copies[name].start()
        for c in samp_copies:
            c.start()
        _tail_rows(*main_in, *weights, out_m_ref, arrived=lambda name: copies[name].wait())

    @pl.when((i > 0) & (i < n_main))
    def _():
        _tail_rows(*main_in, *weights, out_m_ref)

    @pl.when(i == n_main)
    def _():
        for c in samp_copies:
            c.wait()
        _tail_rows(*samp_in, *weights, out_s_ref)


def _tail(main, samp, w_proj_a, w_glu, b_glu, w_proj_b, w_out, final_norm_w, *, tm):
    m, d = main[3].shape
    ms = samp[3].shape[0]
    n_main = m // tm
    resident = lambda shape: pl.BlockSpec(shape, lambda i: (0, 0), pipeline_mode=pl.Buffered(1))
    row = lambda i: jnp.minimum(i, n_main - 1)
    expand = lambda t: (t[0], t[1], t[2], t[2], t[2], t[3])
    in_hbm = pl.BlockSpec(memory_space=pl.ANY)
    assert n_main >= 1 and w_proj_a.dtype == BF16 and w_out.dtype == BF16
    return pl.pallas_call(
        functools.partial(_tail_kernel, n_main=n_main),
        grid=(n_main + 1,),
        in_specs=[
            pl.BlockSpec((tm, SSD_WIDTH), lambda i: (row(i), 0)),
            pl.BlockSpec((tm, S5_WIDTH), lambda i: (row(i), 0)),
            pl.BlockSpec((tm, S5_WIDTH), lambda i: (row(i), COL_ZB // S5_WIDTH)),
            pl.BlockSpec((tm, d), lambda i: (row(i), COL_GA // D_MODEL)),
            pl.BlockSpec((tm, d), lambda i: (row(i), COL_GB // D_MODEL)),
            pl.BlockSpec((tm, d), lambda i: (row(i), 0)),
            in_hbm, in_hbm, in_hbm, in_hbm, in_hbm, in_hbm,
            in_hbm,
            in_hbm,
            resident((1, S5_WIDTH)),
            in_hbm,
            in_hbm,
            resident((1, d)),
        ],
        out_specs=[
            pl.BlockSpec((tm, d), lambda i: (row(i), 0)),
            pl.BlockSpec((ms, d), lambda i: (0, 0)),
        ],
        out_shape=[jax.ShapeDtypeStruct((m, d), F32), jax.ShapeDtypeStruct((ms, d), F32)],
        scratch_shapes=[
            pltpu.VMEM((SSD_WIDTH, d), BF16),
            pltpu.VMEM((S5_WIDTH, S5_WIDTH), BF16),
            pltpu.VMEM((S5_WIDTH, d), BF16),
            pltpu.VMEM((d, d), BF16),
            pltpu.SemaphoreType.DMA((4,)),
            pltpu.VMEM((ms, SSD_WIDTH), main[0].dtype),
            pltpu.VMEM((ms, S5_WIDTH), F32),
            pltpu.VMEM((ms, S5_WIDTH), F32),
            pltpu.VMEM((ms, d), F32),
            pltpu.VMEM((ms, d), F32),
            pltpu.VMEM((ms, d), F32),
            pltpu.SemaphoreType.DMA((6,)),
        ],
        compiler_params=_params("arbitrary", vmem=60 * 1024 * 1024),
        name="tail",
    )(*expand(main), *expand(samp), w_proj_a, w_glu, b_glu, w_proj_b, w_out, final_norm_w)


def kernel(x_prompt, x_sample, state_ssd, state_conv, state_s5_re, state_s5_im, meta_tokens, norm_w, w_in,
           conv_w, conv_b, dt_bias, a_log, d_ssd, ssd_norm_w, w_proj_a, lam_re, lam_im, log_dt_s5, b_re, b_im,
           c_re, c_im, d_s5, w_glu, b_glu, w_proj_b, w_out, final_norm_w):
    bsz, seq, d = x_prompt.shape
    nseq = x_sample.shape[0]
    assert d == D_MODEL and seq % CHUNK == 0 and nseq % SUBLANES == 0 and norm_w.shape[0] == 1
    assert meta_tokens.shape[0] == N_META and N_META <= CHUNK

    assert w_in.shape[2] == _SRC_GB + D_MODEL
    wt = jnp.transpose(w_in[0])
    w_dt = _wprep_dt(wt)
    pad_heads = lambda v: jnp.pad(v.reshape(1, SSD_HEADS), ((0, 0), (0, LANES - SSD_HEADS)))
    dtb = pad_heads(dt_bias[0])
    alog = pad_heads(a_log[0])
    d_exp = jnp.repeat(d_ssd[0], SSD_HEAD_DIM).reshape(1, SSD_WIDTH)
    nw1 = norm_w[0].reshape(1, d)
    ssd_nw = ssd_norm_w[0].reshape(1, SSD_WIDTH)
    convw = conv_w[0]
    convb = conv_b[0].reshape(1, SSD_XBC)
    bglu = b_glu[0].reshape(1, S5_WIDTH)
    fnw = final_norm_w.reshape(1, d)
    ds5 = d_s5[0].reshape(1, S5_WIDTH)

    rep = lambda v, k: jnp.concatenate([v] * k, axis=-1)
    lane_rep = LANES // S5_STATE
    ab_re, ab_im, bb_re, bb_im = _s5prep(
        rep(lam_re[0], lane_rep), rep(lam_im[0], lane_rep), log_dt_s5[0].reshape(S5_GROUPS, 1),
        rep(jnp.transpose(b_re[0], (0, 2, 1)), lane_rep), rep(jnp.transpose(b_im[0], (0, 2, 1)), lane_rep))
    ab_re = ab_re[:, :S5_STATE].reshape(1, S5_NSTATE)
    ab_im = ab_im[:, :S5_STATE].reshape(1, S5_NSTATE)
    ct_re = rep(c_re[0], lane_rep)
    ct_im = rep(c_im[0], lane_rep)

    x_main = x_prompt.reshape(bsz * seq, d)
    x_small = jnp.concatenate(
        [x_sample.reshape(nseq, d), jnp.zeros((CHUNK - N_META, d), x_prompt.dtype),
         meta_tokens.astype(x_prompt.dtype)], axis=0)
    assert nseq % CHUNK == 0
    meta_blk = nseq // CHUNK
    tm_main = 1024 if (bsz * seq) % 1024 == 0 else CHUNK
    w_main, proj_s, dt_s = _wprep_inproj(wt, w_dt, x_small, nw1)
    proj_m, dt_m = _inproj(x_main, nw1, w_main, w_dt, tm=tm_main)

    ssd_args = (convw, convb, dtb, alog, d_exp, ssd_nw)
    s5_w = (bb_re, bb_im, ct_re, ct_im)
    s5_v = (ab_re, ab_im, ds5)

    zeros_ht = jnp.zeros((SSD_STATE, SSD_WIDTH), F32)
    zeros_tail = jnp.zeros((SUBLANES, SSD_XBC), F32)
    zeros_s5 = jnp.zeros((1, S5_NSTATE), F32)
    _, _, ht_meta, tail_meta = _ssd(proj_s, dt_s, zeros_ht, zeros_tail, *ssd_args, batch=1, nchunks=1,
                                    row_block0=meta_blk, mask_rows=CHUNK - N_META, need_y=False)
    _, s5r_meta, s5i_meta = _s5(proj_s, zeros_s5, zeros_s5, *s5_w, *s5_v, batch=1, seq=CHUNK,
                                row_block0=meta_blk, need_y=False)[:3]

    yn_m, h_m, _, tail_m = _ssd(proj_m, dt_m, ht_meta[0], tail_meta[0], *ssd_args, batch=bsz,
                                nchunks=seq // CHUNK, row_block0=0, mask_rows=0)
    ybg_m, s5r_m, s5i_m, wpa, wglu, wpb, wout = _s5(
        proj_m, s5r_meta[0], s5i_meta[0], *s5_w, *s5_v, batch=bsz, seq=seq, row_block0=0,
        convert=(w_proj_a[0], w_glu[0], w_proj_b[0], w_out[0]))
    tail_w = (wpa, wglu, bglu, wpb, wout, fnw)

    yn_s, h_s, cs_s = _ssd_step(proj_s, dt_s, jnp.transpose(state_conv[0], (1, 0, 2)),
                                state_ssd[0].reshape(nseq, SSD_WIDTH, SSD_STATE), *ssd_args, nseq=nseq)
    seq_minor = lambda v: jnp.transpose(v, (1, 2, 0)).reshape(S5_NSTATE, nseq)
    seq_major = lambda v: jnp.transpose(v.reshape(S5_GROUPS, S5_STATE, nseq), (2, 0, 1))[None]
    ybg_s, s5r_s, s5i_s = _s5_step(proj_s, seq_minor(state_s5_re[0]), seq_minor(state_s5_im[0]),
                                   *s5_w, *s5_v, nseq=nseq)

    y_prompt, y_sample = _tail((yn_m, ybg_m, proj_m, x_main), (yn_s, ybg_s, proj_s, x_sample.reshape(nseq, d)),
                               *tail_w, tm=256)

    dt_out = x_prompt.dtype
    return (
        y_prompt.reshape(bsz, seq, d),
        y_sample.reshape(nseq, 1, d),
        h_m.reshape(1, bsz, SSD_HEADS, SSD_HEAD_DIM, SSD_STATE).astype(dt_out),
        tail_m[:, SUBLANES - (SSD_CONV - 1):, :].reshape(1, bsz, SSD_CONV - 1, SSD_XBC),
        s5r_m.reshape(1, bsz, S5_GROUPS, S5_STATE).astype(dt_out),
        s5i_m.reshape(1, bsz, S5_GROUPS, S5_STATE).astype(dt_out),
        h_s.reshape(1, nseq, SSD_HEADS, SSD_HEAD_DIM, SSD_STATE).astype(dt_out),
        jnp.transpose(cs_s, (1, 0, 2)).reshape(1, nseq, SSD_CONV - 1, SSD_XBC),
        seq_major(s5r_s).astype(dt_out),
        seq_major(s5i_s).astype(dt_out),
    )
```
